```python
import jax, jax.numpy as jnp
from jax import lax
import numpy as np

D_MODEL = 2048
BATCH = 4
SEQ = 2048
DEPTH = 1
DEC_BATCH = 128
DEC_SEQ = 1
PAST_LEN = 16384
PAGE_SIZE = 128

POOL_WIDTH = D_MODEL // 2
POOL_WINDOWS = (2, 4, 8, 16)
N_POOL_GROUPS = len(POOL_WINDOWS)
POOL_GROUP = POOL_WIDTH // N_POOL_GROUPS
POOL_HIST = max(POOL_WINDOWS) - 1
CONV_WIDTH = D_MODEL // 2
CONV_K = 3
CONV_HIST = CONV_K - 1
D_FF = 4 * D_MODEL
IN_WIDTH = POOL_WIDTH + 3 * CONV_WIDTH + 2 * D_MODEL
N_ADA = 6
ALPHA = (2 * DEPTH) ** 0.25
BETA = (8 * DEPTH) ** -0.25
LN_EPS = 1e-5

kernel_name = "gated_pool_shortconv_deepnorm_adaln_step"


def _layernorm(x, g, b):
    xf = x.astype(jnp.float32)
    mu = jnp.mean(xf, axis=-1, keepdims=True)
    var = jnp.mean(jnp.square(xf - mu), axis=-1, keepdims=True)
    y = (xf - mu) * lax.rsqrt(var + LN_EPS)
    return (y * g.astype(jnp.float32) + b.astype(jnp.float32)).astype(x.dtype)


def _pool_mixer(z, hist, pos, grp_w, scale):
    L = z.shape[1]
    zz = jnp.concatenate([hist.astype(z.dtype), z], axis=1)
    zf = zz.astype(jnp.float32)
    cs = jnp.concatenate([jnp.zeros_like(zf[:, :1]), jnp.cumsum(zf, axis=1)], axis=1)
    upper = cs[:, POOL_HIST + 1:]
    means = []
    for g, w in enumerate(POOL_WINDOWS):
        sl = slice(g * POOL_GROUP, (g + 1) * POOL_GROUP)
        lower = cs[:, POOL_HIST + 1 - w: POOL_HIST + 1 - w + L, sl]
        cnt = jnp.minimum(w, pos + 1).astype(jnp.float32)
        means.append((upper[..., sl] - lower) / cnt[None, :, None])
    pooled = (jnp.concatenate(means, axis=-1) - z.astype(jnp.float32)).astype(z.dtype)
    B = z.shape[0]
    pg = pooled.reshape(B, L, N_POOL_GROUPS, POOL_GROUP)
    out = jnp.einsum('blgi,gio->blgo', pg, grp_w).reshape(B, L, POOL_WIDTH) * scale
    return out, zz[:, -POOL_HIST:]


def _short_conv(v, hist, conv_w):
    L = v.shape[1]
    vv = jnp.concatenate([hist.astype(v.dtype), v], axis=1)
    out = conv_w[0] * vv[:, 0:L]
    for k in range(1, CONV_K):
        out = out + conv_w[k] * vv[:, k:k + L]
    return out, vv[:, -CONV_HIST:]


def _layer(x, c, pool_hist, conv_hist, pos, w_ada, b_ada, w_in, pool_grp_w, pool_scale, conv_w,
           w_pool_up, w_conv_up, w_o, ln1_g, ln1_b, w_ff1, b_ff1, w_ff2, b_ff2, ln2_g, ln2_b):
    ada = (c @ w_ada + b_ada)[:, None, :]
    sh1, sc1, g1, sh2, sc2, g2 = jnp.split(ada, N_ADA, axis=-1)
    u = x * (1 + sc1) + sh1
    proj = u @ w_in
    o1 = POOL_WIDTH
    o2 = o1 + CONV_WIDTH
    o3 = o2 + CONV_WIDTH
    o4 = o3 + CONV_WIDTH
    o5 = o4 + D_MODEL
    z_p, x_c, b_c, c_c, gate_p, gate_c = jnp.split(proj, [o1, o2, o3, o4, o5], axis=-1)
    y_p, new_pool = _pool_mixer(z_p, pool_hist, pos, pool_grp_w, pool_scale)
    y_p = y_p @ w_pool_up
    y_c, new_conv = _short_conv(c_c * x_c, conv_hist, conv_w)
    y_c = (b_c * y_c) @ w_conv_up
    mixed = (jax.nn.sigmoid(gate_p) * y_p + jax.nn.sigmoid(gate_c) * y_c) @ w_o
    x = _layernorm(ALPHA * x + g1 * mixed, ln1_g, ln1_b)
    u2 = x * (1 + sc2) + sh2
    h = jnp.square(jax.nn.relu(u2 @ w_ff1 + b_ff1))
    x = _layernorm(ALPHA * x + g2 * (h @ w_ff2 + b_ff2), ln2_g, ln2_b)
    return x, new_pool, new_conv


def setup_inputs(seed: int = 0) -> dict:
    key = jax.random.key(seed)
    ks = jax.random.split(key, 32)
    f32 = jnp.float32

    def nrm(k, shape, s):
        return jax.random.normal(k, shape, f32) * s

    d = {}
    d["x_prompt"] = nrm(ks[0], (BATCH, SEQ, D_MODEL), 1.0)
    d["x_sample"] = nrm(ks[1], (DEC_BATCH, DEC_SEQ, D_MODEL), 1.0)
    d["state_pool"] = nrm(ks[2], (DEPTH, DEC_BATCH, POOL_HIST, POOL_WIDTH), 1.0)
    d["state_conv"] = nrm(ks[3], (DEPTH, DEC_BATCH, CONV_HIST, CONV_WIDTH), 1.0)
    d["c_prompt"] = nrm(ks[4], (BATCH, D_MODEL), 1.0)
    d["c_sample"] = nrm(ks[5], (DEC_BATCH, D_MODEL), 1.0)
    d["w_ada"] = nrm(ks[6], (DEPTH, D_MODEL, N_ADA * D_MODEL), 0.3 * D_MODEL ** -0.5)
    d["b_ada"] = nrm(ks[7], (DEPTH, N_ADA * D_MODEL), 0.02)
    d["w_in"] = nrm(ks[8], (DEPTH, D_MODEL, IN_WIDTH), D_MODEL ** -0.5)
    d["pool_grp_w"] = nrm(ks[9], (DEPTH, N_POOL_GROUPS, POOL_GROUP, POOL_GROUP), POOL_GROUP ** -0.5)
    d["pool_scale"] = 1.0 + nrm(ks[10], (DEPTH, POOL_WIDTH), 0.1)
    d["conv_w"] = nrm(ks[11], (DEPTH, CONV_K, CONV_WIDTH), CONV_K ** -0.5)
    d["w_pool_up"] = nrm(ks[12], (DEPTH, POOL_WIDTH, D_MODEL), POOL_WIDTH ** -0.5)
    d["w_conv_up"] = nrm(ks[13], (DEPTH, CONV_WIDTH, D_MODEL), CONV_WIDTH ** -0.5)
    d["w_o"] = nrm(ks[14], (DEPTH, D_MODEL, D_MODEL), BETA * D_MODEL ** -0.5)
    d["ln1_g"] = 1.0 + nrm(ks[15], (DEPTH, D_MODEL), 0.05)
    d["ln1_b"] = nrm(ks[16], (DEPTH, D_MODEL), 0.02)
    d["w_ff1"] = nrm(ks[17], (DEPTH, D_MODEL, D_FF), D_MODEL ** -0.5)
    d["b_ff1"] = nrm(ks[18], (DEPTH, D_FF), 0.02)
    d["w_ff2"] = nrm(ks[19], (DEPTH, D_FF, D_MODEL), BETA * D_FF ** -0.5)
    d["b_ff2"] = nrm(ks[20], (DEPTH, D_MODEL), 0.02)
    d["ln2_g"] = 1.0 + nrm(ks[21], (DEPTH, D_MODEL), 0.05)
    d["ln2_b"] = nrm(ks[22], (DEPTH, D_MODEL), 0.02)
    return d


def reference(x_prompt, x_sample, state_pool, state_conv, c_prompt, c_sample,
              w_ada, b_ada, w_in, pool_grp_w, pool_scale, conv_w, w_pool_up, w_conv_up, w_o,
              ln1_g, ln1_b, w_ff1, b_ff1, w_ff2, b_ff2, ln2_g, ln2_b):
    pos_prompt = jnp.arange(SEQ, dtype=jnp.int32)
    pos_sample = PAST_LEN + jnp.arange(x_sample.shape[1], dtype=jnp.int32)
    nb = x_prompt.shape[0]
    zero_pool = jnp.zeros((nb, POOL_HIST, POOL_WIDTH), x_prompt.dtype)
    zero_conv = jnp.zeros((nb, CONV_HIST, CONV_WIDTH), x_prompt.dtype)
    xp, xs = x_prompt, x_sample
    pool_p, conv_p, pool_s, conv_s = [], [], [], []
    for l in range(DEPTH):
        lw = (w_ada[l], b_ada[l], w_in[l], pool_grp_w[l], pool_scale[l], conv_w[l],
              w_pool_up[l], w_conv_up[l], w_o[l], ln1_g[l], ln1_b[l],
              w_ff1[l], b_ff1[l], w_ff2[l], b_ff2[l], ln2_g[l], ln2_b[l])
        xp, npp, ncp = _layer(xp, c_prompt, zero_pool, zero_conv, pos_prompt, *lw)
        xs, nps, ncs = _layer(xs, c_sample, state_pool[l], state_conv[l], pos_sample, *lw)
        pool_p.append(npp)
        conv_p.append(ncp)
        pool_s.append(nps)
        conv_s.append(ncs)
    return (xp, xs, jnp.stack(pool_p), jnp.stack(conv_p), jnp.stack(pool_s), jnp.stack(conv_s))
```

```python
import functools

import jax
import jax.numpy as jnp
from jax import lax
from jax.experimental import pallas as pl
from jax.experimental.pallas import tpu as pltpu

F32 = jnp.float32
BF16 = jnp.bfloat16

POOL_WINDOWS = (2, 4, 8, 16)
POOL_HIST = max(POOL_WINDOWS) - 1
CONV_K = 3
CONV_HIST = CONV_K - 1
N_ADA = 6
PAST_LEN = 16384
LN_EPS = 1e-5

VMEM_LIMIT_BYTES_V7X = 56 * 1024 * 1024
HIST_ROWS = 16
ROW_CHUNK = 256


def _resident(shape):
    zeros = (0,) * len(shape)
    return pl.BlockSpec(shape, lambda *_: zeros, pipeline_mode=pl.Buffered(1))


def _params(semantics):
    return pltpu.CompilerParams(dimension_semantics=semantics,
                                vmem_limit_bytes=VMEM_LIMIT_BYTES_V7X)


def _layernorm(xf, g, b):
    mu = jnp.mean(xf, axis=-1, keepdims=True)
    var = jnp.mean(jnp.square(xf - mu), axis=-1, keepdims=True)
    return (xf - mu) * lax.rsqrt(var + LN_EPS) * g + b


def _ada_body(c_ref, w_ref, b_ref, o_ref):
    o_ref[...] = jnp.dot(c_ref[...].astype(BF16), w_ref[...].astype(BF16),
                         preferred_element_type=F32) + b_ref[...]


def _ada(c_all, w_ada, b_ada, tn=1024):
    m, d = c_all.shape
    n = w_ada.shape[1]
    return pl.pallas_call(
        _ada_body,
        out_shape=jax.ShapeDtypeStruct((m, n), F32),
        grid=(n // tn,),
        in_specs=[pl.BlockSpec((m, d), lambda j: (0, 0)),
                  pl.BlockSpec((d, tn), lambda j: (0, j)),
                  pl.BlockSpec((1, tn), lambda j: (0, j))],
        out_specs=pl.BlockSpec((m, tn), lambda j: (0, j)),
        compiler_params=_params(("arbitrary",)),
        name="ada",
    )(c_all, w_ada, b_ada.reshape(1, n))


def _cast_body(*refs):
    n = len(refs) // 2
    for src, dst in zip(refs[:n], refs[n:]):
        dst[...] = src[...].astype(BF16)


def _cast_weights(ws, steps=8):
    in_specs, out_specs, out_shape = [], [], []
    for w in ws:
        r, c = w.shape
        blk = pl.BlockSpec((r // steps, c), lambda s: (s, 0))
        in_specs.append(blk)
        out_specs.append(blk)
        out_shape.append(jax.ShapeDtypeStruct((r, c), BF16))
    return pl.pallas_call(
        _cast_body, out_shape=out_shape, grid=(steps,),
        in_specs=in_specs, out_specs=out_specs,
        compiler_params=_params(("arbitrary",)),
        name="cast_weights",
    )(*ws)


def _inproj_body(x_ref, adap_ref, w_ref, xs_ref, adas_ref, o_ref, os_ref, wb_ref,
                 *, tiles_per_seq, d, tm):
    i = pl.program_id(1)

    @pl.when(i == 0)
    def _():
        wb_ref[...] = w_ref[...].astype(BF16)
        us = xs_ref[...] * (1.0 + adas_ref[:, d:2 * d]) + adas_ref[:, 0:d]
        os_ref[...] = jnp.dot(us.astype(BF16), wb_ref[...], preferred_element_type=F32)

    b = i // tiles_per_seq
    sh1 = adap_ref[pl.ds(b, 1), 0:d]
    sc1 = adap_ref[pl.ds(b, 1), d:2 * d]
    for r in range(0, tm, ROW_CHUNK):
        rows = slice(r, r + ROW_CHUNK)
        u = x_ref[rows, :] * (1.0 + sc1) + sh1
        o_ref[rows, :] = jnp.dot(u.astype(BF16), wb_ref[...],
                                 preferred_element_type=F32).astype(BF16)


def _inproj(xp, xs, ada, w_in, *, seq, prompt_row, tm=1024, tn=1024):
    m, d = xp.shape
    ms = xs.shape[0]
    n = w_in.shape[1]
    body = functools.partial(_inproj_body, tiles_per_seq=seq // tm, d=d, tm=tm)
    return pl.pallas_call(
        body,
        out_shape=[jax.ShapeDtypeStruct((m, n), BF16),
                   jax.ShapeDtypeStruct((ms, n), F32)],
        grid=(n // tn, m // tm),
        in_specs=[pl.BlockSpec((tm, d), lambda j, i: (i, 0)),
                  pl.BlockSpec((8, 2 * d), lambda j, i: (prompt_row // 8, 0)),
                  pl.BlockSpec((d, tn), lambda j, i: (0, j)),
                  pl.BlockSpec((ms, d), lambda j, i: (0, 0)),
                  pl.BlockSpec((ms, 2 * d), lambda j, i: (0, 0))],
        out_specs=[pl.BlockSpec((tm, tn), lambda j, i: (i, j)),
                   pl.BlockSpec((ms, tn), lambda j, i: (0, j))],
        scratch_shapes=[pltpu.VMEM((d, tn), BF16)],
        compiler_params=_params(("arbitrary", "arbitrary")),
        name="inproj",
    )(xp, ada, w_in, xs, ada)


def _mix_tail(pooled_bf, conv_bf, gp, gc, x, g1, sc2, sh2, wpu_ref, wcu_ref, wo_ref,
              lng_ref, lnb_ref, alpha):
    y_p = jnp.dot(pooled_bf, wpu_ref[...], preferred_element_type=F32)
    y_c = jnp.dot(conv_bf, wcu_ref[...], preferred_element_type=F32)
    mixed = jax.nn.sigmoid(gp) * y_p + jax.nn.sigmoid(gc) * y_c
    mo = jnp.dot(mixed.astype(BF16), wo_ref[...], preferred_element_type=F32)
    x1 = _layernorm(alpha * x + g1 * mo, lng_ref[...], lnb_ref[...])
    u2 = x1 * (1.0 + sc2) + sh2
    return x1, u2.astype(BF16)


def _mixer_body(proj_ref, hist_ref, x_ref, ada_ref, grp_ref, pscale_ref, convw_ref,
                wpu_ref, wcu_ref, wo_ref, lng_ref, lnb_ref,
                x1_ref, u2_ref, pst_ref, cst_ref, zz_ref, vv_ref, pm_ref,
                *, tm, tiles_per_seq, d, alpha):
    i = pl.program_id(0)
    tis = i % tiles_per_seq
    b = i // tiles_per_seq
    first = tis == 0
    p = d // 2
    gw = p // len(POOL_WINDOWS)
    h = HIST_ROWS

    z = proj_ref[:, 0:p].astype(F32)
    zz_ref[0:h, :] = jnp.where(first, 0.0, hist_ref[:, 0:p].astype(F32))
    zz_ref[h:h + tm, :] = z
    pos = tis * tm + lax.broadcasted_iota(jnp.int32, (tm, 1), 0)
    for g, w in enumerate(POOL_WINDOWS):
        cols = slice(g * gw, (g + 1) * gw)
        s = zz_ref[h:h + tm, cols]
        for k in range(1, w):
            s = s + zz_ref[h - k:h - k + tm, cols]
        cnt = jnp.minimum(w, pos + 1).astype(F32)
        pooled = (s / cnt - z[:, cols]).astype(BF16)
        og = jnp.dot(pooled, grp_ref[g * gw:(g + 1) * gw, :], preferred_element_type=F32)
        pm_ref[:, cols] = (og * pscale_ref[:, cols]).astype(BF16)
    pst_ref[0] = zz_ref[tm:tm + h, :]

    xc = proj_ref[:, p:2 * p].astype(F32)
    bc = proj_ref[:, 2 * p:3 * p].astype(F32)
    cc = proj_ref[:, 3 * p:4 * p].astype(F32)
    v = cc * xc
    vh = hist_ref[:, 3 * p:4 * p].astype(F32) * hist_ref[:, p:2 * p].astype(F32)
    vv_ref[0:h, :] = jnp.where(first, 0.0, vh)
    vv_ref[h:h + tm, :] = v
    conv = convw_ref[0:1, :] * vv_ref[h - 2:h - 2 + tm, :]
    conv = conv + convw_ref[1:2, :] * vv_ref[h - 1:h - 1 + tm, :]
    conv = conv + convw_ref[2:3, :] * v
    cst_ref[0] = vv_ref[tm:tm + h, :]

    gp = proj_ref[:, 4 * p:4 * p + d].astype(F32)
    gc = proj_ref[:, 4 * p + d:4 * p + 2 * d].astype(F32)
    g1 = ada_ref[pl.ds(b, 1), 2 * d:3 * d]
    sh2 = ada_ref[pl.ds(b, 1), 3 * d:4 * d]
    sc2 = ada_ref[pl.ds(b, 1), 4 * d:5 * d]
    x1, u2 = _mix_tail(pm_ref[...], (bc * conv).astype(BF16), gp, gc, x_ref[...],
                       g1, sc2, sh2, wpu_ref, wcu_ref, wo_ref, lng_ref, lnb_ref, alpha)
    x1_ref[...] = x1
    u2_ref[...] = u2


def _mixer(proj, xp, ada, grp_bf, pscale, convw, wpu_bf, wcu_bf, wo_bf, lng, lnb,
           *, seq, prompt_row, alpha, tm=256):
    m, d = xp.shape
    p = d // 2
    nseq = m // seq
    h = HIST_ROWS
    body = functools.partial(_mixer_body, tm=tm, tiles_per_seq=seq // tm, d=d, alpha=alpha)
    return pl.pallas_call(
        body,
        out_shape=[jax.ShapeDtypeStruct((m, d), F32),
                   jax.ShapeDtypeStruct((m, d), BF16),
                   jax.ShapeDtypeStruct((nseq, h, p), F32),
                   jax.ShapeDtypeStruct((nseq, h, p), F32)],
        grid=(m // tm,),
        in_specs=[pl.BlockSpec((tm, proj.shape[1]), lambda i: (i, 0)),
                  pl.BlockSpec((h, proj.shape[1]),
                               lambda i: (jnp.maximum(i * (tm // h) - 1, 0), 0)),
                  pl.BlockSpec((tm, d), lambda i: (i, 0)),
                  pl.BlockSpec((8, N_ADA * d), lambda i: (prompt_row // 8, 0)),
                  _resident(grp_bf.shape),
                  _resident(pscale.shape),
                  _resident(convw.shape),
                  _resident(wpu_bf.shape),
                  _resident(wcu_bf.shape),
                  _resident(wo_bf.shape),
                  _resident(lng.shape),
                  _resident(lnb.shape)],
        out_specs=[pl.BlockSpec((tm, d), lambda i: (i, 0)),
                   pl.BlockSpec((tm, d), lambda i: (i, 0)),
                   pl.BlockSpec((1, h, p), lambda i: (i // (seq // tm), 0, 0)),
                   pl.BlockSpec((1, h, p), lambda i: (i // (seq // tm), 0, 0))],
        scratch_shapes=[pltpu.VMEM((tm + h, p), F32),
                        pltpu.VMEM((tm + h, p), F32),
                        pltpu.VMEM((tm, p), BF16)],
        compiler_params=_params(("arbitrary",)),
        name="mixer",
    )(proj, proj, xp, ada, grp_bf, pscale, convw, wpu_bf, wcu_bf, wo_bf, lng, lnb)


def _mixer_step_body(proj_ref, pool_ref, cstate_ref, x_ref, ada_ref, grp_ref, pscale_ref,
                     convw_ref, wpu_ref, wcu_ref, wo_ref, lng_ref, lnb_ref,
                     x1_ref, u2_ref, npool_ref, nconv_ref, pm_ref, *, d, alpha):
    p = d // 2
    gw = p // len(POOL_WINDOWS)
    z = proj_ref[:, 0:p]
    for g, w in enumerate(POOL_WINDOWS):
        cols = slice(g * gw, (g + 1) * gw)
        s = z[:, cols]
        for k in range(1, w):
            s = s + pool_ref[POOL_HIST - k, :, cols]
        cnt = float(min(w, PAST_LEN + 1))
        pooled = (s / cnt - z[:, cols]).astype(BF16)
        og = jnp.dot(pooled, grp_ref[g * gw:(g + 1) * gw, :], preferred_element_type=F32)
        pm_ref[:, cols] = (og * pscale_ref[:, cols]).astype(BF16)
    for r in range(POOL_HIST - 1):
        npool_ref[r] = pool_ref[r + 1]
    npool_ref[POOL_HIST - 1] = z

    xc = proj_ref[:, p:2 * p]
    bc = proj_ref[:, 2 * p:3 * p]
    cc = proj_ref[:, 3 * p:4 * p]
    v = cc * xc
    conv = convw_ref[0:1, :] * cstate_ref[0]
    conv = conv + convw_ref[1:2, :] * cstate_ref[1]
    conv = conv + convw_ref[2:3, :] * v
    nconv_ref[0] = cstate_ref[1]
    nconv_ref[1] = v

    gp = proj_ref[:, 4 * p:4 * p + d]
    gc = proj_ref[:, 4 * p + d:4 * p + 2 * d]
    g1 = ada_ref[:, 2 * d:3 * d]
    sh2 = ada_ref[:, 3 * d:4 * d]
    sc2 = ada_ref[:, 4 * d:5 * d]
    x1, u2 = _mix_tail(pm_ref[...], (bc * conv).astype(BF16), gp, gc, x_ref[...],
                       g1, sc2, sh2, wpu_ref, wcu_ref, wo_ref, lng_ref, lnb_ref, alpha)
    x1_ref[...] = x1
    u2_ref[...] = u2


def _mixer_step(proj_s, pool_t, conv_t, xs, ada, grp_bf, pscale, convw, wpu_bf, wcu_bf,
                wo_bf, lng, lnb, *, alpha):
    ms, d = xs.shape
    p = d // 2
    body = functools.partial(_mixer_step_body, d=d, alpha=alpha)
    return pl.pallas_call(
        body,
        out_shape=[jax.ShapeDtypeStruct((ms, d), F32),
                   jax.ShapeDtypeStruct((ms, d), BF16),
                   jax.ShapeDtypeStruct(pool_t.shape, F32),
                   jax.ShapeDtypeStruct(conv_t.shape, F32)],
        grid=(1,),
        in_specs=[_resident(proj_s.shape),
                  _resident(pool_t.shape),
                  _resident(conv_t.shape),
                  _resident((ms, d)),
                  _resident((ms, N_ADA * d)),
                  _resident(grp_bf.shape),
                  _resident(pscale.shape),
                  _resident(convw.shape),
                  _resident(wpu_bf.shape),
                  _resident(wcu_bf.shape),
                  _resident(wo_bf.shape),
                  _resident(lng.shape),
                  _resident(lnb.shape)],
        out_specs=[_resident((ms, d)),
                   _resident((ms, d)),
                   _resident(pool_t.shape),
                   _resident(conv_t.shape)],
        scratch_shapes=[pltpu.VMEM((ms, p), BF16)],
        compiler_params=_params(("arbitrary",)),
        name="mixer_step",
    )(proj_s, pool_t, conv_t, xs, ada, grp_bf, pscale, convw, wpu_bf, wcu_bf, wo_bf, lng, lnb)


def _mlp_body(u2_ref, w1_ref, b1_ref, w2_ref, x1_ref, ada_ref, b2_ref, lng_ref, lnb_ref,
              o_ref, acc_ref, *, nf, te, d, alpha, rows_per_cond, tm):
    m = pl.program_id(0)
    f = pl.program_id(1)

    @pl.when(f == 0)
    def _():
        acc_ref[...] = jnp.zeros_like(acc_ref)

    @pl.when(f < nf)
    def _():
        w1 = w1_ref[...].astype(BF16)
        w2 = w2_ref[...].astype(BF16)
        rc = min(tm, ROW_CHUNK)
        for r in range(0, tm, rc):
            rows = slice(r, r + rc)
            hmid = jnp.dot(u2_ref[rows, :], w1, preferred_element_type=F32) + b1_ref[...]
            hmid = jnp.square(jnp.maximum(hmid, 0.0))
            acc_ref[rows, :] += jnp.dot(hmid.astype(BF16), w2, preferred_element_type=F32)

    @pl.when(f >= nf)
    def _():
        e = f - nf
        rows = pl.ds(pl.multiple_of(e * te, te), te)
        if rows_per_cond == 1:
            g2 = ada_ref[:, 5 * d:6 * d]
        else:
            g2 = ada_ref[pl.ds((m * tm) // rows_per_cond, 1), 5 * d:6 * d]
        y = acc_ref[rows, :] + b2_ref[...]
        o_ref[...] = _layernorm(alpha * x1_ref[...] + g2 * y, lng_ref[...], lnb_ref[...])


def _mlp(u2, x1, ada, w1, b1, w2, b2, lng, lnb, *, alpha, rows_per_cond, cond_row,
         tm, te, tf=512):
    m, d = u2.shape
    dff = w1.shape[1]
    nf = dff // tf
    ne = tm // te
    body = functools.partial(_mlp_body, nf=nf, te=te, d=d, alpha=alpha,
                             rows_per_cond=rows_per_cond, tm=tm)
    fw = lambda f: jnp.minimum(f, nf - 1)
    ep = lambda mi, f: (mi * ne + jnp.maximum(f - nf, 0), 0)
    if rows_per_cond == 1:
        ada_spec = pl.BlockSpec((te, N_ADA * d), ep)
    else:
        ada_spec = pl.BlockSpec((8, N_ADA * d), lambda mi, f: (cond_row // 8, 0))
    return pl.pallas_call(
        body,
        out_shape=jax.ShapeDtypeStruct((m, d), F32),
        grid=(m // tm, nf + ne),
        in_specs=[pl.BlockSpec((tm, d), lambda mi, f: (mi, 0)),
                  pl.BlockSpec((d, tf), lambda mi, f: (0, fw(f))),
                  pl.BlockSpec((1, tf), lambda mi, f: (0, fw(f))),
                  pl.BlockSpec((tf, d), lambda mi, f: (fw(f), 0)),
                  pl.BlockSpec((te, d), ep),
                  ada_spec,
                  pl.BlockSpec((1, d), lambda mi, f: (0, 0)),
                  pl.BlockSpec((1, d), lambda mi, f: (0, 0)),
                  pl.BlockSpec((1, d), lambda mi, f: (0, 0))],
        out_specs=pl.BlockSpec((te, d), ep),
        scratch_shapes=[pltpu.VMEM((tm, d), F32)],
        compiler_params=_params(("arbitrary", "arbitrary")),
        name="mlp",
    )(u2, w1, b1.reshape(1, dff), w2, x1, ada, b2.reshape(1, d), lng, lnb)


def kernel(x_prompt, x_sample, state_pool, state_conv, c_prompt, c_sample, w_ada, b_ada, w_in,
           pool_grp_w, pool_scale, conv_w, w_pool_up, w_conv_up, w_o, ln1_g, ln1_b, w_ff1,
           b_ff1, w_ff2, b_ff2, ln2_g, ln2_b):
    depth = w_ada.shape[0]
    nb, seq, d = x_prompt.shape
    ms, dec_seq, _ = x_sample.shape
    assert dec_seq == 1
    p = d // 2
    alpha = (2 * depth) ** 0.25

    prompt_row = ms
    pad = (-(ms + nb)) % 8
    c_all = jnp.concatenate([c_sample, c_prompt, jnp.zeros((pad, d), F32)], axis=0)

    xp = x_prompt.reshape(nb * seq, d)
    xs = x_sample.reshape(ms, d)
    pool_p, conv_p, pool_s, conv_s = [], [], [], []
    for l in range(depth):
        ada = _ada(c_all, w_ada[l], b_ada[l])
        grp_bf, wpu_bf, wcu_bf, wo_bf = _cast_weights(
            [pool_grp_w[l].reshape(p, -1), w_pool_up[l], w_conv_up[l], w_o[l]])
        pscale = pool_scale[l].reshape(1, p)
        lng1, lnb1 = ln1_g[l].reshape(1, d), ln1_b[l].reshape(1, d)
        lng2, lnb2 = ln2_g[l].reshape(1, d), ln2_b[l].reshape(1, d)

        proj_p, proj_s = _inproj(xp, xs, ada, w_in[l], seq=seq, prompt_row=prompt_row)

        x1p, u2p, pst, cst = _mixer(proj_p, xp, ada, grp_bf, pscale, conv_w[l], wpu_bf, wcu_bf,
                                    wo_bf, lng1, lnb1, seq=seq, prompt_row=prompt_row,
                                    alpha=alpha)
        pool_t = jnp.transpose(state_pool[l], (1, 0, 2))
        conv_t = jnp.transpose(state_conv[l], (1, 0, 2))
        x1s, u2s, npool_t, nconv_t = _mixer_step(proj_s, pool_t, conv_t, xs, ada, grp_bf, pscale,
                                                 conv_w[l], wpu_bf, wcu_bf, wo_bf, lng1, lnb1,
                                                 alpha=alpha)

        xp = _mlp(u2p, x1p, ada, w_ff1[l], b_ff1[l], w_ff2[l], b_ff2[l], lng2, lnb2,
                  alpha=alpha, rows_per_cond=seq, cond_row=prompt_row, tm=1024, te=256)
        xs = _mlp(u2s, x1s, ada, w_ff1[l], b_ff1[l], w_ff2[l], b_ff2[l], lng2, lnb2,
                  alpha=alpha, rows_per_cond=1, cond_row=0, tm=ms, te=ms)

        pool_p.append(pst[:, HIST_ROWS - POOL_HIST:, :])
        conv_p.append(cst[:, HIST_ROWS - CONV_HIST:, :])
        pool_s.append(jnp.transpose(npool_t, (1, 0, 2)))
        conv_s.append(jnp.transpose(nconv_t, (1, 0, 2)))

    return (xp.reshape(nb, seq, d), xs.reshape(ms, dec_seq, d),
            jnp.stack(pool_p), jnp.stack(conv_p), jnp.stack(pool_s), jnp.stack(conv_s))
```

```python
import functools

import jax
import jax.numpy as jnp
from jax import lax
from jax.experimental import pallas as pl
from jax.experimental.pallas import tpu as pltpu

F32 = jnp.float32
BF16 = jnp.bfloat16

POOL_WINDOWS = (2, 4, 8, 16)
POOL_HIST = max(POOL_WINDOWS) - 1
CONV_K = 3
CONV_HIST = CONV_K - 1
N_ADA = 6
PAST_LEN = 16384
LN_EPS = 1e-5

VMEM_LIMIT_BYTES_V7X = 56 * 1024 * 1024
HIST_ROWS = 16
ROW_CHUNK = 256
MLP_TOKEN_TILE = 1024
MLP_FF_CHUNK = 1024
MLP_EPILOGUE_ROWS = 256


def _resident(shape):
    zeros = (0,) * len(shape)
    return pl.BlockSpec(shape, lambda *_: zeros, pipeline_mode=pl.Buffered(1))


def _params(semantics):
    return pltpu.CompilerParams(dimension_semantics=semantics,
                                vmem_limit_bytes=VMEM_LIMIT_BYTES_V7X)


def _layernorm(xf, g, b):
    mu = jnp.mean(xf, axis=-1, keepdims=True)
    var = jnp.mean(jnp.square(xf - mu), axis=-1, keepdims=True)
    return (xf - mu) * lax.rsqrt(var + LN_EPS) * g + b


def _ada_body(c_ref, w_ref, b_ref, o_ref):
    o_ref[...] = jnp.dot(c_ref[...].astype(BF16), w_ref[...].astype(BF16),
                         preferred_element_type=F32) + b_ref[...]


def _ada(c_all, w_ada, b_ada, tn=1024):
    m, d = c_all.shape
    n = w_ada.shape[1]
    return pl.pallas_call(
        _ada_body,
        out_shape=jax.ShapeDtypeStruct((m, n), F32),
        grid=(n // tn,),
        in_specs=[pl.BlockSpec((m, d), lambda j: (0, 0)),
                  pl.BlockSpec((d, tn), lambda j: (0, j)),
                  pl.BlockSpec((1, tn), lambda j: (0, j))],
        out_specs=pl.BlockSpec((m, tn), lambda j: (0, j)),
        compiler_params=_params(("arbitrary",)),
        name="ada",
    )(c_all, w_ada, b_ada.reshape(1, n))


def _cast_body(*refs):
    n = len(refs) // 2
    for src, dst in zip(refs[:n], refs[n:]):
        dst[...] = src[...].astype(BF16)


def _cast_weights(ws, steps=8):
    in_specs, out_specs, out_shape = [], [], []
    for w in ws:
        r, c = w.shape
        blk = pl.BlockSpec((r // steps, c), lambda s: (s, 0))
        in_specs.append(blk)
        out_specs.append(blk)
        out_shape.append(jax.ShapeDtypeStruct((r, c), BF16))
    return pl.pallas_call(
        _cast_body, out_shape=out_shape, grid=(steps,),
        in_specs=in_specs, out_specs=out_specs,
        compiler_params=_params(("arbitrary",)),
        name="cast_weights",
    )(*ws)


def _inproj_body(x_ref, adap_ref, w_ref, xs_ref, adas_ref, o_ref, os_ref, wb_ref,
                 *, tiles_per_seq, d, tm):
    i = pl.program_id(1)

    @pl.when(i == 0)
    def _():
        wb_ref[...] = w_ref[...].astype(BF16)
        us = xs_ref[...] * (1.0 + adas_ref[:, d:2 * d]) + adas_ref[:, 0:d]
        os_ref[...] = jnp.dot(us.astype(BF16), wb_ref[...], preferred_element_type=F32)

    b = i // tiles_per_seq
    sh1 = adap_ref[pl.ds(b, 1), 0:d]
    sc1 = adap_ref[pl.ds(b, 1), d:2 * d]
    for r in range(0, tm, ROW_CHUNK):
        rows = slice(r, r + ROW_CHUNK)
        u = x_ref[rows, :] * (1.0 + sc1) + sh1
        o_ref[rows, :] = jnp.dot(u.astype(BF16), wb_ref[...],
                                 preferred_element_type=F32).astype(BF16)


def _inproj(xp, xs, ada, w_in, *, seq, prompt_row, tm=1024, tn=1024):
    m, d = xp.shape
    ms = xs.shape[0]
    n = w_in.shape[1]
    body = functools.partial(_inproj_body, tiles_per_seq=seq // tm, d=d, tm=tm)
    return pl.pallas_call(
        body,
        out_shape=[jax.ShapeDtypeStruct((m, n), BF16),
                   jax.ShapeDtypeStruct((ms, n), F32)],
        grid=(n // tn, m // tm),
        in_specs=[pl.BlockSpec((tm, d), lambda j, i: (i, 0)),
                  pl.BlockSpec((8, 2 * d), lambda j, i: (prompt_row // 8, 0)),
                  pl.BlockSpec((d, tn), lambda j, i: (0, j)),
                  pl.BlockSpec((ms, d), lambda j, i: (0, 0)),
                  pl.BlockSpec((ms, 2 * d), lambda j, i: (0, 0))],
        out_specs=[pl.BlockSpec((tm, tn), lambda j, i: (i, j)),
                   pl.BlockSpec((ms, tn), lambda j, i: (0, j))],
        scratch_shapes=[pltpu.VMEM((d, tn), BF16)],
        compiler_params=_params(("arbitrary", "arbitrary")),
        name="inproj",
    )(xp, ada, w_in, xs, ada)


def _mix_tail(pooled_bf, conv_bf, gp, gc, x, g1, sc2, sh2, wpu_ref, wcu_ref, wo_ref,
              lng_ref, lnb_ref, alpha):
    y_p = jnp.dot(pooled_bf, wpu_ref[...], preferred_element_type=F32)
    y_c = jnp.dot(conv_bf, wcu_ref[...], preferred_element_type=F32)
    mixed = jax.nn.sigmoid(gp) * y_p + jax.nn.sigmoid(gc) * y_c
    mo = jnp.dot(mixed.astype(BF16), wo_ref[...], preferred_element_type=F32)
    x1 = _layernorm(alpha * x + g1 * mo, lng_ref[...], lnb_ref[...])
    u2 = x1 * (1.0 + sc2) + sh2
    return x1, u2.astype(BF16)


def _mixer_body(proj_ref, hist_ref, x_ref, ada_ref, grp_ref, pscale_ref, convw_ref,
                wpu_ref, wcu_ref, wo_ref, lng_ref, lnb_ref, w1_ref, w2_ref,
                x1_ref, u2_ref, pst_ref, cst_ref, w1b_ref, w2b_ref, zz_ref, vv_ref, pm_ref,
                *, tm, tiles_per_seq, d, alpha):
    i = pl.program_id(0)

    tf = w1b_ref.shape[2]
    for fi in range(w1b_ref.shape[0]):
        w1b_ref[fi] = w1_ref[:, fi * tf:(fi + 1) * tf].astype(BF16)
    w2b_ref[...] = w2_ref[...].astype(BF16)

    tis = i % tiles_per_seq
    b = i // tiles_per_seq
    first = tis == 0
    p = d // 2
    gw = p // len(POOL_WINDOWS)
    h = HIST_ROWS

    z = proj_ref[:, 0:p].astype(F32)
    zz_ref[0:h, :] = jnp.where(first, 0.0, hist_ref[:, 0:p].astype(F32))
    zz_ref[h:h + tm, :] = z
    pos = tis * tm + lax.broadcasted_iota(jnp.int32, (tm, 1), 0)
    for g, w in enumerate(POOL_WINDOWS):
        cols = slice(g * gw, (g + 1) * gw)
        s = zz_ref[h:h + tm, cols]
        for k in range(1, w):
            s = s + zz_ref[h - k:h - k + tm, cols]
        cnt = jnp.minimum(w, pos + 1).astype(F32)
        pooled = (s / cnt - z[:, cols]).astype(BF16)
        og = jnp.dot(pooled, grp_ref[g * gw:(g + 1) * gw, :], preferred_element_type=F32)
        pm_ref[:, cols] = (og * pscale_ref[:, cols]).astype(BF16)
    pst_ref[0] = zz_ref[tm:tm + h, :]

    xc = proj_ref[:, p:2 * p].astype(F32)
    bc = proj_ref[:, 2 * p:3 * p].astype(F32)
    cc = proj_ref[:, 3 * p:4 * p].astype(F32)
    v = cc * xc
    vh = hist_ref[:, 3 * p:4 * p].astype(F32) * hist_ref[:, p:2 * p].astype(F32)
    vv_ref[0:h, :] = jnp.where(first, 0.0, vh)
    vv_ref[h:h + tm, :] = v
    conv = convw_ref[0:1, :] * vv_ref[h - 2:h - 2 + tm, :]
    conv = conv + convw_ref[1:2, :] * vv_ref[h - 1:h - 1 + tm, :]
    conv = conv + convw_ref[2:3, :] * v
    cst_ref[0] = vv_ref[tm:tm + h, :]

    gp = proj_ref[:, 4 * p:4 * p + d].astype(F32)
    gc = proj_ref[:, 4 * p + d:4 * p + 2 * d].astype(F32)
    g1 = ada_ref[pl.ds(b, 1), 2 * d:3 * d]
    sh2 = ada_ref[pl.ds(b, 1), 3 * d:4 * d]
    sc2 = ada_ref[pl.ds(b, 1), 4 * d:5 * d]
    x1, u2 = _mix_tail(pm_ref[...], (bc * conv).astype(BF16), gp, gc, x_ref[...],
                       g1, sc2, sh2, wpu_ref, wcu_ref, wo_ref, lng_ref, lnb_ref, alpha)
    x1_ref[...] = x1
    u2_ref[...] = u2


def _mixer(proj, xp, ada, grp_bf, pscale, convw, wpu_bf, wcu_bf, wo_bf, lng, lnb, w1, w2,
           *, seq, prompt_row, alpha, tf, tm=256):
    m, d = xp.shape
    p = d // 2
    nseq = m // seq
    h = HIST_ROWS
    steps = m // tm
    dff = w1.shape[1]
    body = functools.partial(_mixer_body, tm=tm, tiles_per_seq=seq // tm, d=d, alpha=alpha)
    return pl.pallas_call(
        body,
        out_shape=[jax.ShapeDtypeStruct((m, d), F32),
                   jax.ShapeDtypeStruct((m, d), BF16),
                   jax.ShapeDtypeStruct((nseq, h, p), F32),
                   jax.ShapeDtypeStruct((nseq, h, p), F32),
                   jax.ShapeDtypeStruct((dff // tf, d, tf), BF16),
                   jax.ShapeDtypeStruct((dff, d), BF16)],
        grid=(steps,),
        in_specs=[pl.BlockSpec((tm, proj.shape[1]), lambda i: (i, 0)),
                  pl.BlockSpec((h, proj.shape[1]),
                               lambda i: (jnp.maximum(i * (tm // h) - 1, 0), 0)),
                  pl.BlockSpec((tm, d), lambda i: (i, 0)),
                  pl.BlockSpec((8, N_ADA * d), lambda i: (prompt_row // 8, 0)),
                  _resident(grp_bf.shape),
                  _resident(pscale.shape),
                  _resident(convw.shape),
                  _resident(wpu_bf.shape),
                  _resident(wcu_bf.shape),
                  _resident(wo_bf.shape),
                  _resident(lng.shape),
                  _resident(lnb.shape),
                  pl.BlockSpec((d // steps, dff), lambda i: (i, 0)),
                  pl.BlockSpec((dff // steps, d), lambda i: (i, 0))],
        out_specs=[pl.BlockSpec((tm, d), lambda i: (i, 0)),
                   pl.BlockSpec((tm, d), lambda i: (i, 0)),
                   pl.BlockSpec((1, h, p), lambda i: (i // (seq // tm), 0, 0)),
                   pl.BlockSpec((1, h, p), lambda i: (i // (seq // tm), 0, 0)),
                   pl.BlockSpec((dff // tf, d // steps, tf), lambda i: (0, i, 0)),
                   pl.BlockSpec((dff // steps, d), lambda i: (i, 0))],
        scratch_shapes=[pltpu.VMEM((tm + h, p), F32),
                        pltpu.VMEM((tm + h, p), F32),
                        pltpu.VMEM((tm, p), BF16)],
        compiler_params=_params(("arbitrary",)),
        name="mixer",
    )(proj, proj, xp, ada, grp_bf, pscale, convw, wpu_bf, wcu_bf, wo_bf, lng, lnb, w1, w2)


def _mixer_step_body(proj_ref, pool_ref, cstate_ref, x_ref, ada_ref, grp_ref, pscale_ref,
                     convw_ref, wpu_ref, wcu_ref, wo_ref, lng_ref, lnb_ref,
                     x1_ref, u2_ref, npool_ref, nconv_ref, pm_ref, *, d, alpha):
    p = d // 2
    gw = p // len(POOL_WINDOWS)
    z = proj_ref[:, 0:p]
    for g, w in enumerate(POOL_WINDOWS):
        cols = slice(g * gw, (g + 1) * gw)
        s = z[:, cols]
        for k in range(1, w):
            s = s + pool_ref[POOL_HIST - k, :, cols]
        cnt = float(min(w, PAST_LEN + 1))
        pooled = (s / cnt - z[:, cols]).astype(BF16)
        og = jnp.dot(pooled, grp_ref[g * gw:(g + 1) * gw, :], preferred_element_type=F32)
        pm_ref[:, cols] = (og * pscale_ref[:, cols]).astype(BF16)
    for r in range(POOL_HIST - 1):
        npool_ref[r] = pool_ref[r + 1]
    npool_ref[POOL_HIST - 1] = z

    xc = proj_ref[:, p:2 * p]
    bc = proj_ref[:, 2 * p:3 * p]
    cc = proj_ref[:, 3 * p:4 * p]
    v = cc * xc
    conv = convw_ref[0:1, :] * cstate_ref[0]
    conv = conv + convw_ref[1:2, :] * cstate_ref[1]
    conv = conv + convw_ref[2:3, :] * v
    nconv_ref[0] = cstate_ref[1]
    nconv_ref[1] = v

    gp = proj_ref[:, 4 * p:4 * p + d]
    gc = proj_ref[:, 4 * p + d:4 * p + 2 * d]
    g1 = ada_ref[:, 2 * d:3 * d]
    sh2 = ada_ref[:, 3 * d:4 * d]
    sc2 = ada_ref[:, 4 * d:5 * d]
    x1, u2 = _mix_tail(pm_ref[...], (bc * conv).astype(BF16), gp, gc, x_ref[...],
                       g1, sc2, sh2, wpu_ref, wcu_ref, wo_ref, lng_ref, lnb_ref, alpha)
    x1_ref[...] = x1
    u2_ref[...] = u2


def _mixer_step(proj_s, pool_t, conv_t, xs, ada, grp_bf, pscale, convw, wpu_bf, wcu_bf,
                wo_bf, lng, lnb, *, alpha):
    ms, d = xs.shape
    p = d // 2
    body = functools.partial(_mixer_step_body, d=d, alpha=alpha)
    return pl.pallas_call(
        body,
        out_shape=[jax.ShapeDtypeStruct((ms, d), F32),
                   jax.ShapeDtypeStruct((ms, d), BF16),
                   jax.ShapeDtypeStruct(pool_t.shape, F32),
                   jax.ShapeDtypeStruct(conv_t.shape, F32)],
        grid=(1,),
        in_specs=[_resident(proj_s.shape),
                  _resident(pool_t.shape),
                  _resident(conv_t.shape),
                  _resident((ms, d)),
                  _resident((ms, N_ADA * d)),
                  _resident(grp_bf.shape),
                  _resident(pscale.shape),
                  _resident(convw.shape),
                  _resident(wpu_bf.shape),
                  _resident(wcu_bf.shape),
                  _resident(wo_bf.shape),
                  _resident(lng.shape),
                  _resident(lnb.shape)],
        out_specs=[_resident((ms, d)),
                   _resident((ms, d)),
                   _resident(pool_t.shape),
                   _resident(conv_t.shape)],
        scratch_shapes=[pltpu.VMEM((ms, p), BF16)],
        compiler_params=_params(("arbitrary",)),
        name="mixer_step",
    )(proj_s, pool_t, conv_t, xs, ada, grp_bf, pscale, convw, wpu_bf, wcu_bf, wo_bf, lng, lnb)


def _mlp_body(u2_ref, w1_ref, b1_ref, w2_ref, x1_ref, g2_ref, b2_ref, lng_ref, lnb_ref,
              u2s_ref, x1s_ref, g2s_ref, o_ref, os_ref, acc_ref, accs_ref,
              *, nf, te, tm, alpha, rows_per_cond):
    m = pl.program_id(0)
    f = pl.program_id(1)

    def chunk(u2):
        hmid = jnp.dot(u2, w1_ref[0], preferred_element_type=F32) + b1_ref[...]
        hmid = jnp.square(jnp.maximum(hmid, 0.0))
        return jnp.dot(hmid.astype(BF16), w2_ref[...], preferred_element_type=F32)

    def accumulate(first):
        for r in range(0, tm, ROW_CHUNK):
            rows = slice(r, r + ROW_CHUNK)
            part = chunk(u2_ref[rows, :])
            if first:
                acc_ref[rows, :] = part
            else:
                acc_ref[rows, :] += part

        @pl.when(m == 0)
        def _():
            part = chunk(u2s_ref[...])
            if first:
                accs_ref[...] = part
            else:
                accs_ref[...] += part

    pl.when(f == 0)(functools.partial(accumulate, True))
    pl.when(jnp.logical_and(f > 0, f < nf))(functools.partial(accumulate, False))

    def finish(x1, g2, acc):
        return _layernorm(alpha * x1 + g2 * (acc + b2_ref[...]), lng_ref[...], lnb_ref[...])

    @pl.when(f >= nf)
    def _():
        rows = pl.ds(pl.multiple_of((f - nf) * te, te), te)
        g2 = g2_ref[pl.ds((m * tm) // rows_per_cond, 1), :]
        o_ref[...] = finish(x1_ref[...], g2, acc_ref[rows, :])

    @pl.when(jnp.logical_and(m == 0, f == nf))
    def _():
        os_ref[...] = finish(x1s_ref[...], g2s_ref[...], accs_ref[...])


def _mlp(u2, x1, u2s, x1s, ada, w1b, b1, w2b, b2, lng, lnb, *, alpha, rows_per_cond,
         cond_row, tm, te):
    m, d = u2.shape
    ms = u2s.shape[0]
    nf, _, tf = w1b.shape
    ne = tm // te
    g2_col = N_ADA - 1
    body = functools.partial(_mlp_body, nf=nf, te=te, tm=tm, alpha=alpha,
                             rows_per_cond=rows_per_cond)
    fw = lambda f: jnp.minimum(f, nf - 1)
    ep = lambda mi, f: (mi * ne + jnp.maximum(f - nf, 0), 0)
    return pl.pallas_call(
        body,
        out_shape=[jax.ShapeDtypeStruct((m, d), F32),
                   jax.ShapeDtypeStruct((ms, d), F32)],
        grid=(m // tm, nf + ne),
        in_specs=[pl.BlockSpec((tm, d), lambda mi, f: (mi, 0)),
                  pl.BlockSpec((1, d, tf), lambda mi, f: (fw(f), 0, 0)),
                  pl.BlockSpec((1, tf), lambda mi, f: (0, fw(f))),
                  pl.BlockSpec((tf, d), lambda mi, f: (fw(f), 0)),
                  pl.BlockSpec((te, d), ep),
                  pl.BlockSpec((8, d), lambda mi, f: (cond_row // 8, g2_col)),
                  _resident((1, d)),
                  _resident((1, d)),
                  _resident((1, d)),
                  _resident((ms, d)),
                  _resident((ms, d)),
                  pl.BlockSpec((ms, d), lambda mi, f: (0, g2_col),
                               pipeline_mode=pl.Buffered(1))],
        out_specs=[pl.BlockSpec((te, d), ep),
                   pl.BlockSpec((ms, d), lambda mi, f: (0, 0))],
        scratch_shapes=[pltpu.VMEM((tm, d), F32),
                        pltpu.VMEM((ms, d), F32)],
        compiler_params=_params(("arbitrary", "arbitrary")),
        name="mlp",
    )(u2, w1b, b1.reshape(1, -1), w2b, x1, ada, b2.reshape(1, d), lng, lnb, u2s, x1s, ada)


def kernel(x_prompt, x_sample, state_pool, state_conv, c_prompt, c_sample, w_ada, b_ada, w_in,
           pool_grp_w, pool_scale, conv_w, w_pool_up, w_conv_up, w_o, ln1_g, ln1_b, w_ff1,
           b_ff1, w_ff2, b_ff2, ln2_g, ln2_b):
    depth = w_ada.shape[0]
    nb, seq, d = x_prompt.shape
    ms, dec_seq, _ = x_sample.shape
    assert dec_seq == 1
    p = d // 2
    alpha = (2 * depth) ** 0.25

    prompt_row = ms
    pad = (-(ms + nb)) % 8
    c_all = jnp.concatenate([c_sample, c_prompt, jnp.zeros((pad, d), F32)], axis=0)

    xp = x_prompt.reshape(nb * seq, d)
    xs = x_sample.reshape(ms, d)
    pool_p, conv_p, pool_s, conv_s = [], [], [], []
    for l in range(depth):
        ada = _ada(c_all, w_ada[l], b_ada[l])
        grp_bf, wpu_bf, wcu_bf, wo_bf = _cast_weights(
            [pool_grp_w[l].reshape(p, -1), w_pool_up[l], w_conv_up[l], w_o[l]])
        pscale = pool_scale[l].reshape(1, p)
        lng1, lnb1 = ln1_g[l].reshape(1, d), ln1_b[l].reshape(1, d)
        lng2, lnb2 = ln2_g[l].reshape(1, d), ln2_b[l].reshape(1, d)

        proj_p, proj_s = _inproj(xp, xs, ada, w_in[l], seq=seq, prompt_row=prompt_row)

        x1p, u2p, pst, cst, w1b, w2b = _mixer(
            proj_p, xp, ada, grp_bf, pscale, conv_w[l], wpu_bf, wcu_bf, wo_bf, lng1, lnb1,
            w_ff1[l], w_ff2[l], seq=seq, prompt_row=prompt_row, alpha=alpha, tf=MLP_FF_CHUNK)
        pool_t = jnp.transpose(state_pool[l], (1, 0, 2))
        conv_t = jnp.transpose(state_conv[l], (1, 0, 2))
        x1s, u2s, npool_t, nconv_t = _mixer_step(proj_s, pool_t, conv_t, xs, ada, grp_bf, pscale,
                                                 conv_w[l], wpu_bf, wcu_bf, wo_bf, lng1, lnb1,
                                                 alpha=alpha)

        xp, xs = _mlp(u2p, x1p, u2s, x1s, ada, w1b, b_ff1[l], w2b, b_ff2[l], lng2, lnb2,
                      alpha=alpha, rows_per_cond=seq, cond_row=prompt_row,
                      tm=MLP_TOKEN_TILE, te=MLP_EPILOGUE_ROWS)

        pool_p.append(pst[:, HIST_ROWS - POOL_HIST:, :])
        conv_p.append(cst[:, HIST_ROWS - CONV_HIST:, :])
        pool_s.append(jnp.transpose(npool_t, (1, 0, 2)))
        conv_s.append(jnp.transpose(nconv_t, (1, 0, 2)))

    return (xp.reshape(nb, seq, d), xs.reshape(ms, dec_seq, d),
            jnp.stack(pool_p), jnp.stack(conv_p), jnp.stack(pool_s), jnp.stack(conv_s))
```

```python
import functools

import jax
import jax.numpy as jnp
from jax import lax
from jax.experimental import pallas as pl
from jax.experimental.pallas import tpu as pltpu

F32 = jnp.float32
BF16 = jnp.bfloat16
U32 = jnp.uint32

POOL_WINDOWS = (2, 4, 8, 16)
POOL_HIST = max(POOL_WINDOWS) - 1
CONV_K = 3
CONV_HIST = CONV_K - 1
N_ADA = 6
PAST_LEN = 16384
LN_EPS = 1e-5

VMEM_LIMIT_BYTES_V7X = 56 * 1024 * 1024
HIST_ROWS = 16
ROW_CHUNK = 256
MIXER_SLAB_ROWS = 128
MLP_TOKEN_TILE = 1024
MLP_FF_CHUNK = 1024
MLP_EPILOGUE_ROWS = 256


def _resident(shape):
    zeros = (0,) * len(shape)
    return pl.BlockSpec(shape, lambda *_: zeros, pipeline_mode=pl.Buffered(1))


def _params(semantics):
    return pltpu.CompilerParams(dimension_semantics=semantics,
                                vmem_limit_bytes=VMEM_LIMIT_BYTES_V7X)


def _layernorm(xf, g, b):
    mu = jnp.mean(xf, axis=-1, keepdims=True)
    var = jnp.mean(jnp.square(xf - mu), axis=-1, keepdims=True)
    return (xf - mu) * lax.rsqrt(var + LN_EPS) * g + b


def _ada_body(c_ref, w_ref, b_ref, o_ref):
    o_ref[...] = jnp.dot(c_ref[...].astype(BF16), w_ref[...].astype(BF16),
                         preferred_element_type=F32) + b_ref[...]


def _ada(c_all, w_ada, b_ada, tn=1024):
    m, d = c_all.shape
    n = w_ada.shape[1]
    return pl.pallas_call(
        _ada_body,
        out_shape=jax.ShapeDtypeStruct((m, n), F32),
        grid=(n // tn,),
        in_specs=[pl.BlockSpec((m, d), lambda j: (0, 0)),
                  pl.BlockSpec((d, tn), lambda j: (0, j)),
                  pl.BlockSpec((1, tn), lambda j: (0, j))],
        out_specs=pl.BlockSpec((m, tn), lambda j: (0, j)),
        compiler_params=_params(("arbitrary",)),
        name="ada",
    )(c_all, w_ada, b_ada.reshape(1, n))


def _pack_rows(x_bf16):
    return pltpu.bitcast(x_bf16, U32)


def _unpack_rows(x_u32):
    return pltpu.bitcast(x_u32, BF16)


def _cast_body(*refs):
    n = len(refs) // 2
    for src, dst in zip(refs[:n], refs[n:]):
        dst[...] = _pack_rows(src[...].astype(BF16))


def _cast_weights(ws, steps=8):
    in_specs, out_specs, out_shape = [], [], []
    for w in ws:
        r, c = w.shape
        in_specs.append(pl.BlockSpec((r // steps, c), lambda s: (s, 0)))
        out_specs.append(pl.BlockSpec((r // steps // 2, c), lambda s: (s, 0)))
        out_shape.append(jax.ShapeDtypeStruct((r // 2, c), U32))
    return pl.pallas_call(
        _cast_body, out_shape=out_shape, grid=(steps,),
        in_specs=in_specs, out_specs=out_specs,
        compiler_params=_params(("arbitrary",)),
        name="cast_weights",
    )(*ws)


def _inproj_body(x_ref, adap_ref, w_ref, xs_ref, adas_ref, o_ref, os_ref, wb_ref,
                 *, tiles_per_seq, d, tm):
    i = pl.program_id(1)

    @pl.when(i == 0)
    def _():
        wb_ref[...] = w_ref[...].astype(BF16)
        us = xs_ref[...] * (1.0 + adas_ref[:, d:2 * d]) + adas_ref[:, 0:d]
        os_ref[...] = jnp.dot(us.astype(BF16), wb_ref[...], preferred_element_type=F32)

    b = i // tiles_per_seq
    sh1 = adap_ref[pl.ds(b, 1), 0:d]
    sc1 = adap_ref[pl.ds(b, 1), d:2 * d]
    for r in range(0, tm, ROW_CHUNK):
        rows = slice(r, r + ROW_CHUNK)
        u = x_ref[rows, :] * (1.0 + sc1) + sh1
        o_ref[rows, :] = jnp.dot(u.astype(BF16), wb_ref[...],
                                 preferred_element_type=F32).astype(BF16)


def _inproj(xp, xs, ada, w_in, *, seq, prompt_row, tm=1024, tn=1024):
    m, d = xp.shape
    ms = xs.shape[0]
    n = w_in.shape[1]
    body = functools.partial(_inproj_body, tiles_per_seq=seq // tm, d=d, tm=tm)
    return pl.pallas_call(
        body,
        out_shape=[jax.ShapeDtypeStruct((m, n), BF16),
                   jax.ShapeDtypeStruct((ms, n), F32)],
        grid=(n // tn, m // tm),
        in_specs=[pl.BlockSpec((tm, d), lambda j, i: (i, 0)),
                  pl.BlockSpec((8, 2 * d), lambda j, i: (prompt_row // 8, 0)),
                  pl.BlockSpec((d, tn), lambda j, i: (0, j)),
                  pl.BlockSpec((ms, d), lambda j, i: (0, 0)),
                  pl.BlockSpec((ms, 2 * d), lambda j, i: (0, 0))],
        out_specs=[pl.BlockSpec((tm, tn), lambda j, i: (i, j)),
                   pl.BlockSpec((ms, tn), lambda j, i: (0, j))],
        scratch_shapes=[pltpu.VMEM((d, tn), BF16)],
        compiler_params=_params(("arbitrary", "arbitrary")),
        name="inproj",
    )(xp, ada, w_in, xs, ada)


def _sigmoid(x):
    return 0.5 * jnp.tanh(0.5 * x) + 0.5


def _mix_tail(slabs, g1, sc2, sh2, wpu_ref, wcu_ref, wo_ref, lng_ref, lnb_ref, alpha):
    wpu = _unpack_rows(wpu_ref[...])
    wcu = _unpack_rows(wcu_ref[...])
    wo = _unpack_rows(wo_ref[...])

    def up(s):
        pooled_bf, conv_bf = s[0]()
        return (jnp.dot(pooled_bf, wpu, preferred_element_type=F32),
                jnp.dot(conv_bf, wcu, preferred_element_type=F32))

    def gate(s, y):
        gp, gc = s[1]()
        return (_sigmoid(gp) * y[0] + _sigmoid(gc) * y[1]).astype(BF16)

    def finish(s, m_out):
        x1 = _layernorm(alpha * s[2]() + g1 * m_out, lng_ref[...], lnb_ref[...])
        s[3](x1, (x1 * (1.0 + sc2) + sh2).astype(BF16))

    y = [None] * len(slabs)
    mo = [None] * len(slabs)
    y[0] = up(slabs[0])
    for k, s in enumerate(slabs):
        if k + 1 < len(slabs):
            y[k + 1] = up(slabs[k + 1])
        mo[k] = jnp.dot(gate(s, y[k]), wo, preferred_element_type=F32)
        if k > 0:
            finish(slabs[k - 1], mo[k - 1])
    finish(slabs[-1], mo[-1])


def _mixer_body(proj_ref, hist_ref, x_ref, ada_ref, grp_ref, pscale_ref, convw_ref,
                wpu_ref, wcu_ref, wo_ref, lng_ref, lnb_ref, w1_ref, w2_ref,
                x1_ref, u2_ref, pst_ref, cst_ref, w1b_ref, w2b_ref, zz_ref, vv_ref, pm_ref,
                *, tm, tiles_per_seq, d, alpha):
    i = pl.program_id(0)

    tf = w1b_ref.shape[2]
    for fi in range(w1b_ref.shape[0]):
        w1b_ref[fi] = _pack_rows(w1_ref[:, fi * tf:(fi + 1) * tf].astype(BF16))
    w2b_ref[...] = _pack_rows(w2_ref[...].astype(BF16))

    tis = i % tiles_per_seq
    b = i // tiles_per_seq
    first = tis == 0
    p = d // 2
    gw = p // len(POOL_WINDOWS)
    h = HIST_ROWS

    zz_ref[0:h, :] = jnp.where(first, 0.0, hist_ref[:, 0:p].astype(F32))
    zz_ref[h:h + tm, :] = proj_ref[:, 0:p].astype(F32)
    vh = hist_ref[:, 3 * p:4 * p].astype(F32) * hist_ref[:, p:2 * p].astype(F32)
    vv_ref[0:h, :] = jnp.where(first, 0.0, vh)
    vv_ref[h:h + tm, :] = proj_ref[:, 3 * p:4 * p].astype(F32) * proj_ref[:, p:2 * p].astype(F32)
    pst_ref[0] = zz_ref[tm:tm + h, :]
    cst_ref[0] = vv_ref[tm:tm + h, :]

    g1 = ada_ref[pl.ds(b, 1), 2 * d:3 * d]
    sh2 = ada_ref[pl.ds(b, 1), 3 * d:4 * d]
    sc2 = ada_ref[pl.ds(b, 1), 4 * d:5 * d]

    n = MIXER_SLAB_ROWS

    def front(r0):
        rows = slice(r0, r0 + n)
        pos = tis * tm + r0 + lax.broadcasted_iota(jnp.int32, (n, 1), 0)
        for g, w in enumerate(POOL_WINDOWS):
            cols = slice(g * gw, (g + 1) * gw)
            zt = zz_ref[h + r0:h + r0 + n, cols]
            s = zt
            for k in range(1, w):
                s = s + zz_ref[h + r0 - k:h + r0 - k + n, cols]
            inv_cnt = 1.0 / jnp.minimum(w, pos + 1).astype(F32)
            pooled = (s * inv_cnt - zt).astype(BF16)
            og = jnp.dot(pooled, _unpack_rows(grp_ref[g * gw // 2:(g + 1) * gw // 2, :]),
                         preferred_element_type=F32)
            pm_ref[rows, cols] = (og * pscale_ref[:, cols]).astype(BF16)
        conv = convw_ref[0:1, :] * vv_ref[h + r0 - 2:h + r0 - 2 + n, :]
        conv = conv + convw_ref[1:2, :] * vv_ref[h + r0 - 1:h + r0 - 1 + n, :]
        conv = conv + convw_ref[2:3, :] * vv_ref[h + r0:h + r0 + n, :]
        bconv = (proj_ref[rows, 2 * p:3 * p].astype(F32) * conv).astype(BF16)
        return pm_ref[rows, :], bconv

    def gates(r0):
        rows = slice(r0, r0 + n)
        return (proj_ref[rows, 4 * p:4 * p + d].astype(F32),
                proj_ref[rows, 4 * p + d:4 * p + 2 * d].astype(F32))

    def store(r0, x1, u2):
        x1_ref[r0:r0 + n, :] = x1
        u2_ref[r0:r0 + n, :] = u2

    slabs = [(functools.partial(front, r0), functools.partial(gates, r0),
              functools.partial(lambda r: x_ref[r:r + n, :], r0), functools.partial(store, r0))
             for r0 in range(0, tm, n)]
    _mix_tail(slabs, g1, sc2, sh2, wpu_ref, wcu_ref, wo_ref, lng_ref, lnb_ref, alpha)


def _mixer(proj, xp, ada, grp_bf, pscale, convw, wpu_bf, wcu_bf, wo_bf, lng, lnb, w1, w2,
           *, seq, prompt_row, alpha, tf, tm=256):
    m, d = xp.shape
    p = d // 2
    nseq = m // seq
    h = HIST_ROWS
    steps = m // tm
    dff = w1.shape[1]
    body = functools.partial(_mixer_body, tm=tm, tiles_per_seq=seq // tm, d=d, alpha=alpha)
    return pl.pallas_call(
        body,
        out_shape=[jax.ShapeDtypeStruct((m, d), F32),
                   jax.ShapeDtypeStruct((m, d), BF16),
                   jax.ShapeDtypeStruct((nseq, h, p), F32),
                   jax.ShapeDtypeStruct((nseq, h, p), F32),
                   jax.ShapeDtypeStruct((dff // tf, d // 2, tf), U32),
                   jax.ShapeDtypeStruct((dff // 2, d), U32)],
        grid=(steps,),
        in_specs=[pl.BlockSpec((tm, proj.shape[1]), lambda i: (i, 0)),
                  pl.BlockSpec((h, proj.shape[1]),
                               lambda i: (jnp.maximum(i * (tm // h) - 1, 0), 0)),
                  pl.BlockSpec((tm, d), lambda i: (i, 0)),
                  pl.BlockSpec((8, N_ADA * d), lambda i: (prompt_row // 8, 0)),
                  _resident(grp_bf.shape),
                  _resident(pscale.shape),
                  _resident(convw.shape),
                  _resident(wpu_bf.shape),
                  _resident(wcu_bf.shape),
                  _resident(wo_bf.shape),
                  _resident(lng.shape),
                  _resident(lnb.shape),
                  pl.BlockSpec((d // steps, dff), lambda i: (i, 0)),
                  pl.BlockSpec((dff // steps, d), lambda i: (i, 0))],
        out_specs=[pl.BlockSpec((tm, d), lambda i: (i, 0)),
                   pl.BlockSpec((tm, d), lambda i: (i, 0)),
                   pl.BlockSpec((1, h, p), lambda i: (i // (seq // tm), 0, 0)),
                   pl.BlockSpec((1, h, p), lambda i: (i // (seq // tm), 0, 0)),
                   pl.BlockSpec((dff // tf, d // steps // 2, tf), lambda i: (0, i, 0)),
                   pl.BlockSpec((dff // steps // 2, d), lambda i: (i, 0))],
        scratch_shapes=[pltpu.VMEM((tm + h, p), F32),
                        pltpu.VMEM((tm + h, p), F32),
                        pltpu.VMEM((tm, p), BF16)],
        compiler_params=_params(("arbitrary",)),
        name="mixer",
    )(proj, proj, xp, ada, grp_bf, pscale, convw, wpu_bf, wcu_bf, wo_bf, lng, lnb, w1, w2)


def _mixer_step_body(proj_ref, pool_ref, cstate_ref, x_ref, ada_ref, grp_ref, pscale_ref,
                     convw_ref, wpu_ref, wcu_ref, wo_ref, lng_ref, lnb_ref,
                     x1_ref, u2_ref, npool_ref, nconv_ref, pm_ref, *, d, alpha):
    p = d // 2
    gw = p // len(POOL_WINDOWS)
    z = proj_ref[:, 0:p]
    for g, w in enumerate(POOL_WINDOWS):
        cols = slice(g * gw, (g + 1) * gw)
        s = z[:, cols]
        for k in range(1, w):
            s = s + pool_ref[POOL_HIST - k, :, cols]
        inv_cnt = 1.0 / min(w, PAST_LEN + 1)
        pooled = (s * inv_cnt - z[:, cols]).astype(BF16)
        og = jnp.dot(pooled, _unpack_rows(grp_ref[g * gw // 2:(g + 1) * gw // 2, :]),
                         preferred_element_type=F32)
        pm_ref[:, cols] = (og * pscale_ref[:, cols]).astype(BF16)
    for r in range(POOL_HIST - 1):
        npool_ref[r] = pool_ref[r + 1]
    npool_ref[POOL_HIST - 1] = z

    xc = proj_ref[:, p:2 * p]
    bc = proj_ref[:, 2 * p:3 * p]
    cc = proj_ref[:, 3 * p:4 * p]
    v = cc * xc
    conv = convw_ref[0:1, :] * cstate_ref[0]
    conv = conv + convw_ref[1:2, :] * cstate_ref[1]
    conv = conv + convw_ref[2:3, :] * v
    nconv_ref[0] = cstate_ref[1]
    nconv_ref[1] = v

    gp = proj_ref[:, 4 * p:4 * p + d]
    gc = proj_ref[:, 4 * p + d:4 * p + 2 * d]
    g1 = ada_ref[:, 2 * d:3 * d]
    sh2 = ada_ref[:, 3 * d:4 * d]
    sc2 = ada_ref[:, 4 * d:5 * d]

    def store(x1, u2):
        x1_ref[...] = x1
        u2_ref[...] = u2

    slab = (lambda: (pm_ref[...], (bc * conv).astype(BF16)), lambda: (gp, gc),
            lambda: x_ref[...], store)
    _mix_tail([slab], g1, sc2, sh2, wpu_ref, wcu_ref, wo_ref, lng_ref, lnb_ref, alpha)


def _mixer_step(proj_s, pool_t, conv_t, xs, ada, grp_bf, pscale, convw, wpu_bf, wcu_bf,
                wo_bf, lng, lnb, *, alpha):
    ms, d = xs.shape
    p = d // 2
    body = functools.partial(_mixer_step_body, d=d, alpha=alpha)
    return pl.pallas_call(
        body,
        out_shape=[jax.ShapeDtypeStruct((ms, d), F32),
                   jax.ShapeDtypeStruct((ms, d), BF16),
                   jax.ShapeDtypeStruct(pool_t.shape, F32),
                   jax.ShapeDtypeStruct(conv_t.shape, F32)],
        grid=(1,),
        in_specs=[_resident(proj_s.shape),
                  _resident(pool_t.shape),
                  _resident(conv_t.shape),
                  _resident((ms, d)),
                  _resident((ms, N_ADA * d)),
                  _resident(grp_bf.shape),
                  _resident(pscale.shape),
                  _resident(convw.shape),
                  _resident(wpu_bf.shape),
                  _resident(wcu_bf.shape),
                  _resident(wo_bf.shape),
                  _resident(lng.shape),
                  _resident(lnb.shape)],
        out_specs=[_resident((ms, d)),
                   _resident((ms, d)),
                   _resident(pool_t.shape),
                   _resident(conv_t.shape)],
        scratch_shapes=[pltpu.VMEM((ms, p), BF16)],
        compiler_params=_params(("arbitrary",)),
        name="mixer_step",
    )(proj_s, pool_t, conv_t, xs, ada, grp_bf, pscale, convw, wpu_bf, wcu_bf, wo_bf, lng, lnb)


def _mlp_body(u2_ref, w1_ref, b1_ref, w2_ref, x1_ref, g2_ref, b2_ref, lng_ref, lnb_ref,
              u2s_ref, x1s_ref, g2s_ref, o_ref, os_ref, acc_ref, accs_ref,
              *, nf, te, tm, alpha, rows_per_cond):
    m = pl.program_id(0)
    f = pl.program_id(1)

    def chunk(u2):
        hmid = jnp.dot(u2, _unpack_rows(w1_ref[0]), preferred_element_type=F32) + b1_ref[...]
        hmid = jnp.square(jnp.maximum(hmid, 0.0))
        return jnp.dot(hmid.astype(BF16), _unpack_rows(w2_ref[...]),
                       preferred_element_type=F32)

    def accumulate(first):
        for r in range(0, tm, ROW_CHUNK):
            rows = slice(r, r + ROW_CHUNK)
            part = chunk(u2_ref[rows, :])
            if first:
                acc_ref[rows, :] = part
            else:
                acc_ref[rows, :] += part

        @pl.when(m == 0)
        def _():
            part = chunk(u2s_ref[...])
            if first:
                accs_ref[...] = part
            else:
                accs_ref[...] += part

    pl.when(f == 0)(functools.partial(accumulate, True))
    pl.when(jnp.logical_and(f > 0, f < nf))(functools.partial(accumulate, False))

    def finish(x1, g2, acc):
        return _layernorm(alpha * x1 + g2 * (acc + b2_ref[...]), lng_ref[...], lnb_ref[...])

    @pl.when(f >= nf)
    def _():
        rows = pl.ds(pl.multiple_of((f - nf) * te, te), te)
        g2 = g2_ref[pl.ds((m * tm) // rows_per_cond, 1), :]
        o_ref[...] = finish(x1_ref[...], g2, acc_ref[rows, :])

    @pl.when(jnp.logical_and(m == 0, f == nf))
    def _():
        os_ref[...] = finish(x1s_ref[...], g2s_ref[...], accs_ref[...])


def _mlp(u2, x1, u2s, x1s, ada, w1b, b1, w2b, b2, lng, lnb, *, alpha, rows_per_cond,
         cond_row, tm, te):
    m, d = u2.shape
    ms = u2s.shape[0]
    nf, _, tf = w1b.shape
    ne = tm // te
    g2_col = N_ADA - 1
    body = functools.partial(_mlp_body, nf=nf, te=te, tm=tm, alpha=alpha,
                             rows_per_cond=rows_per_cond)
    fw = lambda f: jnp.minimum(f, nf - 1)
    ep = lambda mi, f: (mi * ne + jnp.maximum(f - nf, 0), 0)
    return pl.pallas_call(
        body,
        out_shape=[jax.ShapeDtypeStruct((m, d), F32),
                   jax.ShapeDtypeStruct((ms, d), F32)],
        grid=(m // tm, nf + ne),
        in_specs=[pl.BlockSpec((tm, d), lambda mi, f: (mi, 0)),
                  pl.BlockSpec((1, d // 2, tf), lambda mi, f: (fw(f), 0, 0)),
                  pl.BlockSpec((1, tf), lambda mi, f: (0, fw(f))),
                  pl.BlockSpec((tf // 2, d), lambda mi, f: (fw(f), 0)),
                  pl.BlockSpec((te, d), ep),
                  pl.BlockSpec((8, d), lambda mi, f: (cond_row // 8, g2_col)),
                  _resident((1, d)),
                  _resident((1, d)),
                  _resident((1, d)),
                  _resident((ms, d)),
                  _resident((ms, d)),
                  pl.BlockSpec((ms, d), lambda mi, f: (0, g2_col),
                               pipeline_mode=pl.Buffered(1))],
        out_specs=[pl.BlockSpec((te, d), ep),
                   pl.BlockSpec((ms, d), lambda mi, f: (0, 0))],
        scratch_shapes=[pltpu.VMEM((tm, d), F32),
                        pltpu.VMEM((ms, d), F32)],
        compiler_params=_params(("arbitrary", "arbitrary")),
        name="mlp",
    )(u2, w1b, b1.reshape(1, -1), w2b, x1, ada, b2.reshape(1, d), lng, lnb, u2s, x1s, ada)


def kernel(x_prompt, x_sample, state_pool, state_conv, c_prompt, c_sample, w_ada, b_ada, w_in,
           pool_grp_w, pool_scale, conv_w, w_pool_up, w_conv_up, w_o, ln1_g, ln1_b, w_ff1,
           b_ff1, w_ff2, b_ff2, ln2_g, ln2_b):
    depth = w_ada.shape[0]
    nb, seq, d = x_prompt.shape
    ms, dec_seq, _ = x_sample.shape
    assert dec_seq == 1
    p = d // 2
    alpha = (2 * depth) ** 0.25

    prompt_row = ms
    pad = (-(ms + nb)) % 8
    c_all = jnp.concatenate([c_sample, c_prompt, jnp.zeros((pad, d), F32)], axis=0)

    xp = x_prompt.reshape(nb * seq, d)
    xs = x_sample.reshape(ms, d)
    pool_p, conv_p, pool_s, conv_s = [], [], [], []
    for l in range(depth):
        ada = _ada(c_all, w_ada[l], b_ada[l])
        grp_bf, wpu_bf, wcu_bf, wo_bf = _cast_weights(
            [pool_grp_w[l].reshape(p, -1), w_pool_up[l], w_conv_up[l], w_o[l]])
        pscale = pool_scale[l].reshape(1, p)
        lng1, lnb1 = ln1_g[l].reshape(1, d), ln1_b[l].reshape(1, d)
        lng2, lnb2 = ln2_g[l].reshape(1, d), ln2_b[l].reshape(1, d)

        proj_p, proj_s = _inproj(xp, xs, ada, w_in[l], seq=seq, prompt_row=prompt_row)

        x1p, u2p, pst, cst, w1b, w2b = _mixer(
            proj_p, xp, ada, grp_bf, pscale, conv_w[l], wpu_bf, wcu_bf, wo_bf, lng1, lnb1,
            w_ff1[l], w_ff2[l], seq=seq, prompt_row=prompt_row, alpha=alpha, tf=MLP_FF_CHUNK)
        pool_t = jnp.transpose(state_pool[l], (1, 0, 2))
        conv_t = jnp.transpose(state_conv[l], (1, 0, 2))
        x1s, u2s, npool_t, nconv_t = _mixer_step(proj_s, pool_t, conv_t, xs, ada, grp_bf, pscale,
                                                 conv_w[l], wpu_bf, wcu_bf, wo_bf, lng1, lnb1,
                                                 alpha=alpha)

        xp, xs = _mlp(u2p, x1p, u2s, x1s, ada, w1b, b_ff1[l], w2b, b_ff2[l], lng2, lnb2,
                      alpha=alpha, rows_per_cond=seq, cond_row=prompt_row,
                      tm=MLP_TOKEN_TILE, te=MLP_EPILOGUE_ROWS)

        pool_p.append(pst[:, HIST_ROWS - POOL_HIST:, :])
        conv_p.append(cst[:, HIST_ROWS - CONV_HIST:, :])
        pool_s.append(jnp.transpose(npool_t, (1, 0, 2)))
        conv_s.append(jnp.transpose(nconv_t, (1, 0, 2)))

    return (xp.reshape(nb, seq, d), xs.reshape(ms, dec_seq, d),
            jnp.stack(pool_p), jnp.stack(conv_p), jnp.stack(pool_s), jnp.stack(conv_s))
```

```python
import functools

import jax
import jax.numpy as jnp
from jax import lax
from jax.experimental import pallas as pl
from jax.experimental.pallas import tpu as pltpu

F32 = jnp.float32
BF16 = jnp.bfloat16
U32 = jnp.uint32

POOL_WINDOWS = (2, 4, 8, 16)
POOL_HIST = max(POOL_WINDOWS) - 1
CONV_K = 3
CONV_HIST = CONV_K - 1
N_ADA = 6
PAST_LEN = 16384
LN_EPS = 1e-5

VMEM_LIMIT_BYTES_V7X = 56 * 1024 * 1024
HIST_ROWS = 16
ROW_CHUNK = 256
MIXER_SLAB_ROWS = 256
MLP_TOKEN_TILE = 1024
MLP_FF_CHUNK = 1024
MLP_EPILOGUE_ROWS = 256


def _resident(shape):
    zeros = (0,) * len(shape)
    return pl.BlockSpec(shape, lambda *_: zeros, pipeline_mode=pl.Buffered(1))


def _params(semantics):
    return pltpu.CompilerParams(dimension_semantics=semantics,
                                vmem_limit_bytes=VMEM_LIMIT_BYTES_V7X)


def _layernorm(xf, g, b):
    mu = jnp.mean(xf, axis=-1, keepdims=True)
    var = jnp.mean(jnp.square(xf - mu), axis=-1, keepdims=True)
    return (xf - mu) * lax.rsqrt(var + LN_EPS) * g + b


def _ada_body(c_ref, w_ref, b_ref, o_ref):
    o_ref[...] = jnp.dot(c_ref[...].astype(BF16), w_ref[...].astype(BF16),
                         preferred_element_type=F32) + b_ref[...]


def _ada(c_all, w_ada, b_ada, tn=1024):
    m, d = c_all.shape
    n = w_ada.shape[1]
    return pl.pallas_call(
        _ada_body,
        out_shape=jax.ShapeDtypeStruct((m, n), F32),
        grid=(n // tn,),
        in_specs=[pl.BlockSpec((m, d), lambda j: (0, 0)),
                  pl.BlockSpec((d, tn), lambda j: (0, j)),
                  pl.BlockSpec((1, tn), lambda j: (0, j))],
        out_specs=pl.BlockSpec((m, tn), lambda j: (0, j)),
        compiler_params=_params(("arbitrary",)),
        name="ada",
    )(c_all, w_ada, b_ada.reshape(1, n))


def _pack_rows(x_bf16):
    return pltpu.bitcast(x_bf16, U32)


def _unpack_rows(x_u32):
    return pltpu.bitcast(x_u32, BF16)


def _cast_body(*refs):
    n = len(refs) // 2
    for src, dst in zip(refs[:n], refs[n:]):
        dst[...] = _pack_rows(src[...].astype(BF16))


def _cast_weights(ws, steps=8):
    in_specs, out_specs, out_shape = [], [], []
    for w in ws:
        r, c = w.shape
        in_specs.append(pl.BlockSpec((r // steps, c), lambda s: (s, 0)))
        out_specs.append(pl.BlockSpec((r // steps // 2, c), lambda s: (s, 0)))
        out_shape.append(jax.ShapeDtypeStruct((r // 2, c), U32))
    return pl.pallas_call(
        _cast_body, out_shape=out_shape, grid=(steps,),
        in_specs=in_specs, out_specs=out_specs,
        compiler_params=_params(("arbitrary",)),
        name="cast_weights",
    )(*ws)


def _inproj_body(x_ref, adap_ref, w_ref, xs_ref, adas_ref, o_ref, os_ref, wb_ref,
                 *, tiles_per_seq, d, tm):
    i = pl.program_id(1)

    @pl.when(i == 0)
    def _():
        wb_ref[...] = w_ref[...].astype(BF16)
        us = xs_ref[...] * (1.0 + adas_ref[:, d:2 * d]) + adas_ref[:, 0:d]
        os_ref[...] = jnp.dot(us.astype(BF16), wb_ref[...], preferred_element_type=F32)

    b = i // tiles_per_seq
    sh1 = adap_ref[pl.ds(b, 1), 0:d]
    sc1 = adap_ref[pl.ds(b, 1), d:2 * d]
    for r in range(0, tm, ROW_CHUNK):
        rows = slice(r, r + ROW_CHUNK)
        u = x_ref[rows, :] * (1.0 + sc1) + sh1
        o_ref[rows, :] = jnp.dot(u.astype(BF16), wb_ref[...],
                                 preferred_element_type=F32).astype(BF16)


def _inproj(xp, xs, ada, w_in, *, seq, prompt_row, tm=1024, tn=1024):
    m, d = xp.shape
    ms = xs.shape[0]
    n = w_in.shape[1]
    body = functools.partial(_inproj_body, tiles_per_seq=seq // tm, d=d, tm=tm)
    return pl.pallas_call(
        body,
        out_shape=[jax.ShapeDtypeStruct((m, n), BF16),
                   jax.ShapeDtypeStruct((ms, n), F32)],
        grid=(n // tn, m // tm),
        in_specs=[pl.BlockSpec((tm, d), lambda j, i: (i, 0)),
                  pl.BlockSpec((8, 2 * d), lambda j, i: (prompt_row // 8, 0)),
                  pl.BlockSpec((d, tn), lambda j, i: (0, j)),
                  pl.BlockSpec((ms, d), lambda j, i: (0, 0)),
                  pl.BlockSpec((ms, 2 * d), lambda j, i: (0, 0))],
        out_specs=[pl.BlockSpec((tm, tn), lambda j, i: (i, j)),
                   pl.BlockSpec((ms, tn), lambda j, i: (0, j))],
        scratch_shapes=[pltpu.VMEM((d, tn), BF16)],
        compiler_params=_params(("arbitrary", "arbitrary")),
        name="inproj",
    )(xp, ada, w_in, xs, ada)


def _sigmoid(x):
    return 0.5 * jnp.tanh(0.5 * x) + 0.5


def _mix_tail(slabs, g1, sc2, sh2, wpu_ref, wcu_ref, wo_ref, lng_ref, lnb_ref, alpha):
    wpu = _unpack_rows(wpu_ref[...])
    wcu = _unpack_rows(wcu_ref[...])
    wo = _unpack_rows(wo_ref[...])

    def up(s):
        pooled_bf, conv_bf = s[0]()
        return (jnp.dot(pooled_bf, wpu, preferred_element_type=F32),
                jnp.dot(conv_bf, wcu, preferred_element_type=F32))

    def gate(s, y):
        gp, gc = s[1]()
        return (_sigmoid(gp) * y[0] + _sigmoid(gc) * y[1]).astype(BF16)

    def finish(s, m_out):
        x1 = _layernorm(alpha * s[2]() + g1 * m_out, lng_ref[...], lnb_ref[...])
        s[3](x1, (x1 * (1.0 + sc2) + sh2).astype(BF16))

    y = [None] * len(slabs)
    mo = [None] * len(slabs)
    y[0] = up(slabs[0])
    for k, s in enumerate(slabs):
        if k + 1 < len(slabs):
            y[k + 1] = up(slabs[k + 1])
        mo[k] = jnp.dot(gate(s, y[k]), wo, preferred_element_type=F32)
        if k > 0:
            finish(slabs[k - 1], mo[k - 1])
    finish(slabs[-1], mo[-1])


def _mixer_body(proj_ref, hist_ref, x_ref, ada_ref, grp_ref, pscale_ref, convw_ref,
                wpu_ref, wcu_ref, wo_ref, lng_ref, lnb_ref, w1_ref, w2_ref,
                x1_ref, u2_ref, pst_ref, cst_ref, w1b_ref, w2b_ref, zz_ref, vv_ref, pm_ref,
                *, tm, tiles_per_seq, d, alpha):
    i = pl.program_id(0)

    tf = w1b_ref.shape[2]
    for fi in range(w1b_ref.shape[0]):
        w1b_ref[fi] = _pack_rows(w1_ref[:, fi * tf:(fi + 1) * tf].astype(BF16))
    w2b_ref[...] = _pack_rows(w2_ref[...].astype(BF16))

    tis = i % tiles_per_seq
    b = i // tiles_per_seq
    first = tis == 0
    p = d // 2
    gw = p // len(POOL_WINDOWS)
    h = HIST_ROWS

    zz_ref[0:h, :] = jnp.where(first, 0.0, hist_ref[:, 0:p].astype(F32))
    zz_ref[h:h + tm, :] = proj_ref[:, 0:p].astype(F32)
    vh = hist_ref[:, 3 * p:4 * p].astype(F32) * hist_ref[:, p:2 * p].astype(F32)
    vv_ref[0:h, :] = jnp.where(first, 0.0, vh)
    vv_ref[h:h + tm, :] = proj_ref[:, 3 * p:4 * p].astype(F32) * proj_ref[:, p:2 * p].astype(F32)
    pst_ref[0] = zz_ref[tm:tm + h, :]
    cst_ref[0] = vv_ref[tm:tm + h, :]

    g1 = ada_ref[pl.ds(b, 1), 2 * d:3 * d]
    sh2 = ada_ref[pl.ds(b, 1), 3 * d:4 * d]
    sc2 = ada_ref[pl.ds(b, 1), 4 * d:5 * d]

    n = MIXER_SLAB_ROWS

    def front(r0):
        rows = slice(r0, r0 + n)
        pos = tis * tm + r0 + lax.broadcasted_iota(jnp.int32, (n, 1), 0)
        for g, w in enumerate(POOL_WINDOWS):
            cols = slice(g * gw, (g + 1) * gw)
            zt = zz_ref[h + r0:h + r0 + n, cols]
            s = zt
            for k in range(1, w):
                s = s + zz_ref[h + r0 - k:h + r0 - k + n, cols]
            inv_cnt = 1.0 / jnp.minimum(w, pos + 1).astype(F32)
            pooled = (s * inv_cnt - zt).astype(BF16)
            og = jnp.dot(pooled, _unpack_rows(grp_ref[g * gw // 2:(g + 1) * gw // 2, :]),
                         preferred_element_type=F32)
            pm_ref[rows, cols] = (og * pscale_ref[:, cols]).astype(BF16)
        conv = convw_ref[0:1, :] * vv_ref[h + r0 - 2:h + r0 - 2 + n, :]
        conv = conv + convw_ref[1:2, :] * vv_ref[h + r0 - 1:h + r0 - 1 + n, :]
        conv = conv + convw_ref[2:3, :] * vv_ref[h + r0:h + r0 + n, :]
        bconv = (proj_ref[rows, 2 * p:3 * p].astype(F32) * conv).astype(BF16)
        return pm_ref[rows, :], bconv

    def gates(r0):
        rows = slice(r0, r0 + n)
        return (proj_ref[rows, 4 * p:4 * p + d].astype(F32),
                proj_ref[rows, 4 * p + d:4 * p + 2 * d].astype(F32))

    def store(r0, x1, u2):
        x1_ref[r0:r0 + n, :] = x1
        u2_ref[r0:r0 + n, :] = u2

    slabs = [(functools.partial(front, r0), functools.partial(gates, r0),
              functools.partial(lambda r: x_ref[r:r + n, :], r0), functools.partial(store, r0))
             for r0 in range(0, tm, n)]
    _mix_tail(slabs, g1, sc2, sh2, wpu_ref, wcu_ref, wo_ref, lng_ref, lnb_ref, alpha)


def _mixer(proj, xp, ada, grp_bf, pscale, convw, wpu_bf, wcu_bf, wo_bf, lng, lnb, w1, w2,
           *, seq, prompt_row, alpha, tf, tm=256):
    m, d = xp.shape
    p = d // 2
    nseq = m // seq
    h = HIST_ROWS
    steps = m // tm
    dff = w1.shape[1]
    body = functools.partial(_mixer_body, tm=tm, tiles_per_seq=seq // tm, d=d, alpha=alpha)
    return pl.pallas_call(
        body,
        out_shape=[jax.ShapeDtypeStruct((m, d), F32),
                   jax.ShapeDtypeStruct((m, d), BF16),
                   jax.ShapeDtypeStruct((nseq, h, p), F32),
                   jax.ShapeDtypeStruct((nseq, h, p), F32),
                   jax.ShapeDtypeStruct((dff // tf, d // 2, tf), U32),
                   jax.ShapeDtypeStruct((dff // 2, d), U32)],
        grid=(steps,),
        in_specs=[pl.BlockSpec((tm, proj.shape[1]), lambda i: (i, 0)),
                  pl.BlockSpec((h, proj.shape[1]),
                               lambda i: (jnp.maximum(i * (tm // h) - 1, 0), 0)),
                  pl.BlockSpec((tm, d), lambda i: (i, 0)),
                  pl.BlockSpec((8, N_ADA * d), lambda i: (prompt_row // 8, 0)),
                  _resident(grp_bf.shape),
                  _resident(pscale.shape),
                  _resident(convw.shape),
                  _resident(wpu_bf.shape),
                  _resident(wcu_bf.shape),
                  _resident(wo_bf.shape),
                  _resident(lng.shape),
                  _resident(lnb.shape),
                  pl.BlockSpec((d // steps, dff), lambda i: (i, 0)),
                  pl.BlockSpec((dff // steps, d), lambda i: (i, 0))],
        out_specs=[pl.BlockSpec((tm, d), lambda i: (i, 0)),
                   pl.BlockSpec((tm, d), lambda i: (i, 0)),
                   pl.BlockSpec((1, h, p), lambda i: (i // (seq // tm), 0, 0)),
                   pl.BlockSpec((1, h, p), lambda i: (i // (seq // tm), 0, 0)),
                   pl.BlockSpec((dff // tf, d // steps // 2, tf), lambda i: (0, i, 0)),
                   pl.BlockSpec((dff // steps // 2, d), lambda i: (i, 0))],
        scratch_shapes=[pltpu.VMEM((tm + h, p), F32),
                        pltpu.VMEM((tm + h, p), F32),
                        pltpu.VMEM((tm, p), BF16)],
        compiler_params=_params(("arbitrary",)),
        name="mixer",
    )(proj, proj, xp, ada, grp_bf, pscale, convw, wpu_bf, wcu_bf, wo_bf, lng, lnb, w1, w2)


def _mixer_step_body(proj_ref, pool_ref, cstate_ref, x_ref, ada_ref, grp_ref, pscale_ref,
                     convw_ref, wpu_ref, wcu_ref, wo_ref, lng_ref, lnb_ref,
                     x1_ref, u2_ref, npool_ref, nconv_ref, pm_ref, *, d, alpha):
    p = d // 2
    gw = p // len(POOL_WINDOWS)
    z = proj_ref[:, 0:p]
    for g, w in enumerate(POOL_WINDOWS):
        cols = slice(g * gw, (g + 1) * gw)
        s = z[:, cols]
        for k in range(1, w):
            s = s + pool_ref[POOL_HIST - k, :, cols]
        inv_cnt = 1.0 / min(w, PAST_LEN + 1)
        pooled = (s * inv_cnt - z[:, cols]).astype(BF16)
        og = jnp.dot(pooled, _unpack_rows(grp_ref[g * gw // 2:(g + 1) * gw // 2, :]),
                         preferred_element_type=F32)
        pm_ref[:, cols] = (og * pscale_ref[:, cols]).astype(BF16)
    for r in range(POOL_HIST - 1):
        npool_ref[r] = pool_ref[r + 1]
    npool_ref[POOL_HIST - 1] = z

    xc = proj_ref[:, p:2 * p]
    bc = proj_ref[:, 2 * p:3 * p]
    cc = proj_ref[:, 3 * p:4 * p]
    v = cc * xc
    conv = convw_ref[0:1, :] * cstate_ref[0]
    conv = conv + convw_ref[1:2, :] * cstate_ref[1]
    conv = conv + convw_ref[2:3, :] * v
    nconv_ref[0] = cstate_ref[1]
    nconv_ref[1] = v

    gp = proj_ref[:, 4 * p:4 * p + d]
    gc = proj_ref[:, 4 * p + d:4 * p + 2 * d]
    g1 = ada_ref[:, 2 * d:3 * d]
    sh2 = ada_ref[:, 3 * d:4 * d]
    sc2 = ada_ref[:, 4 * d:5 * d]

    def store(x1, u2):
        x1_ref[...] = x1
        u2_ref[...] = u2

    slab = (lambda: (pm_ref[...], (bc * conv).astype(BF16)), lambda: (gp, gc),
            lambda: x_ref[...], store)
    _mix_tail([slab], g1, sc2, sh2, wpu_ref, wcu_ref, wo_ref, lng_ref, lnb_ref, alpha)


def _mixer_step(proj_s, pool_t, conv_t, xs, ada, grp_bf, pscale, convw, wpu_bf, wcu_bf,
                wo_bf, lng, lnb, *, alpha):
    ms, d = xs.shape
    p = d // 2
    body = functools.partial(_mixer_step_body, d=d, alpha=alpha)
    return pl.pallas_call(
        body,
        out_shape=[jax.ShapeDtypeStruct((ms, d), F32),
                   jax.ShapeDtypeStruct((ms, d), BF16),
                   jax.ShapeDtypeStruct(pool_t.shape, F32),
                   jax.ShapeDtypeStruct(conv_t.shape, F32)],
        grid=(1,),
        in_specs=[_resident(proj_s.shape),
                  _resident(pool_t.shape),
                  _resident(conv_t.shape),
                  _resident((ms, d)),
                  _resident((ms, N_ADA * d)),
                  _resident(grp_bf.shape),
                  _resident(pscale.shape),
                  _resident(convw.shape),
                  _resident(wpu_bf.shape),
                  _resident(wcu_bf.shape),
                  _resident(wo_bf.shape),
                  _resident(lng.shape),
                  _resident(lnb.shape)],
        out_specs=[_resident((ms, d)),
                   _resident((ms, d)),
                   _resident(pool_t.shape),
                   _resident(conv_t.shape)],
        scratch_shapes=[pltpu.VMEM((ms, p), BF16)],
        compiler_params=_params(("arbitrary",)),
        name="mixer_step",
    )(proj_s, pool_t, conv_t, xs, ada, grp_bf, pscale, convw, wpu_bf, wcu_bf, wo_bf, lng, lnb)


def _mlp_body(u2_ref, w1_ref, b1_ref, w2_ref, x1_ref, g2_ref, b2_ref, lng_ref, lnb_ref,
              u2s_ref, x1s_ref, g2s_ref, o_ref, os_ref, acc_ref, accs_ref,
              *, nm, nf, ne, te, tm, alpha, rows_per_cond):
    m = pl.program_id(0)
    f = pl.program_id(1)
    slot = m % 2

    def chunk(u2):
        hmid = jnp.dot(u2, _unpack_rows(w1_ref[0]), preferred_element_type=F32) + b1_ref[...]
        hmid = jnp.square(jnp.maximum(hmid, 0.0))
        return jnp.dot(hmid.astype(BF16), _unpack_rows(w2_ref[...]),
                       preferred_element_type=F32)

    def accumulate(first, between=None):
        n_chunks = tm // ROW_CHUNK
        for c in range(n_chunks):
            if between is not None:
                between(c, n_chunks)
            rows = slice(c * ROW_CHUNK, (c + 1) * ROW_CHUNK)
            part = chunk(u2_ref[rows, :])
            if first:
                acc_ref[slot, rows, :] = part
            else:
                acc_ref[slot, rows, :] += part

    def accumulate_singles(first):
        part = chunk(u2s_ref[...])
        if first:
            accs_ref[...] = part
        else:
            accs_ref[...] += part

    def finish(x1, g2, acc):
        return _layernorm(alpha * x1 + g2 * (acc + b2_ref[...]), lng_ref[...], lnb_ref[...])

    def epilogue(piece=0, n_pieces=1):
        tp = te // n_pieces
        base = pl.multiple_of((f - 1) * te + piece * tp, tp)
        g2 = g2_ref[pl.ds(((m - 1) * tm) // rows_per_cond, 1), :]
        o_ref[piece * tp:(piece + 1) * tp, :] = finish(
            x1_ref[piece * tp:(piece + 1) * tp, :], g2, acc_ref[1 - slot, pl.ds(base, tp), :])

    has_main = m < nm
    has_epi = jnp.logical_and(m >= 1, jnp.logical_and(f >= 1, f <= ne))

    @pl.when(jnp.logical_and(has_main, f == 0))
    def _():
        accumulate(True)
        pl.when(m == 0)(functools.partial(accumulate_singles, True))

    @pl.when(jnp.logical_and(has_main, has_epi))
    def _():
        accumulate(False, between=epilogue)

    @pl.when(jnp.logical_and(has_main, jnp.logical_and(f > 0, jnp.logical_not(has_epi))))
    def _():
        accumulate(False)
        pl.when(m == 0)(functools.partial(accumulate_singles, False))

    @pl.when(jnp.logical_and(jnp.logical_not(has_main), has_epi))
    def _():
        epilogue()

    @pl.when(jnp.logical_and(m == 1, f == 1))
    def _():
        os_ref[...] = finish(x1s_ref[...], g2s_ref[...], accs_ref[...])


def _mlp(u2, x1, u2s, x1s, ada, w1b, b1, w2b, b2, lng, lnb, *, alpha, rows_per_cond,
         cond_row, tm, te):
    m, d = u2.shape
    ms = u2s.shape[0]
    nf, _, tf = w1b.shape
    nm = m // tm
    ne = tm // te
    assert ne < nf
    g2_col = N_ADA - 1
    body = functools.partial(_mlp_body, nm=nm, nf=nf, ne=ne, te=te, tm=tm, alpha=alpha,
                             rows_per_cond=rows_per_cond)
    fw = lambda mi, f: jnp.where(mi == nm, nf - 1, f)
    ep = lambda mi, f: (jnp.where(mi == 0, 0, (mi - 1) * ne + jnp.clip(f - 1, 0, ne - 1)), 0)
    return pl.pallas_call(
        body,
        out_shape=[jax.ShapeDtypeStruct((m, d), F32),
                   jax.ShapeDtypeStruct((ms, d), F32)],
        grid=(nm + 1, nf),
        in_specs=[pl.BlockSpec((tm, d), lambda mi, f: (jnp.minimum(mi, nm - 1), 0)),
                  pl.BlockSpec((1, d // 2, tf), lambda mi, f: (fw(mi, f), 0, 0)),
                  pl.BlockSpec((1, tf), lambda mi, f: (0, fw(mi, f))),
                  pl.BlockSpec((tf // 2, d), lambda mi, f: (fw(mi, f), 0)),
                  pl.BlockSpec((te, d), ep),
                  pl.BlockSpec((8, d), lambda mi, f: (cond_row // 8, g2_col)),
                  _resident((1, d)),
                  _resident((1, d)),
                  _resident((1, d)),
                  _resident((ms, d)),
                  _resident((ms, d)),
                  pl.BlockSpec((ms, d), lambda mi, f: (0, g2_col),
                               pipeline_mode=pl.Buffered(1))],
        out_specs=[pl.BlockSpec((te, d), ep),
                   pl.BlockSpec((ms, d), lambda mi, f: (0, 0))],
        scratch_shapes=[pltpu.VMEM((2, tm, d), F32),
                        pltpu.VMEM((ms, d), F32)],
        compiler_params=_params(("arbitrary", "arbitrary")),
        name="mlp",
    )(u2, w1b, b1.reshape(1, -1), w2b, x1, ada, b2.reshape(1, d), lng, lnb, u2s, x1s, ada)


def kernel(x_prompt, x_sample, state_pool, state_conv, c_prompt, c_sample, w_ada, b_ada, w_in,
           pool_grp_w, pool_scale, conv_w, w_pool_up, w_conv_up, w_o, ln1_g, ln1_b, w_ff1,
           b_ff1, w_ff2, b_ff2, ln2_g, ln2_b):
    depth = w_ada.shape[0]
    nb, seq, d = x_prompt.shape
    ms, dec_seq, _ = x_sample.shape
    assert dec_seq == 1
    p = d // 2
    alpha = (2 * depth) ** 0.25

    prompt_row = ms
    pad = (-(ms + nb)) % 8
    c_all = jnp.concatenate([c_sample, c_prompt, jnp.zeros((pad, d), F32)], axis=0)

    xp = x_prompt.reshape(nb * seq, d)
    xs = x_sample.reshape(ms, d)
    pool_p, conv_p, pool_s, conv_s = [], [], [], []
    for l in range(depth):
        ada = _ada(c_all, w_ada[l], b_ada[l])
        grp_bf, wpu_bf, wcu_bf, wo_bf = _cast_weights(
            [pool_grp_w[l].reshape(p, -1), w_pool_up[l], w_conv_up[l], w_o[l]])
        pscale = pool_scale[l].reshape(1, p)
        lng1, lnb1 = ln1_g[l].reshape(1, d), ln1_b[l].reshape(1, d)
        lng2, lnb2 = ln2_g[l].reshape(1, d), ln2_b[l].reshape(1, d)

        proj_p, proj_s = _inproj(xp, xs, ada, w_in[l], seq=seq, prompt_row=prompt_row)

        x1p, u2p, pst, cst, w1b, w2b = _mixer(
            proj_p, xp, ada, grp_bf, pscale, conv_w[l], wpu_bf, wcu_bf, wo_bf, lng1, lnb1,
            w_ff1[l], w_ff2[l], seq=seq, prompt_row=prompt_row, alpha=alpha, tf=MLP_FF_CHUNK)
        pool_t = jnp.transpose(state_pool[l], (1, 0, 2))
        conv_t = jnp.transpose(state_conv[l], (1, 0, 2))
        x1s, u2s, npool_t, nconv_t = _mixer_step(proj_s, pool_t, conv_t, xs, ada, grp_bf, pscale,
                                                 conv_w[l], wpu_bf, wcu_bf, wo_bf, lng1, lnb1,
                                                 alpha=alpha)

        xp, xs = _mlp(u2p, x1p, u2s, x1s, ada, w1b, b_ff1[l], w2b, b_ff2[l], lng2, lnb2,
                      alpha=alpha, rows_per_cond=seq, cond_row=prompt_row,
                      tm=MLP_TOKEN_TILE, te=MLP_EPILOGUE_ROWS)

        pool_p.append(pst[:, HIST_ROWS - POOL_HIST:, :])
        conv_p.append(cst[:, HIST_ROWS - CONV_HIST:, :])
        pool_s.append(jnp.transpose(npool_t, (1, 0, 2)))
        conv_s.append(jnp.transpose(nconv_t, (1, 0, 2)))

    return (xp.reshape(nb, seq, d), xs.reshape(ms, dec_seq, d),
            jnp.stack(pool_p), jnp.stack(conv_p), jnp.stack(pool_s), jnp.stack(conv_s))
```

```python
import functools

import jax
import jax.numpy as jnp
from jax import lax
from jax.experimental import pallas as pl
from jax.experimental.pallas import tpu as pltpu

F32 = jnp.float32
BF16 = jnp.bfloat16
U32 = jnp.uint32

POOL_WINDOWS = (2, 4, 8, 16)
POOL_HIST = max(POOL_WINDOWS) - 1
CONV_K = 3
CONV_HIST = CONV_K - 1
N_ADA = 6
PAST_LEN = 16384
LN_EPS = 1e-5

VMEM_LIMIT_BYTES_V7X = 56 * 1024 * 1024
HIST_ROWS = 16
ROW_CHUNK = 256
MIXER_SLAB_ROWS = 256
MLP_TOKEN_TILE = 1024
MLP_FF_CHUNK = 1024
MLP_EPILOGUE_ROWS = 256


def _resident(shape):
    zeros = (0,) * len(shape)
    return pl.BlockSpec(shape, lambda *_: zeros, pipeline_mode=pl.Buffered(1))


def _params(semantics):
    return pltpu.CompilerParams(dimension_semantics=semantics,
                                vmem_limit_bytes=VMEM_LIMIT_BYTES_V7X)


def _layernorm(xf, g, b):
    mu = jnp.mean(xf, axis=-1, keepdims=True)
    var = jnp.mean(jnp.square(xf - mu), axis=-1, keepdims=True)
    return (xf - mu) * lax.rsqrt(var + LN_EPS) * g + b


def _ada_body(c_ref, w_ref, b_ref, o_ref):
    o_ref[...] = jnp.dot(c_ref[...].astype(BF16), w_ref[...].astype(BF16),
                         preferred_element_type=F32) + b_ref[...]


def _ada(c_all, w_ada, b_ada, tn=2048):
    m, d = c_all.shape
    n = w_ada.shape[1]
    return pl.pallas_call(
        _ada_body,
        out_shape=jax.ShapeDtypeStruct((m, n), F32),
        grid=(n // tn,),
        in_specs=[pl.BlockSpec((m, d), lambda j: (0, 0)),
                  pl.BlockSpec((d, tn), lambda j: (0, j)),
                  pl.BlockSpec((1, tn), lambda j: (0, j))],
        out_specs=pl.BlockSpec((m, tn), lambda j: (0, j)),
        compiler_params=_params(("arbitrary",)),
        name="ada",
    )(c_all, w_ada, b_ada.reshape(1, n))


def _pack_rows(x_bf16):
    return pltpu.bitcast(x_bf16, U32)


def _unpack_rows(x_u32):
    return pltpu.bitcast(x_u32, BF16)


def _inproj_body(x_ref, adap_ref, w_ref, xs_ref, adas_ref, *rest, tiles_per_seq, d, tm,
                 n_side, side_steps):
    side_in = rest[:n_side]
    o_ref, os_ref = rest[n_side:n_side + 2]
    side_out = rest[n_side + 2:2 * n_side + 2]
    wb_ref = rest[2 * n_side + 2]
    i = pl.program_id(1)

    @pl.when(pl.program_id(0) * pl.num_programs(1) + i < side_steps)
    def _():
        for src, dst in zip(side_in, side_out):
            dst[...] = _pack_rows(src[...].astype(BF16))

    @pl.when(i == 0)
    def _():
        wb_ref[...] = w_ref[...].astype(BF16)
        us = xs_ref[...] * (1.0 + adas_ref[:, d:2 * d]) + adas_ref[:, 0:d]
        os_ref[...] = jnp.dot(us.astype(BF16), wb_ref[...], preferred_element_type=F32)

    b = i // tiles_per_seq
    sh1 = adap_ref[pl.ds(b, 1), 0:d]
    sc1 = adap_ref[pl.ds(b, 1), d:2 * d]
    for r in range(0, tm, ROW_CHUNK):
        rows = slice(r, r + ROW_CHUNK)
        u = x_ref[rows, :] * (1.0 + sc1) + sh1
        o_ref[rows, :] = jnp.dot(u.astype(BF16), wb_ref[...],
                                 preferred_element_type=F32).astype(BF16)


def _inproj(xp, xs, ada, w_in, side_ws, *, seq, prompt_row, tm=1024, tn=1024, side_steps=16):
    m, d = xp.shape
    ms = xs.shape[0]
    n = w_in.shape[1]
    ni = m // tm
    body = functools.partial(_inproj_body, tiles_per_seq=seq // tm, d=d, tm=tm,
                             n_side=len(side_ws), side_steps=side_steps)
    side_idx = lambda j, i: (jnp.minimum(j * ni + i, side_steps - 1), 0)
    side_in = [pl.BlockSpec((w.shape[0] // side_steps, w.shape[1]), side_idx) for w in side_ws]
    side_out = [pl.BlockSpec((w.shape[0] // side_steps // 2, w.shape[1]), side_idx)
                for w in side_ws]
    side_shape = [jax.ShapeDtypeStruct((w.shape[0] // 2, w.shape[1]), U32) for w in side_ws]
    out = pl.pallas_call(
        body,
        out_shape=[jax.ShapeDtypeStruct((m, n), BF16),
                   jax.ShapeDtypeStruct((ms, n), F32)] + side_shape,
        grid=(n // tn, ni),
        in_specs=[pl.BlockSpec((tm, d), lambda j, i: (i, 0)),
                  pl.BlockSpec((8, 2 * d), lambda j, i: (prompt_row // 8, 0)),
                  pl.BlockSpec((d, tn), lambda j, i: (0, j)),
                  _resident((ms, d)),
                  _resident((ms, 2 * d))] + side_in,
        out_specs=[pl.BlockSpec((tm, tn), lambda j, i: (i, j)),
                   pl.BlockSpec((ms, tn), lambda j, i: (0, j))] + side_out,
        scratch_shapes=[pltpu.VMEM((d, tn), BF16)],
        compiler_params=_params(("arbitrary", "arbitrary")),
        name="inproj",
    )(xp, ada, w_in, xs, ada, *side_ws)
    return out[0], out[1], out[2:]


def _sigmoid(x):
    return 0.5 * jnp.tanh(0.5 * x) + 0.5


def _mix_tail(slabs, g1, sc2, sh2, wpu_ref, wcu_ref, wo_ref, lng_ref, lnb_ref, alpha):
    wpu = _unpack_rows(wpu_ref[...])
    wcu = _unpack_rows(wcu_ref[...])
    wo = _unpack_rows(wo_ref[...])

    def up(s):
        pooled_bf, conv_bf = s[0]()
        return (jnp.dot(pooled_bf, wpu, preferred_element_type=F32),
                jnp.dot(conv_bf, wcu, preferred_element_type=F32))

    def gate(s, y):
        gp, gc = s[1]()
        return (_sigmoid(gp) * y[0] + _sigmoid(gc) * y[1]).astype(BF16)

    def finish(s, m_out):
        x1 = _layernorm(alpha * s[2]() + g1 * m_out, lng_ref[...], lnb_ref[...])
        s[3](x1, (x1 * (1.0 + sc2) + sh2).astype(BF16))

    y = [None] * len(slabs)
    mo = [None] * len(slabs)
    y[0] = up(slabs[0])
    for k, s in enumerate(slabs):
        if k + 1 < len(slabs):
            y[k + 1] = up(slabs[k + 1])
        mo[k] = jnp.dot(gate(s, y[k]), wo, preferred_element_type=F32)
        if k > 0:
            finish(slabs[k - 1], mo[k - 1])
    finish(slabs[-1], mo[-1])


def _mixer_body(proj_ref, hist_ref, x_ref, ada_ref, grp_ref, pscale_ref, convw_ref,
                wpu_ref, wcu_ref, wo_ref, lng_ref, lnb_ref, w1_ref, w2_ref,
                x1_ref, u2_ref, pst_ref, cst_ref, w1b_ref, w2b_ref, zz_ref, vv_ref, pm_ref,
                *, tm, tiles_per_seq, d, alpha):
    i = pl.program_id(0)

    tf = w1b_ref.shape[2]
    for fi in range(w1b_ref.shape[0]):
        w1b_ref[fi] = _pack_rows(w1_ref[:, fi * tf:(fi + 1) * tf].astype(BF16))
    w2b_ref[...] = _pack_rows(w2_ref[...].astype(BF16))

    tis = i % tiles_per_seq
    b = i // tiles_per_seq
    first = tis == 0
    p = d // 2
    gw = p // len(POOL_WINDOWS)
    h = HIST_ROWS

    zz_ref[0:h, :] = jnp.where(first, 0.0, hist_ref[:, 0:p].astype(F32))
    zz_ref[h:h + tm, :] = proj_ref[:, 0:p].astype(F32)
    vh = hist_ref[:, 3 * p:4 * p].astype(F32) * hist_ref[:, p:2 * p].astype(F32)
    vv_ref[0:h, :] = jnp.where(first, 0.0, vh)
    vv_ref[h:h + tm, :] = proj_ref[:, 3 * p:4 * p].astype(F32) * proj_ref[:, p:2 * p].astype(F32)
    pst_ref[0] = zz_ref[tm:tm + h, :]
    cst_ref[0] = vv_ref[tm:tm + h, :]

    g1 = ada_ref[pl.ds(b, 1), 2 * d:3 * d]
    sh2 = ada_ref[pl.ds(b, 1), 3 * d:4 * d]
    sc2 = ada_ref[pl.ds(b, 1), 4 * d:5 * d]

    n = MIXER_SLAB_ROWS

    def front(r0):
        rows = slice(r0, r0 + n)
        pos = tis * tm + r0 + lax.broadcasted_iota(jnp.int32, (n, 1), 0)
        for g, w in enumerate(POOL_WINDOWS):
            cols = slice(g * gw, (g + 1) * gw)
            zt = zz_ref[h + r0:h + r0 + n, cols]
            s = zt
            for k in range(1, w):
                s = s + zz_ref[h + r0 - k:h + r0 - k + n, cols]
            inv_cnt = 1.0 / jnp.minimum(w, pos + 1).astype(F32)
            pooled = (s * inv_cnt - zt).astype(BF16)
            og = jnp.dot(pooled, _unpack_rows(grp_ref[g * gw // 2:(g + 1) * gw // 2, :]),
                         preferred_element_type=F32)
            pm_ref[rows, cols] = (og * pscale_ref[:, cols]).astype(BF16)
        conv = convw_ref[0:1, :] * vv_ref[h + r0 - 2:h + r0 - 2 + n, :]
        conv = conv + convw_ref[1:2, :] * vv_ref[h + r0 - 1:h + r0 - 1 + n, :]
        conv = conv + convw_ref[2:3, :] * vv_ref[h + r0:h + r0 + n, :]
        bconv = (proj_ref[rows, 2 * p:3 * p].astype(F32) * conv).astype(BF16)
        return pm_ref[rows, :], bconv

    def gates(r0):
        rows = slice(r0, r0 + n)
        return (proj_ref[rows, 4 * p:4 * p + d].astype(F32),
                proj_ref[rows, 4 * p + d:4 * p + 2 * d].astype(F32))

    def store(r0, x1, u2):
        x1_ref[r0:r0 + n, :] = x1
        u2_ref[r0:r0 + n, :] = u2

    slabs = [(functools.partial(front, r0), functools.partial(gates, r0),
              functools.partial(lambda r: x_ref[r:r + n, :], r0), functools.partial(store, r0))
             for r0 in range(0, tm, n)]
    _mix_tail(slabs, g1, sc2, sh2, wpu_ref, wcu_ref, wo_ref, lng_ref, lnb_ref, alpha)


def _mixer(proj, xp, ada, grp_bf, pscale, convw, wpu_bf, wcu_bf, wo_bf, lng, lnb, w1, w2,
           *, seq, prompt_row, alpha, tf, tm=256):
    m, d = xp.shape
    p = d // 2
    nseq = m // seq
    h = HIST_ROWS
    steps = m // tm
    dff = w1.shape[1]
    body = functools.partial(_mixer_body, tm=tm, tiles_per_seq=seq // tm, d=d, alpha=alpha)
    return pl.pallas_call(
        body,
        out_shape=[jax.ShapeDtypeStruct((m, d), F32),
                   jax.ShapeDtypeStruct((m, d), BF16),
                   jax.ShapeDtypeStruct((nseq, h, p), F32),
                   jax.ShapeDtypeStruct((nseq, h, p), F32),
                   jax.ShapeDtypeStruct((dff // tf, d // 2, tf), U32),
                   jax.ShapeDtypeStruct((dff // 2, d), U32)],
        grid=(steps,),
        in_specs=[pl.BlockSpec((tm, proj.shape[1]), lambda i: (i, 0)),
                  pl.BlockSpec((h, proj.shape[1]),
                               lambda i: (jnp.maximum(i * (tm // h) - 1, 0), 0)),
                  pl.BlockSpec((tm, d), lambda i: (i, 0)),
                  pl.BlockSpec((8, N_ADA * d), lambda i: (prompt_row // 8, 0)),
                  _resident(grp_bf.shape),
                  _resident(pscale.shape),
                  _resident(convw.shape),
                  _resident(wpu_bf.shape),
                  _resident(wcu_bf.shape),
                  _resident(wo_bf.shape),
                  _resident(lng.shape),
                  _resident(lnb.shape),
                  pl.BlockSpec((d // steps, dff), lambda i: (i, 0)),
                  pl.BlockSpec((dff // steps, d), lambda i: (i, 0))],
        out_specs=[pl.BlockSpec((tm, d), lambda i: (i, 0)),
                   pl.BlockSpec((tm, d), lambda i: (i, 0)),
                   pl.BlockSpec((1, h, p), lambda i: (i // (seq // tm), 0, 0)),
                   pl.BlockSpec((1, h, p), lambda i: (i // (seq // tm), 0, 0)),
                   pl.BlockSpec((dff // tf, d // steps // 2, tf), lambda i: (0, i, 0)),
                   pl.BlockSpec((dff // steps // 2, d), lambda i: (i, 0))],
        scratch_shapes=[pltpu.VMEM((tm + h, p), F32),
                        pltpu.VMEM((tm + h, p), F32),
                        pltpu.VMEM((tm, p), BF16)],
        compiler_params=_params(("arbitrary",)),
        name="mixer",
    )(proj, proj, xp, ada, grp_bf, pscale, convw, wpu_bf, wcu_bf, wo_bf, lng, lnb, w1, w2)


def _mixer_step_body(proj_ref, pool_ref, cstate_ref, x_ref, ada_ref, grp_ref, pscale_ref,
                     convw_ref, wpu_ref, wcu_ref, wo_ref, lng_ref, lnb_ref,
                     x1_ref, u2_ref, npool_ref, nconv_ref, pm_ref, *, d, alpha):
    p = d // 2
    gw = p // len(POOL_WINDOWS)
    z = proj_ref[:, 0:p]
    for g, w in enumerate(POOL_WINDOWS):
        cols = slice(g * gw, (g + 1) * gw)
        s = z[:, cols]
        for k in range(1, w):
            s = s + pool_ref[:, POOL_HIST - k, cols]
        inv_cnt = 1.0 / min(w, PAST_LEN + 1)
        pooled = (s * inv_cnt - z[:, cols]).astype(BF16)
        og = jnp.dot(pooled, _unpack_rows(grp_ref[g * gw // 2:(g + 1) * gw // 2, :]),
                         preferred_element_type=F32)
        pm_ref[:, cols] = (og * pscale_ref[:, cols]).astype(BF16)
    for r in range(POOL_HIST - 1):
        npool_ref[:, r, :] = pool_ref[:, r + 1, :]
    npool_ref[:, POOL_HIST - 1, :] = z

    xc = proj_ref[:, p:2 * p]
    bc = proj_ref[:, 2 * p:3 * p]
    cc = proj_ref[:, 3 * p:4 * p]
    v = cc * xc
    conv = convw_ref[0:1, :] * cstate_ref[:, 0, :]
    conv = conv + convw_ref[1:2, :] * cstate_ref[:, 1, :]
    conv = conv + convw_ref[2:3, :] * v
    nconv_ref[:, 0, :] = cstate_ref[:, 1, :]
    nconv_ref[:, 1, :] = v

    gp = proj_ref[:, 4 * p:4 * p + d]
    gc = proj_ref[:, 4 * p + d:4 * p + 2 * d]
    g1 = ada_ref[:, 2 * d:3 * d]
    sh2 = ada_ref[:, 3 * d:4 * d]
    sc2 = ada_ref[:, 4 * d:5 * d]

    def store(x1, u2):
        x1_ref[...] = x1
        u2_ref[...] = u2

    slab = (lambda: (pm_ref[...], (bc * conv).astype(BF16)), lambda: (gp, gc),
            lambda: x_ref[...], store)
    _mix_tail([slab], g1, sc2, sh2, wpu_ref, wcu_ref, wo_ref, lng_ref, lnb_ref, alpha)


def _mixer_step(proj_s, pool_state, conv_state, xs, ada, grp_bf, pscale, convw, wpu_bf, wcu_bf,
                wo_bf, lng, lnb, *, alpha):
    ms, d = xs.shape
    p = d // 2
    body = functools.partial(_mixer_step_body, d=d, alpha=alpha)
    return pl.pallas_call(
        body,
        out_shape=[jax.ShapeDtypeStruct((ms, d), F32),
                   jax.ShapeDtypeStruct((ms, d), BF16),
                   jax.ShapeDtypeStruct(pool_state.shape, F32),
                   jax.ShapeDtypeStruct(conv_state.shape, F32)],
        grid=(1,),
        in_specs=[_resident(proj_s.shape),
                  _resident(pool_state.shape),
                  _resident(conv_state.shape),
                  _resident((ms, d)),
                  _resident((ms, N_ADA * d)),
                  _resident(grp_bf.shape),
                  _resident(pscale.shape),
                  _resident(convw.shape),
                  _resident(wpu_bf.shape),
                  _resident(wcu_bf.shape),
                  _resident(wo_bf.shape),
                  _resident(lng.shape),
                  _resident(lnb.shape)],
        out_specs=[_resident((ms, d)),
                   _resident((ms, d)),
                   _resident(pool_state.shape),
                   _resident(conv_state.shape)],
        scratch_shapes=[pltpu.VMEM((ms, p), BF16)],
        compiler_params=_params(("arbitrary",)),
        name="mixer_step",
    )(proj_s, pool_state, conv_state, xs, ada, grp_bf, pscale, convw, wpu_bf, wcu_bf, wo_bf, lng, lnb)


def _mlp_body(u2_ref, w1_ref, b1_ref, w2_ref, x1_ref, g2_ref, b2_ref, lng_ref, lnb_ref,
              u2s_ref, x1s_ref, g2s_ref, o_ref, os_ref, acc_ref, accs_ref,
              *, nm, nf, ne, te, tm, alpha, rows_per_cond):
    m = pl.program_id(0)
    f = pl.program_id(1)
    slot = m % 2

    def chunk(u2):
        hmid = jnp.dot(u2, _unpack_rows(w1_ref[0]), preferred_element_type=F32) + b1_ref[...]
        hmid = jnp.square(jnp.maximum(hmid, 0.0))
        return jnp.dot(hmid.astype(BF16), _unpack_rows(w2_ref[...]),
                       preferred_element_type=F32)

    def accumulate(first, between=None):
        n_chunks = tm // ROW_CHUNK
        for c in range(n_chunks):
            if between is not None:
                between(c, n_chunks)
            rows = slice(c * ROW_CHUNK, (c + 1) * ROW_CHUNK)
            part = chunk(u2_ref[rows, :])
            if first:
                acc_ref[slot, rows, :] = part
            else:
                acc_ref[slot, rows, :] += part

    def accumulate_singles(first):
        part = chunk(u2s_ref[...])
        if first:
            accs_ref[...] = part
        else:
            accs_ref[...] += part

    def finish(x1, g2, acc):
        return _layernorm(alpha * x1 + g2 * (acc + b2_ref[...]), lng_ref[...], lnb_ref[...])

    def epilogue(piece=0, n_pieces=1):
        tp = te // n_pieces
        base = pl.multiple_of((f - 1) * te + piece * tp, tp)
        g2 = g2_ref[pl.ds(((m - 1) * tm) // rows_per_cond, 1), :]
        o_ref[piece * tp:(piece + 1) * tp, :] = finish(
            x1_ref[piece * tp:(piece + 1) * tp, :], g2, acc_ref[1 - slot, pl.ds(base, tp), :])

    has_main = m < nm
    has_epi = jnp.logical_and(m >= 1, jnp.logical_and(f >= 1, f <= ne))

    @pl.when(jnp.logical_and(has_main, f == 0))
    def _():
        accumulate(True)
        pl.when(m == 0)(functools.partial(accumulate_singles, True))

    @pl.when(jnp.logical_and(has_main, has_epi))
    def _():
        accumulate(False, between=epilogue)

    @pl.when(jnp.logical_and(has_main, jnp.logical_and(f > 0, jnp.logical_not(has_epi))))
    def _():
        accumulate(False)
        pl.when(m == 0)(functools.partial(accumulate_singles, False))

    @pl.when(jnp.logical_and(jnp.logical_not(has_main), has_epi))
    def _():
        epilogue()

    @pl.when(jnp.logical_and(m == 1, f == 1))
    def _():
        os_ref[...] = finish(x1s_ref[...], g2s_ref[...], accs_ref[...])


def _mlp(u2, x1, u2s, x1s, ada, w1b, b1, w2b, b2, lng, lnb, *, alpha, rows_per_cond,
         cond_row, tm, te):
    m, d = u2.shape
    ms = u2s.shape[0]
    nf, _, tf = w1b.shape
    nm = m // tm
    ne = tm // te
    assert ne < nf
    g2_col = N_ADA - 1
    body = functools.partial(_mlp_body, nm=nm, nf=nf, ne=ne, te=te, tm=tm, alpha=alpha,
                             rows_per_cond=rows_per_cond)
    fw = lambda mi, f: jnp.where(mi == nm, nf - 1, f)
    ep = lambda mi, f: (jnp.where(mi == 0, 0, (mi - 1) * ne + jnp.clip(f - 1, 0, ne - 1)), 0)
    return pl.pallas_call(
        body,
        out_shape=[jax.ShapeDtypeStruct((m, d), F32),
                   jax.ShapeDtypeStruct((ms, d), F32)],
        grid=(nm + 1, nf),
        in_specs=[pl.BlockSpec((tm, d), lambda mi, f: (jnp.minimum(mi, nm - 1), 0)),
                  pl.BlockSpec((1, d // 2, tf), lambda mi, f: (fw(mi, f), 0, 0)),
                  pl.BlockSpec((1, tf), lambda mi, f: (0, fw(mi, f))),
                  pl.BlockSpec((tf // 2, d), lambda mi, f: (fw(mi, f), 0)),
                  pl.BlockSpec((te, d), ep),
                  pl.BlockSpec((8, d), lambda mi, f: (cond_row // 8, g2_col)),
                  _resident((1, d)),
                  _resident((1, d)),
                  _resident((1, d)),
                  _resident((ms, d)),
                  _resident((ms, d)),
                  pl.BlockSpec((ms, d), lambda mi, f: (0, g2_col),
                               pipeline_mode=pl.Buffered(1))],
        out_specs=[pl.BlockSpec((te, d), ep),
                   pl.BlockSpec((ms, d), lambda mi, f: (0, 0))],
        scratch_shapes=[pltpu.VMEM((2, tm, d), F32),
                        pltpu.VMEM((ms, d), F32)],
        compiler_params=_params(("arbitrary", "arbitrary")),
        name="mlp",
    )(u2, w1b, b1.reshape(1, -1), w2b, x1, ada, b2.reshape(1, d), lng, lnb, u2s, x1s, ada)


def kernel(x_prompt, x_sample, state_pool, state_conv, c_prompt, c_sample, w_ada, b_ada, w_in,
           pool_grp_w, pool_scale, conv_w, w_pool_up, w_conv_up, w_o, ln1_g, ln1_b, w_ff1,
           b_ff1, w_ff2, b_ff2, ln2_g, ln2_b):
    depth = w_ada.shape[0]
    nb, seq, d = x_prompt.shape
    ms, dec_seq, _ = x_sample.shape
    assert dec_seq == 1
    p = d // 2
    alpha = (2 * depth) ** 0.25

    prompt_row = ms
    pad = (-(ms + nb)) % 8
    c_all = jnp.concatenate([c_sample, c_prompt, jnp.zeros((pad, d), F32)], axis=0)

    xp = x_prompt.reshape(nb * seq, d)
    xs = x_sample.reshape(ms, d)
    pool_p, conv_p, pool_s, conv_s = [], [], [], []
    for l in range(depth):
        ada = _ada(c_all, w_ada[l], b_ada[l])
        pscale = pool_scale[l].reshape(1, p)
        lng1, lnb1 = ln1_g[l].reshape(1, d), ln1_b[l].reshape(1, d)
        lng2, lnb2 = ln2_g[l].reshape(1, d), ln2_b[l].reshape(1, d)

        proj_p, proj_s, (grp_bf, wpu_bf, wcu_bf, wo_bf) = _inproj(
            xp, xs, ada, w_in[l],
            [pool_grp_w[l].reshape(p, -1), w_pool_up[l], w_conv_up[l], w_o[l]],
            seq=seq, prompt_row=prompt_row)

        x1p, u2p, pst, cst, w1b, w2b = _mixer(
            proj_p, xp, ada, grp_bf, pscale, conv_w[l], wpu_bf, wcu_bf, wo_bf, lng1, lnb1,
            w_ff1[l], w_ff2[l], seq=seq, prompt_row=prompt_row, alpha=alpha, tf=MLP_FF_CHUNK)
        x1s, u2s, npool, nconv = _mixer_step(proj_s, state_pool[l], state_conv[l], xs, ada,
                                             grp_bf, pscale, conv_w[l], wpu_bf, wcu_bf, wo_bf,
                                             lng1, lnb1, alpha=alpha)

        xp, xs = _mlp(u2p, x1p, u2s, x1s, ada, w1b, b_ff1[l], w2b, b_ff2[l], lng2, lnb2,
                      alpha=alpha, rows_per_cond=seq, cond_row=prompt_row,
                      tm=MLP_TOKEN_TILE, te=MLP_EPILOGUE_ROWS)

        pool_p.append(pst[:, HIST_ROWS - POOL_HIST:, :])
        conv_p.append(cst[:, HIST_ROWS - CONV_HIST:, :])
        pool_s.append(npool)
        conv_s.append(nconv)

    return (xp.reshape(nb, seq, d), xs.reshape(ms, dec_seq, d),
            jnp.stack(pool_p), jnp.stack(conv_p), jnp.stack(pool_s), jnp.stack(conv_s))
```

```python
import functools

import jax
import jax.numpy as jnp
from jax import lax
from jax.experimental import pallas as pl
from jax.experimental.pallas import tpu as pltpu

F32 = jnp.float32
BF16 = jnp.bfloat16
U32 = jnp.uint32

POOL_WINDOWS = (2, 4, 8, 16)
POOL_HIST = max(POOL_WINDOWS) - 1
CONV_K = 3
CONV_HIST = CONV_K - 1
N_ADA = 6
PAST_LEN = 16384
LN_EPS = 1e-5

VMEM_LIMIT_BYTES_V7X = 56 * 1024 * 1024
HIST_ROWS = 16
ROW_CHUNK = 256
MIXER_SLAB_ROWS = 256
MLP_TOKEN_TILE = 1024
MLP_FF_CHUNK = 1024
MLP_EPILOGUE_ROWS = 256


def _resident(shape):
    zeros = (0,) * len(shape)
    return pl.BlockSpec(shape, lambda *_: zeros, pipeline_mode=pl.Buffered(1))


def _params(semantics):
    return pltpu.CompilerParams(dimension_semantics=semantics,
                                vmem_limit_bytes=VMEM_LIMIT_BYTES_V7X)


def _layernorm(xf, g, b):
    mu = jnp.mean(xf, axis=-1, keepdims=True)
    var = jnp.mean(jnp.square(xf - mu), axis=-1, keepdims=True)
    return (xf - mu) * lax.rsqrt(var + LN_EPS) * g + b


def _ada_body(c_ref, w_ref, b_ref, o_ref):
    o_ref[...] = jnp.dot(c_ref[...].astype(BF16), w_ref[...].astype(BF16),
                         preferred_element_type=F32) + b_ref[...]


def _ada(c_all, w_ada, b_ada, tn=1024):
    m, d = c_all.shape
    n = w_ada.shape[1]
    return pl.pallas_call(
        _ada_body,
        out_shape=jax.ShapeDtypeStruct((m, n), F32),
        grid=(n // tn,),
        in_specs=[pl.BlockSpec((m, d), lambda j: (0, 0)),
                  pl.BlockSpec((d, tn), lambda j: (0, j)),
                  pl.BlockSpec((1, tn), lambda j: (0, j))],
        out_specs=pl.BlockSpec((m, tn), lambda j: (0, j)),
        compiler_params=_params(("arbitrary",)),
        name="ada",
    )(c_all, w_ada, b_ada.reshape(1, n))


def _pack_rows(x_bf16):
    return pltpu.bitcast(x_bf16, U32)


def _unpack_rows(x_u32):
    return pltpu.bitcast(x_u32, BF16)


def _inproj_body(x_ref, adap_ref, w_ref, xs_ref, adas_ref, *rest, tiles_per_seq, d, tm,
                 n_side, side_steps):
    side_in = rest[:n_side]
    o_ref, os_ref = rest[n_side:n_side + 2]
    side_out = rest[n_side + 2:2 * n_side + 2]
    wb_ref = rest[2 * n_side + 2]
    i = pl.program_id(1)

    @pl.when(pl.program_id(0) * pl.num_programs(1) + i < side_steps)
    def _():
        for src, dst in zip(side_in, side_out):
            dst[...] = _pack_rows(src[...].astype(BF16))

    @pl.when(i == 0)
    def _():
        wb_ref[...] = w_ref[...].astype(BF16)
        us = xs_ref[...] * (1.0 + adas_ref[:, d:2 * d]) + adas_ref[:, 0:d]
        os_ref[...] = jnp.dot(us.astype(BF16), wb_ref[...], preferred_element_type=F32)

    b = i // tiles_per_seq
    sh1 = adap_ref[pl.ds(b, 1), 0:d]
    sc1 = adap_ref[pl.ds(b, 1), d:2 * d]
    for r in range(0, tm, ROW_CHUNK):
        rows = slice(r, r + ROW_CHUNK)
        u = x_ref[rows, :] * (1.0 + sc1) + sh1
        o_ref[rows, :] = jnp.dot(u.astype(BF16), wb_ref[...],
                                 preferred_element_type=F32).astype(BF16)


def _inproj(xp, xs, ada, w_in, side_ws, *, seq, prompt_row, tm=1024, tn=1024, side_steps=16):
    m, d = xp.shape
    ms = xs.shape[0]
    n = w_in.shape[1]
    ni = m // tm
    body = functools.partial(_inproj_body, tiles_per_seq=seq // tm, d=d, tm=tm,
                             n_side=len(side_ws), side_steps=side_steps)
    side_idx = lambda j, i: (jnp.minimum(j * ni + i, side_steps - 1), 0)
    side_in = [pl.BlockSpec((w.shape[0] // side_steps, w.shape[1]), side_idx) for w in side_ws]
    side_out = [pl.BlockSpec((w.shape[0] // side_steps // 2, w.shape[1]), side_idx)
                for w in side_ws]
    side_shape = [jax.ShapeDtypeStruct((w.shape[0] // 2, w.shape[1]), U32) for w in side_ws]
    out = pl.pallas_call(
        body,
        out_shape=[jax.ShapeDtypeStruct((m, n), BF16),
                   jax.ShapeDtypeStruct((ms, n), F32)] + side_shape,
        grid=(n // tn, ni),
        in_specs=[pl.BlockSpec((tm, d), lambda j, i: (i, 0)),
                  pl.BlockSpec((8, 2 * d), lambda j, i: (prompt_row // 8, 0)),
                  pl.BlockSpec((d, tn), lambda j, i: (0, j)),
                  _resident((ms, d)),
                  _resident((ms, 2 * d))] + side_in,
        out_specs=[pl.BlockSpec((tm, tn), lambda j, i: (i, j)),
                   pl.BlockSpec((ms, tn), lambda j, i: (0, j))] + side_out,
        scratch_shapes=[pltpu.VMEM((d, tn), BF16)],
        compiler_params=_params(("arbitrary", "arbitrary")),
        name="inproj",
    )(xp, ada, w_in, xs, ada, *side_ws)
    return out[0], out[1], out[2:]


def _sigmoid(x):
    return 0.5 * jnp.tanh(0.5 * x) + 0.5


def _mix_tail(slabs, g1, sc2, sh2, wpu_ref, wcu_ref, wo_ref, lng_ref, lnb_ref, alpha):
    wpu = _unpack_rows(wpu_ref[...])
    wcu = _unpack_rows(wcu_ref[...])
    wo = _unpack_rows(wo_ref[...])

    def up(s):
        pooled_bf, conv_bf = s[0]()
        return (jnp.dot(pooled_bf, wpu, preferred_element_type=F32),
                jnp.dot(conv_bf, wcu, preferred_element_type=F32))

    def gate(s, y):
        gp, gc = s[1]()
        return (_sigmoid(gp) * y[0] + _sigmoid(gc) * y[1]).astype(BF16)

    def finish(s, m_out):
        x1 = _layernorm(alpha * s[2]() + g1 * m_out, lng_ref[...], lnb_ref[...])
        s[3](x1, (x1 * (1.0 + sc2) + sh2).astype(BF16))

    y = [None] * len(slabs)
    mo = [None] * len(slabs)
    y[0] = up(slabs[0])
    for k, s in enumerate(slabs):
        if k + 1 < len(slabs):
            y[k + 1] = up(slabs[k + 1])
        mo[k] = jnp.dot(gate(s, y[k]), wo, preferred_element_type=F32)
        if k > 0:
            finish(slabs[k - 1], mo[k - 1])
    finish(slabs[-1], mo[-1])


def _mixer_body(proj_ref, hist_ref, x_ref, ada_ref, grp_ref, pscale_ref, convw_ref,
                wpu_ref, wcu_ref, wo_ref, lng_ref, lnb_ref, w1_ref, w2_ref,
                x1_ref, u2_ref, pst_ref, cst_ref, w1b_ref, w2b_ref, zz_ref, vv_ref, pm_ref,
                *, tm, tiles_per_seq, d, alpha):
    i = pl.program_id(0)

    tf = w1b_ref.shape[2]
    for fi in range(w1b_ref.shape[0]):
        w1b_ref[fi] = _pack_rows(w1_ref[:, fi * tf:(fi + 1) * tf].astype(BF16))
    w2b_ref[...] = _pack_rows(w2_ref[...].astype(BF16))

    tis = i % tiles_per_seq
    b = i // tiles_per_seq
    first = tis == 0
    p = d // 2
    gw = p // len(POOL_WINDOWS)
    h = HIST_ROWS

    zz_ref[0:h, :] = jnp.where(first, 0.0, hist_ref[:, 0:p].astype(F32))
    zz_ref[h:h + tm, :] = proj_ref[:, 0:p].astype(F32)
    vh = hist_ref[:, 3 * p:4 * p].astype(F32) * hist_ref[:, p:2 * p].astype(F32)
    vv_ref[0:h, :] = jnp.where(first, 0.0, vh)
    vv_ref[h:h + tm, :] = proj_ref[:, 3 * p:4 * p].astype(F32) * proj_ref[:, p:2 * p].astype(F32)
    pst_ref[0] = zz_ref[tm:tm + h, :]
    cst_ref[0] = vv_ref[tm:tm + h, :]

    g1 = ada_ref[pl.ds(b, 1), 2 * d:3 * d]
    sh2 = ada_ref[pl.ds(b, 1), 3 * d:4 * d]
    sc2 = ada_ref[pl.ds(b, 1), 4 * d:5 * d]

    n = MIXER_SLAB_ROWS

    def front(r0):
        rows = slice(r0, r0 + n)
        pos = tis * tm + r0 + lax.broadcasted_iota(jnp.int32, (n, 1), 0)
        for g, w in enumerate(POOL_WINDOWS):
            cols = slice(g * gw, (g + 1) * gw)
            zt = zz_ref[h + r0:h + r0 + n, cols]
            s = zt
            for k in range(1, w):
                s = s + zz_ref[h + r0 - k:h + r0 - k + n, cols]
            inv_cnt = 1.0 / jnp.minimum(w, pos + 1).astype(F32)
            pooled = (s * inv_cnt - zt).astype(BF16)
            og = jnp.dot(pooled, _unpack_rows(grp_ref[g * gw // 2:(g + 1) * gw // 2, :]),
                         preferred_element_type=F32)
            pm_ref[rows, cols] = (og * pscale_ref[:, cols]).astype(BF16)
        conv = convw_ref[0:1, :] * vv_ref[h + r0 - 2:h + r0 - 2 + n, :]
        conv = conv + convw_ref[1:2, :] * vv_ref[h + r0 - 1:h + r0 - 1 + n, :]
        conv = conv + convw_ref[2:3, :] * vv_ref[h + r0:h + r0 + n, :]
        bconv = (proj_ref[rows, 2 * p:3 * p].astype(F32) * conv).astype(BF16)
        return pm_ref[rows, :], bconv

    def gates(r0):
        rows = slice(r0, r0 + n)
        return (proj_ref[rows, 4 * p:4 * p + d].astype(F32),
                proj_ref[rows, 4 * p + d:4 * p + 2 * d].astype(F32))

    def store(r0, x1, u2):
        x1_ref[r0:r0 + n, :] = x1
        u2_ref[r0:r0 + n, :] = u2

    slabs = [(functools.partial(front, r0), functools.partial(gates, r0),
              functools.partial(lambda r: x_ref[r:r + n, :], r0), functools.partial(store, r0))
             for r0 in range(0, tm, n)]
    _mix_tail(slabs, g1, sc2, sh2, wpu_ref, wcu_ref, wo_ref, lng_ref, lnb_ref, alpha)


def _mixer(proj, xp, ada, grp_bf, pscale, convw, wpu_bf, wcu_bf, wo_bf, lng, lnb, w1, w2,
           *, seq, prompt_row, alpha, tf, tm=256):
    m, d = xp.shape
    p = d // 2
    nseq = m // seq
    h = HIST_ROWS
    steps = m // tm
    dff = w1.shape[1]
    body = functools.partial(_mixer_body, tm=tm, tiles_per_seq=seq // tm, d=d, alpha=alpha)
    return pl.pallas_call(
        body,
        out_shape=[jax.ShapeDtypeStruct((m, d), F32),
                   jax.ShapeDtypeStruct((m, d), BF16),
                   jax.ShapeDtypeStruct((nseq, h, p), F32),
                   jax.ShapeDtypeStruct((nseq, h, p), F32),
                   jax.ShapeDtypeStruct((dff // tf, d // 2, tf), U32),
                   jax.ShapeDtypeStruct((dff // 2, d), U32)],
        grid=(steps,),
        in_specs=[pl.BlockSpec((tm, proj.shape[1]), lambda i: (i, 0)),
                  pl.BlockSpec((h, proj.shape[1]),
                               lambda i: (jnp.maximum(i * (tm // h) - 1, 0), 0)),
                  pl.BlockSpec((tm, d), lambda i: (i, 0)),
                  pl.BlockSpec((8, N_ADA * d), lambda i: (prompt_row // 8, 0)),
                  _resident(grp_bf.shape),
                  _resident(pscale.shape),
                  _resident(convw.shape),
                  _resident(wpu_bf.shape),
                  _resident(wcu_bf.shape),
                  _resident(wo_bf.shape),
                  _resident(lng.shape),
                  _resident(lnb.shape),
                  pl.BlockSpec((d // steps, dff), lambda i: (i, 0)),
                  pl.BlockSpec((dff // steps, d), lambda i: (i, 0))],
        out_specs=[pl.BlockSpec((tm, d), lambda i: (i, 0)),
                   pl.BlockSpec((tm, d), lambda i: (i, 0)),
                   pl.BlockSpec((1, h, p), lambda i: (i // (seq // tm), 0, 0)),
                   pl.BlockSpec((1, h, p), lambda i: (i // (seq // tm), 0, 0)),
                   pl.BlockSpec((dff // tf, d // steps // 2, tf), lambda i: (0, i, 0)),
                   pl.BlockSpec((dff // steps // 2, d), lambda i: (i, 0))],
        scratch_shapes=[pltpu.VMEM((tm + h, p), F32),
                        pltpu.VMEM((tm + h, p), F32),
                        pltpu.VMEM((tm, p), BF16)],
        compiler_params=_params(("arbitrary",)),
        name="mixer",
    )(proj, proj, xp, ada, grp_bf, pscale, convw, wpu_bf, wcu_bf, wo_bf, lng, lnb, w1, w2)


def _mixer_step_body(proj_ref, pool_ref, cstate_ref, x_ref, ada_ref, grp_ref, pscale_ref,
                     convw_ref, wpu_ref, wcu_ref, wo_ref, lng_ref, lnb_ref,
                     x1_ref, u2_ref, npool_ref, nconv_ref, pm_ref, *, d, alpha):
    p = d // 2
    gw = p // len(POOL_WINDOWS)
    z = proj_ref[:, 0:p]
    for g, w in enumerate(POOL_WINDOWS):
        cols = slice(g * gw, (g + 1) * gw)
        s = z[:, cols]
        for k in range(1, w):
            s = s + pool_ref[POOL_HIST - k, :, cols]
        inv_cnt = 1.0 / min(w, PAST_LEN + 1)
        pooled = (s * inv_cnt - z[:, cols]).astype(BF16)
        og = jnp.dot(pooled, _unpack_rows(grp_ref[g * gw // 2:(g + 1) * gw // 2, :]),
                         preferred_element_type=F32)
        pm_ref[:, cols] = (og * pscale_ref[:, cols]).astype(BF16)
    for r in range(POOL_HIST - 1):
        npool_ref[r] = pool_ref[r + 1]
    npool_ref[POOL_HIST - 1] = z

    xc = proj_ref[:, p:2 * p]
    bc = proj_ref[:, 2 * p:3 * p]
    cc = proj_ref[:, 3 * p:4 * p]
    v = cc * xc
    conv = convw_ref[0:1, :] * cstate_ref[0]
    conv = conv + convw_ref[1:2, :] * cstate_ref[1]
    conv = conv + convw_ref[2:3, :] * v
    nconv_ref[0] = cstate_ref[1]
    nconv_ref[1] = v

    gp = proj_ref[:, 4 * p:4 * p + d]
    gc = proj_ref[:, 4 * p + d:4 * p + 2 * d]
    g1 = ada_ref[:, 2 * d:3 * d]
    sh2 = ada_ref[:, 3 * d:4 * d]
    sc2 = ada_ref[:, 4 * d:5 * d]

    def store(x1, u2):
        x1_ref[...] = x1
        u2_ref[...] = u2

    slab = (lambda: (pm_ref[...], (bc * conv).astype(BF16)), lambda: (gp, gc),
            lambda: x_ref[...], store)
    _mix_tail([slab], g1, sc2, sh2, wpu_ref, wcu_ref, wo_ref, lng_ref, lnb_ref, alpha)


def _mixer_step(proj_s, pool_state, conv_state, xs, ada, grp_bf, pscale, convw, wpu_bf, wcu_bf,
                wo_bf, lng, lnb, *, alpha):
    ms, d = xs.shape
    p = d // 2
    body = functools.partial(_mixer_step_body, d=d, alpha=alpha)
    return pl.pallas_call(
        body,
        out_shape=[jax.ShapeDtypeStruct((ms, d), F32),
                   jax.ShapeDtypeStruct((ms, d), BF16),
                   jax.ShapeDtypeStruct(pool_state.shape, F32),
                   jax.ShapeDtypeStruct(conv_state.shape, F32)],
        grid=(1,),
        in_specs=[_resident(proj_s.shape),
                  _resident(pool_state.shape),
                  _resident(conv_state.shape),
                  _resident((ms, d)),
                  _resident((ms, N_ADA * d)),
                  _resident(grp_bf.shape),
                  _resident(pscale.shape),
                  _resident(convw.shape),
                  _resident(wpu_bf.shape),
                  _resident(wcu_bf.shape),
                  _resident(wo_bf.shape),
                  _resident(lng.shape),
                  _resident(lnb.shape)],
        out_specs=[_resident((ms, d)),
                   _resident((ms, d)),
                   _resident(pool_state.shape),
                   _resident(conv_state.shape)],
        scratch_shapes=[pltpu.VMEM((ms, p), BF16)],
        compiler_params=_params(("arbitrary",)),
        name="mixer_step",
    )(proj_s, pool_state, conv_state, xs, ada, grp_bf, pscale, convw, wpu_bf, wcu_bf, wo_bf, lng, lnb)


def _mlp_body(u2_ref, w1_ref, b1_ref, w2_ref, x1_ref, g2_ref, b2_ref, lng_ref, lnb_ref,
              u2s_ref, x1s_ref, g2s_ref, o_ref, os_ref, acc_ref, accs_ref,
              *, nm, nf, ne, te, tm, alpha, rows_per_cond):
    m = pl.program_id(0)
    f = pl.program_id(1)
    slot = m % 2

    def chunk(u2):
        hmid = jnp.dot(u2, _unpack_rows(w1_ref[0]), preferred_element_type=F32) + b1_ref[...]
        hmid = jnp.square(jnp.maximum(hmid, 0.0))
        return jnp.dot(hmid.astype(BF16), _unpack_rows(w2_ref[...]),
                       preferred_element_type=F32)

    def accumulate(first, between=None):
        n_chunks = tm // ROW_CHUNK
        for c in range(n_chunks):
            if between is not None:
                between(c, n_chunks)
            rows = slice(c * ROW_CHUNK, (c + 1) * ROW_CHUNK)
            part = chunk(u2_ref[rows, :])
            if first:
                acc_ref[slot, rows, :] = part
            else:
                acc_ref[slot, rows, :] += part

    def accumulate_singles(first):
        part = chunk(u2s_ref[...])
        if first:
            accs_ref[...] = part
        else:
            accs_ref[...] += part

    def finish(x1, g2, acc):
        return _layernorm(alpha * x1 + g2 * (acc + b2_ref[...]), lng_ref[...], lnb_ref[...])

    def epilogue(piece=0, n_pieces=1):
        tp = te // n_pieces
        base = pl.multiple_of((f - 1) * te + piece * tp, tp)
        g2 = g2_ref[pl.ds(((m - 1) * tm) // rows_per_cond, 1), :]
        o_ref[piece * tp:(piece + 1) * tp, :] = finish(
            x1_ref[piece * tp:(piece + 1) * tp, :], g2, acc_ref[1 - slot, pl.ds(base, tp), :])

    has_main = m < nm
    has_epi = jnp.logical_and(m >= 1, jnp.logical_and(f >= 1, f <= ne))

    @pl.when(jnp.logical_and(has_main, f == 0))
    def _():
        accumulate(True)
        pl.when(m == 0)(functools.partial(accumulate_singles, True))

    @pl.when(jnp.logical_and(has_main, has_epi))
    def _():
        accumulate(False, between=epilogue)

    @pl.when(jnp.logical_and(has_main, jnp.logical_and(f > 0, jnp.logical_not(has_epi))))
    def _():
        accumulate(False)
        pl.when(m == 0)(functools.partial(accumulate_singles, False))

    @pl.when(jnp.logical_and(jnp.logical_not(has_main), has_epi))
    def _():
        epilogue()

    @pl.when(jnp.logical_and(m == 1, f == 1))
    def _():
        os_ref[...] = finish(x1s_ref[...], g2s_ref[...], accs_ref[...])


def _mlp(u2, x1, u2s, x1s, ada, w1b, b1, w2b, b2, lng, lnb, *, alpha, rows_per_cond,
         cond_row, tm, te):
    m, d = u2.shape
    ms = u2s.shape[0]
    nf, _, tf = w1b.shape
    nm = m // tm
    ne = tm // te
    assert ne < nf
    g2_col = N_ADA - 1
    body = functools.partial(_mlp_body, nm=nm, nf=nf, ne=ne, te=te, tm=tm, alpha=alpha,
                             rows_per_cond=rows_per_cond)
    fw = lambda mi, f: jnp.where(mi == nm, nf - 1, f)
    ep = lambda mi, f: (jnp.where(mi == 0, 0, (mi - 1) * ne + jnp.clip(f - 1, 0, ne - 1)), 0)
    return pl.pallas_call(
        body,
        out_shape=[jax.ShapeDtypeStruct((m, d), F32),
                   jax.ShapeDtypeStruct((ms, d), F32)],
        grid=(nm + 1, nf),
        in_specs=[pl.BlockSpec((tm, d), lambda mi, f: (jnp.minimum(mi, nm - 1), 0)),
                  pl.BlockSpec((1, d // 2, tf), lambda mi, f: (fw(mi, f), 0, 0)),
                  pl.BlockSpec((1, tf), lambda mi, f: (0, fw(mi, f))),
                  pl.BlockSpec((tf // 2, d), lambda mi, f: (fw(mi, f), 0)),
                  pl.BlockSpec((te, d), ep),
                  pl.BlockSpec((8, d), lambda mi, f: (cond_row // 8, g2_col)),
                  _resident((1, d)),
                  _resident((1, d)),
                  _resident((1, d)),
                  _resident((ms, d)),
                  _resident((ms, d)),
                  pl.BlockSpec((ms, d), lambda mi, f: (0, g2_col),
                               pipeline_mode=pl.Buffered(1))],
        out_specs=[pl.BlockSpec((te, d), ep),
                   pl.BlockSpec((ms, d), lambda mi, f: (0, 0))],
        scratch_shapes=[pltpu.VMEM((2, tm, d), F32),
                        pltpu.VMEM((ms, d), F32)],
        compiler_params=_params(("arbitrary", "arbitrary")),
        name="mlp",
    )(u2, w1b, b1.reshape(1, -1), w2b, x1, ada, b2.reshape(1, d), lng, lnb, u2s, x1s, ada)


def kernel(x_prompt, x_sample, state_pool, state_conv, c_prompt, c_sample, w_ada, b_ada, w_in,
           pool_grp_w, pool_scale, conv_w, w_pool_up, w_conv_up, w_o, ln1_g, ln1_b, w_ff1,
           b_ff1, w_ff2, b_ff2, ln2_g, ln2_b):
    depth = w_ada.shape[0]
    nb, seq, d = x_prompt.shape
    ms, dec_seq, _ = x_sample.shape
    assert dec_seq == 1
    p = d // 2
    alpha = (2 * depth) ** 0.25

    prompt_row = ms
    pad = (-(ms + nb)) % 8
    c_all = jnp.concatenate([c_sample, c_prompt, jnp.zeros((pad, d), F32)], axis=0)

    xp = x_prompt.reshape(nb * seq, d)
    xs = x_sample.reshape(ms, d)
    pool_p, conv_p, pool_s, conv_s = [], [], [], []
    for l in range(depth):
        ada = _ada(c_all, w_ada[l], b_ada[l])
        pscale = pool_scale[l].reshape(1, p)
        lng1, lnb1 = ln1_g[l].reshape(1, d), ln1_b[l].reshape(1, d)
        lng2, lnb2 = ln2_g[l].reshape(1, d), ln2_b[l].reshape(1, d)

        proj_p, proj_s, (grp_bf, wpu_bf, wcu_bf, wo_bf) = _inproj(
            xp, xs, ada, w_in[l],
            [pool_grp_w[l].reshape(p, -1), w_pool_up[l], w_conv_up[l], w_o[l]],
            seq=seq, prompt_row=prompt_row)

        x1p, u2p, pst, cst, w1b, w2b = _mixer(
            proj_p, xp, ada, grp_bf, pscale, conv_w[l], wpu_bf, wcu_bf, wo_bf, lng1, lnb1,
            w_ff1[l], w_ff2[l], seq=seq, prompt_row=prompt_row, alpha=alpha, tf=MLP_FF_CHUNK)
        x1s, u2s, npool, nconv = _mixer_step(
            proj_s, jnp.transpose(state_pool[l], (1, 0, 2)),
            jnp.transpose(state_conv[l], (1, 0, 2)), xs, ada, grp_bf, pscale, conv_w[l],
            wpu_bf, wcu_bf, wo_bf, lng1, lnb1, alpha=alpha)

        xp, xs = _mlp(u2p, x1p, u2s, x1s, ada, w1b, b_ff1[l], w2b, b_ff2[l], lng2, lnb2,
                      alpha=alpha, rows_per_cond=seq, cond_row=prompt_row,
                      tm=MLP_TOKEN_TILE, te=MLP_EPILOGUE_ROWS)

        pool_p.append(pst[:, HIST_ROWS - POOL_HIST:, :])
        conv_p.append(cst[:, HIST_ROWS - CONV_HIST:, :])
        pool_s.append(jnp.transpose(npool, (1, 0, 2)))
        conv_s.append(jnp.transpose(nconv, (1, 0, 2)))

    return (xp.reshape(nb, seq, d), xs.reshape(ms, dec_seq, d),
            jnp.stack(pool_p), jnp.stack(conv_p), jnp.stack(pool_s), jnp.stack(conv_s))
```

```python
import functools

import jax
import jax.numpy as jnp
from jax import lax
from jax.experimental import pallas as pl
from jax.experimental.pallas import tpu as pltpu

F32 = jnp.float32
BF16 = jnp.bfloat16
U32 = jnp.uint32

POOL_WINDOWS = (2, 4, 8, 16)
POOL_HIST = max(POOL_WINDOWS) - 1
CONV_K = 3
CONV_HIST = CONV_K - 1
N_ADA = 6
PAST_LEN = 16384
LN_EPS = 1e-5

VMEM_LIMIT_BYTES_V7X = 56 * 1024 * 1024
HIST_ROWS = 16
ROW_CHUNK = 256
MIXER_SLAB_ROWS = 256
MLP_TOKEN_TILE = 1024
MLP_FF_CHUNK = 1024
MLP_EPILOGUE_ROWS = 256


def _resident(shape):
    zeros = (0,) * len(shape)
    return pl.BlockSpec(shape, lambda *_: zeros, pipeline_mode=pl.Buffered(1))


def _params(semantics):
    return pltpu.CompilerParams(dimension_semantics=semantics,
                                vmem_limit_bytes=VMEM_LIMIT_BYTES_V7X)


def _layernorm(xf, g, b):
    mu = jnp.mean(xf, axis=-1, keepdims=True)
    var = jnp.mean(jnp.square(xf - mu), axis=-1, keepdims=True)
    return (xf - mu) * lax.rsqrt(var + LN_EPS) * g + b


def _ada_body(c_ref, w_ref, b_ref, o_ref):
    o_ref[...] = jnp.dot(c_ref[...].astype(BF16), w_ref[...].astype(BF16),
                         preferred_element_type=F32) + b_ref[...]


def _ada(c_all, w_ada, b_ada, tn=1024):
    m, d = c_all.shape
    n = w_ada.shape[1]
    return pl.pallas_call(
        _ada_body,
        out_shape=jax.ShapeDtypeStruct((m, n), F32),
        grid=(n // tn,),
        in_specs=[pl.BlockSpec((m, d), lambda j: (0, 0)),
                  pl.BlockSpec((d, tn), lambda j: (0, j)),
                  pl.BlockSpec((1, tn), lambda j: (0, j))],
        out_specs=pl.BlockSpec((m, tn), lambda j: (0, j)),
        compiler_params=_params(("arbitrary",)),
        name="ada",
    )(c_all, w_ada, b_ada.reshape(1, n))


def _pack_rows(x_bf16):
    return pltpu.bitcast(x_bf16, U32)


def _unpack_rows(x_u32):
    return pltpu.bitcast(x_u32, BF16)


def _inproj_body(x_hbm, adap_ref, w_ref, xs_ref, adas_ref, *rest, tiles_per_seq, d, tm, tn,
                 ni, n_side):
    side_in = rest[:n_side]
    o_hbm, os_ref = rest[n_side:n_side + 2]
    side_out = rest[n_side + 2:2 * n_side + 2]
    wb_ref, xbuf, obuf, in_sem, out_sem = rest[2 * n_side + 2:]
    j = pl.program_id(0)
    nj = pl.num_programs(0)

    def x_copy(i, slot):
        return pltpu.make_async_copy(x_hbm.at[pl.ds(i * tm, tm), :], xbuf.at[slot],
                                     in_sem.at[slot])

    def o_copy(i, slot):
        return pltpu.make_async_copy(obuf.at[slot],
                                     o_hbm.at[pl.ds(i * tm, tm), pl.ds(j * tn, tn)],
                                     out_sem.at[slot])

    @pl.when(j == 0)
    def _():
        x_copy(0, 0).start()

    for src, dst in zip(side_in, side_out):
        dst[...] = _pack_rows(src[...].astype(BF16))

    wb_ref[...] = w_ref[...].astype(BF16)
    us = xs_ref[...] * (1.0 + adas_ref[:, d:2 * d]) + adas_ref[:, 0:d]
    os_ref[...] = jnp.dot(us.astype(BF16), wb_ref[...], preferred_element_type=F32)

    def tile(i, carry):
        slot = i % 2

        @pl.when(i + 1 < ni)
        def _():
            x_copy(i + 1, 1 - slot).start()

        x_copy(i, slot).wait()

        @pl.when(i >= 2)
        def _():
            o_copy(i - 2, slot).wait()

        b = i // tiles_per_seq
        sh1 = adap_ref[pl.ds(b, 1), 0:d]
        sc1 = adap_ref[pl.ds(b, 1), d:2 * d]
        for r in range(0, tm, ROW_CHUNK):
            rows = slice(r, r + ROW_CHUNK)
            u = xbuf[slot, rows, :] * (1.0 + sc1) + sh1
            obuf[slot, rows, :] = jnp.dot(u.astype(BF16), wb_ref[...],
                                          preferred_element_type=F32).astype(BF16)
        o_copy(i, slot).start()
        return carry

    lax.fori_loop(0, ni, tile, 0)

    @pl.when(j + 1 < nj)
    def _():
        x_copy(0, 0).start()

    o_copy(ni - 2, (ni - 2) % 2).wait()
    o_copy(ni - 1, (ni - 1) % 2).wait()


def _inproj(xp, xs, ada, w_in, side_ws, *, seq, prompt_row, tm=512, tn=1024):
    m, d = xp.shape
    ms = xs.shape[0]
    n = w_in.shape[1]
    ni = m // tm
    nj = n // tn
    assert ni % 2 == 0 and ni >= 2
    body = functools.partial(_inproj_body, tiles_per_seq=seq // tm, d=d, tm=tm, tn=tn, ni=ni,
                             n_side=len(side_ws))
    side_in = [pl.BlockSpec((w.shape[0] // nj, w.shape[1]), lambda j: (j, 0)) for w in side_ws]
    side_out = [pl.BlockSpec((w.shape[0] // nj // 2, w.shape[1]), lambda j: (j, 0))
                for w in side_ws]
    side_shape = [jax.ShapeDtypeStruct((w.shape[0] // 2, w.shape[1]), U32) for w in side_ws]
    out = pl.pallas_call(
        body,
        out_shape=[jax.ShapeDtypeStruct((m, n), BF16),
                   jax.ShapeDtypeStruct((ms, n), F32)] + side_shape,
        grid=(nj,),
        in_specs=[pl.BlockSpec(memory_space=pl.ANY),
                  pl.BlockSpec((8, 2 * d), lambda j: (prompt_row // 8, 0)),
                  pl.BlockSpec((d, tn), lambda j: (0, j)),
                  _resident((ms, d)),
                  _resident((ms, 2 * d))] + side_in,
        out_specs=[pl.BlockSpec(memory_space=pl.ANY),
                   pl.BlockSpec((ms, tn), lambda j: (0, j))] + side_out,
        scratch_shapes=[pltpu.VMEM((d, tn), BF16),
                        pltpu.VMEM((2, tm, d), F32),
                        pltpu.VMEM((2, tm, tn), BF16),
                        pltpu.SemaphoreType.DMA((2,)),
                        pltpu.SemaphoreType.DMA((2,))],
        compiler_params=_params(("arbitrary",)),
        name="inproj",
    )(xp, ada, w_in, xs, ada, *side_ws)
    return out[0], out[1], out[2:]


def _sigmoid(x):
    return 0.5 * jnp.tanh(0.5 * x) + 0.5


def _mix_tail(slabs, g1, sc2, sh2, wpu_ref, wcu_ref, wo_ref, lng_ref, lnb_ref, alpha):
    wpu = _unpack_rows(wpu_ref[...])
    wcu = _unpack_rows(wcu_ref[...])
    wo = _unpack_rows(wo_ref[...])

    def up(s):
        pooled_bf, conv_bf = s[0]()
        return (jnp.dot(pooled_bf, wpu, preferred_element_type=F32),
                jnp.dot(conv_bf, wcu, preferred_element_type=F32))

    def gate(s, y):
        gp, gc = s[1]()
        return (_sigmoid(gp) * y[0] + _sigmoid(gc) * y[1]).astype(BF16)

    def finish(s, m_out):
        x1 = _layernorm(alpha * s[2]() + g1 * m_out, lng_ref[...], lnb_ref[...])
        s[3](x1, (x1 * (1.0 + sc2) + sh2).astype(BF16))

    y = [None] * len(slabs)
    mo = [None] * len(slabs)
    y[0] = up(slabs[0])
    for k, s in enumerate(slabs):
        if k + 1 < len(slabs):
            y[k + 1] = up(slabs[k + 1])
        mo[k] = jnp.dot(gate(s, y[k]), wo, preferred_element_type=F32)
        if k > 0:
            finish(slabs[k - 1], mo[k - 1])
    finish(slabs[-1], mo[-1])


def _mixer_body(proj_ref, hist_ref, x_ref, ada_ref, grp_ref, pscale_ref, convw_ref,
                wpu_ref, wcu_ref, wo_ref, lng_ref, lnb_ref, w1_ref, w2_ref,
                x1_ref, u2_ref, pst_ref, cst_ref, w1b_ref, w2b_ref, zz_ref, vv_ref, pm_ref,
                *, tm, tiles_per_seq, d, alpha):
    i = pl.program_id(0)

    tf = w1b_ref.shape[2]
    for fi in range(w1b_ref.shape[0]):
        w1b_ref[fi] = _pack_rows(w1_ref[:, fi * tf:(fi + 1) * tf].astype(BF16))
    w2b_ref[...] = _pack_rows(w2_ref[...].astype(BF16))

    tis = i % tiles_per_seq
    b = i // tiles_per_seq
    first = tis == 0
    p = d // 2
    gw = p // len(POOL_WINDOWS)
    h = HIST_ROWS

    zz_ref[0:h, :] = jnp.where(first, 0.0, hist_ref[:, 0:p].astype(F32))
    zz_ref[h:h + tm, :] = proj_ref[:, 0:p].astype(F32)
    vh = hist_ref[:, 3 * p:4 * p].astype(F32) * hist_ref[:, p:2 * p].astype(F32)
    vv_ref[0:h, :] = jnp.where(first, 0.0, vh)
    vv_ref[h:h + tm, :] = proj_ref[:, 3 * p:4 * p].astype(F32) * proj_ref[:, p:2 * p].astype(F32)
    pst_ref[0] = zz_ref[tm:tm + h, :]
    cst_ref[0] = vv_ref[tm:tm + h, :]

    g1 = ada_ref[pl.ds(b, 1), 2 * d:3 * d]
    sh2 = ada_ref[pl.ds(b, 1), 3 * d:4 * d]
    sc2 = ada_ref[pl.ds(b, 1), 4 * d:5 * d]

    n = MIXER_SLAB_ROWS

    def front(r0):
        rows = slice(r0, r0 + n)
        pos = tis * tm + r0 + lax.broadcasted_iota(jnp.int32, (n, 1), 0)
        for g, w in enumerate(POOL_WINDOWS):
            cols = slice(g * gw, (g + 1) * gw)
            zt = zz_ref[h + r0:h + r0 + n, cols]
            s = zt
            for k in range(1, w):
                s = s + zz_ref[h + r0 - k:h + r0 - k + n, cols]
            inv_cnt = 1.0 / jnp.minimum(w, pos + 1).astype(F32)
            pooled = (s * inv_cnt - zt).astype(BF16)
            og = jnp.dot(pooled, _unpack_rows(grp_ref[g * gw // 2:(g + 1) * gw // 2, :]),
                         preferred_element_type=F32)
            pm_ref[rows, cols] = (og * pscale_ref[:, cols]).astype(BF16)
        conv = convw_ref[0:1, :] * vv_ref[h + r0 - 2:h + r0 - 2 + n, :]
        conv = conv + convw_ref[1:2, :] * vv_ref[h + r0 - 1:h + r0 - 1 + n, :]
        conv = conv + convw_ref[2:3, :] * vv_ref[h + r0:h + r0 + n, :]
        bconv = (proj_ref[rows, 2 * p:3 * p].astype(F32) * conv).astype(BF16)
        return pm_ref[rows, :], bconv

    def gates(r0):
        rows = slice(r0, r0 + n)
        return (proj_ref[rows, 4 * p:4 * p + d].astype(F32),
                proj_ref[rows, 4 * p + d:4 * p + 2 * d].astype(F32))

    def store(r0, x1, u2):
        x1_ref[r0:r0 + n, :] = x1
        u2_ref[r0:r0 + n, :] = u2

    slabs = [(functools.partial(front, r0), functools.partial(gates, r0),
              functools.partial(lambda r: x_ref[r:r + n, :], r0), functools.partial(store, r0))
             for r0 in range(0, tm, n)]
    _mix_tail(slabs, g1, sc2, sh2, wpu_ref, wcu_ref, wo_ref, lng_ref, lnb_ref, alpha)


def _mixer(proj, xp, ada, grp_bf, pscale, convw, wpu_bf, wcu_bf, wo_bf, lng, lnb, w1, w2,
           *, seq, prompt_row, alpha, tf, tm=256):
    m, d = xp.shape
    p = d // 2
    nseq = m // seq
    h = HIST_ROWS
    steps = m // tm
    dff = w1.shape[1]
    body = functools.partial(_mixer_body, tm=tm, tiles_per_seq=seq // tm, d=d, alpha=alpha)
    return pl.pallas_call(
        body,
        out_shape=[jax.ShapeDtypeStruct((m, d), F32),
                   jax.ShapeDtypeStruct((m, d), BF16),
                   jax.ShapeDtypeStruct((nseq, h, p), F32),
                   jax.ShapeDtypeStruct((nseq, h, p), F32),
                   jax.ShapeDtypeStruct((dff // tf, d // 2, tf), U32),
                   jax.ShapeDtypeStruct((dff // 2, d), U32)],
        grid=(steps,),
        in_specs=[pl.BlockSpec((tm, proj.shape[1]), lambda i: (i, 0)),
                  pl.BlockSpec((h, proj.shape[1]),
                               lambda i: (jnp.maximum(i * (tm // h) - 1, 0), 0)),
                  pl.BlockSpec((tm, d), lambda i: (i, 0)),
                  pl.BlockSpec((8, N_ADA * d), lambda i: (prompt_row // 8, 0)),
                  _resident(grp_bf.shape),
                  _resident(pscale.shape),
                  _resident(convw.shape),
                  _resident(wpu_bf.shape),
                  _resident(wcu_bf.shape),
                  _resident(wo_bf.shape),
                  _resident(lng.shape),
                  _resident(lnb.shape),
                  pl.BlockSpec((d // steps, dff), lambda i: (i, 0)),
                  pl.BlockSpec((dff // steps, d), lambda i: (i, 0))],
        out_specs=[pl.BlockSpec((tm, d), lambda i: (i, 0)),
                   pl.BlockSpec((tm, d), lambda i: (i, 0)),
                   pl.BlockSpec((1, h, p), lambda i: (i // (seq // tm), 0, 0)),
                   pl.BlockSpec((1, h, p), lambda i: (i // (seq // tm), 0, 0)),
                   pl.BlockSpec((dff // tf, d // steps // 2, tf), lambda i: (0, i, 0)),
                   pl.BlockSpec((dff // steps // 2, d), lambda i: (i, 0))],
        scratch_shapes=[pltpu.VMEM((tm + h, p), F32),
                        pltpu.VMEM((tm + h, p), F32),
                        pltpu.VMEM((tm, p), BF16)],
        compiler_params=_params(("arbitrary",)),
        name="mixer",
    )(proj, proj, xp, ada, grp_bf, pscale, convw, wpu_bf, wcu_bf, wo_bf, lng, lnb, w1, w2)


def _mixer_step_body(proj_ref, pool_ref, cstate_ref, x_ref, ada_ref, grp_ref, pscale_ref,
                     convw_ref, wpu_ref, wcu_ref, wo_ref, lng_ref, lnb_ref,
                     x1_ref, u2_ref, npool_ref, nconv_ref, pm_ref, *, d, alpha):
    p = d // 2
    gw = p // len(POOL_WINDOWS)
    z = proj_ref[:, 0:p]
    for g, w in enumerate(POOL_WINDOWS):
        cols = slice(g * gw, (g + 1) * gw)
        s = z[:, cols]
        for k in range(1, w):
            s = s + pool_ref[POOL_HIST - k, :, cols]
        inv_cnt = 1.0 / min(w, PAST_LEN + 1)
        pooled = (s * inv_cnt - z[:, cols]).astype(BF16)
        og = jnp.dot(pooled, _unpack_rows(grp_ref[g * gw // 2:(g + 1) * gw // 2, :]),
                         preferred_element_type=F32)
        pm_ref[:, cols] = (og * pscale_ref[:, cols]).astype(BF16)
    for r in range(POOL_HIST - 1):
        npool_ref[r] = pool_ref[r + 1]
    npool_ref[POOL_HIST - 1] = z

    xc = proj_ref[:, p:2 * p]
    bc = proj_ref[:, 2 * p:3 * p]
    cc = proj_ref[:, 3 * p:4 * p]
    v = cc * xc
    conv = convw_ref[0:1, :] * cstate_ref[0]
    conv = conv + convw_ref[1:2, :] * cstate_ref[1]
    conv = conv + convw_ref[2:3, :] * v
    nconv_ref[0] = cstate_ref[1]
    nconv_ref[1] = v

    gp = proj_ref[:, 4 * p:4 * p + d]
    gc = proj_ref[:, 4 * p + d:4 * p + 2 * d]
    g1 = ada_ref[:, 2 * d:3 * d]
    sh2 = ada_ref[:, 3 * d:4 * d]
    sc2 = ada_ref[:, 4 * d:5 * d]

    def store(x1, u2):
        x1_ref[...] = x1
        u2_ref[...] = u2

    slab = (lambda: (pm_ref[...], (bc * conv).astype(BF16)), lambda: (gp, gc),
            lambda: x_ref[...], store)
    _mix_tail([slab], g1, sc2, sh2, wpu_ref, wcu_ref, wo_ref, lng_ref, lnb_ref, alpha)


def _mixer_step(proj_s, pool_state, conv_state, xs, ada, grp_bf, pscale, convw, wpu_bf, wcu_bf,
                wo_bf, lng, lnb, *, alpha):
    ms, d = xs.shape
    p = d // 2
    body = functools.partial(_mixer_step_body, d=d, alpha=alpha)
    return pl.pallas_call(
        body,
        out_shape=[jax.ShapeDtypeStruct((ms, d), F32),
                   jax.ShapeDtypeStruct((ms, d), BF16),
                   jax.ShapeDtypeStruct(pool_state.shape, F32),
                   jax.ShapeDtypeStruct(conv_state.shape, F32)],
        grid=(1,),
        in_specs=[_resident(proj_s.shape),
                  _resident(pool_state.shape),
                  _resident(conv_state.shape),
                  _resident((ms, d)),
                  _resident((ms, N_ADA * d)),
                  _resident(grp_bf.shape),
                  _resident(pscale.shape),
                  _resident(convw.shape),
                  _resident(wpu_bf.shape),
                  _resident(wcu_bf.shape),
                  _resident(wo_bf.shape),
                  _resident(lng.shape),
                  _resident(lnb.shape)],
        out_specs=[_resident((ms, d)),
                   _resident((ms, d)),
                   _resident(pool_state.shape),
                   _resident(conv_state.shape)],
        scratch_shapes=[pltpu.VMEM((ms, p), BF16)],
        compiler_params=_params(("arbitrary",)),
        name="mixer_step",
    )(proj_s, pool_state, conv_state, xs, ada, grp_bf, pscale, convw, wpu_bf, wcu_bf, wo_bf, lng, lnb)


def _mlp_body(u2_ref, w1_ref, b1_ref, w2_ref, x1_ref, g2_ref, b2_ref, lng_ref, lnb_ref,
              u2s_ref, x1s_ref, g2s_ref, o_ref, os_ref, acc_ref, accs_ref,
              *, nm, nf, ne, te, tm, alpha, rows_per_cond):
    m = pl.program_id(0)
    f = pl.program_id(1)
    slot = m % 2

    def chunk(u2):
        hmid = jnp.dot(u2, _unpack_rows(w1_ref[0]), preferred_element_type=F32) + b1_ref[...]
        hmid = jnp.square(jnp.maximum(hmid, 0.0))
        return jnp.dot(hmid.astype(BF16), _unpack_rows(w2_ref[...]),
                       preferred_element_type=F32)

    def accumulate(first, between=None):
        n_chunks = tm // ROW_CHUNK
        for c in range(n_chunks):
            if between is not None:
                between(c, n_chunks)
            rows = slice(c * ROW_CHUNK, (c + 1) * ROW_CHUNK)
            part = chunk(u2_ref[rows, :])
            if first:
                acc_ref[slot, rows, :] = part
            else:
                acc_ref[slot, rows, :] += part

    def accumulate_singles(first):
        part = chunk(u2s_ref[...])
        if first:
            accs_ref[...] = part
        else:
            accs_ref[...] += part

    def finish(x1, g2, acc):
        return _layernorm(alpha * x1 + g2 * (acc + b2_ref[...]), lng_ref[...], lnb_ref[...])

    def epilogue(piece=0, n_pieces=1):
        tp = te // n_pieces
        base = pl.multiple_of((f - 1) * te + piece * tp, tp)
        g2 = g2_ref[pl.ds(((m - 1) * tm) // rows_per_cond, 1), :]
        o_ref[piece * tp:(piece + 1) * tp, :] = finish(
            x1_ref[piece * tp:(piece + 1) * tp, :], g2, acc_ref[1 - slot, pl.ds(base, tp), :])

    has_main = m < nm
    has_epi = jnp.logical_and(m >= 1, jnp.logical_and(f >= 1, f <= ne))

    @pl.when(jnp.logical_and(has_main, f == 0))
    def _():
        accumulate(True)
        pl.when(m == 0)(functools.partial(accumulate_singles, True))

    @pl.when(jnp.logical_and(has_main, has_epi))
    def _():
        accumulate(False, between=epilogue)

    @pl.when(jnp.logical_and(has_main, jnp.logical_and(f > 0, jnp.logical_not(has_epi))))
    def _():
        accumulate(False)
        pl.when(m == 0)(functools.partial(accumulate_singles, False))

    @pl.when(jnp.logical_and(jnp.logical_not(has_main), has_epi))
    def _():
        epilogue()

    @pl.when(jnp.logical_and(m == 1, f == 1))
    def _():
        os_ref[...] = finish(x1s_ref[...], g2s_ref[...], accs_ref[...])


def _mlp(u2, x1, u2s, x1s, ada, w1b, b1, w2b, b2, lng, lnb, *, alpha, rows_per_cond,
         cond_row, tm, te):
    m, d = u2.shape
    ms = u2s.shape[0]
    nf, _, tf = w1b.shape
    nm = m // tm
    ne = tm // te
    assert ne < nf
    g2_col = N_ADA - 1
    body = functools.partial(_mlp_body, nm=nm, nf=nf, ne=ne, te=te, tm=tm, alpha=alpha,
                             rows_per_cond=rows_per_cond)
    fw = lambda mi, f: jnp.where(mi == nm, nf - 1, f)
    ep = lambda mi, f: (jnp.where(mi == 0, 0, (mi - 1) * ne + jnp.clip(f - 1, 0, ne - 1)), 0)
    return pl.pallas_call(
        body,
        out_shape=[jax.ShapeDtypeStruct((m, d), F32),
                   jax.ShapeDtypeStruct((ms, d), F32)],
        grid=(nm + 1, nf),
        in_specs=[pl.BlockSpec((tm, d), lambda mi, f: (jnp.minimum(mi, nm - 1), 0)),
                  pl.BlockSpec((1, d // 2, tf), lambda mi, f: (fw(mi, f), 0, 0)),
                  pl.BlockSpec((1, tf), lambda mi, f: (0, fw(mi, f))),
                  pl.BlockSpec((tf // 2, d), lambda mi, f: (fw(mi, f), 0)),
                  pl.BlockSpec((te, d), ep),
                  pl.BlockSpec((8, d), lambda mi, f: (cond_row // 8, g2_col)),
                  _resident((1, d)),
                  _resident((1, d)),
                  _resident((1, d)),
                  _resident((ms, d)),
                  _resident((ms, d)),
                  pl.BlockSpec((ms, d), lambda mi, f: (0, g2_col),
                               pipeline_mode=pl.Buffered(1))],
        out_specs=[pl.BlockSpec((te, d), ep),
                   pl.BlockSpec((ms, d), lambda mi, f: (0, 0))],
        scratch_shapes=[pltpu.VMEM((2, tm, d), F32),
                        pltpu.VMEM((ms, d), F32)],
        compiler_params=_params(("arbitrary", "arbitrary")),
        name="mlp",
    )(u2, w1b, b1.reshape(1, -1), w2b, x1, ada, b2.reshape(1, d), lng, lnb, u2s, x1s, ada)


def kernel(x_prompt, x_sample, state_pool, state_conv, c_prompt, c_sample, w_ada, b_ada, w_in,
           pool_grp_w, pool_scale, conv_w, w_pool_up, w_conv_up, w_o, ln1_g, ln1_b, w_ff1,
           b_ff1, w_ff2, b_ff2, ln2_g, ln2_b):
    depth = w_ada.shape[0]
    nb, seq, d = x_prompt.shape
    ms, dec_seq, _ = x_sample.shape
    assert dec_seq == 1
    p = d // 2
    alpha = (2 * depth) ** 0.25

    prompt_row = ms
    pad = (-(ms + nb)) % 8
    c_all = jnp.concatenate([c_sample, c_prompt, jnp.zeros((pad, d), F32)], axis=0)

    xp = x_prompt.reshape(nb * seq, d)
    xs = x_sample.reshape(ms, d)
    pool_p, conv_p, pool_s, conv_s = [], [], [], []
    for l in range(depth):
        ada = _ada(c_all, w_ada[l], b_ada[l])
        pscale = pool_scale[l].reshape(1, p)
        lng1, lnb1 = ln1_g[l].reshape(1, d), ln1_b[l].reshape(1, d)
        lng2, lnb2 = ln2_g[l].reshape(1, d), ln2_b[l].reshape(1, d)

        proj_p, proj_s, (grp_bf, wpu_bf, wcu_bf, wo_bf) = _inproj(
            xp, xs, ada, w_in[l],
            [pool_grp_w[l].reshape(p, -1), w_pool_up[l], w_conv_up[l], w_o[l]],
            seq=seq, prompt_row=prompt_row)

        x1p, u2p, pst, cst, w1b, w2b = _mixer(
            proj_p, xp, ada, grp_bf, pscale, conv_w[l], wpu_bf, wcu_bf, wo_bf, lng1, lnb1,
            w_ff1[l], w_ff2[l], seq=seq, prompt_row=prompt_row, alpha=alpha, tf=MLP_FF_CHUNK)
        x1s, u2s, npool, nconv = _mixer_step(
            proj_s, jnp.transpose(state_pool[l], (1, 0, 2)),
            jnp.transpose(state_conv[l], (1, 0, 2)), xs, ada, grp_bf, pscale, conv_w[l],
            wpu_bf, wcu_bf, wo_bf, lng1, lnb1, alpha=alpha)

        xp, xs = _mlp(u2p, x1p, u2s, x1s, ada, w1b, b_ff1[l], w2b, b_ff2[l], lng2, lnb2,
                      alpha=alpha, rows_per_cond=seq, cond_row=prompt_row,
                      tm=MLP_TOKEN_TILE, te=MLP_EPILOGUE_ROWS)

        pool_p.append(pst[:, HIST_ROWS - POOL_HIST:, :])
        conv_p.append(cst[:, HIST_ROWS - CONV_HIST:, :])
        pool_s.append(jnp.transpose(npool, (1, 0, 2)))
        conv_s.append(jnp.transpose(nconv, (1, 0, 2)))

    return (xp.reshape(nb, seq, d), xs.reshape(ms, dec_seq, d),
            jnp.stack(pool_p), jnp.stack(conv_p), jnp.stack(pool_s), jnp.stack(conv_s))
```

```python
import functools

import jax
import jax.numpy as jnp
from jax import lax
from jax.experimental import pallas as pl
from jax.experimental.pallas import tpu as pltpu

F32 = jnp.float32
BF16 = jnp.bfloat16
U32 = jnp.uint32

POOL_WINDOWS = (2, 4, 8, 16)
POOL_HIST = max(POOL_WINDOWS) - 1
CONV_K = 3
CONV_HIST = CONV_K - 1
N_ADA = 6
PAST_LEN = 16384
LN_EPS = 1e-5

VMEM_LIMIT_BYTES_V7X = 56 * 1024 * 1024
HIST_ROWS = 16
ROW_CHUNK = 256
MIXER_SLAB_ROWS = 256
MLP_TOKEN_TILE = 1024
MLP_FF_CHUNK = 1024
MLP_EPILOGUE_ROWS = 256


def _resident(shape):
    zeros = (0,) * len(shape)
    return pl.BlockSpec(shape, lambda *_: zeros, pipeline_mode=pl.Buffered(1))


def _params(semantics, vmem_limit_bytes=VMEM_LIMIT_BYTES_V7X):
    return pltpu.CompilerParams(dimension_semantics=semantics,
                                vmem_limit_bytes=vmem_limit_bytes)


def _layernorm(xf, g, b):
    mu = jnp.mean(xf, axis=-1, keepdims=True)
    var = jnp.mean(jnp.square(xf - mu), axis=-1, keepdims=True)
    return (xf - mu) * lax.rsqrt(var + LN_EPS) * g + b


def _ada_body(c_ref, w_ref, b_ref, o_ref):
    o_ref[...] = jnp.dot(c_ref[...].astype(BF16), w_ref[...].astype(BF16),
                         preferred_element_type=F32) + b_ref[...]


def _ada(c_all, w_ada, b_ada, tn=1024):
    m, d = c_all.shape
    n = w_ada.shape[1]
    return pl.pallas_call(
        _ada_body,
        out_shape=jax.ShapeDtypeStruct((m, n), F32),
        grid=(n // tn,),
        in_specs=[pl.BlockSpec((m, d), lambda j: (0, 0)),
                  pl.BlockSpec((d, tn), lambda j: (0, j)),
                  pl.BlockSpec((1, tn), lambda j: (0, j))],
        out_specs=pl.BlockSpec((m, tn), lambda j: (0, j)),
        compiler_params=_params(("arbitrary",)),
        name="ada",
    )(c_all, w_ada, b_ada.reshape(1, n))


def _pack_rows(x_bf16):
    return pltpu.bitcast(x_bf16, U32)


def _unpack_rows(x_u32):
    return pltpu.bitcast(x_u32, BF16)


def _inproj_body(x_ref, adap_ref, w_ref, xs_ref, adas_ref, *rest, tiles_per_seq, d, tm,
                 n_side, side_steps):
    side_in = rest[:n_side]
    o_ref, os_ref = rest[n_side:n_side + 2]
    side_out = rest[n_side + 2:2 * n_side + 2]
    wb_ref = rest[2 * n_side + 2]
    i = pl.program_id(1)

    @pl.when(pl.program_id(0) * pl.num_programs(1) + i < side_steps)
    def _():
        for src, dst in zip(side_in, side_out):
            dst[...] = _pack_rows(src[...].astype(BF16))

    @pl.when(i == 0)
    def _():
        wb_ref[...] = w_ref[...].astype(BF16)
        us = xs_ref[...] * (1.0 + adas_ref[:, d:2 * d]) + adas_ref[:, 0:d]
        os_ref[...] = jnp.dot(us.astype(BF16), wb_ref[...], preferred_element_type=F32)

    b = i // tiles_per_seq
    sh1 = adap_ref[pl.ds(b, 1), 0:d]
    sc1 = adap_ref[pl.ds(b, 1), d:2 * d]
    for r in range(0, tm, ROW_CHUNK):
        rows = slice(r, r + ROW_CHUNK)
        u = x_ref[rows, :] * (1.0 + sc1) + sh1
        o_ref[rows, :] = jnp.dot(u.astype(BF16), wb_ref[...],
                                 preferred_element_type=F32).astype(BF16)


def _inproj(xp, xs, ada, w_in, side_ws, *, seq, prompt_row, tm=1024, tn=1024, side_steps=16):
    m, d = xp.shape
    ms = xs.shape[0]
    n = w_in.shape[1]
    ni = m // tm
    body = functools.partial(_inproj_body, tiles_per_seq=seq // tm, d=d, tm=tm,
                             n_side=len(side_ws), side_steps=side_steps)
    side_idx = lambda j, i: (jnp.minimum(j * ni + i, side_steps - 1), 0)
    side_in = [pl.BlockSpec((w.shape[0] // side_steps, w.shape[1]), side_idx) for w in side_ws]
    side_out = [pl.BlockSpec((w.shape[0] // side_steps // 2, w.shape[1]), side_idx)
                for w in side_ws]
    side_shape = [jax.ShapeDtypeStruct((w.shape[0] // 2, w.shape[1]), U32) for w in side_ws]
    out = pl.pallas_call(
        body,
        out_shape=[jax.ShapeDtypeStruct((m, n), BF16),
                   jax.ShapeDtypeStruct((ms, n), F32)] + side_shape,
        grid=(n // tn, ni),
        in_specs=[pl.BlockSpec((tm, d), lambda j, i: (i, 0)),
                  pl.BlockSpec((8, 2 * d), lambda j, i: (prompt_row // 8, 0)),
                  pl.BlockSpec((d, tn), lambda j, i: (0, j)),
                  _resident((ms, d)),
                  _resident((ms, 2 * d))] + side_in,
        out_specs=[pl.BlockSpec((tm, tn), lambda j, i: (i, j)),
                   pl.BlockSpec((ms, tn), lambda j, i: (0, j))] + side_out,
        scratch_shapes=[pltpu.VMEM((d, tn), BF16)],
        compiler_params=_params(("arbitrary", "arbitrary")),
        name="inproj",
    )(xp, ada, w_in, xs, ada, *side_ws)
    return out[0], out[1], out[2:]


def _sigmoid(x):
    return 0.5 * jnp.tanh(0.5 * x) + 0.5


def _mix_tail(slabs, g1, sc2, sh2, wpu_ref, wcu_ref, wo_ref, lng_ref, lnb_ref, alpha):
    wpu = _unpack_rows(wpu_ref[...])
    wcu = _unpack_rows(wcu_ref[...])
    wo = _unpack_rows(wo_ref[...])

    def up(s):
        pooled_bf, conv_bf = s[0]()
        return (jnp.dot(pooled_bf, wpu, preferred_element_type=F32),
                jnp.dot(conv_bf, wcu, preferred_element_type=F32))

    def gate(s, y):
        gp, gc = s[1]()
        return (_sigmoid(gp) * y[0] + _sigmoid(gc) * y[1]).astype(BF16)

    def finish(s, m_out):
        x1 = _layernorm(alpha * s[2]() + g1 * m_out, lng_ref[...], lnb_ref[...])
        s[3](x1, (x1 * (1.0 + sc2) + sh2).astype(BF16))

    y = [None] * len(slabs)
    mo = [None] * len(slabs)
    y[0] = up(slabs[0])
    for k, s in enumerate(slabs):
        if k + 1 < len(slabs):
            y[k + 1] = up(slabs[k + 1])
        mo[k] = jnp.dot(gate(s, y[k]), wo, preferred_element_type=F32)
        if k > 0:
            finish(slabs[k - 1], mo[k - 1])
    finish(slabs[-1], mo[-1])


def _mixer_body(proj_ref, hist_ref, x_ref, ada_ref, grp_ref, pscale_ref, convw_ref,
                wpu_ref, wcu_ref, wo_ref, lng_ref, lnb_ref, w1_ref, w2_ref,
                x1_ref, u2_ref, pst_ref, cst_ref, w1b_ref, w2b_ref, zz_ref, vv_ref, pm_ref,
                *, tm, tiles_per_seq, d, alpha):
    i = pl.program_id(0)

    tf = w1b_ref.shape[2]
    for fi in range(w1b_ref.shape[0]):
        w1b_ref[fi] = _pack_rows(w1_ref[:, fi * tf:(fi + 1) * tf].astype(BF16))
    w2b_ref[...] = _pack_rows(w2_ref[...].astype(BF16))

    tis = i % tiles_per_seq
    b = i // tiles_per_seq
    first = tis == 0
    p = d // 2
    gw = p // len(POOL_WINDOWS)
    h = HIST_ROWS

    zz_ref[0:h, :] = jnp.where(first, 0.0, hist_ref[:, 0:p].astype(F32))
    zz_ref[h:h + tm, :] = proj_ref[:, 0:p].astype(F32)
    vh = hist_ref[:, 3 * p:4 * p].astype(F32) * hist_ref[:, p:2 * p].astype(F32)
    vv_ref[0:h, :] = jnp.where(first, 0.0, vh)
    vv_ref[h:h + tm, :] = proj_ref[:, 3 * p:4 * p].astype(F32) * proj_ref[:, p:2 * p].astype(F32)
    pst_ref[0] = zz_ref[tm:tm + h, :]
    cst_ref[0] = vv_ref[tm:tm + h, :]

    g1 = ada_ref[pl.ds(b, 1), 2 * d:3 * d]
    sh2 = ada_ref[pl.ds(b, 1), 3 * d:4 * d]
    sc2 = ada_ref[pl.ds(b, 1), 4 * d:5 * d]

    n = MIXER_SLAB_ROWS

    def front(r0):
        rows = slice(r0, r0 + n)
        pos = tis * tm + r0 + lax.broadcasted_iota(jnp.int32, (n, 1), 0)
        for g, w in enumerate(POOL_WINDOWS):
            cols = slice(g * gw, (g + 1) * gw)
            s = zz_ref[r0:r0 + h + n, cols]
            zt = s[h:, :]
            k = 1
            while k < w:
                s = s + pltpu.roll(s, k, axis=0)
                k *= 2
            inv_cnt = 1.0 / jnp.minimum(w, pos + 1).astype(F32)
            pooled = (s[h:, :] * inv_cnt - zt).astype(BF16)
            og = jnp.dot(pooled, _unpack_rows(grp_ref[g * gw // 2:(g + 1) * gw // 2, :]),
                         preferred_element_type=F32)
            pm_ref[rows, cols] = (og * pscale_ref[:, cols]).astype(BF16)
        conv = convw_ref[0:1, :] * vv_ref[h + r0 - 2:h + r0 - 2 + n, :]
        conv = conv + convw_ref[1:2, :] * vv_ref[h + r0 - 1:h + r0 - 1 + n, :]
        conv = conv + convw_ref[2:3, :] * vv_ref[h + r0:h + r0 + n, :]
        bconv = (proj_ref[rows, 2 * p:3 * p].astype(F32) * conv).astype(BF16)
        return pm_ref[rows, :], bconv

    def gates(r0):
        rows = slice(r0, r0 + n)
        return (proj_ref[rows, 4 * p:4 * p + d].astype(F32),
                proj_ref[rows, 4 * p + d:4 * p + 2 * d].astype(F32))

    def store(r0, x1, u2):
        x1_ref[r0:r0 + n, :] = x1
        u2_ref[r0:r0 + n, :] = u2

    slabs = [(functools.partial(front, r0), functools.partial(gates, r0),
              functools.partial(lambda r: x_ref[r:r + n, :], r0), functools.partial(store, r0))
             for r0 in range(0, tm, n)]
    _mix_tail(slabs, g1, sc2, sh2, wpu_ref, wcu_ref, wo_ref, lng_ref, lnb_ref, alpha)


def _mixer(proj, xp, ada, grp_bf, pscale, convw, wpu_bf, wcu_bf, wo_bf, lng, lnb, w1, w2,
           *, seq, prompt_row, alpha, tf, tm=256):
    m, d = xp.shape
    p = d // 2
    nseq = m // seq
    h = HIST_ROWS
    steps = m // tm
    dff = w1.shape[1]
    body = functools.partial(_mixer_body, tm=tm, tiles_per_seq=seq // tm, d=d, alpha=alpha)
    return pl.pallas_call(
        body,
        out_shape=[jax.ShapeDtypeStruct((m, d), F32),
                   jax.ShapeDtypeStruct((m, d), BF16),
                   jax.ShapeDtypeStruct((nseq, h, p), F32),
                   jax.ShapeDtypeStruct((nseq, h, p), F32),
                   jax.ShapeDtypeStruct((dff // tf, d // 2, tf), U32),
                   jax.ShapeDtypeStruct((dff // 2, d), U32)],
        grid=(steps,),
        in_specs=[pl.BlockSpec((tm, proj.shape[1]), lambda i: (i, 0)),
                  pl.BlockSpec((h, proj.shape[1]),
                               lambda i: (jnp.maximum(i * (tm // h) - 1, 0), 0)),
                  pl.BlockSpec((tm, d), lambda i: (i, 0)),
                  pl.BlockSpec((8, N_ADA * d), lambda i: (prompt_row // 8, 0)),
                  _resident(grp_bf.shape),
                  _resident(pscale.shape),
                  _resident(convw.shape),
                  _resident(wpu_bf.shape),
                  _resident(wcu_bf.shape),
                  _resident(wo_bf.shape),
                  _resident(lng.shape),
                  _resident(lnb.shape),
                  pl.BlockSpec((d // steps, dff), lambda i: (i, 0)),
                  pl.BlockSpec((dff // steps, d), lambda i: (i, 0))],
        out_specs=[pl.BlockSpec((tm, d), lambda i: (i, 0)),
                   pl.BlockSpec((tm, d), lambda i: (i, 0)),
                   pl.BlockSpec((1, h, p), lambda i: (i // (seq // tm), 0, 0)),
                   pl.BlockSpec((1, h, p), lambda i: (i // (seq // tm), 0, 0)),
                   pl.BlockSpec((dff // tf, d // steps // 2, tf), lambda i: (0, i, 0)),
                   pl.BlockSpec((dff // steps // 2, d), lambda i: (i, 0))],
        scratch_shapes=[pltpu.VMEM((tm + h, p), F32),
                        pltpu.VMEM((tm + h, p), F32),
                        pltpu.VMEM((tm, p), BF16)],
        compiler_params=_params(("arbitrary",)),
        name="mixer",
    )(proj, proj, xp, ada, grp_bf, pscale, convw, wpu_bf, wcu_bf, wo_bf, lng, lnb, w1, w2)


def _mixer_step_body(proj_ref, pool_ref, cstate_ref, x_ref, ada_ref, grp_ref, pscale_ref,
                     convw_ref, wpu_ref, wcu_ref, wo_ref, lng_ref, lnb_ref,
                     x1_ref, u2_ref, npool_ref, nconv_ref, pm_ref, *, d, alpha):
    p = d // 2
    gw = p // len(POOL_WINDOWS)
    z = proj_ref[:, 0:p]
    for g, w in enumerate(POOL_WINDOWS):
        cols = slice(g * gw, (g + 1) * gw)
        s = z[:, cols]
        for k in range(1, w):
            s = s + pool_ref[POOL_HIST - k, :, cols]
        inv_cnt = 1.0 / min(w, PAST_LEN + 1)
        pooled = (s * inv_cnt - z[:, cols]).astype(BF16)
        og = jnp.dot(pooled, _unpack_rows(grp_ref[g * gw // 2:(g + 1) * gw // 2, :]),
                         preferred_element_type=F32)
        pm_ref[:, cols] = (og * pscale_ref[:, cols]).astype(BF16)
    for r in range(POOL_HIST - 1):
        npool_ref[r] = pool_ref[r + 1]
    npool_ref[POOL_HIST - 1] = z

    xc = proj_ref[:, p:2 * p]
    bc = proj_ref[:, 2 * p:3 * p]
    cc = proj_ref[:, 3 * p:4 * p]
    v = cc * xc
    conv = convw_ref[0:1, :] * cstate_ref[0]
    conv = conv + convw_ref[1:2, :] * cstate_ref[1]
    conv = conv + convw_ref[2:3, :] * v
    nconv_ref[0] = cstate_ref[1]
    nconv_ref[1] = v

    gp = proj_ref[:, 4 * p:4 * p + d]
    gc = proj_ref[:, 4 * p + d:4 * p + 2 * d]
    g1 = ada_ref[:, 2 * d:3 * d]
    sh2 = ada_ref[:, 3 * d:4 * d]
    sc2 = ada_ref[:, 4 * d:5 * d]

    def store(x1, u2):
        x1_ref[...] = x1
        u2_ref[...] = u2

    slab = (lambda: (pm_ref[...], (bc * conv).astype(BF16)), lambda: (gp, gc),
            lambda: x_ref[...], store)
    _mix_tail([slab], g1, sc2, sh2, wpu_ref, wcu_ref, wo_ref, lng_ref, lnb_ref, alpha)


def _mixer_step(proj_s, pool_state, conv_state, xs, ada, grp_bf, pscale, convw, wpu_bf, wcu_bf,
                wo_bf, lng, lnb, *, alpha):
    ms, d = xs.shape
    p = d // 2
    body = functools.partial(_mixer_step_body, d=d, alpha=alpha)
    return pl.pallas_call(
        body,
        out_shape=[jax.ShapeDtypeStruct((ms, d), F32),
                   jax.ShapeDtypeStruct((ms, d), BF16),
                   jax.ShapeDtypeStruct(pool_state.shape, F32),
                   jax.ShapeDtypeStruct(conv_state.shape, F32)],
        grid=(1,),
        in_specs=[_resident(proj_s.shape),
                  _resident(pool_state.shape),
                  _resident(conv_state.shape),
                  _resident((ms, d)),
                  _resident((ms, N_ADA * d)),
                  _resident(grp_bf.shape),
                  _resident(pscale.shape),
                  _resident(convw.shape),
                  _resident(wpu_bf.shape),
                  _resident(wcu_bf.shape),
                  _resident(wo_bf.shape),
                  _resident(lng.shape),
                  _resident(lnb.shape)],
        out_specs=[_resident((ms, d)),
                   _resident((ms, d)),
                   _resident(pool_state.shape),
                   _resident(conv_state.shape)],
        scratch_shapes=[pltpu.VMEM((ms, p), BF16)],
        compiler_params=_params(("arbitrary",)),
        name="mixer_step",
    )(proj_s, pool_state, conv_state, xs, ada, grp_bf, pscale, convw, wpu_bf, wcu_bf, wo_bf, lng, lnb)


def _mlp_body(u2_ref, w1_ref, b1_ref, w2_ref, x1_ref, g2_ref, b2_ref, lng_ref, lnb_ref,
              u2s_ref, x1s_ref, g2s_ref, o_ref, os_ref, acc_ref, accs_ref,
              *, nm, nf, ne, te, tm, alpha, rows_per_cond):
    m = pl.program_id(0)
    f = pl.program_id(1)
    slot = m % 2

    def chunk(u2):
        hmid = jnp.dot(u2, _unpack_rows(w1_ref[0]), preferred_element_type=F32) + b1_ref[f]
        hmid = jnp.square(jnp.maximum(hmid, 0.0))
        return jnp.dot(hmid.astype(BF16), _unpack_rows(w2_ref[...]),
                       preferred_element_type=F32)

    def accumulate(first, between=None):
        n_chunks = tm // ROW_CHUNK
        for c in range(n_chunks):
            if between is not None:
                between(c, n_chunks)
            rows = slice(c * ROW_CHUNK, (c + 1) * ROW_CHUNK)
            part = chunk(u2_ref[rows, :])
            if first:
                acc_ref[slot, rows, :] = part
            else:
                acc_ref[slot, rows, :] += part

    def accumulate_singles(first):
        part = chunk(u2s_ref[...])
        if first:
            accs_ref[...] = part
        else:
            accs_ref[...] += part

    def finish(x1, g2, acc):
        return _layernorm(alpha * x1 + g2 * (acc + b2_ref[...]), lng_ref[...], lnb_ref[...])

    def epilogue(piece=0, n_pieces=1):
        tp = te // n_pieces
        base = pl.multiple_of((f - 1) * te + piece * tp, tp)
        g2 = g2_ref[pl.ds(((m - 1) * tm) // rows_per_cond, 1), :]
        o_ref[piece * tp:(piece + 1) * tp, :] = finish(
            x1_ref[piece * tp:(piece + 1) * tp, :], g2, acc_ref[1 - slot, pl.ds(base, tp), :])

    has_main = m < nm
    has_epi = jnp.logical_and(m >= 1, jnp.logical_and(f >= 1, f <= ne))

    @pl.when(jnp.logical_and(has_main, f == 0))
    def _():
        accumulate(True)
        pl.when(m == 0)(functools.partial(accumulate_singles, True))

    @pl.when(jnp.logical_and(has_main, has_epi))
    def _():
        accumulate(False, between=epilogue)

    @pl.when(jnp.logical_and(has_main, jnp.logical_and(f > 0, jnp.logical_not(has_epi))))
    def _():
        accumulate(False)
        pl.when(m == 0)(functools.partial(accumulate_singles, False))

    @pl.when(jnp.logical_and(jnp.logical_not(has_main), has_epi))
    def _():
        epilogue()

    @pl.when(jnp.logical_and(m == 1, f == 1))
    def _():
        os_ref[...] = finish(x1s_ref[...], g2s_ref[...], accs_ref[...])


def _mlp(u2, x1, u2s, x1s, ada, w1b, b1, w2b, b2, lng, lnb, *, alpha, rows_per_cond,
         cond_row, tm, te):
    m, d = u2.shape
    ms = u2s.shape[0]
    nf, _, tf = w1b.shape
    nm = m // tm
    ne = tm // te
    assert ne < nf
    g2_col = N_ADA - 1
    body = functools.partial(_mlp_body, nm=nm, nf=nf, ne=ne, te=te, tm=tm, alpha=alpha,
                             rows_per_cond=rows_per_cond)
    fw = lambda mi, f: jnp.where(mi == nm, nf - 1, f)
    ep = lambda mi, f: (jnp.where(mi == 0, 0, (mi - 1) * ne + jnp.clip(f - 1, 0, ne - 1)), 0)
    return pl.pallas_call(
        body,
        out_shape=[jax.ShapeDtypeStruct((m, d), F32),
                   jax.ShapeDtypeStruct((ms, d), F32)],
        grid=(nm + 1, nf),
        in_specs=[pl.BlockSpec((tm, d), lambda mi, f: (jnp.minimum(mi, nm - 1), 0)),
                  pl.BlockSpec((1, d // 2, tf), lambda mi, f: (fw(mi, f), 0, 0)),
                  _resident((nf, 1, tf)),
                  pl.BlockSpec((tf // 2, d), lambda mi, f: (fw(mi, f), 0)),
                  pl.BlockSpec((te, d), ep),
                  pl.BlockSpec((8, d), lambda mi, f: (cond_row // 8, g2_col)),
                  _resident((1, d)),
                  _resident((1, d)),
                  _resident((1, d)),
                  _resident((ms, d)),
                  _resident((ms, d)),
                  pl.BlockSpec((ms, d), lambda mi, f: (0, g2_col),
                               pipeline_mode=pl.Buffered(1))],
        out_specs=[pl.BlockSpec((te, d), ep),
                   pl.BlockSpec((ms, d), lambda mi, f: (0, 0))],
        scratch_shapes=[pltpu.VMEM((2, tm, d), F32),
                        pltpu.VMEM((ms, d), F32)],
        compiler_params=_params(("arbitrary", "arbitrary")),
        name="mlp",
    )(u2, w1b, b1.reshape(nf, 1, tf), w2b, x1, ada, b2.reshape(1, d), lng, lnb, u2s, x1s, ada)


def kernel(x_prompt, x_sample, state_pool, state_conv, c_prompt, c_sample, w_ada, b_ada, w_in,
           pool_grp_w, pool_scale, conv_w, w_pool_up, w_conv_up, w_o, ln1_g, ln1_b, w_ff1,
           b_ff1, w_ff2, b_ff2, ln2_g, ln2_b):
    depth = w_ada.shape[0]
    nb, seq, d = x_prompt.shape
    ms, dec_seq, _ = x_sample.shape
    assert dec_seq == 1
    p = d // 2
    alpha = (2 * depth) ** 0.25

    prompt_row = ms
    pad = (-(ms + nb)) % 8
    c_all = jnp.concatenate([c_sample, c_prompt, jnp.zeros((pad, d), F32)], axis=0)

    xp = x_prompt.reshape(nb * seq, d)
    xs = x_sample.reshape(ms, d)
    pool_p, conv_p, pool_s, conv_s = [], [], [], []
    for l in range(depth):
        ada = _ada(c_all, w_ada[l], b_ada[l])
        pscale = pool_scale[l].reshape(1, p)
        lng1, lnb1 = ln1_g[l].reshape(1, d), ln1_b[l].reshape(1, d)
        lng2, lnb2 = ln2_g[l].reshape(1, d), ln2_b[l].reshape(1, d)

        proj_p, proj_s, (grp_bf, wpu_bf, wcu_bf, wo_bf) = _inproj(
            xp, xs, ada, w_in[l],
            [pool_grp_w[l].reshape(p, -1), w_pool_up[l], w_conv_up[l], w_o[l]],
            seq=seq, prompt_row=prompt_row)

        x1p, u2p, pst, cst, w1b, w2b = _mixer(
            proj_p, xp, ada, grp_bf, pscale, conv_w[l], wpu_bf, wcu_bf, wo_bf, lng1, lnb1,
            w_ff1[l], w_ff2[l], seq=seq, prompt_row=prompt_row, alpha=alpha, tf=MLP_FF_CHUNK)
        x1s, u2s, npool, nconv = _mixer_step(
            proj_s, jnp.transpose(state_pool[l], (1, 0, 2)),
            jnp.transpose(state_conv[l], (1, 0, 2)), xs, ada, grp_bf, pscale, conv_w[l],
            wpu_bf, wcu_bf, wo_bf, lng1, lnb1, alpha=alpha)

        xp, xs = _mlp(u2p, x1p, u2s, x1s, ada, w1b, b_ff1[l], w2b, b_ff2[l], lng2, lnb2,
                      alpha=alpha, rows_per_cond=seq, cond_row=prompt_row,
                      tm=MLP_TOKEN_TILE, te=MLP_EPILOGUE_ROWS)

        pool_p.append(pst[:, HIST_ROWS - POOL_HIST:, :])
        conv_p.append(cst[:, HIST_ROWS - CONV_HIST:, :])
        pool_s.append(jnp.transpose(npool, (1, 0, 2)))
        conv_s.append(jnp.transpose(nconv, (1, 0, 2)))

    return (xp.reshape(nb, seq, d), xs.reshape(ms, dec_seq, d),
            jnp.stack(pool_p), jnp.stack(conv_p), jnp.stack(pool_s), jnp.stack(conv_s))
```

```python
import functools

import jax
import jax.numpy as jnp
from jax import lax
from jax.experimental import pallas as pl
from jax.experimental.pallas import tpu as pltpu

F32 = jnp.float32
BF16 = jnp.bfloat16
U32 = jnp.uint32

POOL_WINDOWS = (2, 4, 8, 16)
POOL_HIST = max(POOL_WINDOWS) - 1
CONV_K = 3
CONV_HIST = CONV_K - 1
N_ADA = 6
PAST_LEN = 16384
LN_EPS = 1e-5

VMEM_LIMIT_BYTES_V7X = 56 * 1024 * 1024
HIST_ROWS = 16
ROW_CHUNK = 256
MIXER_COL_BLOCK = 256
MIXER_FINISH_SLABS = 4
MLP_TOKEN_TILE = 1024
MLP_FF_CHUNK = 1024
MLP_EPILOGUE_ROWS = 256


def _resident(shape):
    zeros = (0,) * len(shape)
    return pl.BlockSpec(shape, lambda *_: zeros, pipeline_mode=pl.Buffered(1))


def _params(semantics, vmem_limit_bytes=VMEM_LIMIT_BYTES_V7X):
    return pltpu.CompilerParams(dimension_semantics=semantics,
                                vmem_limit_bytes=vmem_limit_bytes)


def _layernorm(xf, g, b):
    mu = jnp.mean(xf, axis=-1, keepdims=True)
    var = jnp.mean(jnp.square(xf - mu), axis=-1, keepdims=True)
    return (xf - mu) * lax.rsqrt(var + LN_EPS) * g + b


def _ada_body(c_ref, w_ref, b_ref, o_ref):
    o_ref[...] = jnp.dot(c_ref[...].astype(BF16), w_ref[...].astype(BF16),
                         preferred_element_type=F32) + b_ref[...]


def _ada(c_all, w_ada, b_ada, tn=1024):
    m, d = c_all.shape
    n = w_ada.shape[1]
    return pl.pallas_call(
        _ada_body,
        out_shape=jax.ShapeDtypeStruct((m, n), F32),
        grid=(n // tn,),
        in_specs=[pl.BlockSpec((m, d), lambda j: (0, 0)),
                  pl.BlockSpec((d, tn), lambda j: (0, j)),
                  pl.BlockSpec((1, tn), lambda j: (0, j))],
        out_specs=pl.BlockSpec((m, tn), lambda j: (0, j)),
        compiler_params=_params(("arbitrary",)),
        name="ada",
    )(c_all, w_ada, b_ada.reshape(1, n))


def _pack_rows(x_bf16):
    return pltpu.bitcast(x_bf16, U32)


def _unpack_rows(x_u32):
    return pltpu.bitcast(x_u32, BF16)


def _inproj_body(x_ref, adap_ref, w_ref, xs_ref, adas_ref, *rest, tiles_per_seq, d, tm,
                 n_side, side_steps):
    side_in = rest[:n_side]
    o_ref, os_ref = rest[n_side:n_side + 2]
    side_out = rest[n_side + 2:2 * n_side + 2]
    wb_ref = rest[2 * n_side + 2]
    i = pl.program_id(1)

    @pl.when(pl.program_id(0) * pl.num_programs(1) + i < side_steps)
    def _():
        for src, dst in zip(side_in, side_out):
            dst[...] = _pack_rows(src[...].astype(BF16))

    @pl.when(i == 0)
    def _():
        wb_ref[...] = w_ref[...].astype(BF16)
        us = xs_ref[...] * (1.0 + adas_ref[:, d:2 * d]) + adas_ref[:, 0:d]
        os_ref[...] = jnp.dot(us.astype(BF16), wb_ref[...], preferred_element_type=F32)

    b = i // tiles_per_seq
    sh1 = adap_ref[pl.ds(b, 1), 0:d]
    sc1 = adap_ref[pl.ds(b, 1), d:2 * d]
    for r in range(0, tm, ROW_CHUNK):
        rows = slice(r, r + ROW_CHUNK)
        u = x_ref[rows, :] * (1.0 + sc1) + sh1
        o_ref[rows, :] = jnp.dot(u.astype(BF16), wb_ref[...],
                                 preferred_element_type=F32).astype(BF16)


def _inproj(xp, xs, ada, w_in, side_ws, *, seq, prompt_row, tm=1024, tn=1024, side_steps=16):
    m, d = xp.shape
    ms = xs.shape[0]
    n = w_in.shape[1]
    ni = m // tm
    body = functools.partial(_inproj_body, tiles_per_seq=seq // tm, d=d, tm=tm,
                             n_side=len(side_ws), side_steps=side_steps)
    side_idx = lambda j, i: (jnp.minimum(j * ni + i, side_steps - 1), 0)
    side_in = [pl.BlockSpec((w.shape[0] // side_steps, w.shape[1]), side_idx) for w in side_ws]
    side_out = [pl.BlockSpec((w.shape[0] // side_steps // 2, w.shape[1]), side_idx)
                for w in side_ws]
    side_shape = [jax.ShapeDtypeStruct((w.shape[0] // 2, w.shape[1]), U32) for w in side_ws]
    out = pl.pallas_call(
        body,
        out_shape=[jax.ShapeDtypeStruct((m, n), BF16),
                   jax.ShapeDtypeStruct((ms, n), F32)] + side_shape,
        grid=(n // tn, ni),
        in_specs=[pl.BlockSpec((tm, d), lambda j, i: (i, 0)),
                  pl.BlockSpec((8, 2 * d), lambda j, i: (prompt_row // 8, 0)),
                  pl.BlockSpec((d, tn), lambda j, i: (0, j)),
                  _resident((ms, d)),
                  _resident((ms, 2 * d))] + side_in,
        out_specs=[pl.BlockSpec((tm, tn), lambda j, i: (i, j)),
                   pl.BlockSpec((ms, tn), lambda j, i: (0, j))] + side_out,
        scratch_shapes=[pltpu.VMEM((d, tn), BF16)],
        compiler_params=_params(("arbitrary", "arbitrary")),
        name="inproj",
    )(xp, ada, w_in, xs, ada, *side_ws)
    return out[0], out[1], out[2:]


def _sigmoid(x):
    return 0.5 * jnp.tanh(0.5 * x) + 0.5


def _mix_tail(slabs, g1, sc2, sh2, wpu_ref, wcu_ref, wo_ref, lng_ref, lnb_ref, alpha):
    wpu = _unpack_rows(wpu_ref[...])
    wcu = _unpack_rows(wcu_ref[...])
    wo = _unpack_rows(wo_ref[...])

    def up(s):
        pooled_bf, conv_bf = s[0]()
        return (jnp.dot(pooled_bf, wpu, preferred_element_type=F32),
                jnp.dot(conv_bf, wcu, preferred_element_type=F32))

    def gate(s, y):
        gp, gc = s[1]()
        return (_sigmoid(gp) * y[0] + _sigmoid(gc) * y[1]).astype(BF16)

    def finish(s, m_out):
        x1 = _layernorm(alpha * s[2]() + g1 * m_out, lng_ref[...], lnb_ref[...])
        s[3](x1, (x1 * (1.0 + sc2) + sh2).astype(BF16))

    y = [None] * len(slabs)
    mo = [None] * len(slabs)
    y[0] = up(slabs[0])
    for k, s in enumerate(slabs):
        if k + 1 < len(slabs):
            y[k + 1] = up(slabs[k + 1])
        mo[k] = jnp.dot(gate(s, y[k]), wo, preferred_element_type=F32)
        if k > 0:
            finish(slabs[k - 1], mo[k - 1])
    finish(slabs[-1], mo[-1])


def _mixer_body(proj_ref, hist_ref, x_ref, ada_ref, grp_ref, pscale_ref, convw_ref,
                wpu_ref, wcu_ref, wo_ref, lng_ref, lnb_ref, w1_ref, w2_ref,
                x1_ref, u2_ref, pst_ref, cst_ref, w1b_ref, w2b_ref,
                zz_ref, vv_ref, pm_ref, bc_ref, mixed_ref,
                *, tm, tiles_per_seq, d, alpha, steps):
    i = pl.program_id(0)
    p = d // 2
    gw = p // len(POOL_WINDOWS)
    h = HIST_ROWS

    cb = MIXER_COL_BLOCK
    ncb = d // cb
    tis = i % tiles_per_seq
    first = tis == 0

    def side_cast():
        tf = w1b_ref.shape[2]
        for fi in range(w1b_ref.shape[0]):
            w1b_ref[fi] = _pack_rows(w1_ref[:, fi * tf:(fi + 1) * tf].astype(BF16))
        w2b_ref[...] = _pack_rows(w2_ref[...].astype(BF16))

    def setup():
        zz_ref[0:h, :] = jnp.where(first, 0.0, hist_ref[:, 0:p].astype(F32))
        zz_ref[h:h + tm, :] = proj_ref[:, 0:p].astype(F32)
        vh = hist_ref[:, 3 * p:4 * p].astype(F32) * hist_ref[:, p:2 * p].astype(F32)
        vv_ref[0:h, :] = jnp.where(first, 0.0, vh)
        vv_ref[h:h + tm, :] = (proj_ref[:, 3 * p:4 * p].astype(F32)
                               * proj_ref[:, p:2 * p].astype(F32))
        pst_ref[0] = zz_ref[tm:tm + h, :]
        cst_ref[0] = vv_ref[tm:tm + h, :]

    def pool_group(g):
        w = POOL_WINDOWS[g]
        cols = slice(g * gw, (g + 1) * gw)
        pos = tis * tm + lax.broadcasted_iota(jnp.int32, (tm, 1), 0)
        s = zz_ref[:, cols]
        zt = s[h:, :]
        k = 1
        while k < w:
            s = s + pltpu.roll(s, k, axis=0)
            k *= 2
        inv_cnt = 1.0 / jnp.minimum(w, pos + 1).astype(F32)
        pooled = (s[h:, :] * inv_cnt - zt).astype(BF16)
        og = jnp.dot(pooled, _unpack_rows(grp_ref[g * gw // 2:(g + 1) * gw // 2, :]),
                     preferred_element_type=F32)
        pm_ref[:, cols] = (og * pscale_ref[:, cols]).astype(BF16)

    def conv_branch():
        conv = convw_ref[0:1, :] * vv_ref[h - 2:h - 2 + tm, :]
        conv = conv + convw_ref[1:2, :] * vv_ref[h - 1:h - 1 + tm, :]
        conv = conv + convw_ref[2:3, :] * vv_ref[h:h + tm, :]
        bc_ref[...] = (proj_ref[:, 2 * p:3 * p].astype(F32) * conv).astype(BF16)

    def up_and_gate(k):
        cols = slice(k * cb, (k + 1) * cb)
        y_p = jnp.dot(pm_ref[...], _unpack_rows(wpu_ref[:, cols]), preferred_element_type=F32)
        y_c = jnp.dot(bc_ref[...], _unpack_rows(wcu_ref[:, cols]), preferred_element_type=F32)
        gp = proj_ref[:, 4 * p + k * cb:4 * p + (k + 1) * cb].astype(F32)
        gc = proj_ref[:, 4 * p + d + k * cb:4 * p + d + (k + 1) * cb].astype(F32)
        mixed_ref[:, cols] = (_sigmoid(gp) * y_p + _sigmoid(gc) * y_c).astype(BF16)

    def out_block(k):
        return jnp.dot(mixed_ref[...], _unpack_rows(wo_ref[:, k * cb:(k + 1) * cb]),
                       preferred_element_type=F32)

    def finish_rows(mo, r):
        nr = tm // MIXER_FINISH_SLABS
        rows = slice(r * nr, (r + 1) * nr)
        b = (i - 1) // tiles_per_seq
        g1 = ada_ref[pl.ds(b, 1), 2 * d:3 * d]
        sh2 = ada_ref[pl.ds(b, 1), 3 * d:4 * d]
        sc2 = ada_ref[pl.ds(b, 1), 4 * d:5 * d]
        x1 = _layernorm(alpha * x_ref[rows, :] + g1 * mo[rows, :], lng_ref[...], lnb_ref[...])
        x1_ref[rows, :] = x1
        u2_ref[rows, :] = (x1 * (1.0 + sc2) + sh2).astype(BF16)

    def step(has_head, has_tail):
        front = [side_cast, setup] + [functools.partial(pool_group, g)
                                      for g in range(len(POOL_WINDOWS))] + [conv_branch]
        mo_blocks = []
        for k in range(ncb):
            if has_tail:
                mo_blocks.append(out_block(k))
            if has_head and k < len(front):
                front[k]()
        if has_head:
            for piece in front[ncb:]:
                piece()
        mo = jnp.concatenate(mo_blocks, axis=1) if has_tail else None
        fin_at = {(r + 1) * ncb // MIXER_FINISH_SLABS - 1: r for r in range(MIXER_FINISH_SLABS)}
        for k in range(ncb):
            if has_head:
                up_and_gate(k)
            if has_tail and k in fin_at:
                finish_rows(mo, fin_at[k])

    pl.when(i == 0)(functools.partial(step, True, False))
    pl.when(jnp.logical_and(i > 0, i < steps))(functools.partial(step, True, True))
    pl.when(i == steps)(functools.partial(step, False, True))


def _mixer(proj, xp, ada, grp_bf, pscale, convw, wpu_bf, wcu_bf, wo_bf, lng, lnb, w1, w2,
           *, seq, prompt_row, alpha, tf, tm=256):
    m, d = xp.shape
    p = d // 2
    nseq = m // seq
    h = HIST_ROWS
    steps = m // tm
    dff = w1.shape[1]
    body = functools.partial(_mixer_body, tm=tm, tiles_per_seq=seq // tm, d=d, alpha=alpha,
                             steps=steps)
    cur = lambda i: jnp.minimum(i, steps - 1)
    prev = lambda i: jnp.maximum(i - 1, 0)
    return pl.pallas_call(
        body,
        out_shape=[jax.ShapeDtypeStruct((m, d), F32),
                   jax.ShapeDtypeStruct((m, d), BF16),
                   jax.ShapeDtypeStruct((nseq, h, p), F32),
                   jax.ShapeDtypeStruct((nseq, h, p), F32),
                   jax.ShapeDtypeStruct((dff // tf, d // 2, tf), U32),
                   jax.ShapeDtypeStruct((dff // 2, d), U32)],
        grid=(steps + 1,),
        in_specs=[pl.BlockSpec((tm, proj.shape[1]), lambda i: (cur(i), 0)),
                  pl.BlockSpec((h, proj.shape[1]),
                               lambda i: (jnp.maximum(cur(i) * (tm // h) - 1, 0), 0)),
                  pl.BlockSpec((tm, d), lambda i: (prev(i), 0)),
                  pl.BlockSpec((8, N_ADA * d), lambda i: (prompt_row // 8, 0)),
                  _resident(grp_bf.shape),
                  _resident(pscale.shape),
                  _resident(convw.shape),
                  _resident(wpu_bf.shape),
                  _resident(wcu_bf.shape),
                  _resident(wo_bf.shape),
                  _resident(lng.shape),
                  _resident(lnb.shape),
                  pl.BlockSpec((d // steps, dff), lambda i: (cur(i), 0)),
                  pl.BlockSpec((dff // steps, d), lambda i: (cur(i), 0))],
        out_specs=[pl.BlockSpec((tm, d), lambda i: (prev(i), 0)),
                   pl.BlockSpec((tm, d), lambda i: (prev(i), 0)),
                   pl.BlockSpec((1, h, p), lambda i: (cur(i) // (seq // tm), 0, 0)),
                   pl.BlockSpec((1, h, p), lambda i: (cur(i) // (seq // tm), 0, 0)),
                   pl.BlockSpec((dff // tf, d // steps // 2, tf), lambda i: (0, cur(i), 0)),
                   pl.BlockSpec((dff // steps // 2, d), lambda i: (cur(i), 0))],
        scratch_shapes=[pltpu.VMEM((tm + h, p), F32),
                        pltpu.VMEM((tm + h, p), F32),
                        pltpu.VMEM((tm, p), BF16),
                        pltpu.VMEM((tm, p), BF16),
                        pltpu.VMEM((tm, d), BF16)],
        compiler_params=_params(("arbitrary",)),
        name="mixer",
    )(proj, proj, xp, ada, grp_bf, pscale, convw, wpu_bf, wcu_bf, wo_bf, lng, lnb, w1, w2)


def _mixer_step_body(proj_ref, pool_ref, cstate_ref, x_ref, ada_ref, grp_ref, pscale_ref,
                     convw_ref, wpu_ref, wcu_ref, wo_ref, lng_ref, lnb_ref,
                     x1_ref, u2_ref, npool_ref, nconv_ref, pm_ref, *, d, alpha):
    p = d // 2
    gw = p // len(POOL_WINDOWS)
    z = proj_ref[:, 0:p]
    for g, w in enumerate(POOL_WINDOWS):
        cols = slice(g * gw, (g + 1) * gw)
        s = z[:, cols]
        for k in range(1, w):
            s = s + pool_ref[POOL_HIST - k, :, cols]
        inv_cnt = 1.0 / min(w, PAST_LEN + 1)
        pooled = (s * inv_cnt - z[:, cols]).astype(BF16)
        og = jnp.dot(pooled, _unpack_rows(grp_ref[g * gw // 2:(g + 1) * gw // 2, :]),
                         preferred_element_type=F32)
        pm_ref[:, cols] = (og * pscale_ref[:, cols]).astype(BF16)
    for r in range(POOL_HIST - 1):
        npool_ref[r] = pool_ref[r + 1]
    npool_ref[POOL_HIST - 1] = z

    xc = proj_ref[:, p:2 * p]
    bc = proj_ref[:, 2 * p:3 * p]
    cc = proj_ref[:, 3 * p:4 * p]
    v = cc * xc
    conv = convw_ref[0:1, :] * cstate_ref[0]
    conv = conv + convw_ref[1:2, :] * cstate_ref[1]
    conv = conv + convw_ref[2:3, :] * v
    nconv_ref[0] = cstate_ref[1]
    nconv_ref[1] = v

    gp = proj_ref[:, 4 * p:4 * p + d]
    gc = proj_ref[:, 4 * p + d:4 * p + 2 * d]
    g1 = ada_ref[:, 2 * d:3 * d]
    sh2 = ada_ref[:, 3 * d:4 * d]
    sc2 = ada_ref[:, 4 * d:5 * d]

    def store(x1, u2):
        x1_ref[...] = x1
        u2_ref[...] = u2

    slab = (lambda: (pm_ref[...], (bc * conv).astype(BF16)), lambda: (gp, gc),
            lambda: x_ref[...], store)
    _mix_tail([slab], g1, sc2, sh2, wpu_ref, wcu_ref, wo_ref, lng_ref, lnb_ref, alpha)


def _mixer_step(proj_s, pool_state, conv_state, xs, ada, grp_bf, pscale, convw, wpu_bf, wcu_bf,
                wo_bf, lng, lnb, *, alpha):
    ms, d = xs.shape
    p = d // 2
    body = functools.partial(_mixer_step_body, d=d, alpha=alpha)
    return pl.pallas_call(
        body,
        out_shape=[jax.ShapeDtypeStruct((ms, d), F32),
                   jax.ShapeDtypeStruct((ms, d), BF16),
                   jax.ShapeDtypeStruct(pool_state.shape, F32),
                   jax.ShapeDtypeStruct(conv_state.shape, F32)],
        grid=(1,),
        in_specs=[_resident(proj_s.shape),
                  _resident(pool_state.shape),
                  _resident(conv_state.shape),
                  _resident((ms, d)),
                  _resident((ms, N_ADA * d)),
                  _resident(grp_bf.shape),
                  _resident(pscale.shape),
                  _resident(convw.shape),
                  _resident(wpu_bf.shape),
                  _resident(wcu_bf.shape),
                  _resident(wo_bf.shape),
                  _resident(lng.shape),
                  _resident(lnb.shape)],
        out_specs=[_resident((ms, d)),
                   _resident((ms, d)),
                   _resident(pool_state.shape),
                   _resident(conv_state.shape)],
        scratch_shapes=[pltpu.VMEM((ms, p), BF16)],
        compiler_params=_params(("arbitrary",)),
        name="mixer_step",
    )(proj_s, pool_state, conv_state, xs, ada, grp_bf, pscale, convw, wpu_bf, wcu_bf, wo_bf, lng, lnb)


def _mlp_body(u2_ref, w1_ref, b1_ref, w2_ref, x1_ref, g2_ref, b2_ref, lng_ref, lnb_ref,
              u2s_ref, x1s_ref, g2s_ref, o_ref, os_ref, acc_ref, accs_ref,
              *, nm, nf, ne, te, tm, alpha, rows_per_cond):
    m = pl.program_id(0)
    f = pl.program_id(1)
    slot = m % 2

    def chunk(u2):
        hmid = jnp.dot(u2, _unpack_rows(w1_ref[0]), preferred_element_type=F32) + b1_ref[f]
        hmid = jnp.square(jnp.maximum(hmid, 0.0))
        return jnp.dot(hmid.astype(BF16), _unpack_rows(w2_ref[...]),
                       preferred_element_type=F32)

    def accumulate(first, between=None):
        n_chunks = tm // ROW_CHUNK
        for c in range(n_chunks):
            if between is not None:
                between(c, n_chunks)
            rows = slice(c * ROW_CHUNK, (c + 1) * ROW_CHUNK)
            part = chunk(u2_ref[rows, :])
            if first:
                acc_ref[slot, rows, :] = part
            else:
                acc_ref[slot, rows, :] += part

    def accumulate_singles(first):
        part = chunk(u2s_ref[...])
        if first:
            accs_ref[...] = part
        else:
            accs_ref[...] += part

    def finish(x1, g2, acc):
        return _layernorm(alpha * x1 + g2 * (acc + b2_ref[...]), lng_ref[...], lnb_ref[...])

    def epilogue(piece=0, n_pieces=1):
        tp = te // n_pieces
        base = pl.multiple_of((f - 1) * te + piece * tp, tp)
        g2 = g2_ref[pl.ds(((m - 1) * tm) // rows_per_cond, 1), :]
        o_ref[piece * tp:(piece + 1) * tp, :] = finish(
            x1_ref[piece * tp:(piece + 1) * tp, :], g2, acc_ref[1 - slot, pl.ds(base, tp), :])

    has_main = m < nm
    has_epi = jnp.logical_and(m >= 1, jnp.logical_and(f >= 1, f <= ne))

    @pl.when(jnp.logical_and(has_main, f == 0))
    def _():
        accumulate(True)
        pl.when(m == 0)(functools.partial(accumulate_singles, True))

    @pl.when(jnp.logical_and(has_main, has_epi))
    def _():
        accumulate(False, between=epilogue)

    @pl.when(jnp.logical_and(has_main, jnp.logical_and(f > 0, jnp.logical_not(has_epi))))
    def _():
        accumulate(False)
        pl.when(m == 0)(functools.partial(accumulate_singles, False))

    @pl.when(jnp.logical_and(jnp.logical_not(has_main), has_epi))
    def _():
        epilogue()

    @pl.when(jnp.logical_and(m == 1, f == 1))
    def _():
        os_ref[...] = finish(x1s_ref[...], g2s_ref[...], accs_ref[...])


def _mlp(u2, x1, u2s, x1s, ada, w1b, b1, w2b, b2, lng, lnb, *, alpha, rows_per_cond,
         cond_row, tm, te):
    m, d = u2.shape
    ms = u2s.shape[0]
    nf, _, tf = w1b.shape
    nm = m // tm
    ne = tm // te
    assert ne < nf
    g2_col = N_ADA - 1
    body = functools.partial(_mlp_body, nm=nm, nf=nf, ne=ne, te=te, tm=tm, alpha=alpha,
                             rows_per_cond=rows_per_cond)
    fw = lambda mi, f: jnp.where(mi == nm, nf - 1, f)
    ep = lambda mi, f: (jnp.where(mi == 0, 0, (mi - 1) * ne + jnp.clip(f - 1, 0, ne - 1)), 0)
    return pl.pallas_call(
        body,
        out_shape=[jax.ShapeDtypeStruct((m, d), F32),
                   jax.ShapeDtypeStruct((ms, d), F32)],
        grid=(nm + 1, nf),
        in_specs=[pl.BlockSpec((tm, d), lambda mi, f: (jnp.minimum(mi, nm - 1), 0)),
                  pl.BlockSpec((1, d // 2, tf), lambda mi, f: (fw(mi, f), 0, 0)),
                  _resident((nf, 1, tf)),
                  pl.BlockSpec((tf // 2, d), lambda mi, f: (fw(mi, f), 0)),
                  pl.BlockSpec((te, d), ep),
                  pl.BlockSpec((8, d), lambda mi, f: (cond_row // 8, g2_col)),
                  _resident((1, d)),
                  _resident((1, d)),
                  _resident((1, d)),
                  _resident((ms, d)),
                  _resident((ms, d)),
                  pl.BlockSpec((ms, d), lambda mi, f: (0, g2_col),
                               pipeline_mode=pl.Buffered(1))],
        out_specs=[pl.BlockSpec((te, d), ep),
                   pl.BlockSpec((ms, d), lambda mi, f: (0, 0))],
        scratch_shapes=[pltpu.VMEM((2, tm, d), F32),
                        pltpu.VMEM((ms, d), F32)],
        compiler_params=_params(("arbitrary", "arbitrary")),
        name="mlp",
    )(u2, w1b, b1.reshape(nf, 1, tf), w2b, x1, ada, b2.reshape(1, d), lng, lnb, u2s, x1s, ada)


def kernel(x_prompt, x_sample, state_pool, state_conv, c_prompt, c_sample, w_ada, b_ada, w_in,
           pool_grp_w, pool_scale, conv_w, w_pool_up, w_conv_up, w_o, ln1_g, ln1_b, w_ff1,
           b_ff1, w_ff2, b_ff2, ln2_g, ln2_b):
    depth = w_ada.shape[0]
    nb, seq, d = x_prompt.shape
    ms, dec_seq, _ = x_sample.shape
    assert dec_seq == 1
    p = d // 2
    alpha = (2 * depth) ** 0.25

    prompt_row = ms
    pad = (-(ms + nb)) % 8
    c_all = jnp.concatenate([c_sample, c_prompt, jnp.zeros((pad, d), F32)], axis=0)

    xp = x_prompt.reshape(nb * seq, d)
    xs = x_sample.reshape(ms, d)
    pool_p, conv_p, pool_s, conv_s = [], [], [], []
    for l in range(depth):
        ada = _ada(c_all, w_ada[l], b_ada[l])
        pscale = pool_scale[l].reshape(1, p)
        lng1, lnb1 = ln1_g[l].reshape(1, d), ln1_b[l].reshape(1, d)
        lng2, lnb2 = ln2_g[l].reshape(1, d), ln2_b[l].reshape(1, d)

        proj_p, proj_s, (grp_bf, wpu_bf, wcu_bf, wo_bf) = _inproj(
            xp, xs, ada, w_in[l],
            [pool_grp_w[l].reshape(p, -1), w_pool_up[l], w_conv_up[l], w_o[l]],
            seq=seq, prompt_row=prompt_row)

        x1p, u2p, pst, cst, w1b, w2b = _mixer(
            proj_p, xp, ada, grp_bf, pscale, conv_w[l], wpu_bf, wcu_bf, wo_bf, lng1, lnb1,
            w_ff1[l], w_ff2[l], seq=seq, prompt_row=prompt_row, alpha=alpha, tf=MLP_FF_CHUNK)
        x1s, u2s, npool, nconv = _mixer_step(
            proj_s, jnp.transpose(state_pool[l], (1, 0, 2)),
            jnp.transpose(state_conv[l], (1, 0, 2)), xs, ada, grp_bf, pscale, conv_w[l],
            wpu_bf, wcu_bf, wo_bf, lng1, lnb1, alpha=alpha)

        xp, xs = _mlp(u2p, x1p, u2s, x1s, ada, w1b, b_ff1[l], w2b, b_ff2[l], lng2, lnb2,
                      alpha=alpha, rows_per_cond=seq, cond_row=prompt_row,
                      tm=MLP_TOKEN_TILE, te=MLP_EPILOGUE_ROWS)

        pool_p.append(pst[:, HIST_ROWS - POOL_HIST:, :])
        conv_p.append(cst[:, HIST_ROWS - CONV_HIST:, :])
        pool_s.append(jnp.transpose(npool, (1, 0, 2)))
        conv_s.append(jnp.transpose(nconv, (1, 0, 2)))

    return (xp.reshape(nb, seq, d), xs.reshape(ms, dec_seq, d),
            jnp.stack(pool_p), jnp.stack(conv_p), jnp.stack(pool_s), jnp.stack(conv_s))
```

```python
import functools

import jax
import jax.numpy as jnp
from jax import lax
from jax.experimental import pallas as pl
from jax.experimental.pallas import tpu as pltpu

F32 = jnp.float32
BF16 = jnp.bfloat16
U32 = jnp.uint32

POOL_WINDOWS = (2, 4, 8, 16)
POOL_HIST = max(POOL_WINDOWS) - 1
CONV_K = 3
CONV_HIST = CONV_K - 1
N_ADA = 6
PAST_LEN = 16384
LN_EPS = 1e-5

VMEM_LIMIT_BYTES_V7X = 56 * 1024 * 1024
HIST_ROWS = 16
ROW_CHUNK = 256
MIXER_COL_BLOCK = 256
MIXER_FINISH_SLABS = 4
MLP_TOKEN_TILE = 1024
MLP_FF_CHUNK = 1024
MLP_EPILOGUE_ROWS = 256


def _resident(shape):
    zeros = (0,) * len(shape)
    return pl.BlockSpec(shape, lambda *_: zeros, pipeline_mode=pl.Buffered(1))


def _params(semantics, vmem_limit_bytes=VMEM_LIMIT_BYTES_V7X):
    return pltpu.CompilerParams(dimension_semantics=semantics,
                                vmem_limit_bytes=vmem_limit_bytes)


def _layernorm(xf, g, b):
    mu = jnp.mean(xf, axis=-1, keepdims=True)
    var = jnp.mean(jnp.square(xf - mu), axis=-1, keepdims=True)
    return (xf - mu) * lax.rsqrt(var + LN_EPS) * g + b


def _ada_body(c_ref, w_ref, b_ref, o_ref):
    o_ref[...] = jnp.dot(c_ref[...].astype(BF16), w_ref[...].astype(BF16),
                         preferred_element_type=F32) + b_ref[...]


def _ada(c_all, w_ada, b_ada, tn=1024):
    m, d = c_all.shape
    n = w_ada.shape[1]
    return pl.pallas_call(
        _ada_body,
        out_shape=jax.ShapeDtypeStruct((m, n), F32),
        grid=(n // tn,),
        in_specs=[pl.BlockSpec((m, d), lambda j: (0, 0)),
                  pl.BlockSpec((d, tn), lambda j: (0, j)),
                  pl.BlockSpec((1, tn), lambda j: (0, j))],
        out_specs=pl.BlockSpec((m, tn), lambda j: (0, j)),
        compiler_params=_params(("arbitrary",)),
        name="ada",
    )(c_all, w_ada, b_ada.reshape(1, n))


def _pack_rows(x_bf16):
    return pltpu.bitcast(x_bf16, U32)


def _unpack_rows(x_u32):
    return pltpu.bitcast(x_u32, BF16)


def _inproj_body(lhs_ref, adap_ref, w_ref, xs_ref, adas_ref, *rest, tiles_per_seq, d, tm,
                 n_side, side_steps, modulate):
    side_in = rest[:n_side]
    outs = rest[n_side:]
    o_ref, os_ref = outs[:2]
    n_main = 3 if modulate else 2
    u_ref = outs[2] if modulate else None
    side_out = outs[n_main:n_main + n_side]
    wb_ref = outs[n_main + n_side]
    i = pl.program_id(1)

    if n_side:
        @pl.when(pl.program_id(0) * pl.num_programs(1) + i < side_steps)
        def _():
            for src, dst in zip(side_in, side_out):
                dst[...] = _pack_rows(src[...].astype(BF16))

    @pl.when(i == 0)
    def _():
        wb_ref[...] = w_ref[...].astype(BF16)
        us = xs_ref[...] * (1.0 + adas_ref[:, d:2 * d]) + adas_ref[:, 0:d]
        os_ref[...] = jnp.dot(us.astype(BF16), wb_ref[...], preferred_element_type=F32)

    b = i // tiles_per_seq
    sh1 = adap_ref[pl.ds(b, 1), 0:d]
    sc1 = adap_ref[pl.ds(b, 1), d:2 * d]
    for r in range(0, tm, ROW_CHUNK):
        rows = slice(r, r + ROW_CHUNK)
        if modulate:
            u = (lhs_ref[rows, :] * (1.0 + sc1) + sh1).astype(BF16)
            u_ref[rows, :] = u
        else:
            u = lhs_ref[rows, :]
        o_ref[rows, :] = jnp.dot(u, wb_ref[...], preferred_element_type=F32).astype(BF16)


def _inproj(lhs, xs, ada, w_in, side_ws, *, col0, ncols, modulate, seq, prompt_row,
            tm=1024, tn=1024, side_steps=16):
    m, d = lhs.shape
    ms = xs.shape[0]
    ni = m // tm
    jb = col0 // tn
    body = functools.partial(_inproj_body, tiles_per_seq=seq // tm, d=d, tm=tm,
                             n_side=len(side_ws), side_steps=side_steps, modulate=modulate)
    side_idx = lambda j, i: (jnp.minimum(j * ni + i, side_steps - 1), 0)
    side_in = [pl.BlockSpec((w.shape[0] // side_steps, w.shape[1]), side_idx) for w in side_ws]
    side_out = [pl.BlockSpec((w.shape[0] // side_steps // 2, w.shape[1]), side_idx)
                for w in side_ws]
    side_shape = [jax.ShapeDtypeStruct((w.shape[0] // 2, w.shape[1]), U32) for w in side_ws]
    u_shape = [jax.ShapeDtypeStruct((m, d), BF16)] if modulate else []
    u_spec = [pl.BlockSpec((tm, d), lambda j, i: (i, 0))] if modulate else []
    out = pl.pallas_call(
        body,
        out_shape=[jax.ShapeDtypeStruct((m, ncols), BF16),
                   jax.ShapeDtypeStruct((ms, ncols), F32)] + u_shape + side_shape,
        grid=(ncols // tn, ni),
        in_specs=[pl.BlockSpec((tm, d), lambda j, i: (i, 0)),
                  pl.BlockSpec((8, 2 * d), lambda j, i: (prompt_row // 8, 0)),
                  pl.BlockSpec((d, tn), lambda j, i: (0, j + jb)),
                  _resident((ms, d)),
                  _resident((ms, 2 * d))] + side_in,
        out_specs=[pl.BlockSpec((tm, tn), lambda j, i: (i, j)),
                   pl.BlockSpec((ms, tn), lambda j, i: (0, j))] + u_spec + side_out,
        scratch_shapes=[pltpu.VMEM((d, tn), BF16)],
        compiler_params=_params(("arbitrary", "arbitrary")),
        name="inproj_first" if modulate else "inproj_rest",
    )(lhs, ada, w_in, xs, ada, *side_ws)
    n_main = 3 if modulate else 2
    return out[:n_main], out[n_main:]


def _sigmoid(x):
    return 0.5 * jnp.tanh(0.5 * x) + 0.5


def _mix_tail(slabs, g1, sc2, sh2, wpu_ref, wcu_ref, wo_ref, lng_ref, lnb_ref, alpha):
    wpu = _unpack_rows(wpu_ref[...])
    wcu = _unpack_rows(wcu_ref[...])
    wo = _unpack_rows(wo_ref[...])

    def up(s):
        pooled_bf, conv_bf = s[0]()
        return (jnp.dot(pooled_bf, wpu, preferred_element_type=F32),
                jnp.dot(conv_bf, wcu, preferred_element_type=F32))

    def gate(s, y):
        gp, gc = s[1]()
        return (_sigmoid(gp) * y[0] + _sigmoid(gc) * y[1]).astype(BF16)

    def finish(s, m_out):
        x1 = _layernorm(alpha * s[2]() + g1 * m_out, lng_ref[...], lnb_ref[...])
        s[3](x1, (x1 * (1.0 + sc2) + sh2).astype(BF16))

    y = [None] * len(slabs)
    mo = [None] * len(slabs)
    y[0] = up(slabs[0])
    for k, s in enumerate(slabs):
        if k + 1 < len(slabs):
            y[k + 1] = up(slabs[k + 1])
        mo[k] = jnp.dot(gate(s, y[k]), wo, preferred_element_type=F32)
        if k > 0:
            finish(slabs[k - 1], mo[k - 1])
    finish(slabs[-1], mo[-1])


def _split_cols(first_ref, rest_ref):
    p = first_ref.shape[-1]

    def cols(lo, hi):
        return first_ref[:, lo:hi] if hi <= p else rest_ref[:, lo - p:hi - p]
    return cols


def _mixer_body(pa_ref, pb_ref, ha_ref, hb_ref, x_ref, ada_ref, grp_ref, pscale_ref, convw_ref,
                wpu_ref, wcu_ref, wo_ref, lng_ref, lnb_ref, w1_ref, w2_ref,
                x1_ref, u2_ref, pst_ref, cst_ref, w1b_ref, w2b_ref,
                zz_ref, vv_ref, pm_ref, bc_ref, mixed_ref,
                *, tm, tiles_per_seq, d, alpha, steps):
    i = pl.program_id(0)
    p = d // 2
    gw = p // len(POOL_WINDOWS)
    h = HIST_ROWS
    proj = _split_cols(pa_ref, pb_ref)
    hist = _split_cols(ha_ref, hb_ref)

    cb = MIXER_COL_BLOCK
    ncb = d // cb
    tis = i % tiles_per_seq
    first = tis == 0

    def side_cast():
        tf = w1b_ref.shape[2]
        for fi in range(w1b_ref.shape[0]):
            w1b_ref[fi] = _pack_rows(w1_ref[:, fi * tf:(fi + 1) * tf].astype(BF16))
        w2b_ref[...] = _pack_rows(w2_ref[...].astype(BF16))

    def setup():
        zz_ref[0:h, :] = jnp.where(first, 0.0, hist(0, p).astype(F32))
        zz_ref[h:h + tm, :] = proj(0, p).astype(F32)
        vh = hist(3 * p, 4 * p).astype(F32) * hist(p, 2 * p).astype(F32)
        vv_ref[0:h, :] = jnp.where(first, 0.0, vh)
        vv_ref[h:h + tm, :] = proj(3 * p, 4 * p).astype(F32) * proj(p, 2 * p).astype(F32)
        pst_ref[0] = zz_ref[tm:tm + h, :]
        cst_ref[0] = vv_ref[tm:tm + h, :]

    def pool_group(g):
        w = POOL_WINDOWS[g]
        cols = slice(g * gw, (g + 1) * gw)
        pos = tis * tm + lax.broadcasted_iota(jnp.int32, (tm, 1), 0)
        s = zz_ref[:, cols]
        zt = s[h:, :]
        k = 1
        while k < w:
            s = s + pltpu.roll(s, k, axis=0)
            k *= 2
        inv_cnt = 1.0 / jnp.minimum(w, pos + 1).astype(F32)
        pooled = (s[h:, :] * inv_cnt - zt).astype(BF16)
        og = jnp.dot(pooled, _unpack_rows(grp_ref[g * gw // 2:(g + 1) * gw // 2, :]),
                     preferred_element_type=F32)
        pm_ref[:, cols] = (og * pscale_ref[:, cols]).astype(BF16)

    def conv_branch():
        conv = convw_ref[0:1, :] * vv_ref[h - 2:h - 2 + tm, :]
        conv = conv + convw_ref[1:2, :] * vv_ref[h - 1:h - 1 + tm, :]
        conv = conv + convw_ref[2:3, :] * vv_ref[h:h + tm, :]
        bc_ref[...] = (proj(2 * p, 3 * p).astype(F32) * conv).astype(BF16)

    def up_and_gate(k):
        cols = slice(k * cb, (k + 1) * cb)
        y_p = jnp.dot(pm_ref[...], _unpack_rows(wpu_ref[:, cols]), preferred_element_type=F32)
        y_c = jnp.dot(bc_ref[...], _unpack_rows(wcu_ref[:, cols]), preferred_element_type=F32)
        gp = proj(4 * p + k * cb, 4 * p + (k + 1) * cb).astype(F32)
        gc = proj(4 * p + d + k * cb, 4 * p + d + (k + 1) * cb).astype(F32)
        mixed_ref[:, cols] = (_sigmoid(gp) * y_p + _sigmoid(gc) * y_c).astype(BF16)

    def out_block(k):
        return jnp.dot(mixed_ref[...], _unpack_rows(wo_ref[:, k * cb:(k + 1) * cb]),
                       preferred_element_type=F32)

    def finish_rows(mo, r):
        nr = tm // MIXER_FINISH_SLABS
        rows = slice(r * nr, (r + 1) * nr)
        b = (i - 1) // tiles_per_seq
        g1 = ada_ref[pl.ds(b, 1), 2 * d:3 * d]
        sh2 = ada_ref[pl.ds(b, 1), 3 * d:4 * d]
        sc2 = ada_ref[pl.ds(b, 1), 4 * d:5 * d]
        x1 = _layernorm(alpha * x_ref[rows, :] + g1 * mo[rows, :], lng_ref[...], lnb_ref[...])
        x1_ref[rows, :] = x1
        u2_ref[rows, :] = (x1 * (1.0 + sc2) + sh2).astype(BF16)

    def step(has_head, has_tail):
        front = [side_cast, setup] + [functools.partial(pool_group, g)
                                      for g in range(len(POOL_WINDOWS))] + [conv_branch]
        mo_blocks = []
        for k in range(ncb):
            if has_tail:
                mo_blocks.append(out_block(k))
            if has_head and k < len(front):
                front[k]()
        if has_head:
            for piece in front[ncb:]:
                piece()
        mo = jnp.concatenate(mo_blocks, axis=1) if has_tail else None
        fin_at = {(r + 1) * ncb // MIXER_FINISH_SLABS - 1: r for r in range(MIXER_FINISH_SLABS)}
        for k in range(ncb):
            if has_head:
                up_and_gate(k)
            if has_tail and k in fin_at:
                finish_rows(mo, fin_at[k])

    pl.when(i == 0)(functools.partial(step, True, False))
    pl.when(jnp.logical_and(i > 0, i < steps))(functools.partial(step, True, True))
    pl.when(i == steps)(functools.partial(step, False, True))


def _mixer(proj_a, proj_b, xp, ada, grp_bf, pscale, convw, wpu_bf, wcu_bf, wo_bf, lng, lnb,
           w1, w2, *, seq, prompt_row, alpha, tf, tm=256):
    m, d = xp.shape
    p = d // 2
    nseq = m // seq
    h = HIST_ROWS
    steps = m // tm
    dff = w1.shape[1]
    body = functools.partial(_mixer_body, tm=tm, tiles_per_seq=seq // tm, d=d, alpha=alpha,
                             steps=steps)
    cur = lambda i: jnp.minimum(i, steps - 1)
    prev = lambda i: jnp.maximum(i - 1, 0)
    hist_idx = lambda i: (jnp.maximum(cur(i) * (tm // h) - 1, 0), 0)
    return pl.pallas_call(
        body,
        out_shape=[jax.ShapeDtypeStruct((m, d), F32),
                   jax.ShapeDtypeStruct((m, d), BF16),
                   jax.ShapeDtypeStruct((nseq, h, p), F32),
                   jax.ShapeDtypeStruct((nseq, h, p), F32),
                   jax.ShapeDtypeStruct((dff // tf, d // 2, tf), U32),
                   jax.ShapeDtypeStruct((dff // 2, d), U32)],
        grid=(steps + 1,),
        in_specs=[pl.BlockSpec((tm, proj_a.shape[1]), lambda i: (cur(i), 0)),
                  pl.BlockSpec((tm, proj_b.shape[1]), lambda i: (cur(i), 0)),
                  pl.BlockSpec((h, proj_a.shape[1]), hist_idx),
                  pl.BlockSpec((h, proj_b.shape[1]), hist_idx),
                  pl.BlockSpec((tm, d), lambda i: (prev(i), 0)),
                  pl.BlockSpec((8, N_ADA * d), lambda i: (prompt_row // 8, 0)),
                  _resident(grp_bf.shape),
                  _resident(pscale.shape),
                  _resident(convw.shape),
                  _resident(wpu_bf.shape),
                  _resident(wcu_bf.shape),
                  _resident(wo_bf.shape),
                  _resident(lng.shape),
                  _resident(lnb.shape),
                  pl.BlockSpec((d // steps, dff), lambda i: (cur(i), 0)),
                  pl.BlockSpec((dff // steps, d), lambda i: (cur(i), 0))],
        out_specs=[pl.BlockSpec((tm, d), lambda i: (prev(i), 0)),
                   pl.BlockSpec((tm, d), lambda i: (prev(i), 0)),
                   pl.BlockSpec((1, h, p), lambda i: (cur(i) // (seq // tm), 0, 0)),
                   pl.BlockSpec((1, h, p), lambda i: (cur(i) // (seq // tm), 0, 0)),
                   pl.BlockSpec((dff // tf, d // steps // 2, tf), lambda i: (0, cur(i), 0)),
                   pl.BlockSpec((dff // steps // 2, d), lambda i: (cur(i), 0))],
        scratch_shapes=[pltpu.VMEM((tm + h, p), F32),
                        pltpu.VMEM((tm + h, p), F32),
                        pltpu.VMEM((tm, p), BF16),
                        pltpu.VMEM((tm, p), BF16),
                        pltpu.VMEM((tm, d), BF16)],
        compiler_params=_params(("arbitrary",)),
        name="mixer",
    )(proj_a, proj_b, proj_a, proj_b, xp, ada, grp_bf, pscale, convw, wpu_bf, wcu_bf, wo_bf,
      lng, lnb, w1, w2)


def _mixer_step_body(pa_ref, pb_ref, pool_ref, cstate_ref, x_ref, ada_ref, grp_ref, pscale_ref,
                     convw_ref, wpu_ref, wcu_ref, wo_ref, lng_ref, lnb_ref,
                     x1_ref, u2_ref, npool_ref, nconv_ref, pm_ref, *, d, alpha):
    p = d // 2
    gw = p // len(POOL_WINDOWS)
    proj = _split_cols(pa_ref, pb_ref)
    z = proj(0, p)
    for g, w in enumerate(POOL_WINDOWS):
        cols = slice(g * gw, (g + 1) * gw)
        s = z[:, cols]
        for k in range(1, w):
            s = s + pool_ref[POOL_HIST - k, :, cols]
        inv_cnt = 1.0 / min(w, PAST_LEN + 1)
        pooled = (s * inv_cnt - z[:, cols]).astype(BF16)
        og = jnp.dot(pooled, _unpack_rows(grp_ref[g * gw // 2:(g + 1) * gw // 2, :]),
                         preferred_element_type=F32)
        pm_ref[:, cols] = (og * pscale_ref[:, cols]).astype(BF16)
    for r in range(POOL_HIST - 1):
        npool_ref[r] = pool_ref[r + 1]
    npool_ref[POOL_HIST - 1] = z

    xc = proj(p, 2 * p)
    bc = proj(2 * p, 3 * p)
    cc = proj(3 * p, 4 * p)
    v = cc * xc
    conv = convw_ref[0:1, :] * cstate_ref[0]
    conv = conv + convw_ref[1:2, :] * cstate_ref[1]
    conv = conv + convw_ref[2:3, :] * v
    nconv_ref[0] = cstate_ref[1]
    nconv_ref[1] = v

    gp = proj(4 * p, 4 * p + d)
    gc = proj(4 * p + d, 4 * p + 2 * d)
    g1 = ada_ref[:, 2 * d:3 * d]
    sh2 = ada_ref[:, 3 * d:4 * d]
    sc2 = ada_ref[:, 4 * d:5 * d]

    def store(x1, u2):
        x1_ref[...] = x1
        u2_ref[...] = u2

    slab = (lambda: (pm_ref[...], (bc * conv).astype(BF16)), lambda: (gp, gc),
            lambda: x_ref[...], store)
    _mix_tail([slab], g1, sc2, sh2, wpu_ref, wcu_ref, wo_ref, lng_ref, lnb_ref, alpha)


def _mixer_step(proj_a, proj_b, pool_state, conv_state, xs, ada, grp_bf, pscale, convw,
                wpu_bf, wcu_bf, wo_bf, lng, lnb, *, alpha):
    ms, d = xs.shape
    p = d // 2
    body = functools.partial(_mixer_step_body, d=d, alpha=alpha)
    return pl.pallas_call(
        body,
        out_shape=[jax.ShapeDtypeStruct((ms, d), F32),
                   jax.ShapeDtypeStruct((ms, d), BF16),
                   jax.ShapeDtypeStruct(pool_state.shape, F32),
                   jax.ShapeDtypeStruct(conv_state.shape, F32)],
        grid=(1,),
        in_specs=[_resident(proj_a.shape),
                  _resident(proj_b.shape),
                  _resident(pool_state.shape),
                  _resident(conv_state.shape),
                  _resident((ms, d)),
                  _resident((ms, N_ADA * d)),
                  _resident(grp_bf.shape),
                  _resident(pscale.shape),
                  _resident(convw.shape),
                  _resident(wpu_bf.shape),
                  _resident(wcu_bf.shape),
                  _resident(wo_bf.shape),
                  _resident(lng.shape),
                  _resident(lnb.shape)],
        out_specs=[_resident((ms, d)),
                   _resident((ms, d)),
                   _resident(pool_state.shape),
                   _resident(conv_state.shape)],
        scratch_shapes=[pltpu.VMEM((ms, p), BF16)],
        compiler_params=_params(("arbitrary",)),
        name="mixer_step",
    )(proj_a, proj_b, pool_state, conv_state, xs, ada, grp_bf, pscale, convw, wpu_bf, wcu_bf,
      wo_bf, lng, lnb)


def _mlp_body(u2_ref, w1_ref, b1_ref, w2_ref, x1_ref, g2_ref, b2_ref, lng_ref, lnb_ref,
              u2s_ref, x1s_ref, g2s_ref, o_ref, os_ref, acc_ref, accs_ref,
              *, nm, nf, ne, te, tm, alpha, rows_per_cond):
    m = pl.program_id(0)
    f = pl.program_id(1)
    slot = m % 2

    def chunk(u2):
        hmid = jnp.dot(u2, _unpack_rows(w1_ref[0]), preferred_element_type=F32) + b1_ref[f]
        hmid = jnp.square(jnp.maximum(hmid, 0.0))
        return jnp.dot(hmid.astype(BF16), _unpack_rows(w2_ref[...]),
                       preferred_element_type=F32)

    def accumulate(first, between=None):
        n_chunks = tm // ROW_CHUNK
        for c in range(n_chunks):
            if between is not None:
                between(c, n_chunks)
            rows = slice(c * ROW_CHUNK, (c + 1) * ROW_CHUNK)
            part = chunk(u2_ref[rows, :])
            if first:
                acc_ref[slot, rows, :] = part
            else:
                acc_ref[slot, rows, :] += part

    def accumulate_singles(first):
        part = chunk(u2s_ref[...])
        if first:
            accs_ref[...] = part
        else:
            accs_ref[...] += part

    def finish(x1, g2, acc):
        return _layernorm(alpha * x1 + g2 * (acc + b2_ref[...]), lng_ref[...], lnb_ref[...])

    def epilogue(piece=0, n_pieces=1):
        tp = te // n_pieces
        base = pl.multiple_of((f - 1) * te + piece * tp, tp)
        g2 = g2_ref[pl.ds(((m - 1) * tm) // rows_per_cond, 1), :]
        o_ref[piece * tp:(piece + 1) * tp, :] = finish(
            x1_ref[piece * tp:(piece + 1) * tp, :], g2, acc_ref[1 - slot, pl.ds(base, tp), :])

    has_main = m < nm
    has_epi = jnp.logical_and(m >= 1, jnp.logical_and(f >= 1, f <= ne))

    @pl.when(jnp.logical_and(has_main, f == 0))
    def _():
        accumulate(True)
        pl.when(m == 0)(functools.partial(accumulate_singles, True))

    @pl.when(jnp.logical_and(has_main, has_epi))
    def _():
        accumulate(False, between=epilogue)

    @pl.when(jnp.logical_and(has_main, jnp.logical_and(f > 0, jnp.logical_not(has_epi))))
    def _():
        accumulate(False)
        pl.when(m == 0)(functools.partial(accumulate_singles, False))

    @pl.when(jnp.logical_and(jnp.logical_not(has_main), has_epi))
    def _():
        epilogue()

    @pl.when(jnp.logical_and(m == 1, f == 1))
    def _():
        os_ref[...] = finish(x1s_ref[...], g2s_ref[...], accs_ref[...])


def _mlp(u2, x1, u2s, x1s, ada, w1b, b1, w2b, b2, lng, lnb, *, alpha, rows_per_cond,
         cond_row, tm, te):
    m, d = u2.shape
    ms = u2s.shape[0]
    nf, _, tf = w1b.shape
    nm = m // tm
    ne = tm // te
    assert ne < nf
    g2_col = N_ADA - 1
    body = functools.partial(_mlp_body, nm=nm, nf=nf, ne=ne, te=te, tm=tm, alpha=alpha,
                             rows_per_cond=rows_per_cond)
    fw = lambda mi, f: jnp.where(mi == nm, nf - 1, f)
    ep = lambda mi, f: (jnp.where(mi == 0, 0, (mi - 1) * ne + jnp.clip(f - 1, 0, ne - 1)), 0)
    return pl.pallas_call(
        body,
        out_shape=[jax.ShapeDtypeStruct((m, d), F32),
                   jax.ShapeDtypeStruct((ms, d), F32)],
        grid=(nm + 1, nf),
        in_specs=[pl.BlockSpec((tm, d), lambda mi, f: (jnp.minimum(mi, nm - 1), 0)),
                  pl.BlockSpec((1, d // 2, tf), lambda mi, f: (fw(mi, f), 0, 0)),
                  _resident((nf, 1, tf)),
                  pl.BlockSpec((tf // 2, d), lambda mi, f: (fw(mi, f), 0)),
                  pl.BlockSpec((te, d), ep),
                  pl.BlockSpec((8, d), lambda mi, f: (cond_row // 8, g2_col)),
                  _resident((1, d)),
                  _resident((1, d)),
                  _resident((1, d)),
                  _resident((ms, d)),
                  _resident((ms, d)),
                  pl.BlockSpec((ms, d), lambda mi, f: (0, g2_col),
                               pipeline_mode=pl.Buffered(1))],
        out_specs=[pl.BlockSpec((te, d), ep),
                   pl.BlockSpec((ms, d), lambda mi, f: (0, 0))],
        scratch_shapes=[pltpu.VMEM((2, tm, d), F32),
                        pltpu.VMEM((ms, d), F32)],
        compiler_params=_params(("arbitrary", "arbitrary")),
        name="mlp",
    )(u2, w1b, b1.reshape(nf, 1, tf), w2b, x1, ada, b2.reshape(1, d), lng, lnb, u2s, x1s, ada)


def kernel(x_prompt, x_sample, state_pool, state_conv, c_prompt, c_sample, w_ada, b_ada, w_in,
           pool_grp_w, pool_scale, conv_w, w_pool_up, w_conv_up, w_o, ln1_g, ln1_b, w_ff1,
           b_ff1, w_ff2, b_ff2, ln2_g, ln2_b):
    depth = w_ada.shape[0]
    nb, seq, d = x_prompt.shape
    ms, dec_seq, _ = x_sample.shape
    assert dec_seq == 1
    p = d // 2
    alpha = (2 * depth) ** 0.25

    prompt_row = ms
    pad = (-(ms + nb)) % 8
    c_all = jnp.concatenate([c_sample, c_prompt, jnp.zeros((pad, d), F32)], axis=0)

    xp = x_prompt.reshape(nb * seq, d)
    xs = x_sample.reshape(ms, d)
    pool_p, conv_p, pool_s, conv_s = [], [], [], []
    for l in range(depth):
        ada = _ada(c_all, w_ada[l], b_ada[l])
        pscale = pool_scale[l].reshape(1, p)
        lng1, lnb1 = ln1_g[l].reshape(1, d), ln1_b[l].reshape(1, d)
        lng2, lnb2 = ln2_g[l].reshape(1, d), ln2_b[l].reshape(1, d)

        n_in = w_in.shape[2]
        (pa_p, pa_s, u_bf), _ = _inproj(xp, xs, ada, w_in[l], [], col0=0, ncols=p,
                                        modulate=True, seq=seq, prompt_row=prompt_row)
        (pb_p, pb_s), (grp_bf, wpu_bf, wcu_bf, wo_bf) = _inproj(
            u_bf, xs, ada, w_in[l],
            [pool_grp_w[l].reshape(p, -1), w_pool_up[l], w_conv_up[l], w_o[l]],
            col0=p, ncols=n_in - p, modulate=False, seq=seq, prompt_row=prompt_row)

        x1p, u2p, pst, cst, w1b, w2b = _mixer(
            pa_p, pb_p, xp, ada, grp_bf, pscale, conv_w[l], wpu_bf, wcu_bf, wo_bf, lng1, lnb1,
            w_ff1[l], w_ff2[l], seq=seq, prompt_row=prompt_row, alpha=alpha, tf=MLP_FF_CHUNK)
        x1s, u2s, npool, nconv = _mixer_step(
            pa_s, pb_s, jnp.transpose(state_pool[l], (1, 0, 2)),
            jnp.transpose(state_conv[l], (1, 0, 2)), xs, ada, grp_bf, pscale, conv_w[l],
            wpu_bf, wcu_bf, wo_bf, lng1, lnb1, alpha=alpha)

        xp, xs = _mlp(u2p, x1p, u2s, x1s, ada, w1b, b_ff1[l], w2b, b_ff2[l], lng2, lnb2,
                      alpha=alpha, rows_per_cond=seq, cond_row=prompt_row,
                      tm=MLP_TOKEN_TILE, te=MLP_EPILOGUE_ROWS)

        pool_p.append(pst[:, HIST_ROWS - POOL_HIST:, :])
        conv_p.append(cst[:, HIST_ROWS - CONV_HIST:, :])
        pool_s.append(jnp.transpose(npool, (1, 0, 2)))
        conv_s.append(jnp.transpose(nconv, (1, 0, 2)))

    return (xp.reshape(nb, seq, d), xs.reshape(ms, dec_seq, d),
            jnp.stack(pool_p), jnp.stack(conv_p), jnp.stack(pool_s), jnp.stack(conv_s))
```

```python
import functools

import jax
import jax.numpy as jnp
from jax import lax
from jax.experimental import pallas as pl
from jax.experimental.pallas import tpu as pltpu

F32 = jnp.float32
BF16 = jnp.bfloat16
U32 = jnp.uint32

POOL_WINDOWS = (2, 4, 8, 16)
POOL_HIST = max(POOL_WINDOWS) - 1
CONV_K = 3
CONV_HIST = CONV_K - 1
N_ADA = 6
PAST_LEN = 16384
LN_EPS = 1e-5

VMEM_LIMIT_BYTES_V7X = 56 * 1024 * 1024
VMEM_LIMIT_BIG_TILE_BYTES_V7X = 60 * 1024 * 1024
HIST_ROWS = 16
ROW_CHUNK = 256
MIXER_COL_BLOCK = 256
MIXER_FINISH_SLABS = 4
INPROJ_BF16_TOKEN_TILE = 2048
MLP_TOKEN_TILE = 1024
MLP_FF_CHUNK = 1024
MLP_EPILOGUE_ROWS = 256


def _resident(shape):
    zeros = (0,) * len(shape)
    return pl.BlockSpec(shape, lambda *_: zeros, pipeline_mode=pl.Buffered(1))


def _params(semantics, vmem_limit_bytes=VMEM_LIMIT_BYTES_V7X):
    return pltpu.CompilerParams(dimension_semantics=semantics,
                                vmem_limit_bytes=vmem_limit_bytes)


def _layernorm(xf, g, b):
    mu = jnp.mean(xf, axis=-1, keepdims=True)
    var = jnp.mean(jnp.square(xf - mu), axis=-1, keepdims=True)
    return (xf - mu) * lax.rsqrt(var + LN_EPS) * g + b


def _ada_body(c_ref, w_ref, b_ref, o_ref):
    o_ref[...] = jnp.dot(c_ref[...].astype(BF16), w_ref[...].astype(BF16),
                         preferred_element_type=F32) + b_ref[...]


def _ada(c_all, w_ada, b_ada, tn=1024):
    m, d = c_all.shape
    n = w_ada.shape[1]
    return pl.pallas_call(
        _ada_body,
        out_shape=jax.ShapeDtypeStruct((m, n), F32),
        grid=(n // tn,),
        in_specs=[pl.BlockSpec((m, d), lambda j: (0, 0)),
                  pl.BlockSpec((d, tn), lambda j: (0, j)),
                  pl.BlockSpec((1, tn), lambda j: (0, j))],
        out_specs=pl.BlockSpec((m, tn), lambda j: (0, j)),
        compiler_params=pltpu.CompilerParams(
            dimension_semantics=("arbitrary",), vmem_limit_bytes=VMEM_LIMIT_BYTES_V7X,
            allow_input_fusion=[True, False, False]),
        name="ada",
    )(c_all, w_ada, b_ada.reshape(1, n))


def _pack_rows(x_bf16):
    return pltpu.bitcast(x_bf16, U32)


def _unpack_rows(x_u32):
    return pltpu.bitcast(x_u32, BF16)


def _inproj_body(lhs_ref, adap_ref, w_ref, xs_ref, adas_ref, *rest, tiles_per_seq, d, tm,
                 n_side, side_steps, modulate):
    side_in = rest[:n_side]
    outs = rest[n_side:]
    o_ref, os_ref = outs[:2]
    n_main = 3 if modulate else 2
    u_ref = outs[2] if modulate else None
    side_out = outs[n_main:n_main + n_side]
    wb_ref = outs[n_main + n_side]
    i = pl.program_id(1)

    if n_side:
        @pl.when(pl.program_id(0) * pl.num_programs(1) + i < side_steps)
        def _():
            for src, dst in zip(side_in, side_out):
                dst[...] = _pack_rows(src[...].astype(BF16))

    @pl.when(i == 0)
    def _():
        wb_ref[...] = w_ref[...].astype(BF16)
        us = xs_ref[...] * (1.0 + adas_ref[:, d:2 * d]) + adas_ref[:, 0:d]
        os_ref[...] = jnp.dot(us.astype(BF16), wb_ref[...], preferred_element_type=F32)

    b = i // tiles_per_seq
    sh1 = adap_ref[pl.ds(b, 1), 0:d]
    sc1 = adap_ref[pl.ds(b, 1), d:2 * d]
    for r in range(0, tm, ROW_CHUNK):
        rows = slice(r, r + ROW_CHUNK)
        if modulate:
            u = (lhs_ref[rows, :] * (1.0 + sc1) + sh1).astype(BF16)
            u_ref[rows, :] = u
        else:
            u = lhs_ref[rows, :]
        o_ref[rows, :] = jnp.dot(u, wb_ref[...], preferred_element_type=F32).astype(BF16)


def _inproj(lhs, xs, ada, w_in, side_ws, *, col0, ncols, modulate, seq, prompt_row,
            tm=1024, tn=1024, side_steps=16):
    m, d = lhs.shape
    ms = xs.shape[0]
    assert seq % tm == 0 and m % seq == 0 and tm % ROW_CHUNK == 0
    assert col0 % tn == 0 and ncols % tn == 0 and prompt_row % 8 == 0
    ni = m // tm
    jb = col0 // tn
    body = functools.partial(_inproj_body, tiles_per_seq=seq // tm, d=d, tm=tm,
                             n_side=len(side_ws), side_steps=side_steps, modulate=modulate)
    side_idx = lambda j, i: (jnp.minimum(j * ni + i, side_steps - 1), 0)
    side_in = [pl.BlockSpec((w.shape[0] // side_steps, w.shape[1]), side_idx) for w in side_ws]
    side_out = [pl.BlockSpec((w.shape[0] // side_steps // 2, w.shape[1]), side_idx)
                for w in side_ws]
    side_shape = [jax.ShapeDtypeStruct((w.shape[0] // 2, w.shape[1]), U32) for w in side_ws]
    u_shape = [jax.ShapeDtypeStruct((m, d), BF16)] if modulate else []
    u_spec = [pl.BlockSpec((tm, d), lambda j, i: (i, 0))] if modulate else []
    out = pl.pallas_call(
        body,
        out_shape=[jax.ShapeDtypeStruct((m, ncols), BF16),
                   jax.ShapeDtypeStruct((ms, ncols), F32)] + u_shape + side_shape,
        grid=(ncols // tn, ni),
        in_specs=[pl.BlockSpec((tm, d), lambda j, i: (i, 0)),
                  pl.BlockSpec((8, 2 * d), lambda j, i: (prompt_row // 8, 0)),
                  pl.BlockSpec((d, tn), lambda j, i: (0, j + jb)),
                  _resident((ms, d)),
                  _resident((ms, 2 * d))] + side_in,
        out_specs=[pl.BlockSpec((tm, tn), lambda j, i: (i, j)),
                   pl.BlockSpec((ms, tn), lambda j, i: (0, j))] + u_spec + side_out,
        scratch_shapes=[pltpu.VMEM((d, tn), BF16)],
        compiler_params=_params(("arbitrary", "arbitrary"), VMEM_LIMIT_BIG_TILE_BYTES_V7X),
        name="inproj_first" if modulate else "inproj_rest",
    )(lhs, ada, w_in, xs, ada, *side_ws)
    n_main = 3 if modulate else 2
    return out[:n_main], out[n_main:]


def _sigmoid(x):
    return 0.5 * jnp.tanh(0.5 * x) + 0.5


def _mix_tail(pooled_bf, conv_bf, gp, gc, x, g1, sc2, sh2, wpu_ref, wcu_ref, wo_ref,
              lng_ref, lnb_ref, alpha):
    y_p = jnp.dot(pooled_bf, _unpack_rows(wpu_ref[...]), preferred_element_type=F32)
    y_c = jnp.dot(conv_bf, _unpack_rows(wcu_ref[...]), preferred_element_type=F32)
    mixed = (_sigmoid(gp) * y_p + _sigmoid(gc) * y_c).astype(BF16)
    mo = jnp.dot(mixed, _unpack_rows(wo_ref[...]), preferred_element_type=F32)
    x1 = _layernorm(alpha * x + g1 * mo, lng_ref[...], lnb_ref[...])
    return x1, (x1 * (1.0 + sc2) + sh2).astype(BF16)


def _split_cols(first_ref, rest_ref):
    p = first_ref.shape[-1]

    def cols(lo, hi):
        return first_ref[:, lo:hi] if hi <= p else rest_ref[:, lo - p:hi - p]
    return cols


def _mixer_body(pa_ref, pb_ref, ha_ref, hb_ref, x_ref, ada_ref, grp_ref, pscale_ref, convw_ref,
                wpu_ref, wcu_ref, wo_ref, lng_ref, lnb_ref, w1_ref, w2_ref,
                x1_ref, u2_ref, pst_ref, cst_ref, w1b_ref, w2b_ref,
                zz_ref, vv_ref, pm_ref, bc_ref, mixed_ref,
                *, tm, tiles_per_seq, d, alpha, steps):
    i = pl.program_id(0)
    p = d // 2
    gw = p // len(POOL_WINDOWS)
    h = HIST_ROWS
    proj = _split_cols(pa_ref, pb_ref)
    hist = _split_cols(ha_ref, hb_ref)

    cb = MIXER_COL_BLOCK
    ncb = d // cb
    tis = i % tiles_per_seq
    first = tis == 0

    def side_cast():
        tf = w1b_ref.shape[2]
        for fi in range(w1b_ref.shape[0]):
            w1b_ref[fi] = _pack_rows(w1_ref[:, fi * tf:(fi + 1) * tf].astype(BF16))
        w2b_ref[...] = _pack_rows(w2_ref[...].astype(BF16))

    def setup():
        zz_ref[0:h, :] = jnp.where(first, 0.0, hist(0, p).astype(F32))
        zz_ref[h:h + tm, :] = proj(0, p).astype(F32)
        vh = hist(3 * p, 4 * p).astype(F32) * hist(p, 2 * p).astype(F32)
        vv_ref[0:h, :] = jnp.where(first, 0.0, vh)
        vv_ref[h:h + tm, :] = proj(3 * p, 4 * p).astype(F32) * proj(p, 2 * p).astype(F32)
        pst_ref[0] = zz_ref[tm:tm + h, :]
        cst_ref[0] = vv_ref[tm:tm + h, :]

    def pool_group(g):
        w = POOL_WINDOWS[g]
        cols = slice(g * gw, (g + 1) * gw)
        pos = tis * tm + lax.broadcasted_iota(jnp.int32, (tm, 1), 0)
        s = zz_ref[:, cols]
        zt = s[h:, :]
        k = 1
        while k < w:
            s = s + pltpu.roll(s, k, axis=0)
            k *= 2
        inv_cnt = 1.0 / jnp.minimum(w, pos + 1).astype(F32)
        pooled = (s[h:, :] * inv_cnt - zt).astype(BF16)
        og = jnp.dot(pooled, _unpack_rows(grp_ref[g * gw // 2:(g + 1) * gw // 2, :]),
                     preferred_element_type=F32)
        pm_ref[:, cols] = (og * pscale_ref[:, cols]).astype(BF16)

    def conv_branch():
        conv = convw_ref[0:1, :] * vv_ref[h - 2:h - 2 + tm, :]
        conv = conv + convw_ref[1:2, :] * vv_ref[h - 1:h - 1 + tm, :]
        conv = conv + convw_ref[2:3, :] * vv_ref[h:h + tm, :]
        bc_ref[...] = (proj(2 * p, 3 * p).astype(F32) * conv).astype(BF16)

    def up_and_gate(k):
        cols = slice(k * cb, (k + 1) * cb)
        y_p = jnp.dot(pm_ref[...], _unpack_rows(wpu_ref[:, cols]), preferred_element_type=F32)
        y_c = jnp.dot(bc_ref[...], _unpack_rows(wcu_ref[:, cols]), preferred_element_type=F32)
        gp = proj(4 * p + k * cb, 4 * p + (k + 1) * cb).astype(F32)
        gc = proj(4 * p + d + k * cb, 4 * p + d + (k + 1) * cb).astype(F32)
        mixed_ref[:, cols] = (_sigmoid(gp) * y_p + _sigmoid(gc) * y_c).astype(BF16)

    def out_block(k):
        return jnp.dot(mixed_ref[...], _unpack_rows(wo_ref[:, k * cb:(k + 1) * cb]),
                       preferred_element_type=F32)

    def finish_rows(mo, r):
        nr = tm // MIXER_FINISH_SLABS
        rows = slice(r * nr, (r + 1) * nr)
        b = (i - 1) // tiles_per_seq
        g1 = ada_ref[pl.ds(b, 1), 2 * d:3 * d]
        sh2 = ada_ref[pl.ds(b, 1), 3 * d:4 * d]
        sc2 = ada_ref[pl.ds(b, 1), 4 * d:5 * d]
        x1 = _layernorm(alpha * x_ref[rows, :] + g1 * mo[rows, :], lng_ref[...], lnb_ref[...])
        x1_ref[rows, :] = x1
        u2_ref[rows, :] = (x1 * (1.0 + sc2) + sh2).astype(BF16)

    def step(has_head, has_tail):
        front = [side_cast, setup] + [functools.partial(pool_group, g)
                                      for g in range(len(POOL_WINDOWS))] + [conv_branch]
        mo_blocks = []
        for k in range(ncb):
            if has_tail:
                mo_blocks.append(out_block(k))
            if has_head and k < len(front):
                front[k]()
        if has_head:
            for piece in front[ncb:]:
                piece()
        mo = jnp.concatenate(mo_blocks, axis=1) if has_tail else None
        fin_at = {(r + 1) * ncb // MIXER_FINISH_SLABS - 1: r for r in range(MIXER_FINISH_SLABS)}
        for k in range(ncb):
            if has_head:
                up_and_gate(k)
            if has_tail and k in fin_at:
                finish_rows(mo, fin_at[k])

    pl.when(i == 0)(functools.partial(step, True, False))
    pl.when(jnp.logical_and(i > 0, i < steps))(functools.partial(step, True, True))
    pl.when(i == steps)(functools.partial(step, False, True))


def _mixer(proj_a, proj_b, xp, ada, grp_bf, pscale, convw, wpu_bf, wcu_bf, wo_bf, lng, lnb,
           w1, w2, *, seq, prompt_row, alpha, tf, tm=256):
    m, d = xp.shape
    p = d // 2
    nseq = m // seq
    h = HIST_ROWS
    assert seq % tm == 0 and m % seq == 0 and tm % h == 0 and d % MIXER_COL_BLOCK == 0
    assert proj_a.shape[1] == p and tm % MIXER_FINISH_SLABS == 0
    steps = m // tm
    dff = w1.shape[1]
    assert d % steps == 0 and dff % steps == 0 and dff % tf == 0
    body = functools.partial(_mixer_body, tm=tm, tiles_per_seq=seq // tm, d=d, alpha=alpha,
                             steps=steps)
    cur = lambda i: jnp.minimum(i, steps - 1)
    prev = lambda i: jnp.maximum(i - 1, 0)
    hist_idx = lambda i: (jnp.maximum(cur(i) * (tm // h) - 1, 0), 0)
    return pl.pallas_call(
        body,
        out_shape=[jax.ShapeDtypeStruct((m, d), F32),
                   jax.ShapeDtypeStruct((m, d), BF16),
                   jax.ShapeDtypeStruct((nseq, h, p), F32),
                   jax.ShapeDtypeStruct((nseq, h, p), F32),
                   jax.ShapeDtypeStruct((dff // tf, d // 2, tf), U32),
                   jax.ShapeDtypeStruct((dff // 2, d), U32)],
        grid=(steps + 1,),
        in_specs=[pl.BlockSpec((tm, proj_a.shape[1]), lambda i: (cur(i), 0)),
                  pl.BlockSpec((tm, proj_b.shape[1]), lambda i: (cur(i), 0)),
                  pl.BlockSpec((h, proj_a.shape[1]), hist_idx),
                  pl.BlockSpec((h, proj_b.shape[1]), hist_idx),
                  pl.BlockSpec((tm, d), lambda i: (prev(i), 0)),
                  pl.BlockSpec((8, N_ADA * d), lambda i: (prompt_row // 8, 0)),
                  _resident(grp_bf.shape),
                  _resident(pscale.shape),
                  _resident(convw.shape),
                  _resident(wpu_bf.shape),
                  _resident(wcu_bf.shape),
                  _resident(wo_bf.shape),
                  _resident(lng.shape),
                  _resident(lnb.shape),
                  pl.BlockSpec((d // steps, dff), lambda i: (cur(i), 0)),
                  pl.BlockSpec((dff // steps, d), lambda i: (cur(i), 0))],
        out_specs=[pl.BlockSpec((tm, d), lambda i: (prev(i), 0)),
                   pl.BlockSpec((tm, d), lambda i: (prev(i), 0)),
                   pl.BlockSpec((1, h, p), lambda i: (cur(i) // (seq // tm), 0, 0)),
                   pl.BlockSpec((1, h, p), lambda i: (cur(i) // (seq // tm), 0, 0)),
                   pl.BlockSpec((dff // tf, d // steps // 2, tf), lambda i: (0, cur(i), 0)),
                   pl.BlockSpec((dff // steps // 2, d), lambda i: (cur(i), 0))],
        scratch_shapes=[pltpu.VMEM((tm + h, p), F32),
                        pltpu.VMEM((tm + h, p), F32),
                        pltpu.VMEM((tm, p), BF16),
                        pltpu.VMEM((tm, p), BF16),
                        pltpu.VMEM((tm, d), BF16)],
        compiler_params=_params(("arbitrary",)),
        name="mixer",
    )(proj_a, proj_b, proj_a, proj_b, xp, ada, grp_bf, pscale, convw, wpu_bf, wcu_bf, wo_bf,
      lng, lnb, w1, w2)


def _mixer_step_body(pa_ref, pb_ref, pool_ref, cstate_ref, x_ref, ada_ref, grp_ref, pscale_ref,
                     convw_ref, wpu_ref, wcu_ref, wo_ref, lng_ref, lnb_ref,
                     x1_ref, u2_ref, npool_ref, nconv_ref, pm_ref, *, d, alpha):
    p = d // 2
    gw = p // len(POOL_WINDOWS)
    proj = _split_cols(pa_ref, pb_ref)
    z = proj(0, p)
    for g, w in enumerate(POOL_WINDOWS):
        cols = slice(g * gw, (g + 1) * gw)
        s = z[:, cols]
        for k in range(1, w):
            s = s + pool_ref[POOL_HIST - k, :, cols]
        inv_cnt = 1.0 / min(w, PAST_LEN + 1)
        pooled = (s * inv_cnt - z[:, cols]).astype(BF16)
        og = jnp.dot(pooled, _unpack_rows(grp_ref[g * gw // 2:(g + 1) * gw // 2, :]),
                         preferred_element_type=F32)
        pm_ref[:, cols] = (og * pscale_ref[:, cols]).astype(BF16)
    for r in range(POOL_HIST - 1):
        npool_ref[r] = pool_ref[r + 1]
    npool_ref[POOL_HIST - 1] = z

    xc = proj(p, 2 * p)
    bc = proj(2 * p, 3 * p)
    cc = proj(3 * p, 4 * p)
    v = cc * xc
    conv = convw_ref[0:1, :] * cstate_ref[0]
    conv = conv + convw_ref[1:2, :] * cstate_ref[1]
    conv = conv + convw_ref[2:3, :] * v
    nconv_ref[0] = cstate_ref[1]
    nconv_ref[1] = v

    gp = proj(4 * p, 4 * p + d)
    gc = proj(4 * p + d, 4 * p + 2 * d)
    g1 = ada_ref[:, 2 * d:3 * d]
    sh2 = ada_ref[:, 3 * d:4 * d]
    sc2 = ada_ref[:, 4 * d:5 * d]
    x1, u2 = _mix_tail(pm_ref[...], (bc * conv).astype(BF16), gp, gc, x_ref[...], g1, sc2, sh2,
                       wpu_ref, wcu_ref, wo_ref, lng_ref, lnb_ref, alpha)
    x1_ref[...] = x1
    u2_ref[...] = u2


def _mixer_step(proj_a, proj_b, pool_state, conv_state, xs, ada, grp_bf, pscale, convw,
                wpu_bf, wcu_bf, wo_bf, lng, lnb, *, alpha):
    ms, d = xs.shape
    p = d // 2
    body = functools.partial(_mixer_step_body, d=d, alpha=alpha)
    return pl.pallas_call(
        body,
        out_shape=[jax.ShapeDtypeStruct((ms, d), F32),
                   jax.ShapeDtypeStruct((ms, d), BF16),
                   jax.ShapeDtypeStruct(pool_state.shape, F32),
                   jax.ShapeDtypeStruct(conv_state.shape, F32)],
        grid=(1,),
        in_specs=[_resident(proj_a.shape),
                  _resident(proj_b.shape),
                  _resident(pool_state.shape),
                  _resident(conv_state.shape),
                  _resident((ms, d)),
                  _resident((ms, N_ADA * d)),
                  _resident(grp_bf.shape),
                  _resident(pscale.shape),
                  _resident(convw.shape),
                  _resident(wpu_bf.shape),
                  _resident(wcu_bf.shape),
                  _resident(wo_bf.shape),
                  _resident(lng.shape),
                  _resident(lnb.shape)],
        out_specs=[_resident((ms, d)),
                   _resident((ms, d)),
                   _resident(pool_state.shape),
                   _resident(conv_state.shape)],
        scratch_shapes=[pltpu.VMEM((ms, p), BF16)],
        compiler_params=_params(("arbitrary",)),
        name="mixer_step",
    )(proj_a, proj_b, pool_state, conv_state, xs, ada, grp_bf, pscale, convw, wpu_bf, wcu_bf,
      wo_bf, lng, lnb)


def _mlp_body(u2_ref, w1_ref, b1_ref, w2_ref, x1_ref, g2_ref, b2_ref, lng_ref, lnb_ref,
              u2s_ref, x1s_ref, g2s_ref, o_ref, os_ref, acc_ref, accs_ref,
              *, nm, nf, ne, te, tm, alpha, rows_per_cond):
    m = pl.program_id(0)
    f = pl.program_id(1)
    slot = m % 2

    def chunk(u2):
        hmid = jnp.dot(u2, _unpack_rows(w1_ref[0]), preferred_element_type=F32) + b1_ref[f]
        hmid = jnp.square(jnp.maximum(hmid, 0.0))
        return jnp.dot(hmid.astype(BF16), _unpack_rows(w2_ref[...]),
                       preferred_element_type=F32)

    def accumulate(first, between=None):
        n_chunks = tm // ROW_CHUNK
        for c in range(n_chunks):
            if between is not None:
                between(c, n_chunks)
            rows = slice(c * ROW_CHUNK, (c + 1) * ROW_CHUNK)
            part = chunk(u2_ref[rows, :])
            if first:
                acc_ref[slot, rows, :] = part
            else:
                acc_ref[slot, rows, :] += part

    def accumulate_singles(first):
        part = chunk(u2s_ref[...])
        if first:
            accs_ref[...] = part
        else:
            accs_ref[...] += part

    def finish(x1, g2, acc):
        return _layernorm(alpha * x1 + g2 * (acc + b2_ref[...]), lng_ref[...], lnb_ref[...])

    def epilogue(piece=0, n_pieces=1):
        tp = te // n_pieces
        base = pl.multiple_of((f - 1) * te + piece * tp, tp)
        g2 = g2_ref[pl.ds(((m - 1) * tm) // rows_per_cond, 1), :]
        o_ref[piece * tp:(piece + 1) * tp, :] = finish(
            x1_ref[piece * tp:(piece + 1) * tp, :], g2, acc_ref[1 - slot, pl.ds(base, tp), :])

    has_main = m < nm
    has_epi = jnp.logical_and(m >= 1, jnp.logical_and(f >= 1, f <= ne))

    @pl.when(jnp.logical_and(has_main, f == 0))
    def _():
        accumulate(True)
        pl.when(m == 0)(functools.partial(accumulate_singles, True))

    @pl.when(jnp.logical_and(has_main, has_epi))
    def _():
        accumulate(False, between=epilogue)

    @pl.when(jnp.logical_and(has_main, jnp.logical_and(f > 0, jnp.logical_not(has_epi))))
    def _():
        accumulate(False)
        pl.when(m == 0)(functools.partial(accumulate_singles, False))

    @pl.when(jnp.logical_and(jnp.logical_not(has_main), has_epi))
    def _():
        epilogue()

    @pl.when(jnp.logical_and(m == 1, f == 1))
    def _():
        os_ref[...] = finish(x1s_ref[...], g2s_ref[...], accs_ref[...])


def _mlp(u2, x1, u2s, x1s, ada, w1b, b1, w2b, b2, lng, lnb, *, alpha, rows_per_cond,
         cond_row, tm, te):
    m, d = u2.shape
    ms = u2s.shape[0]
    nf, _, tf = w1b.shape
    nm = m // tm
    ne = tm // te
    assert ne < nf and m % tm == 0 and tm % te == 0 and tm % ROW_CHUNK == 0
    assert rows_per_cond % tm == 0 and te % (tm // ROW_CHUNK) == 0 and cond_row % 8 == 0
    g2_col = N_ADA - 1
    body = functools.partial(_mlp_body, nm=nm, nf=nf, ne=ne, te=te, tm=tm, alpha=alpha,
                             rows_per_cond=rows_per_cond)
    fw = lambda mi, f: jnp.where(mi == nm, nf - 1, f)
    ep = lambda mi, f: (jnp.where(mi == 0, 0, (mi - 1) * ne + jnp.clip(f - 1, 0, ne - 1)), 0)
    return pl.pallas_call(
        body,
        out_shape=[jax.ShapeDtypeStruct((m, d), F32),
                   jax.ShapeDtypeStruct((ms, d), F32)],
        grid=(nm + 1, nf),
        in_specs=[pl.BlockSpec((tm, d), lambda mi, f: (jnp.minimum(mi, nm - 1), 0)),
                  pl.BlockSpec((1, d // 2, tf), lambda mi, f: (fw(mi, f), 0, 0)),
                  _resident((nf, 1, tf)),
                  pl.BlockSpec((tf // 2, d), lambda mi, f: (fw(mi, f), 0)),
                  pl.BlockSpec((te, d), ep),
                  pl.BlockSpec((8, d), lambda mi, f: (cond_row // 8, g2_col)),
                  _resident((1, d)),
                  _resident((1, d)),
                  _resident((1, d)),
                  _resident((ms, d)),
                  _resident((ms, d)),
                  pl.BlockSpec((ms, d), lambda mi, f: (0, g2_col),
                               pipeline_mode=pl.Buffered(1))],
        out_specs=[pl.BlockSpec((te, d), ep),
                   pl.BlockSpec((ms, d), lambda mi, f: (0, 0))],
        scratch_shapes=[pltpu.VMEM((2, tm, d), F32),
                        pltpu.VMEM((ms, d), F32)],
        compiler_params=_params(("arbitrary", "arbitrary")),
        name="mlp",
    )(u2, w1b, b1.reshape(nf, 1, tf), w2b, x1, ada, b2.reshape(1, d), lng, lnb, u2s, x1s, ada)


def kernel(x_prompt, x_sample, state_pool, state_conv, c_prompt, c_sample, w_ada, b_ada, w_in,
           pool_grp_w, pool_scale, conv_w, w_pool_up, w_conv_up, w_o, ln1_g, ln1_b, w_ff1,
           b_ff1, w_ff2, b_ff2, ln2_g, ln2_b):
    depth = w_ada.shape[0]
    nb, seq, d = x_prompt.shape
    ms, dec_seq, _ = x_sample.shape
    assert dec_seq == 1
    p = d // 2
    alpha = (2 * depth) ** 0.25

    prompt_row = ms
    pad = (-(ms + nb)) % 8
    c_all = jnp.concatenate([c_sample, c_prompt, jnp.zeros((pad, d), F32)], axis=0)

    xp = x_prompt.reshape(nb * seq, d)
    xs = x_sample.reshape(ms, d)
    pool_p, conv_p, pool_s, conv_s = [], [], [], []
    for l in range(depth):
        ada = _ada(c_all, w_ada[l], b_ada[l])
        pscale = pool_scale[l].reshape(1, p)
        lng1, lnb1 = ln1_g[l].reshape(1, d), ln1_b[l].reshape(1, d)
        lng2, lnb2 = ln2_g[l].reshape(1, d), ln2_b[l].reshape(1, d)

        n_in = w_in.shape[2]
        (pa_p, pa_s, u_bf), _ = _inproj(xp, xs, ada, w_in[l], [], col0=0, ncols=p,
                                        modulate=True, seq=seq, prompt_row=prompt_row)
        (pb_p, pb_s), (grp_bf, wpu_bf, wcu_bf, wo_bf) = _inproj(
            u_bf, xs, ada, w_in[l],
            [pool_grp_w[l].reshape(p, -1), w_pool_up[l], w_conv_up[l], w_o[l]],
            col0=p, ncols=n_in - p, modulate=False, seq=seq, prompt_row=prompt_row,
            tm=INPROJ_BF16_TOKEN_TILE)

        x1p, u2p, pst, cst, w1b, w2b = _mixer(
            pa_p, pb_p, xp, ada, grp_bf, pscale, conv_w[l], wpu_bf, wcu_bf, wo_bf, lng1, lnb1,
            w_ff1[l], w_ff2[l], seq=seq, prompt_row=prompt_row, alpha=alpha, tf=MLP_FF_CHUNK)
        x1s, u2s, npool, nconv = _mixer_step(
            pa_s, pb_s, jnp.transpose(state_pool[l], (1, 0, 2)),
            jnp.transpose(state_conv[l], (1, 0, 2)), xs, ada, grp_bf, pscale, conv_w[l],
            wpu_bf, wcu_bf, wo_bf, lng1, lnb1, alpha=alpha)

        xp, xs = _mlp(u2p, x1p, u2s, x1s, ada, w1b, b_ff1[l], w2b, b_ff2[l], lng2, lnb2,
                      alpha=alpha, rows_per_cond=seq, cond_row=prompt_row,
                      tm=MLP_TOKEN_TILE, te=MLP_EPILOGUE_ROWS)

        pool_p.append(pst[:, HIST_ROWS - POOL_HIST:, :])
        conv_p.append(cst[:, HIST_ROWS - CONV_HIST:, :])
        pool_s.append(jnp.transpose(npool, (1, 0, 2)))
        conv_s.append(jnp.transpose(nconv, (1, 0, 2)))

    return (xp.reshape(nb, seq, d), xs.reshape(ms, dec_seq, d),
            jnp.stack(pool_p), jnp.stack(conv_p), jnp.stack(pool_s), jnp.stack(conv_s))
```

```python
import functools

import jax
import jax.numpy as jnp
from jax import lax
from jax.experimental import pallas as pl
from jax.experimental.pallas import tpu as pltpu

F32 = jnp.float32
BF16 = jnp.bfloat16
U32 = jnp.uint32

POOL_WINDOWS = (2, 4, 8, 16)
POOL_HIST = max(POOL_WINDOWS) - 1
CONV_K = 3
CONV_HIST = CONV_K - 1
N_ADA = 6
PAST_LEN = 16384
LN_EPS = 1e-5

VMEM_LIMIT_BYTES_V7X = 56 * 1024 * 1024
VMEM_LIMIT_BIG_TILE_BYTES_V7X = 60 * 1024 * 1024
MLP_VMEM_LIMIT_BYTES_V7X = 62 * 1024 * 1024
HIST_ROWS = 16
ROW_CHUNK = 256
MIXER_COL_BLOCK = 256
MIXER_FINISH_SLABS = 4
INPROJ_BF16_TOKEN_TILE = 2048
MLP_TOKEN_TILE = 1024
MLP_FF_CHUNK = 2048


def _resident(shape):
    zeros = (0,) * len(shape)
    return pl.BlockSpec(shape, lambda *_: zeros, pipeline_mode=pl.Buffered(1))


def _params(semantics, vmem_limit_bytes=VMEM_LIMIT_BYTES_V7X):
    return pltpu.CompilerParams(dimension_semantics=semantics,
                                vmem_limit_bytes=vmem_limit_bytes)


def _layernorm(xf, g, b):
    mu = jnp.mean(xf, axis=-1, keepdims=True)
    var = jnp.mean(jnp.square(xf - mu), axis=-1, keepdims=True)
    return (xf - mu) * lax.rsqrt(var + LN_EPS) * g + b


def _ada_body(c_ref, w_ref, b_ref, o_ref):
    o_ref[...] = jnp.dot(c_ref[...].astype(BF16), w_ref[...].astype(BF16),
                         preferred_element_type=F32) + b_ref[...]


def _ada(c_all, w_ada, b_ada, tn=1024):
    m, d = c_all.shape
    n = w_ada.shape[1]
    return pl.pallas_call(
        _ada_body,
        out_shape=jax.ShapeDtypeStruct((m, n), F32),
        grid=(n // tn,),
        in_specs=[pl.BlockSpec((m, d), lambda j: (0, 0)),
                  pl.BlockSpec((d, tn), lambda j: (0, j)),
                  pl.BlockSpec((1, tn), lambda j: (0, j))],
        out_specs=pl.BlockSpec((m, tn), lambda j: (0, j)),
        compiler_params=pltpu.CompilerParams(
            dimension_semantics=("arbitrary",), vmem_limit_bytes=VMEM_LIMIT_BYTES_V7X,
            allow_input_fusion=[True, False, False]),
        name="ada",
    )(c_all, w_ada, b_ada.reshape(1, n))


def _pack_rows(x_bf16):
    return pltpu.bitcast(x_bf16, U32)


def _unpack_rows(x_u32):
    return pltpu.bitcast(x_u32, BF16)


def _inproj_body(lhs_ref, adap_ref, w_ref, xs_ref, adas_ref, *rest, tiles_per_seq, d, tm,
                 n_side, side_steps, modulate):
    side_in = rest[:n_side]
    outs = rest[n_side:]
    o_ref, os_ref = outs[:2]
    n_main = 3 if modulate else 2
    u_ref = outs[2] if modulate else None
    side_out = outs[n_main:n_main + n_side]
    wb_ref = outs[n_main + n_side]
    i = pl.program_id(1)

    if n_side:
        @pl.when(pl.program_id(0) * pl.num_programs(1) + i < side_steps)
        def _():
            for src, dst in zip(side_in, side_out):
                dst[...] = _pack_rows(src[...].astype(BF16))

    @pl.when(i == 0)
    def _():
        wb_ref[...] = w_ref[...].astype(BF16)
        us = xs_ref[...] * (1.0 + adas_ref[:, d:2 * d]) + adas_ref[:, 0:d]
        os_ref[...] = jnp.dot(us.astype(BF16), wb_ref[...], preferred_element_type=F32)

    b = i // tiles_per_seq
    sh1 = adap_ref[pl.ds(b, 1), 0:d]
    sc1 = adap_ref[pl.ds(b, 1), d:2 * d]
    for r in range(0, tm, ROW_CHUNK):
        rows = slice(r, r + ROW_CHUNK)
        if modulate:
            u = (lhs_ref[rows, :] * (1.0 + sc1) + sh1).astype(BF16)
            u_ref[rows, :] = u
        else:
            u = lhs_ref[rows, :]
        o_ref[rows, :] = jnp.dot(u, wb_ref[...], preferred_element_type=F32).astype(BF16)


def _inproj(lhs, xs, ada, w_in, side_ws, *, col0, ncols, modulate, seq, prompt_row,
            tm=1024, tn=1024, side_steps=16):
    m, d = lhs.shape
    ms = xs.shape[0]
    assert seq % tm == 0 and m % seq == 0 and tm % ROW_CHUNK == 0
    assert col0 % tn == 0 and ncols % tn == 0 and prompt_row % 8 == 0
    ni = m // tm
    jb = col0 // tn
    body = functools.partial(_inproj_body, tiles_per_seq=seq // tm, d=d, tm=tm,
                             n_side=len(side_ws), side_steps=side_steps, modulate=modulate)
    side_idx = lambda j, i: (jnp.minimum(j * ni + i, side_steps - 1), 0)
    side_in = [pl.BlockSpec((w.shape[0] // side_steps, w.shape[1]), side_idx) for w in side_ws]
    side_out = [pl.BlockSpec((w.shape[0] // side_steps // 2, w.shape[1]), side_idx)
                for w in side_ws]
    side_shape = [jax.ShapeDtypeStruct((w.shape[0] // 2, w.shape[1]), U32) for w in side_ws]
    u_shape = [jax.ShapeDtypeStruct((m, d), BF16)] if modulate else []
    u_spec = [pl.BlockSpec((tm, d), lambda j, i: (i, 0))] if modulate else []
    out = pl.pallas_call(
        body,
        out_shape=[jax.ShapeDtypeStruct((m, ncols), BF16),
                   jax.ShapeDtypeStruct((ms, ncols), F32)] + u_shape + side_shape,
        grid=(ncols // tn, ni),
        in_specs=[pl.BlockSpec((tm, d), lambda j, i: (i, 0)),
                  pl.BlockSpec((8, 2 * d), lambda j, i: (prompt_row // 8, 0)),
                  pl.BlockSpec((d, tn), lambda j, i: (0, j + jb)),
                  _resident((ms, d)),
                  _resident((ms, 2 * d))] + side_in,
        out_specs=[pl.BlockSpec((tm, tn), lambda j, i: (i, j)),
                   pl.BlockSpec((ms, tn), lambda j, i: (0, j))] + u_spec + side_out,
        scratch_shapes=[pltpu.VMEM((d, tn), BF16)],
        compiler_params=_params(("arbitrary", "arbitrary"), VMEM_LIMIT_BIG_TILE_BYTES_V7X),
        name="inproj_first" if modulate else "inproj_rest",
    )(lhs, ada, w_in, xs, ada, *side_ws)
    n_main = 3 if modulate else 2
    return out[:n_main], out[n_main:]


def _sigmoid(x):
    return 0.5 * jnp.tanh(0.5 * x) + 0.5


def _mix_tail(pooled_bf, conv_bf, gp, gc, x, g1, sc2, sh2, wpu_ref, wcu_ref, wo_ref,
              lng_ref, lnb_ref, alpha):
    y_p = jnp.dot(pooled_bf, _unpack_rows(wpu_ref[...]), preferred_element_type=F32)
    y_c = jnp.dot(conv_bf, _unpack_rows(wcu_ref[...]), preferred_element_type=F32)
    mixed = (_sigmoid(gp) * y_p + _sigmoid(gc) * y_c).astype(BF16)
    mo = jnp.dot(mixed, _unpack_rows(wo_ref[...]), preferred_element_type=F32)
    x1 = _layernorm(alpha * x + g1 * mo, lng_ref[...], lnb_ref[...])
    return x1, (x1 * (1.0 + sc2) + sh2).astype(BF16)


def _split_cols(first_ref, rest_ref):
    p = first_ref.shape[-1]

    def cols(lo, hi):
        return first_ref[:, lo:hi] if hi <= p else rest_ref[:, lo - p:hi - p]
    return cols


def _mixer_body(pa_ref, pb_ref, ha_ref, hb_ref, x_ref, ada_ref, grp_ref, pscale_ref, convw_ref,
                wpu_ref, wcu_ref, wo_ref, lng_ref, lnb_ref, w1_ref, w2_ref,
                x1_ref, u2_ref, pst_ref, cst_ref, w1b_ref, w2b_ref,
                zz_ref, vv_ref, pm_ref, bc_ref, mixed_ref,
                *, tm, tiles_per_seq, d, alpha, steps):
    i = pl.program_id(0)
    p = d // 2
    gw = p // len(POOL_WINDOWS)
    h = HIST_ROWS
    proj = _split_cols(pa_ref, pb_ref)
    hist = _split_cols(ha_ref, hb_ref)

    cb = MIXER_COL_BLOCK
    ncb = d // cb
    tis = i % tiles_per_seq
    first = tis == 0

    def side_cast():
        tf = w1b_ref.shape[2]
        for fi in range(w1b_ref.shape[0]):
            w1b_ref[fi] = _pack_rows(w1_ref[:, fi * tf:(fi + 1) * tf].astype(BF16))
        w2b_ref[...] = _pack_rows(w2_ref[...].astype(BF16))

    def setup():
        zz_ref[0:h, :] = jnp.where(first, 0.0, hist(0, p).astype(F32))
        zz_ref[h:h + tm, :] = proj(0, p).astype(F32)
        vh = hist(3 * p, 4 * p).astype(F32) * hist(p, 2 * p).astype(F32)
        vv_ref[0:h, :] = jnp.where(first, 0.0, vh)
        vv_ref[h:h + tm, :] = proj(3 * p, 4 * p).astype(F32) * proj(p, 2 * p).astype(F32)
        pst_ref[0] = zz_ref[tm:tm + h, :]
        cst_ref[0] = vv_ref[tm:tm + h, :]

    def pool_group(g):
        w = POOL_WINDOWS[g]
        cols = slice(g * gw, (g + 1) * gw)
        pos = tis * tm + lax.broadcasted_iota(jnp.int32, (tm, 1), 0)
        s = zz_ref[:, cols]
        zt = s[h:, :]
        k = 1
        while k < w:
            s = s + pltpu.roll(s, k, axis=0)
            k *= 2
        inv_cnt = 1.0 / jnp.minimum(w, pos + 1).astype(F32)
        pooled = (s[h:, :] * inv_cnt - zt).astype(BF16)
        og = jnp.dot(pooled, _unpack_rows(grp_ref[g * gw // 2:(g + 1) * gw // 2, :]),
                     preferred_element_type=F32)
        pm_ref[:, cols] = (og * pscale_ref[:, cols]).astype(BF16)

    def conv_branch():
        conv = convw_ref[0:1, :] * vv_ref[h - 2:h - 2 + tm, :]
        conv = conv + convw_ref[1:2, :] * vv_ref[h - 1:h - 1 + tm, :]
        conv = conv + convw_ref[2:3, :] * vv_ref[h:h + tm, :]
        bc_ref[...] = (proj(2 * p, 3 * p).astype(F32) * conv).astype(BF16)

    def up_and_gate(k):
        cols = slice(k * cb, (k + 1) * cb)
        y_p = jnp.dot(pm_ref[...], _unpack_rows(wpu_ref[:, cols]), preferred_element_type=F32)
        y_c = jnp.dot(bc_ref[...], _unpack_rows(wcu_ref[:, cols]), preferred_element_type=F32)
        gp = proj(4 * p + k * cb, 4 * p + (k + 1) * cb).astype(F32)
        gc = proj(4 * p + d + k * cb, 4 * p + d + (k + 1) * cb).astype(F32)
        mixed_ref[:, cols] = (_sigmoid(gp) * y_p + _sigmoid(gc) * y_c).astype(BF16)

    def out_block(k):
        return jnp.dot(mixed_ref[...], _unpack_rows(wo_ref[:, k * cb:(k + 1) * cb]),
                       preferred_element_type=F32)

    def finish_rows(mo, r):
        nr = tm // MIXER_FINISH_SLABS
        rows = slice(r * nr, (r + 1) * nr)
        b = (i - 1) // tiles_per_seq
        g1 = ada_ref[pl.ds(b, 1), 2 * d:3 * d]
        sh2 = ada_ref[pl.ds(b, 1), 3 * d:4 * d]
        sc2 = ada_ref[pl.ds(b, 1), 4 * d:5 * d]
        x1 = _layernorm(alpha * x_ref[rows, :] + g1 * mo[rows, :], lng_ref[...], lnb_ref[...])
        x1_ref[rows, :] = x1
        u2_ref[rows, :] = (x1 * (1.0 + sc2) + sh2).astype(BF16)

    def step(has_head, has_tail):
        front = [side_cast, setup] + [functools.partial(pool_group, g)
                                      for g in range(len(POOL_WINDOWS))] + [conv_branch]
        mo_blocks = []
        for k in range(ncb):
            if has_tail:
                mo_blocks.append(out_block(k))
            if has_head and k < len(front):
                front[k]()
        if has_head:
            for piece in front[ncb:]:
                piece()
        mo = jnp.concatenate(mo_blocks, axis=1) if has_tail else None
        fin_at = {(r + 1) * ncb // MIXER_FINISH_SLABS - 1: r for r in range(MIXER_FINISH_SLABS)}
        for k in range(ncb):
            if has_head:
                up_and_gate(k)
            if has_tail and k in fin_at:
                finish_rows(mo, fin_at[k])

    pl.when(i == 0)(functools.partial(step, True, False))
    pl.when(jnp.logical_and(i > 0, i < steps))(functools.partial(step, True, True))
    pl.when(i == steps)(functools.partial(step, False, True))


def _mixer(proj_a, proj_b, xp, ada, grp_bf, pscale, convw, wpu_bf, wcu_bf, wo_bf, lng, lnb,
           w1, w2, *, seq, prompt_row, alpha, tf, tm=256):
    m, d = xp.shape
    p = d // 2
    nseq = m // seq
    h = HIST_ROWS
    assert seq % tm == 0 and m % seq == 0 and tm % h == 0 and d % MIXER_COL_BLOCK == 0
    assert proj_a.shape[1] == p and tm % MIXER_FINISH_SLABS == 0
    steps = m // tm
    dff = w1.shape[1]
    assert d % steps == 0 and dff % steps == 0 and dff % tf == 0
    body = functools.partial(_mixer_body, tm=tm, tiles_per_seq=seq // tm, d=d, alpha=alpha,
                             steps=steps)
    cur = lambda i: jnp.minimum(i, steps - 1)
    prev = lambda i: jnp.maximum(i - 1, 0)
    hist_idx = lambda i: (jnp.maximum(cur(i) * (tm // h) - 1, 0), 0)
    return pl.pallas_call(
        body,
        out_shape=[jax.ShapeDtypeStruct((m, d), F32),
                   jax.ShapeDtypeStruct((m, d), BF16),
                   jax.ShapeDtypeStruct((nseq, h, p), F32),
                   jax.ShapeDtypeStruct((nseq, h, p), F32),
                   jax.ShapeDtypeStruct((dff // tf, d // 2, tf), U32),
                   jax.ShapeDtypeStruct((dff // 2, d), U32)],
        grid=(steps + 1,),
        in_specs=[pl.BlockSpec((tm, proj_a.shape[1]), lambda i: (cur(i), 0)),
                  pl.BlockSpec((tm, proj_b.shape[1]), lambda i: (cur(i), 0)),
                  pl.BlockSpec((h, proj_a.shape[1]), hist_idx),
                  pl.BlockSpec((h, proj_b.shape[1]), hist_idx),
                  pl.BlockSpec((tm, d), lambda i: (prev(i), 0)),
                  pl.BlockSpec((8, N_ADA * d), lambda i: (prompt_row // 8, 0)),
                  _resident(grp_bf.shape),
                  _resident(pscale.shape),
                  _resident(convw.shape),
                  _resident(wpu_bf.shape),
                  _resident(wcu_bf.shape),
                  _resident(wo_bf.shape),
                  _resident(lng.shape),
                  _resident(lnb.shape),
                  pl.BlockSpec((d // steps, dff), lambda i: (cur(i), 0)),
                  pl.BlockSpec((dff // steps, d), lambda i: (cur(i), 0))],
        out_specs=[pl.BlockSpec((tm, d), lambda i: (prev(i), 0)),
                   pl.BlockSpec((tm, d), lambda i: (prev(i), 0)),
                   pl.BlockSpec((1, h, p), lambda i: (cur(i) // (seq // tm), 0, 0)),
                   pl.BlockSpec((1, h, p), lambda i: (cur(i) // (seq // tm), 0, 0)),
                   pl.BlockSpec((dff // tf, d // steps // 2, tf), lambda i: (0, cur(i), 0)),
                   pl.BlockSpec((dff // steps // 2, d), lambda i: (cur(i), 0))],
        scratch_shapes=[pltpu.VMEM((tm + h, p), F32),
                        pltpu.VMEM((tm + h, p), F32),
                        pltpu.VMEM((tm, p), BF16),
                        pltpu.VMEM((tm, p), BF16),
                        pltpu.VMEM((tm, d), BF16)],
        compiler_params=_params(("arbitrary",)),
        name="mixer",
    )(proj_a, proj_b, proj_a, proj_b, xp, ada, grp_bf, pscale, convw, wpu_bf, wcu_bf, wo_bf,
      lng, lnb, w1, w2)


def _mixer_step_body(pa_ref, pb_ref, pool_ref, cstate_ref, x_ref, ada_ref, grp_ref, pscale_ref,
                     convw_ref, wpu_ref, wcu_ref, wo_ref, lng_ref, lnb_ref,
                     x1_ref, u2_ref, npool_ref, nconv_ref, pm_ref, *, d, alpha):
    p = d // 2
    gw = p // len(POOL_WINDOWS)
    proj = _split_cols(pa_ref, pb_ref)
    z = proj(0, p)
    for g, w in enumerate(POOL_WINDOWS):
        cols = slice(g * gw, (g + 1) * gw)
        s = z[:, cols]
        for k in range(1, w):
            s = s + pool_ref[POOL_HIST - k, :, cols]
        inv_cnt = 1.0 / min(w, PAST_LEN + 1)
        pooled = (s * inv_cnt - z[:, cols]).astype(BF16)
        og = jnp.dot(pooled, _unpack_rows(grp_ref[g * gw // 2:(g + 1) * gw // 2, :]),
                         preferred_element_type=F32)
        pm_ref[:, cols] = (og * pscale_ref[:, cols]).astype(BF16)
    for r in range(POOL_HIST - 1):
        npool_ref[r] = pool_ref[r + 1]
    npool_ref[POOL_HIST - 1] = z

    xc = proj(p, 2 * p)
    bc = proj(2 * p, 3 * p)
    cc = proj(3 * p, 4 * p)
    v = cc * xc
    conv = convw_ref[0:1, :] * cstate_ref[0]
    conv = conv + convw_ref[1:2, :] * cstate_ref[1]
    conv = conv + convw_ref[2:3, :] * v
    nconv_ref[0] = cstate_ref[1]
    nconv_ref[1] = v

    gp = proj(4 * p, 4 * p + d)
    gc = proj(4 * p + d, 4 * p + 2 * d)
    g1 = ada_ref[:, 2 * d:3 * d]
    sh2 = ada_ref[:, 3 * d:4 * d]
    sc2 = ada_ref[:, 4 * d:5 * d]
    x1, u2 = _mix_tail(pm_ref[...], (bc * conv).astype(BF16), gp, gc, x_ref[...], g1, sc2, sh2,
                       wpu_ref, wcu_ref, wo_ref, lng_ref, lnb_ref, alpha)
    x1_ref[...] = x1
    u2_ref[...] = u2


def _mixer_step(proj_a, proj_b, pool_state, conv_state, xs, ada, grp_bf, pscale, convw,
                wpu_bf, wcu_bf, wo_bf, lng, lnb, *, alpha):
    ms, d = xs.shape
    p = d // 2
    body = functools.partial(_mixer_step_body, d=d, alpha=alpha)
    return pl.pallas_call(
        body,
        out_shape=[jax.ShapeDtypeStruct((ms, d), F32),
                   jax.ShapeDtypeStruct((ms, d), BF16),
                   jax.ShapeDtypeStruct(pool_state.shape, F32),
                   jax.ShapeDtypeStruct(conv_state.shape, F32)],
        grid=(1,),
        in_specs=[_resident(proj_a.shape),
                  _resident(proj_b.shape),
                  _resident(pool_state.shape),
                  _resident(conv_state.shape),
                  _resident((ms, d)),
                  _resident((ms, N_ADA * d)),
                  _resident(grp_bf.shape),
                  _resident(pscale.shape),
                  _resident(convw.shape),
                  _resident(wpu_bf.shape),
                  _resident(wcu_bf.shape),
                  _resident(wo_bf.shape),
                  _resident(lng.shape),
                  _resident(lnb.shape)],
        out_specs=[_resident((ms, d)),
                   _resident((ms, d)),
                   _resident(pool_state.shape),
                   _resident(conv_state.shape)],
        scratch_shapes=[pltpu.VMEM((ms, p), BF16)],
        compiler_params=_params(("arbitrary",)),
        name="mixer_step",
    )(proj_a, proj_b, pool_state, conv_state, xs, ada, grp_bf, pscale, convw, wpu_bf, wcu_bf,
      wo_bf, lng, lnb)


def _mlp_body(u2_ref, w1_hbm, b1_ref, w2_hbm, x1_ref, g2_ref, b2_ref, lng_ref, lnb_ref,
              u2s_ref, x1s_ref, g2s_ref, o_ref, os_ref,
              acc_ref, accs_ref, w1_buf, w2_buf, h_buf, hs_buf, sem,
              *, nm, nf, te, tm, alpha, rows_per_cond):
    m = pl.program_id(0)
    f = pl.program_id(1)
    slot = m % 2
    t = m * nf + f
    w2_rows = w2_buf.shape[0]
    n_chunks = tm // ROW_CHUNK

    def w1_copy(fi):
        return pltpu.make_async_copy(w1_hbm.at[fi], w1_buf, sem.at[0])

    def w2_copy(fi):
        return pltpu.make_async_copy(w2_hbm.at[pl.ds(fi * w2_rows, w2_rows), :], w2_buf,
                                     sem.at[1])

    has_main = m < nm

    @pl.when(t == 0)
    def _():
        w1_copy(0).start()

    @pl.when(has_main)
    def _():
        w2_copy(f).start()
        w1_copy(f).wait()

    def hidden(u2):
        h = jnp.dot(u2, _unpack_rows(w1_buf[...]), preferred_element_type=F32) + b1_ref[f]
        return jnp.square(jnp.maximum(h, 0.0)).astype(BF16)

    def finish(x1, g2, acc):
        return _layernorm(alpha * x1 + g2 * (acc + b2_ref[...]), lng_ref[...], lnb_ref[...])

    def epilogue(piece=0, n_pieces=1):
        tp = te // n_pieces
        base = pl.multiple_of(f * te + piece * tp, tp)
        g2 = g2_ref[pl.ds(((m - 1) * tm) // rows_per_cond, 1), :]
        o_ref[piece * tp:(piece + 1) * tp, :] = finish(
            x1_ref[piece * tp:(piece + 1) * tp, :], g2, acc_ref[1 - slot, pl.ds(base, tp), :])

    def main(first, singles, with_epilogue):
        for c in range(n_chunks):
            if with_epilogue:
                epilogue(c, n_chunks)
            rows = slice(c * ROW_CHUNK, (c + 1) * ROW_CHUNK)
            h_buf[rows, :] = hidden(u2_ref[rows, :])
        if singles:
            hs_buf[...] = hidden(u2s_ref[...])

        @pl.when(t + 1 < nm * nf)
        def _():
            w1_copy(jnp.where(f + 1 == nf, 0, f + 1)).start()

        w2_copy(f).wait()
        w2 = _unpack_rows(w2_buf[...])
        for c in range(n_chunks):
            rows = slice(c * ROW_CHUNK, (c + 1) * ROW_CHUNK)
            part = jnp.dot(h_buf[rows, :], w2, preferred_element_type=F32)
            if first:
                acc_ref[slot, rows, :] = part
            else:
                acc_ref[slot, rows, :] += part
        if singles:
            part = jnp.dot(hs_buf[...], w2, preferred_element_type=F32)
            if first:
                accs_ref[...] = part
            else:
                accs_ref[...] += part

    tile0 = m == 0
    later = jnp.logical_and(has_main, m > 0)
    pl.when(jnp.logical_and(tile0, f == 0))(functools.partial(main, True, True, False))
    pl.when(jnp.logical_and(tile0, f > 0))(functools.partial(main, False, True, False))
    pl.when(jnp.logical_and(later, f == 0))(functools.partial(main, True, False, True))
    pl.when(jnp.logical_and(later, f > 0))(functools.partial(main, False, False, True))
    pl.when(m == nm)(epilogue)

    @pl.when(jnp.logical_and(m == 1, f == 0))
    def _():
        os_ref[...] = finish(x1s_ref[...], g2s_ref[...], accs_ref[...])


def _mlp(u2, x1, u2s, x1s, ada, w1b, b1, w2b, b2, lng, lnb, *, alpha, rows_per_cond,
         cond_row, tm):
    m, d = u2.shape
    ms = u2s.shape[0]
    nf, _, tf = w1b.shape
    nm = m // tm
    te = tm // nf
    assert m % tm == 0 and tm % nf == 0 and tm % ROW_CHUNK == 0
    assert rows_per_cond % tm == 0 and te % (tm // ROW_CHUNK) == 0 and cond_row % 8 == 0
    g2_col = N_ADA - 1
    body = functools.partial(_mlp_body, nm=nm, nf=nf, te=te, tm=tm, alpha=alpha,
                             rows_per_cond=rows_per_cond)
    ep = lambda mi, f: (jnp.where(mi == 0, 0, (mi - 1) * nf + f), 0)
    return pl.pallas_call(
        body,
        out_shape=[jax.ShapeDtypeStruct((m, d), F32),
                   jax.ShapeDtypeStruct((ms, d), F32)],
        grid=(nm + 1, nf),
        in_specs=[pl.BlockSpec((tm, d), lambda mi, f: (jnp.minimum(mi, nm - 1), 0)),
                  pl.BlockSpec(memory_space=pl.ANY),
                  _resident((nf, 1, tf)),
                  pl.BlockSpec(memory_space=pl.ANY),
                  pl.BlockSpec((te, d), ep),
                  pl.BlockSpec((8, d), lambda mi, f: (cond_row // 8, g2_col)),
                  _resident((1, d)),
                  _resident((1, d)),
                  _resident((1, d)),
                  _resident((ms, d)),
                  _resident((ms, d)),
                  pl.BlockSpec((ms, d), lambda mi, f: (0, g2_col),
                               pipeline_mode=pl.Buffered(1))],
        out_specs=[pl.BlockSpec((te, d), ep),
                   _resident((ms, d))],
        scratch_shapes=[pltpu.VMEM((2, tm, d), F32),
                        pltpu.VMEM((ms, d), F32),
                        pltpu.VMEM((d // 2, tf), U32),
                        pltpu.VMEM((tf // 2, d), U32),
                        pltpu.VMEM((tm, tf), BF16),
                        pltpu.VMEM((ms, tf), BF16),
                        pltpu.SemaphoreType.DMA((2,))],
        compiler_params=_params(("arbitrary", "arbitrary"), MLP_VMEM_LIMIT_BYTES_V7X),
        name="mlp",
    )(u2, w1b, b1.reshape(nf, 1, tf), w2b, x1, ada, b2.reshape(1, d), lng, lnb, u2s, x1s, ada)


def kernel(x_prompt, x_sample, state_pool, state_conv, c_prompt, c_sample, w_ada, b_ada, w_in,
           pool_grp_w, pool_scale, conv_w, w_pool_up, w_conv_up, w_o, ln1_g, ln1_b, w_ff1,
           b_ff1, w_ff2, b_ff2, ln2_g, ln2_b):
    depth = w_ada.shape[0]
    nb, seq, d = x_prompt.shape
    ms, dec_seq, _ = x_sample.shape
    assert dec_seq == 1
    p = d // 2
    alpha = (2 * depth) ** 0.25

    prompt_row = ms
    pad = (-(ms + nb)) % 8
    c_all = jnp.concatenate([c_sample, c_prompt, jnp.zeros((pad, d), F32)], axis=0)

    xp = x_prompt.reshape(nb * seq, d)
    xs = x_sample.reshape(ms, d)
    pool_p, conv_p, pool_s, conv_s = [], [], [], []
    for l in range(depth):
        ada = _ada(c_all, w_ada[l], b_ada[l])
        pscale = pool_scale[l].reshape(1, p)
        lng1, lnb1 = ln1_g[l].reshape(1, d), ln1_b[l].reshape(1, d)
        lng2, lnb2 = ln2_g[l].reshape(1, d), ln2_b[l].reshape(1, d)

        n_in = w_in.shape[2]
        (pa_p, pa_s, u_bf), _ = _inproj(xp, xs, ada, w_in[l], [], col0=0, ncols=p,
                                        modulate=True, seq=seq, prompt_row=prompt_row)
        (pb_p, pb_s), (grp_bf, wpu_bf, wcu_bf, wo_bf) = _inproj(
            u_bf, xs, ada, w_in[l],
            [pool_grp_w[l].reshape(p, -1), w_pool_up[l], w_conv_up[l], w_o[l]],
            col0=p, ncols=n_in - p, modulate=False, seq=seq, prompt_row=prompt_row,
            tm=INPROJ_BF16_TOKEN_TILE)

        x1p, u2p, pst, cst, w1b, w2b = _mixer(
            pa_p, pb_p, xp, ada, grp_bf, pscale, conv_w[l], wpu_bf, wcu_bf, wo_bf, lng1, lnb1,
            w_ff1[l], w_ff2[l], seq=seq, prompt_row=prompt_row, alpha=alpha, tf=MLP_FF_CHUNK)
        x1s, u2s, npool, nconv = _mixer_step(
            pa_s, pb_s, jnp.transpose(state_pool[l], (1, 0, 2)),
            jnp.transpose(state_conv[l], (1, 0, 2)), xs, ada, grp_bf, pscale, conv_w[l],
            wpu_bf, wcu_bf, wo_bf, lng1, lnb1, alpha=alpha)

        xp, xs = _mlp(u2p, x1p, u2s, x1s, ada, w1b, b_ff1[l], w2b, b_ff2[l], lng2, lnb2,
                      alpha=alpha, rows_per_cond=seq, cond_row=prompt_row,
                      tm=MLP_TOKEN_TILE)

        pool_p.append(pst[:, HIST_ROWS - POOL_HIST:, :])
        conv_p.append(cst[:, HIST_ROWS - CONV_HIST:, :])
        pool_s.append(jnp.transpose(npool, (1, 0, 2)))
        conv_s.append(jnp.transpose(nconv, (1, 0, 2)))

    return (xp.reshape(nb, seq, d), xs.reshape(ms, dec_seq, d),
            jnp.stack(pool_p), jnp.stack(conv_p), jnp.stack(pool_s), jnp.stack(conv_s))
```

```python
import functools

import jax
import jax.numpy as jnp
from jax import lax
from jax.experimental import pallas as pl
from jax.experimental.pallas import tpu as pltpu

F32 = jnp.float32
BF16 = jnp.bfloat16
U32 = jnp.uint32

POOL_WINDOWS = (2, 4, 8, 16)
POOL_HIST = max(POOL_WINDOWS) - 1
CONV_K = 3
CONV_HIST = CONV_K - 1
N_ADA = 6
PAST_LEN = 16384
LN_EPS = 1e-5

VMEM_LIMIT_BYTES_V7X = 56 * 1024 * 1024
VMEM_LIMIT_BIG_TILE_BYTES_V7X = 60 * 1024 * 1024
MLP_VMEM_LIMIT_BYTES_V7X = 62 * 1024 * 1024
HIST_ROWS = 16
ROW_CHUNK = 256
MIXER_COL_BLOCK = 256
MIXER_FINISH_SLABS = 4
INPROJ_BF16_TOKEN_TILE = 2048
MLP_TOKEN_TILE = 1024
MLP_FF_CHUNK = 2048


def _resident(shape):
    zeros = (0,) * len(shape)
    return pl.BlockSpec(shape, lambda *_: zeros, pipeline_mode=pl.Buffered(1))


def _params(semantics, vmem_limit_bytes=VMEM_LIMIT_BYTES_V7X):
    return pltpu.CompilerParams(dimension_semantics=semantics,
                                vmem_limit_bytes=vmem_limit_bytes)


def _layernorm(xf, g, b):
    mu = jnp.mean(xf, axis=-1, keepdims=True)
    var = jnp.mean(jnp.square(xf - mu), axis=-1, keepdims=True)
    return (xf - mu) * lax.rsqrt(var + LN_EPS) * g + b


def _ada_body(c_ref, w_ref, b_ref, o_ref):
    o_ref[...] = jnp.dot(c_ref[...].astype(BF16), w_ref[...].astype(BF16),
                         preferred_element_type=F32) + b_ref[...]


def _ada(c_all, w_ada, b_ada, tn=1024):
    m, d = c_all.shape
    n = w_ada.shape[1]
    return pl.pallas_call(
        _ada_body,
        out_shape=jax.ShapeDtypeStruct((m, n), F32),
        grid=(n // tn,),
        in_specs=[pl.BlockSpec((m, d), lambda j: (0, 0)),
                  pl.BlockSpec((d, tn), lambda j: (0, j)),
                  pl.BlockSpec((1, tn), lambda j: (0, j))],
        out_specs=pl.BlockSpec((m, tn), lambda j: (0, j)),
        compiler_params=pltpu.CompilerParams(
            dimension_semantics=("arbitrary",), vmem_limit_bytes=VMEM_LIMIT_BYTES_V7X,
            allow_input_fusion=[True, False, False]),
        name="ada",
    )(c_all, w_ada, b_ada.reshape(1, n))


def _pack_rows(x_bf16):
    return pltpu.bitcast(x_bf16, U32)


def _unpack_rows(x_u32):
    return pltpu.bitcast(x_u32, BF16)


def _inproj_body(lhs_ref, adap_ref, w_ref, xs_ref, adas_ref, *rest, tiles_per_seq, d, tm,
                 n_side, side_steps, modulate):
    side_in = rest[:n_side]
    outs = rest[n_side:]
    o_ref, os_ref = outs[:2]
    n_main = 3 if modulate else 2
    u_ref = outs[2] if modulate else None
    side_out = outs[n_main:n_main + n_side]
    wb_ref = outs[n_main + n_side]
    i = pl.program_id(1)

    if n_side:
        @pl.when(pl.program_id(0) * pl.num_programs(1) + i < side_steps)
        def _():
            for src, dst in zip(side_in, side_out):
                dst[...] = _pack_rows(src[...].astype(BF16))

    @pl.when(i == 0)
    def _():
        wb_ref[...] = w_ref[...].astype(BF16)
        us = xs_ref[...] * (1.0 + adas_ref[:, d:2 * d]) + adas_ref[:, 0:d]
        os_ref[...] = jnp.dot(us.astype(BF16), wb_ref[...], preferred_element_type=F32)

    b = i // tiles_per_seq
    sh1 = adap_ref[pl.ds(b, 1), 0:d]
    sc1 = adap_ref[pl.ds(b, 1), d:2 * d]
    for r in range(0, tm, ROW_CHUNK):
        rows = slice(r, r + ROW_CHUNK)
        if modulate:
            u = (lhs_ref[rows, :] * (1.0 + sc1) + sh1).astype(BF16)
            u_ref[rows, :] = u
        else:
            u = lhs_ref[rows, :]
        o_ref[rows, :] = jnp.dot(u, wb_ref[...], preferred_element_type=F32).astype(BF16)


def _inproj(lhs, xs, ada, w_in, side_ws, *, col0, ncols, modulate, seq, prompt_row,
            tm=1024, tn=1024, side_steps=16):
    m, d = lhs.shape
    ms = xs.shape[0]
    assert seq % tm == 0 and m % seq == 0 and tm % ROW_CHUNK == 0
    assert col0 % tn == 0 and ncols % tn == 0 and prompt_row % 8 == 0
    ni = m // tm
    jb = col0 // tn
    body = functools.partial(_inproj_body, tiles_per_seq=seq // tm, d=d, tm=tm,
                             n_side=len(side_ws), side_steps=side_steps, modulate=modulate)
    side_idx = lambda j, i: (jnp.minimum(j * ni + i, side_steps - 1), 0)
    side_in = [pl.BlockSpec((w.shape[0] // side_steps, w.shape[1]), side_idx) for w in side_ws]
    side_out = [pl.BlockSpec((w.shape[0] // side_steps // 2, w.shape[1]), side_idx)
                for w in side_ws]
    side_shape = [jax.ShapeDtypeStruct((w.shape[0] // 2, w.shape[1]), U32) for w in side_ws]
    u_shape = [jax.ShapeDtypeStruct((m, d), BF16)] if modulate else []
    u_spec = [pl.BlockSpec((tm, d), lambda j, i: (i, 0))] if modulate else []
    out = pl.pallas_call(
        body,
        out_shape=[jax.ShapeDtypeStruct((m, ncols), BF16),
                   jax.ShapeDtypeStruct((ms, ncols), F32)] + u_shape + side_shape,
        grid=(ncols // tn, ni),
        in_specs=[pl.BlockSpec((tm, d), lambda j, i: (i, 0)),
                  pl.BlockSpec((8, 2 * d), lambda j, i: (prompt_row // 8, 0)),
                  pl.BlockSpec((d, tn), lambda j, i: (0, j + jb)),
                  _resident((ms, d)),
                  _resident((ms, 2 * d))] + side_in,
        out_specs=[pl.BlockSpec((tm, tn), lambda j, i: (i, j)),
                   pl.BlockSpec((ms, tn), lambda j, i: (0, j))] + u_spec + side_out,
        scratch_shapes=[pltpu.VMEM((d, tn), BF16)],
        compiler_params=_params(("arbitrary", "arbitrary"), VMEM_LIMIT_BIG_TILE_BYTES_V7X),
        name="inproj_first" if modulate else "inproj_rest",
    )(lhs, ada, w_in, xs, ada, *side_ws)
    n_main = 3 if modulate else 2
    return out[:n_main], out[n_main:]


def _sigmoid(x):
    return 0.5 * jnp.tanh(0.5 * x) + 0.5


def _mix_tail(pooled_bf, conv_bf, gp, gc, x, g1, sc2, sh2, wpu_ref, wcu_ref, wo_ref,
              lng_ref, lnb_ref, alpha):
    y_p = jnp.dot(pooled_bf, _unpack_rows(wpu_ref[...]), preferred_element_type=F32)
    y_c = jnp.dot(conv_bf, _unpack_rows(wcu_ref[...]), preferred_element_type=F32)
    mixed = (_sigmoid(gp) * y_p + _sigmoid(gc) * y_c).astype(BF16)
    mo = jnp.dot(mixed, _unpack_rows(wo_ref[...]), preferred_element_type=F32)
    x1 = _layernorm(alpha * x + g1 * mo, lng_ref[...], lnb_ref[...])
    return x1, (x1 * (1.0 + sc2) + sh2).astype(BF16)


def _split_cols(first_ref, rest_ref):
    p = first_ref.shape[-1]

    def cols(lo, hi):
        return first_ref[:, lo:hi] if hi <= p else rest_ref[:, lo - p:hi - p]
    return cols


def _mixer_body(pa_ref, pb_ref, ha_ref, hb_ref, x_ref, ada_ref, grp_ref, pscale_ref, convw_ref,
                wpu_ref, wcu_ref, wo_ref, lng_ref, lnb_ref, w1_ref, w2_ref,
                x1_ref, u2_ref, pst_ref, cst_ref, w1b_ref, w2b_ref,
                zz_ref, vv_ref, pm_ref, bc_ref, mixed_ref,
                *, tm, tiles_per_seq, d, alpha, steps):
    i = pl.program_id(0)
    p = d // 2
    gw = p // len(POOL_WINDOWS)
    h = HIST_ROWS
    proj = _split_cols(pa_ref, pb_ref)
    hist = _split_cols(ha_ref, hb_ref)

    cb = MIXER_COL_BLOCK
    ncb = d // cb
    tis = i % tiles_per_seq
    first = tis == 0

    def side_cast():
        tf = w1b_ref.shape[2]
        for fi in range(w1b_ref.shape[0]):
            w1b_ref[fi] = _pack_rows(w1_ref[:, fi * tf:(fi + 1) * tf].astype(BF16))
        w2b_ref[...] = _pack_rows(w2_ref[...].astype(BF16))

    def setup():
        zz_ref[0:h, :] = jnp.where(first, 0.0, hist(0, p).astype(F32))
        zz_ref[h:h + tm, :] = proj(0, p).astype(F32)
        vh = hist(3 * p, 4 * p).astype(F32) * hist(p, 2 * p).astype(F32)
        vv_ref[0:h, :] = jnp.where(first, 0.0, vh)
        vv_ref[h:h + tm, :] = proj(3 * p, 4 * p).astype(F32) * proj(p, 2 * p).astype(F32)
        pst_ref[0] = zz_ref[tm:tm + h, :]
        cst_ref[0] = vv_ref[tm:tm + h, :]

    def pool_group(g):
        w = POOL_WINDOWS[g]
        cols = slice(g * gw, (g + 1) * gw)
        pos = tis * tm + lax.broadcasted_iota(jnp.int32, (tm, 1), 0)
        s = zz_ref[:, cols]
        zt = s[h:, :]
        k = 1
        while k < w:
            s = s + pltpu.roll(s, k, axis=0)
            k *= 2
        inv_cnt = 1.0 / jnp.minimum(w, pos + 1).astype(F32)
        pooled = (s[h:, :] * inv_cnt - zt).astype(BF16)
        og = jnp.dot(pooled, _unpack_rows(grp_ref[g * gw // 2:(g + 1) * gw // 2, :]),
                     preferred_element_type=F32)
        pm_ref[:, cols] = (og * pscale_ref[:, cols]).astype(BF16)

    def conv_branch():
        conv = convw_ref[0:1, :] * vv_ref[h - 2:h - 2 + tm, :]
        conv = conv + convw_ref[1:2, :] * vv_ref[h - 1:h - 1 + tm, :]
        conv = conv + convw_ref[2:3, :] * vv_ref[h:h + tm, :]
        bc_ref[...] = (proj(2 * p, 3 * p).astype(F32) * conv).astype(BF16)

    def up_and_gate(k):
        cols = slice(k * cb, (k + 1) * cb)
        y_p = jnp.dot(pm_ref[...], _unpack_rows(wpu_ref[:, cols]), preferred_element_type=F32)
        y_c = jnp.dot(bc_ref[...], _unpack_rows(wcu_ref[:, cols]), preferred_element_type=F32)
        gp = proj(4 * p + k * cb, 4 * p + (k + 1) * cb).astype(F32)
        gc = proj(4 * p + d + k * cb, 4 * p + d + (k + 1) * cb).astype(F32)
        mixed_ref[:, cols] = (_sigmoid(gp) * y_p + _sigmoid(gc) * y_c).astype(BF16)

    def out_block(k):
        return jnp.dot(mixed_ref[...], _unpack_rows(wo_ref[:, k * cb:(k + 1) * cb]),
                       preferred_element_type=F32)

    def finish_rows(mo, r):
        nr = tm // MIXER_FINISH_SLABS
        rows = slice(r * nr, (r + 1) * nr)
        b = (i - 1) // tiles_per_seq
        g1 = ada_ref[pl.ds(b, 1), 2 * d:3 * d]
        sh2 = ada_ref[pl.ds(b, 1), 3 * d:4 * d]
        sc2 = ada_ref[pl.ds(b, 1), 4 * d:5 * d]
        x1 = _layernorm(alpha * x_ref[rows, :] + g1 * mo[rows, :], lng_ref[...], lnb_ref[...])
        x1_ref[rows, :] = x1
        u2_ref[rows, :] = (x1 * (1.0 + sc2) + sh2).astype(BF16)

    def step(has_head, has_tail):
        front = [side_cast, setup] + [functools.partial(pool_group, g)
                                      for g in range(len(POOL_WINDOWS))] + [conv_branch]
        mo_blocks = []
        for k in range(ncb):
            if has_tail:
                mo_blocks.append(out_block(k))
            if has_head and k < len(front):
                front[k]()
        if has_head:
            for piece in front[ncb:]:
                piece()
        mo = jnp.concatenate(mo_blocks, axis=1) if has_tail else None
        fin_at = {(r + 1) * ncb // MIXER_FINISH_SLABS - 1: r for r in range(MIXER_FINISH_SLABS)}
        for k in range(ncb):
            if has_head:
                up_and_gate(k)
            if has_tail and k in fin_at:
                finish_rows(mo, fin_at[k])

    pl.when(i == 0)(functools.partial(step, True, False))
    pl.when(jnp.logical_and(i > 0, i < steps))(functools.partial(step, True, True))
    pl.when(i == steps)(functools.partial(step, False, True))


def _mixer(proj_a, proj_b, xp, ada, grp_bf, pscale, convw, wpu_bf, wcu_bf, wo_bf, lng, lnb,
           w1, w2, *, seq, prompt_row, alpha, tf, tm=256):
    m, d = xp.shape
    p = d // 2
    nseq = m // seq
    h = HIST_ROWS
    assert seq % tm == 0 and m % seq == 0 and tm % h == 0 and d % MIXER_COL_BLOCK == 0
    assert proj_a.shape[1] == p and tm % MIXER_FINISH_SLABS == 0
    steps = m // tm
    dff = w1.shape[1]
    assert d % steps == 0 and dff % steps == 0 and dff % tf == 0
    body = functools.partial(_mixer_body, tm=tm, tiles_per_seq=seq // tm, d=d, alpha=alpha,
                             steps=steps)
    cur = lambda i: jnp.minimum(i, steps - 1)
    prev = lambda i: jnp.maximum(i - 1, 0)
    hist_idx = lambda i: (jnp.maximum(cur(i) * (tm // h) - 1, 0), 0)
    return pl.pallas_call(
        body,
        out_shape=[jax.ShapeDtypeStruct((m, d), F32),
                   jax.ShapeDtypeStruct((m, d), BF16),
                   jax.ShapeDtypeStruct((nseq, h, p), F32),
                   jax.ShapeDtypeStruct((nseq, h, p), F32),
                   jax.ShapeDtypeStruct((dff // tf, d // 2, tf), U32),
                   jax.ShapeDtypeStruct((dff // 2, d), U32)],
        grid=(steps + 1,),
        in_specs=[pl.BlockSpec((tm, proj_a.shape[1]), lambda i: (cur(i), 0)),
                  pl.BlockSpec((tm, proj_b.shape[1]), lambda i: (cur(i), 0)),
                  pl.BlockSpec((h, proj_a.shape[1]), hist_idx),
                  pl.BlockSpec((h, proj_b.shape[1]), hist_idx),
                  pl.BlockSpec((tm, d), lambda i: (prev(i), 0)),
                  pl.BlockSpec((8, N_ADA * d), lambda i: (prompt_row // 8, 0)),
                  _resident(grp_bf.shape),
                  _resident(pscale.shape),
                  _resident(convw.shape),
                  _resident(wpu_bf.shape),
                  _resident(wcu_bf.shape),
                  _resident(wo_bf.shape),
                  _resident(lng.shape),
                  _resident(lnb.shape),
                  pl.BlockSpec((d // steps, dff), lambda i: (cur(i), 0)),
                  pl.BlockSpec((dff // steps, d), lambda i: (cur(i), 0))],
        out_specs=[pl.BlockSpec((tm, d), lambda i: (prev(i), 0)),
                   pl.BlockSpec((tm, d), lambda i: (prev(i), 0)),
                   pl.BlockSpec((1, h, p), lambda i: (cur(i) // (seq // tm), 0, 0)),
                   pl.BlockSpec((1, h, p), lambda i: (cur(i) // (seq // tm), 0, 0)),
                   pl.BlockSpec((dff // tf, d // steps // 2, tf), lambda i: (0, cur(i), 0)),
                   pl.BlockSpec((dff // steps // 2, d), lambda i: (cur(i), 0))],
        scratch_shapes=[pltpu.VMEM((tm + h, p), F32),
                        pltpu.VMEM((tm + h, p), F32),
                        pltpu.VMEM((tm, p), BF16),
                        pltpu.VMEM((tm, p), BF16),
                        pltpu.VMEM((tm, d), BF16)],
        compiler_params=_params(("arbitrary",)),
        name="mixer",
    )(proj_a, proj_b, proj_a, proj_b, xp, ada, grp_bf, pscale, convw, wpu_bf, wcu_bf, wo_bf,
      lng, lnb, w1, w2)


def _mixer_step_body(pa_ref, pb_ref, pool_ref, cstate_ref, x_ref, ada_ref, grp_ref, pscale_ref,
                     convw_ref, wpu_ref, wcu_ref, wo_ref, lng_ref, lnb_ref,
                     x1_ref, u2_ref, npool_ref, nconv_ref, pm_ref, *, d, alpha):
    p = d // 2
    gw = p // len(POOL_WINDOWS)
    proj = _split_cols(pa_ref, pb_ref)
    z = proj(0, p)
    for g, w in enumerate(POOL_WINDOWS):
        cols = slice(g * gw, (g + 1) * gw)
        s = z[:, cols]
        for k in range(1, w):
            s = s + pool_ref[POOL_HIST - k, :, cols]
        inv_cnt = 1.0 / min(w, PAST_LEN + 1)
        pooled = (s * inv_cnt - z[:, cols]).astype(BF16)
        og = jnp.dot(pooled, _unpack_rows(grp_ref[g * gw // 2:(g + 1) * gw // 2, :]),
                         preferred_element_type=F32)
        pm_ref[:, cols] = (og * pscale_ref[:, cols]).astype(BF16)
    for r in range(POOL_HIST - 1):
        npool_ref[r] = pool_ref[r + 1]
    npool_ref[POOL_HIST - 1] = z

    xc = proj(p, 2 * p)
    bc = proj(2 * p, 3 * p)
    cc = proj(3 * p, 4 * p)
    v = cc * xc
    conv = convw_ref[0:1, :] * cstate_ref[0]
    conv = conv + convw_ref[1:2, :] * cstate_ref[1]
    conv = conv + convw_ref[2:3, :] * v
    nconv_ref[0] = cstate_ref[1]
    nconv_ref[1] = v

    gp = proj(4 * p, 4 * p + d)
    gc = proj(4 * p + d, 4 * p + 2 * d)
    g1 = ada_ref[:, 2 * d:3 * d]
    sh2 = ada_ref[:, 3 * d:4 * d]
    sc2 = ada_ref[:, 4 * d:5 * d]
    x1, u2 = _mix_tail(pm_ref[...], (bc * conv).astype(BF16), gp, gc, x_ref[...], g1, sc2, sh2,
                       wpu_ref, wcu_ref, wo_ref, lng_ref, lnb_ref, alpha)
    x1_ref[...] = x1
    u2_ref[...] = u2


def _mixer_step(proj_a, proj_b, pool_state, conv_state, xs, ada, grp_bf, pscale, convw,
                wpu_bf, wcu_bf, wo_bf, lng, lnb, *, alpha):
    ms, d = xs.shape
    p = d // 2
    body = functools.partial(_mixer_step_body, d=d, alpha=alpha)
    return pl.pallas_call(
        body,
        out_shape=[jax.ShapeDtypeStruct((ms, d), F32),
                   jax.ShapeDtypeStruct((ms, d), BF16),
                   jax.ShapeDtypeStruct(pool_state.shape, F32),
                   jax.ShapeDtypeStruct(conv_state.shape, F32)],
        grid=(1,),
        in_specs=[_resident(proj_a.shape),
                  _resident(proj_b.shape),
                  _resident(pool_state.shape),
                  _resident(conv_state.shape),
                  _resident((ms, d)),
                  _resident((ms, N_ADA * d)),
                  _resident(grp_bf.shape),
                  _resident(pscale.shape),
                  _resident(convw.shape),
                  _resident(wpu_bf.shape),
                  _resident(wcu_bf.shape),
                  _resident(wo_bf.shape),
                  _resident(lng.shape),
                  _resident(lnb.shape)],
        out_specs=[_resident((ms, d)),
                   _resident((ms, d)),
                   _resident(pool_state.shape),
                   _resident(conv_state.shape)],
        scratch_shapes=[pltpu.VMEM((ms, p), BF16)],
        compiler_params=_params(("arbitrary",)),
        name="mixer_step",
    )(proj_a, proj_b, pool_state, conv_state, xs, ada, grp_bf, pscale, convw, wpu_bf, wcu_bf,
      wo_bf, lng, lnb)


def _mlp_body(u2_ref, w1_hbm, b1_ref, w2_hbm, x1_ref, g2_ref, b2_ref, lng_ref, lnb_ref,
              u2s_ref, x1s_ref, g2s_ref, o_ref, os_ref,
              acc_ref, accs_ref, w1_buf, w2_buf, h_buf, hs_buf, sem,
              *, nm, nf, te, tm, alpha, rows_per_cond):
    m = pl.program_id(0)
    f = pl.program_id(1)
    slot = m % 2
    t = m * nf + f
    w2_rows = w2_buf.shape[0]
    n_chunks = tm // ROW_CHUNK
    tp = te // n_chunks

    def w1_copy(fi):
        return pltpu.make_async_copy(w1_hbm.at[fi], w1_buf, sem.at[0])

    def w2_copy(fi):
        return pltpu.make_async_copy(w2_hbm.at[pl.ds(fi * w2_rows, w2_rows), :], w2_buf,
                                     sem.at[1])

    has_main = m < nm

    @pl.when(t == 0)
    def _():
        w1_copy(0).start()

    @pl.when(has_main)
    def _():
        w2_copy(f).start()
        w1_copy(f).wait()

    def hidden(u2):
        h = jnp.dot(u2, _unpack_rows(w1_buf[...]), preferred_element_type=F32) + b1_ref[f]
        return jnp.square(jnp.maximum(h, 0.0)).astype(BF16)

    def finish(x1, g2, acc):
        return _layernorm(alpha * x1 + g2 * (acc + b2_ref[...]), lng_ref[...], lnb_ref[...])

    def epilogue(c):
        part_rows = pl.ds(pl.multiple_of(c * tp, tp), tp)
        base = pl.multiple_of(f * te + c * tp, tp)
        g2 = g2_ref[pl.ds(((m - 1) * tm) // rows_per_cond, 1), :]
        o_ref[part_rows, :] = finish(x1_ref[part_rows, :], g2,
                                     acc_ref[1 - slot, pl.ds(base, tp), :])

    def main(first, singles, with_epilogue):
        def phase_a(c, carry):
            if with_epilogue:
                epilogue(c)
            rows = pl.ds(pl.multiple_of(c * ROW_CHUNK, ROW_CHUNK), ROW_CHUNK)
            h_buf[rows, :] = hidden(u2_ref[rows, :])
            return carry

        lax.fori_loop(0, n_chunks, phase_a, 0)
        if singles:
            hs_buf[...] = hidden(u2s_ref[...])

        @pl.when(t + 1 < nm * nf)
        def _():
            w1_copy(jnp.where(f + 1 == nf, 0, f + 1)).start()

        w2_copy(f).wait()

        def phase_b(c, carry):
            rows = pl.ds(pl.multiple_of(c * ROW_CHUNK, ROW_CHUNK), ROW_CHUNK)
            part = jnp.dot(h_buf[rows, :], _unpack_rows(w2_buf[...]),
                           preferred_element_type=F32)
            if first:
                acc_ref[slot, rows, :] = part
            else:
                acc_ref[slot, rows, :] += part
            return carry

        lax.fori_loop(0, n_chunks, phase_b, 0)
        if singles:
            part = jnp.dot(hs_buf[...], _unpack_rows(w2_buf[...]), preferred_element_type=F32)
            if first:
                accs_ref[...] = part
            else:
                accs_ref[...] += part

    tile0 = m == 0
    later = jnp.logical_and(has_main, m > 0)
    pl.when(jnp.logical_and(tile0, f == 0))(functools.partial(main, True, True, False))
    pl.when(jnp.logical_and(tile0, f > 0))(functools.partial(main, False, True, False))
    pl.when(jnp.logical_and(later, f == 0))(functools.partial(main, True, False, True))
    pl.when(jnp.logical_and(later, f > 0))(functools.partial(main, False, False, True))

    @pl.when(m == nm)
    def _():
        lax.fori_loop(0, n_chunks, lambda c, carry: (epilogue(c), carry)[1], 0)

    @pl.when(jnp.logical_and(m == 1, f == 0))
    def _():
        os_ref[...] = finish(x1s_ref[...], g2s_ref[...], accs_ref[...])


def _mlp(u2, x1, u2s, x1s, ada, w1b, b1, w2b, b2, lng, lnb, *, alpha, rows_per_cond,
         cond_row, tm):
    m, d = u2.shape
    ms = u2s.shape[0]
    nf, _, tf = w1b.shape
    nm = m // tm
    te = tm // nf
    assert m % tm == 0 and tm % nf == 0 and tm % ROW_CHUNK == 0
    assert rows_per_cond % tm == 0 and te % (tm // ROW_CHUNK) == 0 and cond_row % 8 == 0
    g2_col = N_ADA - 1
    body = functools.partial(_mlp_body, nm=nm, nf=nf, te=te, tm=tm, alpha=alpha,
                             rows_per_cond=rows_per_cond)
    ep = lambda mi, f: (jnp.where(mi == 0, 0, (mi - 1) * nf + f), 0)
    return pl.pallas_call(
        body,
        out_shape=[jax.ShapeDtypeStruct((m, d), F32),
                   jax.ShapeDtypeStruct((ms, d), F32)],
        grid=(nm + 1, nf),
        in_specs=[pl.BlockSpec((tm, d), lambda mi, f: (jnp.minimum(mi, nm - 1), 0)),
                  pl.BlockSpec(memory_space=pl.ANY),
                  _resident((nf, 1, tf)),
                  pl.BlockSpec(memory_space=pl.ANY),
                  pl.BlockSpec((te, d), ep),
                  pl.BlockSpec((8, d), lambda mi, f: (cond_row // 8, g2_col)),
                  _resident((1, d)),
                  _resident((1, d)),
                  _resident((1, d)),
                  _resident((ms, d)),
                  _resident((ms, d)),
                  pl.BlockSpec((ms, d), lambda mi, f: (0, g2_col),
                               pipeline_mode=pl.Buffered(1))],
        out_specs=[pl.BlockSpec((te, d), ep),
                   _resident((ms, d))],
        scratch_shapes=[pltpu.VMEM((2, tm, d), F32),
                        pltpu.VMEM((ms, d), F32),
                        pltpu.VMEM((d // 2, tf), U32),
                        pltpu.VMEM((tf // 2, d), U32),
                        pltpu.VMEM((tm, tf), BF16),
                        pltpu.VMEM((ms, tf), BF16),
                        pltpu.SemaphoreType.DMA((2,))],
        compiler_params=_params(("arbitrary", "arbitrary"), MLP_VMEM_LIMIT_BYTES_V7X),
        name="mlp",
    )(u2, w1b, b1.reshape(nf, 1, tf), w2b, x1, ada, b2.reshape(1, d), lng, lnb, u2s, x1s, ada)


def kernel(x_prompt, x_sample, state_pool, state_conv, c_prompt, c_sample, w_ada, b_ada, w_in,
           pool_grp_w, pool_scale, conv_w, w_pool_up, w_conv_up, w_o, ln1_g, ln1_b, w_ff1,
           b_ff1, w_ff2, b_ff2, ln2_g, ln2_b):
    depth = w_ada.shape[0]
    nb, seq, d = x_prompt.shape
    ms, dec_seq, _ = x_sample.shape
    assert dec_seq == 1
    p = d // 2
    alpha = (2 * depth) ** 0.25

    prompt_row = ms
    pad = (-(ms + nb)) % 8
    c_all = jnp.concatenate([c_sample, c_prompt, jnp.zeros((pad, d), F32)], axis=0)

    xp = x_prompt.reshape(nb * seq, d)
    xs = x_sample.reshape(ms, d)
    pool_p, conv_p, pool_s, conv_s = [], [], [], []
    for l in range(depth):
        ada = _ada(c_all, w_ada[l], b_ada[l])
        pscale = pool_scale[l].reshape(1, p)
        lng1, lnb1 = ln1_g[l].reshape(1, d), ln1_b[l].reshape(1, d)
        lng2, lnb2 = ln2_g[l].reshape(1, d), ln2_b[l].reshape(1, d)

        n_in = w_in.shape[2]
        (pa_p, pa_s, u_bf), _ = _inproj(xp, xs, ada, w_in[l], [], col0=0, ncols=p,
                                        modulate=True, seq=seq, prompt_row=prompt_row)
        (pb_p, pb_s), (grp_bf, wpu_bf, wcu_bf, wo_bf) = _inproj(
            u_bf, xs, ada, w_in[l],
            [pool_grp_w[l].reshape(p, -1), w_pool_up[l], w_conv_up[l], w_o[l]],
            col0=p, ncols=n_in - p, modulate=False, seq=seq, prompt_row=prompt_row,
            tm=INPROJ_BF16_TOKEN_TILE)

        x1p, u2p, pst, cst, w1b, w2b = _mixer(
            pa_p, pb_p, xp, ada, grp_bf, pscale, conv_w[l], wpu_bf, wcu_bf, wo_bf, lng1, lnb1,
            w_ff1[l], w_ff2[l], seq=seq, prompt_row=prompt_row, alpha=alpha, tf=MLP_FF_CHUNK)
        x1s, u2s, npool, nconv = _mixer_step(
            pa_s, pb_s, jnp.transpose(state_pool[l], (1, 0, 2)),
            jnp.transpose(state_conv[l], (1, 0, 2)), xs, ada, grp_bf, pscale, conv_w[l],
            wpu_bf, wcu_bf, wo_bf, lng1, lnb1, alpha=alpha)

        xp, xs = _mlp(u2p, x1p, u2s, x1s, ada, w1b, b_ff1[l], w2b, b_ff2[l], lng2, lnb2,
                      alpha=alpha, rows_per_cond=seq, cond_row=prompt_row,
                      tm=MLP_TOKEN_TILE)

        pool_p.append(pst[:, HIST_ROWS - POOL_HIST:, :])
        conv_p.append(cst[:, HIST_ROWS - CONV_HIST:, :])
        pool_s.append(jnp.transpose(npool, (1, 0, 2)))
        conv_s.append(jnp.transpose(nconv, (1, 0, 2)))

    return (xp.reshape(nb, seq, d), xs.reshape(ms, dec_seq, d),
            jnp.stack(pool_p), jnp.stack(conv_p), jnp.stack(pool_s), jnp.stack(conv_s))
```

```python
import functools

import jax
import jax.numpy as jnp
from jax import lax
from jax.experimental import pallas as pl
from jax.experimental.pallas import tpu as pltpu

F32 = jnp.float32
BF16 = jnp.bfloat16
U32 = jnp.uint32

POOL_WINDOWS = (2, 4, 8, 16)
POOL_HIST = max(POOL_WINDOWS) - 1
CONV_K = 3
CONV_HIST = CONV_K - 1
N_ADA = 6
PAST_LEN = 16384
LN_EPS = 1e-5

VMEM_LIMIT_BYTES_V7X = 56 * 1024 * 1024
VMEM_LIMIT_BIG_TILE_BYTES_V7X = 60 * 1024 * 1024
MLP_VMEM_LIMIT_BYTES_V7X = 62 * 1024 * 1024
HIST_ROWS = 16
ROW_CHUNK = 256
MIXER_COL_BLOCK = 256
MIXER_FINISH_SLABS = 4
INPROJ_BF16_TOKEN_TILE = 2048
MLP_TOKEN_TILE = 1024
MLP_FF_CHUNK = 2048
MLP_ROW_CHUNK = 512


def _resident(shape):
    zeros = (0,) * len(shape)
    return pl.BlockSpec(shape, lambda *_: zeros, pipeline_mode=pl.Buffered(1))


def _params(semantics, vmem_limit_bytes=VMEM_LIMIT_BYTES_V7X):
    return pltpu.CompilerParams(dimension_semantics=semantics,
                                vmem_limit_bytes=vmem_limit_bytes)


def _layernorm(xf, g, b):
    mu = jnp.mean(xf, axis=-1, keepdims=True)
    var = jnp.mean(jnp.square(xf - mu), axis=-1, keepdims=True)
    return (xf - mu) * lax.rsqrt(var + LN_EPS) * g + b


def _ada_body(c_ref, w_ref, b_ref, o_ref):
    o_ref[...] = jnp.dot(c_ref[...].astype(BF16), w_ref[...].astype(BF16),
                         preferred_element_type=F32) + b_ref[...]


def _ada(c_all, w_ada, b_ada, tn=1024):
    m, d = c_all.shape
    n = w_ada.shape[1]
    return pl.pallas_call(
        _ada_body,
        out_shape=jax.ShapeDtypeStruct((m, n), F32),
        grid=(n // tn,),
        in_specs=[pl.BlockSpec((m, d), lambda j: (0, 0)),
                  pl.BlockSpec((d, tn), lambda j: (0, j)),
                  pl.BlockSpec((1, tn), lambda j: (0, j))],
        out_specs=pl.BlockSpec((m, tn), lambda j: (0, j)),
        compiler_params=pltpu.CompilerParams(
            dimension_semantics=("arbitrary",), vmem_limit_bytes=VMEM_LIMIT_BYTES_V7X,
            allow_input_fusion=[True, False, False]),
        name="ada",
    )(c_all, w_ada, b_ada.reshape(1, n))


def _pack_rows(x_bf16):
    return pltpu.bitcast(x_bf16, U32)


def _unpack_rows(x_u32):
    return pltpu.bitcast(x_u32, BF16)


def _inproj_body(lhs_ref, adap_ref, w_ref, xs_ref, adas_ref, *rest, tiles_per_seq, d, tm,
                 n_side, side_steps, modulate):
    side_in = rest[:n_side]
    outs = rest[n_side:]
    o_ref, os_ref = outs[:2]
    n_main = 3 if modulate else 2
    u_ref = outs[2] if modulate else None
    side_out = outs[n_main:n_main + n_side]
    wb_ref = outs[n_main + n_side]
    i = pl.program_id(1)

    if n_side:
        @pl.when(pl.program_id(0) * pl.num_programs(1) + i < side_steps)
        def _():
            for src, dst in zip(side_in, side_out):
                dst[...] = _pack_rows(src[...].astype(BF16))

    @pl.when(i == 0)
    def _():
        wb_ref[...] = w_ref[...].astype(BF16)
        us = xs_ref[...] * (1.0 + adas_ref[:, d:2 * d]) + adas_ref[:, 0:d]
        os_ref[...] = jnp.dot(us.astype(BF16), wb_ref[...], preferred_element_type=F32)

    b = i // tiles_per_seq
    sh1 = adap_ref[pl.ds(b, 1), 0:d]
    sc1 = adap_ref[pl.ds(b, 1), d:2 * d]
    for r in range(0, tm, ROW_CHUNK):
        rows = slice(r, r + ROW_CHUNK)
        if modulate:
            u = (lhs_ref[rows, :] * (1.0 + sc1) + sh1).astype(BF16)
            u_ref[rows, :] = u
        else:
            u = lhs_ref[rows, :]
        o_ref[rows, :] = jnp.dot(u, wb_ref[...], preferred_element_type=F32).astype(BF16)


def _inproj(lhs, xs, ada, w_in, side_ws, *, col0, ncols, modulate, seq, prompt_row,
            tm=1024, tn=1024, side_steps=16):
    m, d = lhs.shape
    ms = xs.shape[0]
    assert seq % tm == 0 and m % seq == 0 and tm % ROW_CHUNK == 0
    assert col0 % tn == 0 and ncols % tn == 0 and prompt_row % 8 == 0
    ni = m // tm
    jb = col0 // tn
    body = functools.partial(_inproj_body, tiles_per_seq=seq // tm, d=d, tm=tm,
                             n_side=len(side_ws), side_steps=side_steps, modulate=modulate)
    side_idx = lambda j, i: (jnp.minimum(j * ni + i, side_steps - 1), 0)
    side_in = [pl.BlockSpec((w.shape[0] // side_steps, w.shape[1]), side_idx) for w in side_ws]
    side_out = [pl.BlockSpec((w.shape[0] // side_steps // 2, w.shape[1]), side_idx)
                for w in side_ws]
    side_shape = [jax.ShapeDtypeStruct((w.shape[0] // 2, w.shape[1]), U32) for w in side_ws]
    u_shape = [jax.ShapeDtypeStruct((m, d), BF16)] if modulate else []
    u_spec = [pl.BlockSpec((tm, d), lambda j, i: (i, 0))] if modulate else []
    out = pl.pallas_call(
        body,
        out_shape=[jax.ShapeDtypeStruct((m, ncols), BF16),
                   jax.ShapeDtypeStruct((ms, ncols), F32)] + u_shape + side_shape,
        grid=(ncols // tn, ni),
        in_specs=[pl.BlockSpec((tm, d), lambda j, i: (i, 0)),
                  pl.BlockSpec((8, 2 * d), lambda j, i: (prompt_row // 8, 0)),
                  pl.BlockSpec((d, tn), lambda j, i: (0, j + jb)),
                  _resident((ms, d)),
                  _resident((ms, 2 * d))] + side_in,
        out_specs=[pl.BlockSpec((tm, tn), lambda j, i: (i, j)),
                   pl.BlockSpec((ms, tn), lambda j, i: (0, j))] + u_spec + side_out,
        scratch_shapes=[pltpu.VMEM((d, tn), BF16)],
        compiler_params=_params(("arbitrary", "arbitrary"), VMEM_LIMIT_BIG_TILE_BYTES_V7X),
        name="inproj_first" if modulate else "inproj_rest",
    )(lhs, ada, w_in, xs, ada, *side_ws)
    n_main = 3 if modulate else 2
    return out[:n_main], out[n_main:]


def _sigmoid(x):
    return 0.5 * jnp.tanh(0.5 * x) + 0.5


def _mix_tail(pooled_bf, conv_bf, gp, gc, x, g1, sc2, sh2, wpu_ref, wcu_ref, wo_ref,
              lng_ref, lnb_ref, alpha):
    y_p = jnp.dot(pooled_bf, _unpack_rows(wpu_ref[...]), preferred_element_type=F32)
    y_c = jnp.dot(conv_bf, _unpack_rows(wcu_ref[...]), preferred_element_type=F32)
    mixed = (_sigmoid(gp) * y_p + _sigmoid(gc) * y_c).astype(BF16)
    mo = jnp.dot(mixed, _unpack_rows(wo_ref[...]), preferred_element_type=F32)
    x1 = _layernorm(alpha * x + g1 * mo, lng_ref[...], lnb_ref[...])
    return x1, (x1 * (1.0 + sc2) + sh2).astype(BF16)


def _split_cols(first_ref, rest_ref):
    p = first_ref.shape[-1]

    def cols(lo, hi):
        return first_ref[:, lo:hi] if hi <= p else rest_ref[:, lo - p:hi - p]
    return cols


def _mixer_body(pa_ref, pb_ref, ha_ref, hb_ref, x_ref, ada_ref, grp_ref, pscale_ref, convw_ref,
                wpu_ref, wcu_ref, wo_ref, lng_ref, lnb_ref, w1_ref, w2_ref,
                x1_ref, u2_ref, pst_ref, cst_ref, w1b_ref, w2b_ref,
                zz_ref, vv_ref, pm_ref, bc_ref, mixed_ref,
                *, tm, tiles_per_seq, d, alpha, steps):
    i = pl.program_id(0)
    p = d // 2
    gw = p // len(POOL_WINDOWS)
    h = HIST_ROWS
    proj = _split_cols(pa_ref, pb_ref)
    hist = _split_cols(ha_ref, hb_ref)

    cb = MIXER_COL_BLOCK
    ncb = d // cb
    tis = i % tiles_per_seq
    first = tis == 0

    def side_cast():
        tf = w1b_ref.shape[2]
        for fi in range(w1b_ref.shape[0]):
            w1b_ref[fi] = _pack_rows(w1_ref[:, fi * tf:(fi + 1) * tf].astype(BF16))
        w2b_ref[...] = _pack_rows(w2_ref[...].astype(BF16))

    def setup():
        zz_ref[0:h, :] = jnp.where(first, 0.0, hist(0, p).astype(F32))
        zz_ref[h:h + tm, :] = proj(0, p).astype(F32)
        vh = hist(3 * p, 4 * p).astype(F32) * hist(p, 2 * p).astype(F32)
        vv_ref[0:h, :] = jnp.where(first, 0.0, vh)
        vv_ref[h:h + tm, :] = proj(3 * p, 4 * p).astype(F32) * proj(p, 2 * p).astype(F32)
        pst_ref[0] = zz_ref[tm:tm + h, :]
        cst_ref[0] = vv_ref[tm:tm + h, :]

    def pool_group(g):
        w = POOL_WINDOWS[g]
        cols = slice(g * gw, (g + 1) * gw)
        pos = tis * tm + lax.broadcasted_iota(jnp.int32, (tm, 1), 0)
        s = zz_ref[:, cols]
        zt = s[h:, :]
        k = 1
        while k < w:
            s = s + pltpu.roll(s, k, axis=0)
            k *= 2
        inv_cnt = 1.0 / jnp.minimum(w, pos + 1).astype(F32)
        pooled = (s[h:, :] * inv_cnt - zt).astype(BF16)
        og = jnp.dot(pooled, _unpack_rows(grp_ref[g * gw // 2:(g + 1) * gw // 2, :]),
                     preferred_element_type=F32)
        pm_ref[:, cols] = (og * pscale_ref[:, cols]).astype(BF16)

    def conv_branch():
        conv = convw_ref[0:1, :] * vv_ref[h - 2:h - 2 + tm, :]
        conv = conv + convw_ref[1:2, :] * vv_ref[h - 1:h - 1 + tm, :]
        conv = conv + convw_ref[2:3, :] * vv_ref[h:h + tm, :]
        bc_ref[...] = (proj(2 * p, 3 * p).astype(F32) * conv).astype(BF16)

    def up_and_gate(k):
        cols = slice(k * cb, (k + 1) * cb)
        y_p = jnp.dot(pm_ref[...], _unpack_rows(wpu_ref[:, cols]), preferred_element_type=F32)
        y_c = jnp.dot(bc_ref[...], _unpack_rows(wcu_ref[:, cols]), preferred_element_type=F32)
        gp = proj(4 * p + k * cb, 4 * p + (k + 1) * cb).astype(F32)
        gc = proj(4 * p + d + k * cb, 4 * p + d + (k + 1) * cb).astype(F32)
        mixed_ref[:, cols] = (_sigmoid(gp) * y_p + _sigmoid(gc) * y_c).astype(BF16)

    def out_block(k):
        return jnp.dot(mixed_ref[...], _unpack_rows(wo_ref[:, k * cb:(k + 1) * cb]),
                       preferred_element_type=F32)

    def finish_rows(mo, r):
        nr = tm // MIXER_FINISH_SLABS
        rows = slice(r * nr, (r + 1) * nr)
        b = (i - 1) // tiles_per_seq
        g1 = ada_ref[pl.ds(b, 1), 2 * d:3 * d]
        sh2 = ada_ref[pl.ds(b, 1), 3 * d:4 * d]
        sc2 = ada_ref[pl.ds(b, 1), 4 * d:5 * d]
        x1 = _layernorm(alpha * x_ref[rows, :] + g1 * mo[rows, :], lng_ref[...], lnb_ref[...])
        x1_ref[rows, :] = x1
        u2_ref[rows, :] = (x1 * (1.0 + sc2) + sh2).astype(BF16)

    def step(has_head, has_tail):
        front = [side_cast, setup] + [functools.partial(pool_group, g)
                                      for g in range(len(POOL_WINDOWS))] + [conv_branch]
        mo_blocks = []
        for k in range(ncb):
            if has_tail:
                mo_blocks.append(out_block(k))
            if has_head and k < len(front):
                front[k]()
        if has_head:
            for piece in front[ncb:]:
                piece()
        mo = jnp.concatenate(mo_blocks, axis=1) if has_tail else None
        fin_at = {(r + 1) * ncb // MIXER_FINISH_SLABS - 1: r for r in range(MIXER_FINISH_SLABS)}
        for k in range(ncb):
            if has_head:
                up_and_gate(k)
            if has_tail and k in fin_at:
                finish_rows(mo, fin_at[k])

    pl.when(i == 0)(functools.partial(step, True, False))
    pl.when(jnp.logical_and(i > 0, i < steps))(functools.partial(step, True, True))
    pl.when(i == steps)(functools.partial(step, False, True))


def _mixer(proj_a, proj_b, xp, ada, grp_bf, pscale, convw, wpu_bf, wcu_bf, wo_bf, lng, lnb,
           w1, w2, *, seq, prompt_row, alpha, tf, tm=256):
    m, d = xp.shape
    p = d // 2
    nseq = m // seq
    h = HIST_ROWS
    assert seq % tm == 0 and m % seq == 0 and tm % h == 0 and d % MIXER_COL_BLOCK == 0
    assert proj_a.shape[1] == p and tm % MIXER_FINISH_SLABS == 0
    steps = m // tm
    dff = w1.shape[1]
    assert d % steps == 0 and dff % steps == 0 and dff % tf == 0
    body = functools.partial(_mixer_body, tm=tm, tiles_per_seq=seq // tm, d=d, alpha=alpha,
                             steps=steps)
    cur = lambda i: jnp.minimum(i, steps - 1)
    prev = lambda i: jnp.maximum(i - 1, 0)
    hist_idx = lambda i: (jnp.maximum(cur(i) * (tm // h) - 1, 0), 0)
    return pl.pallas_call(
        body,
        out_shape=[jax.ShapeDtypeStruct((m, d), F32),
                   jax.ShapeDtypeStruct((m, d), BF16),
                   jax.ShapeDtypeStruct((nseq, h, p), F32),
                   jax.ShapeDtypeStruct((nseq, h, p), F32),
                   jax.ShapeDtypeStruct((dff // tf, d // 2, tf), U32),
                   jax.ShapeDtypeStruct((dff // 2, d), U32)],
        grid=(steps + 1,),
        in_specs=[pl.BlockSpec((tm, proj_a.shape[1]), lambda i: (cur(i), 0)),
                  pl.BlockSpec((tm, proj_b.shape[1]), lambda i: (cur(i), 0)),
                  pl.BlockSpec((h, proj_a.shape[1]), hist_idx),
                  pl.BlockSpec((h, proj_b.shape[1]), hist_idx),
                  pl.BlockSpec((tm, d), lambda i: (prev(i), 0)),
                  pl.BlockSpec((8, N_ADA * d), lambda i: (prompt_row // 8, 0)),
                  _resident(grp_bf.shape),
                  _resident(pscale.shape),
                  _resident(convw.shape),
                  _resident(wpu_bf.shape),
                  _resident(wcu_bf.shape),
                  _resident(wo_bf.shape),
                  _resident(lng.shape),
                  _resident(lnb.shape),
                  pl.BlockSpec((d // steps, dff), lambda i: (cur(i), 0)),
                  pl.BlockSpec((dff // steps, d), lambda i: (cur(i), 0))],
        out_specs=[pl.BlockSpec((tm, d), lambda i: (prev(i), 0)),
                   pl.BlockSpec((tm, d), lambda i: (prev(i), 0)),
                   pl.BlockSpec((1, h, p), lambda i: (cur(i) // (seq // tm), 0, 0)),
                   pl.BlockSpec((1, h, p), lambda i: (cur(i) // (seq // tm), 0, 0)),
                   pl.BlockSpec((dff // tf, d // steps // 2, tf), lambda i: (0, cur(i), 0)),
                   pl.BlockSpec((dff // steps // 2, d), lambda i: (cur(i), 0))],
        scratch_shapes=[pltpu.VMEM((tm + h, p), F32),
                        pltpu.VMEM((tm + h, p), F32),
                        pltpu.VMEM((tm, p), BF16),
                        pltpu.VMEM((tm, p), BF16),
                        pltpu.VMEM((tm, d), BF16)],
        compiler_params=_params(("arbitrary",)),
        name="mixer",
    )(proj_a, proj_b, proj_a, proj_b, xp, ada, grp_bf, pscale, convw, wpu_bf, wcu_bf, wo_bf,
      lng, lnb, w1, w2)


def _mixer_step_body(pa_ref, pb_ref, pool_ref, cstate_ref, x_ref, ada_ref, grp_ref, pscale_ref,
                     convw_ref, wpu_ref, wcu_ref, wo_ref, lng_ref, lnb_ref,
                     x1_ref, u2_ref, npool_ref, nconv_ref, pm_ref, *, d, alpha):
    p = d // 2
    gw = p // len(POOL_WINDOWS)
    proj = _split_cols(pa_ref, pb_ref)
    z = proj(0, p)
    for g, w in enumerate(POOL_WINDOWS):
        cols = slice(g * gw, (g + 1) * gw)
        s = z[:, cols]
        for k in range(1, w):
            s = s + pool_ref[POOL_HIST - k, :, cols]
        inv_cnt = 1.0 / min(w, PAST_LEN + 1)
        pooled = (s * inv_cnt - z[:, cols]).astype(BF16)
        og = jnp.dot(pooled, _unpack_rows(grp_ref[g * gw // 2:(g + 1) * gw // 2, :]),
                         preferred_element_type=F32)
        pm_ref[:, cols] = (og * pscale_ref[:, cols]).astype(BF16)
    for r in range(POOL_HIST - 1):
        npool_ref[r] = pool_ref[r + 1]
    npool_ref[POOL_HIST - 1] = z

    xc = proj(p, 2 * p)
    bc = proj(2 * p, 3 * p)
    cc = proj(3 * p, 4 * p)
    v = cc * xc
    conv = convw_ref[0:1, :] * cstate_ref[0]
    conv = conv + convw_ref[1:2, :] * cstate_ref[1]
    conv = conv + convw_ref[2:3, :] * v
    nconv_ref[0] = cstate_ref[1]
    nconv_ref[1] = v

    gp = proj(4 * p, 4 * p + d)
    gc = proj(4 * p + d, 4 * p + 2 * d)
    g1 = ada_ref[:, 2 * d:3 * d]
    sh2 = ada_ref[:, 3 * d:4 * d]
    sc2 = ada_ref[:, 4 * d:5 * d]
    x1, u2 = _mix_tail(pm_ref[...], (bc * conv).astype(BF16), gp, gc, x_ref[...], g1, sc2, sh2,
                       wpu_ref, wcu_ref, wo_ref, lng_ref, lnb_ref, alpha)
    x1_ref[...] = x1
    u2_ref[...] = u2


def _mixer_step(proj_a, proj_b, pool_state, conv_state, xs, ada, grp_bf, pscale, convw,
                wpu_bf, wcu_bf, wo_bf, lng, lnb, *, alpha):
    ms, d = xs.shape
    p = d // 2
    body = functools.partial(_mixer_step_body, d=d, alpha=alpha)
    return pl.pallas_call(
        body,
        out_shape=[jax.ShapeDtypeStruct((ms, d), F32),
                   jax.ShapeDtypeStruct((ms, d), BF16),
                   jax.ShapeDtypeStruct(pool_state.shape, F32),
                   jax.ShapeDtypeStruct(conv_state.shape, F32)],
        grid=(1,),
        in_specs=[_resident(proj_a.shape),
                  _resident(proj_b.shape),
                  _resident(pool_state.shape),
                  _resident(conv_state.shape),
                  _resident((ms, d)),
                  _resident((ms, N_ADA * d)),
                  _resident(grp_bf.shape),
                  _resident(pscale.shape),
                  _resident(convw.shape),
                  _resident(wpu_bf.shape),
                  _resident(wcu_bf.shape),
                  _resident(wo_bf.shape),
                  _resident(lng.shape),
                  _resident(lnb.shape)],
        out_specs=[_resident((ms, d)),
                   _resident((ms, d)),
                   _resident(pool_state.shape),
                   _resident(conv_state.shape)],
        scratch_shapes=[pltpu.VMEM((ms, p), BF16)],
        compiler_params=_params(("arbitrary",)),
        name="mixer_step",
    )(proj_a, proj_b, pool_state, conv_state, xs, ada, grp_bf, pscale, convw, wpu_bf, wcu_bf,
      wo_bf, lng, lnb)


def _mlp_body(u2_ref, w1_hbm, b1_ref, w2_hbm, x1_ref, g2_ref, b2_ref, lng_ref, lnb_ref,
              u2s_ref, x1s_ref, g2s_ref, o_ref, os_ref,
              acc_ref, accs_ref, w1_buf, w2_buf, h_buf, hs_buf, sem,
              *, nm, nf, te, tm, alpha, rows_per_cond):
    m = pl.program_id(0)
    f = pl.program_id(1)
    slot = m % 2
    t = m * nf + f
    w2_rows = w2_buf.shape[0]
    n_chunks = tm // MLP_ROW_CHUNK
    tp = te // n_chunks

    def w1_copy(fi):
        return pltpu.make_async_copy(w1_hbm.at[fi], w1_buf, sem.at[0])

    def w2_copy(fi):
        return pltpu.make_async_copy(w2_hbm.at[pl.ds(fi * w2_rows, w2_rows), :], w2_buf,
                                     sem.at[1])

    has_main = m < nm

    @pl.when(t == 0)
    def _():
        w1_copy(0).start()

    @pl.when(has_main)
    def _():
        w2_copy(f).start()
        w1_copy(f).wait()

    def hidden(u2):
        h = jnp.dot(u2, _unpack_rows(w1_buf[...]), preferred_element_type=F32) + b1_ref[f]
        return jnp.square(jnp.maximum(h, 0.0)).astype(BF16)

    def finish(x1, g2, acc):
        return _layernorm(alpha * x1 + g2 * (acc + b2_ref[...]), lng_ref[...], lnb_ref[...])

    def epilogue(c):
        part_rows = pl.ds(pl.multiple_of(c * tp, tp), tp)
        base = pl.multiple_of(f * te + c * tp, tp)
        g2 = g2_ref[pl.ds(((m - 1) * tm) // rows_per_cond, 1), :]
        o_ref[part_rows, :] = finish(x1_ref[part_rows, :], g2,
                                     acc_ref[1 - slot, pl.ds(base, tp), :])

    def main(first, singles, with_epilogue):
        def phase_a(c, carry):
            if with_epilogue:
                epilogue(c)
            rows = pl.ds(pl.multiple_of(c * MLP_ROW_CHUNK, MLP_ROW_CHUNK), MLP_ROW_CHUNK)
            h_buf[rows, :] = hidden(u2_ref[rows, :])
            return carry

        lax.fori_loop(0, n_chunks, phase_a, 0)
        if singles:
            hs_buf[...] = hidden(u2s_ref[...])

        @pl.when(t + 1 < nm * nf)
        def _():
            w1_copy(jnp.where(f + 1 == nf, 0, f + 1)).start()

        w2_copy(f).wait()

        def phase_b(c, carry):
            rows = pl.ds(pl.multiple_of(c * MLP_ROW_CHUNK, MLP_ROW_CHUNK), MLP_ROW_CHUNK)
            part = jnp.dot(h_buf[rows, :], _unpack_rows(w2_buf[...]),
                           preferred_element_type=F32)
            if first:
                acc_ref[slot, rows, :] = part
            else:
                acc_ref[slot, rows, :] += part
            return carry

        lax.fori_loop(0, n_chunks, phase_b, 0)
        if singles:
            part = jnp.dot(hs_buf[...], _unpack_rows(w2_buf[...]), preferred_element_type=F32)
            if first:
                accs_ref[...] = part
            else:
                accs_ref[...] += part

    tile0 = m == 0
    later = jnp.logical_and(has_main, m > 0)
    pl.when(jnp.logical_and(tile0, f == 0))(functools.partial(main, True, True, False))
    pl.when(jnp.logical_and(tile0, f > 0))(functools.partial(main, False, True, False))
    pl.when(jnp.logical_and(later, f == 0))(functools.partial(main, True, False, True))
    pl.when(jnp.logical_and(later, f > 0))(functools.partial(main, False, False, True))

    @pl.when(m == nm)
    def _():
        lax.fori_loop(0, n_chunks, lambda c, carry: (epilogue(c), carry)[1], 0)

    @pl.when(jnp.logical_and(m == 1, f == 0))
    def _():
        os_ref[...] = finish(x1s_ref[...], g2s_ref[...], accs_ref[...])


def _mlp(u2, x1, u2s, x1s, ada, w1b, b1, w2b, b2, lng, lnb, *, alpha, rows_per_cond,
         cond_row, tm):
    m, d = u2.shape
    ms = u2s.shape[0]
    nf, _, tf = w1b.shape
    nm = m // tm
    te = tm // nf
    assert m % tm == 0 and tm % nf == 0 and tm % MLP_ROW_CHUNK == 0
    assert rows_per_cond % tm == 0 and te % (tm // MLP_ROW_CHUNK) == 0 and cond_row % 8 == 0
    g2_col = N_ADA - 1
    body = functools.partial(_mlp_body, nm=nm, nf=nf, te=te, tm=tm, alpha=alpha,
                             rows_per_cond=rows_per_cond)
    ep = lambda mi, f: (jnp.where(mi == 0, 0, (mi - 1) * nf + f), 0)
    return pl.pallas_call(
        body,
        out_shape=[jax.ShapeDtypeStruct((m, d), F32),
                   jax.ShapeDtypeStruct((ms, d), F32)],
        grid=(nm + 1, nf),
        in_specs=[pl.BlockSpec((tm, d), lambda mi, f: (jnp.minimum(mi, nm - 1), 0)),
                  pl.BlockSpec(memory_space=pl.ANY),
                  _resident((nf, 1, tf)),
                  pl.BlockSpec(memory_space=pl.ANY),
                  pl.BlockSpec((te, d), ep),
                  pl.BlockSpec((8, d), lambda mi, f: (cond_row // 8, g2_col)),
                  _resident((1, d)),
                  _resident((1, d)),
                  _resident((1, d)),
                  _resident((ms, d)),
                  _resident((ms, d)),
                  pl.BlockSpec((ms, d), lambda mi, f: (0, g2_col),
                               pipeline_mode=pl.Buffered(1))],
        out_specs=[pl.BlockSpec((te, d), ep),
                   _resident((ms, d))],
        scratch_shapes=[pltpu.VMEM((2, tm, d), F32),
                        pltpu.VMEM((ms, d), F32),
                        pltpu.VMEM((d // 2, tf), U32),
                        pltpu.VMEM((tf // 2, d), U32),
                        pltpu.VMEM((tm, tf), BF16),
                        pltpu.VMEM((ms, tf), BF16),
                        pltpu.SemaphoreType.DMA((2,))],
        compiler_params=_params(("arbitrary", "arbitrary"), MLP_VMEM_LIMIT_BYTES_V7X),
        name="mlp",
    )(u2, w1b, b1.reshape(nf, 1, tf), w2b, x1, ada, b2.reshape(1, d), lng, lnb, u2s, x1s, ada)


def kernel(x_prompt, x_sample, state_pool, state_conv, c_prompt, c_sample, w_ada, b_ada, w_in,
           pool_grp_w, pool_scale, conv_w, w_pool_up, w_conv_up, w_o, ln1_g, ln1_b, w_ff1,
           b_ff1, w_ff2, b_ff2, ln2_g, ln2_b):
    depth = w_ada.shape[0]
    nb, seq, d = x_prompt.shape
    ms, dec_seq, _ = x_sample.shape
    assert dec_seq == 1
    p = d // 2
    alpha = (2 * depth) ** 0.25

    prompt_row = ms
    pad = (-(ms + nb)) % 8
    c_all = jnp.concatenate([c_sample, c_prompt, jnp.zeros((pad, d), F32)], axis=0)

    xp = x_prompt.reshape(nb * seq, d)
    xs = x_sample.reshape(ms, d)
    pool_p, conv_p, pool_s, conv_s = [], [], [], []
    for l in range(depth):
        ada = _ada(c_all, w_ada[l], b_ada[l])
        pscale = pool_scale[l].reshape(1, p)
        lng1, lnb1 = ln1_g[l].reshape(1, d), ln1_b[l].reshape(1, d)
        lng2, lnb2 = ln2_g[l].reshape(1, d), ln2_b[l].reshape(1, d)

        n_in = w_in.shape[2]
        (pa_p, pa_s, u_bf), _ = _inproj(xp, xs, ada, w_in[l], [], col0=0, ncols=p,
                                        modulate=True, seq=seq, prompt_row=prompt_row)
        (pb_p, pb_s), (grp_bf, wpu_bf, wcu_bf, wo_bf) = _inproj(
            u_bf, xs, ada, w_in[l],
            [pool_grp_w[l].reshape(p, -1), w_pool_up[l], w_conv_up[l], w_o[l]],
            col0=p, ncols=n_in - p, modulate=False, seq=seq, prompt_row=prompt_row,
            tm=INPROJ_BF16_TOKEN_TILE)

        x1p, u2p, pst, cst, w1b, w2b = _mixer(
            pa_p, pb_p, xp, ada, grp_bf, pscale, conv_w[l], wpu_bf, wcu_bf, wo_bf, lng1, lnb1,
            w_ff1[l], w_ff2[l], seq=seq, prompt_row=prompt_row, alpha=alpha, tf=MLP_FF_CHUNK)
        x1s, u2s, npool, nconv = _mixer_step(
            pa_s, pb_s, jnp.transpose(state_pool[l], (1, 0, 2)),
            jnp.transpose(state_conv[l], (1, 0, 2)), xs, ada, grp_bf, pscale, conv_w[l],
            wpu_bf, wcu_bf, wo_bf, lng1, lnb1, alpha=alpha)

        xp, xs = _mlp(u2p, x1p, u2s, x1s, ada, w1b, b_ff1[l], w2b, b_ff2[l], lng2, lnb2,
                      alpha=alpha, rows_per_cond=seq, cond_row=prompt_row,
                      tm=MLP_TOKEN_TILE)

        pool_p.append(pst[:, HIST_ROWS - POOL_HIST:, :])
        conv_p.append(cst[:, HIST_ROWS - CONV_HIST:, :])
        pool_s.append(jnp.transpose(npool, (1, 0, 2)))
        conv_s.append(jnp.transpose(nconv, (1, 0, 2)))

    return (xp.reshape(nb, seq, d), xs.reshape(ms, dec_seq, d),
            jnp.stack(pool_p), jnp.stack(conv_p), jnp.stack(pool_s), jnp.stack(conv_s))
```

```python
import functools

import jax
import jax.numpy as jnp
from jax import lax
from jax.experimental import pallas as pl
from jax.experimental.pallas import tpu as pltpu

F32 = jnp.float32
BF16 = jnp.bfloat16
U32 = jnp.uint32

POOL_WINDOWS = (2, 4, 8, 16)
POOL_HIST = max(POOL_WINDOWS) - 1
CONV_K = 3
CONV_HIST = CONV_K - 1
N_ADA = 6
PAST_LEN = 16384
LN_EPS = 1e-5

VMEM_LIMIT_BYTES_V7X = 56 * 1024 * 1024
VMEM_LIMIT_BIG_TILE_BYTES_V7X = 60 * 1024 * 1024
MLP_VMEM_LIMIT_BYTES_V7X = 62 * 1024 * 1024
HIST_ROWS = 16
ROW_CHUNK = 256
MIXER_COL_BLOCK = 256
MIXER_FINISH_SLABS = 4
INPROJ_BF16_TOKEN_TILE = 2048
INPROJ_WIDE_COLS = 1792
MLP_TOKEN_TILE = 1024
MLP_FF_CHUNK = 2048
MLP_ROW_CHUNK = 512


def _resident(shape):
    zeros = (0,) * len(shape)
    return pl.BlockSpec(shape, lambda *_: zeros, pipeline_mode=pl.Buffered(1))


def _params(semantics, vmem_limit_bytes=VMEM_LIMIT_BYTES_V7X):
    return pltpu.CompilerParams(dimension_semantics=semantics,
                                vmem_limit_bytes=vmem_limit_bytes)


def _layernorm(xf, g, b):
    mu = jnp.mean(xf, axis=-1, keepdims=True)
    var = jnp.mean(jnp.square(xf - mu), axis=-1, keepdims=True)
    return (xf - mu) * lax.rsqrt(var + LN_EPS) * g + b


def _ada_body(c_ref, w_ref, b_ref, o_ref):
    o_ref[...] = jnp.dot(c_ref[...].astype(BF16), w_ref[...].astype(BF16),
                         preferred_element_type=F32) + b_ref[...]


def _ada(c_all, w_ada, b_ada, tn=1024):
    m, d = c_all.shape
    n = w_ada.shape[1]
    return pl.pallas_call(
        _ada_body,
        out_shape=jax.ShapeDtypeStruct((m, n), F32),
        grid=(n // tn,),
        in_specs=[pl.BlockSpec((m, d), lambda j: (0, 0)),
                  pl.BlockSpec((d, tn), lambda j: (0, j)),
                  pl.BlockSpec((1, tn), lambda j: (0, j))],
        out_specs=pl.BlockSpec((m, tn), lambda j: (0, j)),
        compiler_params=pltpu.CompilerParams(
            dimension_semantics=("arbitrary",), vmem_limit_bytes=VMEM_LIMIT_BYTES_V7X,
            allow_input_fusion=[True, False, False]),
        name="ada",
    )(c_all, w_ada, b_ada.reshape(1, n))


def _pack_rows(x_bf16):
    return pltpu.bitcast(x_bf16, U32)


def _unpack_rows(x_u32):
    return pltpu.bitcast(x_u32, BF16)


def _inproj_body(lhs_ref, adap_ref, w_ref, xs_ref, adas_ref, *rest, tiles_per_seq, d, tm,
                 n_side, side_steps, modulate):
    side_in = rest[:n_side]
    outs = rest[n_side:]
    o_ref, os_ref = outs[:2]
    n_main = 4 if modulate else 2
    u_ref, usb_ref = outs[2:4] if modulate else (None, None)
    side_out = outs[n_main:n_main + n_side]
    wb_ref = outs[n_main + n_side]
    i = pl.program_id(1)

    if n_side:
        @pl.when(pl.program_id(0) * pl.num_programs(1) + i < side_steps)
        def _():
            for src, dst in zip(side_in, side_out):
                dst[...] = _pack_rows(src[...].astype(BF16))

    @pl.when(i == 0)
    def _():
        wb_ref[...] = w_ref[...].astype(BF16)
        us = (xs_ref[...] * (1.0 + adas_ref[:, d:2 * d]) + adas_ref[:, 0:d]).astype(BF16)
        os_ref[...] = jnp.dot(us, wb_ref[...], preferred_element_type=F32)
        if modulate:
            usb_ref[...] = us

    b = i // tiles_per_seq
    sh1 = adap_ref[pl.ds(b, 1), 0:d]
    sc1 = adap_ref[pl.ds(b, 1), d:2 * d]
    for r in range(0, tm, ROW_CHUNK):
        rows = slice(r, r + ROW_CHUNK)
        if modulate:
            u = (lhs_ref[rows, :] * (1.0 + sc1) + sh1).astype(BF16)
            u_ref[rows, :] = u
        else:
            u = lhs_ref[rows, :]
        o_ref[rows, :] = jnp.dot(u, wb_ref[...], preferred_element_type=F32).astype(BF16)


def _inproj(lhs, xs, ada, w_in, side_ws, *, col0, ncols, modulate, seq, prompt_row,
            tm=1024, tn=1024, side_steps=16):
    m, d = lhs.shape
    ms = xs.shape[0]
    assert seq % tm == 0 and m % seq == 0 and tm % ROW_CHUNK == 0
    assert col0 % tn == 0 and ncols % tn == 0 and prompt_row % 8 == 0
    ni = m // tm
    jb = col0 // tn
    body = functools.partial(_inproj_body, tiles_per_seq=seq // tm, d=d, tm=tm,
                             n_side=len(side_ws), side_steps=side_steps, modulate=modulate)
    side_idx = lambda j, i: (jnp.minimum(j * ni + i, side_steps - 1), 0)
    side_in = [pl.BlockSpec((w.shape[0] // side_steps, w.shape[1]), side_idx) for w in side_ws]
    side_out = [pl.BlockSpec((w.shape[0] // side_steps // 2, w.shape[1]), side_idx)
                for w in side_ws]
    side_shape = [jax.ShapeDtypeStruct((w.shape[0] // 2, w.shape[1]), U32) for w in side_ws]
    u_shape = ([jax.ShapeDtypeStruct((m, d), BF16), jax.ShapeDtypeStruct((ms, d), BF16)]
               if modulate else [])
    u_spec = ([pl.BlockSpec((tm, d), lambda j, i: (i, 0)), _resident((ms, d))]
              if modulate else [])
    out = pl.pallas_call(
        body,
        out_shape=[jax.ShapeDtypeStruct((m, ncols), BF16),
                   jax.ShapeDtypeStruct((ms, ncols), F32)] + u_shape + side_shape,
        grid=(ncols // tn, ni),
        in_specs=[pl.BlockSpec((tm, d), lambda j, i: (i, 0)),
                  pl.BlockSpec((8, 2 * d), lambda j, i: (prompt_row // 8, 0)),
                  pl.BlockSpec((d, tn), lambda j, i: (0, j + jb)),
                  _resident((ms, d)),
                  _resident((ms, 2 * d))] + side_in,
        out_specs=[pl.BlockSpec((tm, tn), lambda j, i: (i, j)),
                   pl.BlockSpec((ms, tn), lambda j, i: (0, j))] + u_spec + side_out,
        scratch_shapes=[pltpu.VMEM((d, tn), BF16)],
        compiler_params=_params(("arbitrary", "arbitrary"), VMEM_LIMIT_BIG_TILE_BYTES_V7X),
        name="inproj_first" if modulate else "inproj_rest",
    )(lhs, ada, w_in, xs, ada, *side_ws)
    n_main = 4 if modulate else 2
    return out[:n_main], out[n_main:]


def _inproj_wide_body(u_ref, us_ref, w_hbm, *rest, tm, tn, col0, n_side, side_steps):
    side_in = rest[:n_side]
    o_ref, os_ref = rest[n_side:n_side + 2]
    side_out = rest[n_side + 2:2 * n_side + 2]
    wf_ref, wb_ref, sem = rest[2 * n_side + 2:]
    j = pl.program_id(0)
    i = pl.program_id(1)

    def w_copy(jj):
        return pltpu.make_async_copy(w_hbm.at[:, pl.ds(col0 + jj * tn, tn)], wf_ref, sem.at[0])

    @pl.when(jnp.logical_and(j == 0, i == 0))
    def _():
        w_copy(0).start()

    @pl.when(j * pl.num_programs(1) + i < side_steps)
    def _():
        for src, dst in zip(side_in, side_out):
            dst[...] = _pack_rows(src[...].astype(BF16))

    @pl.when(i == 0)
    def _():
        w_copy(j).wait()
        wb_ref[...] = wf_ref[...].astype(BF16)
        os_ref[...] = jnp.dot(us_ref[...], wb_ref[...], preferred_element_type=F32)

    @pl.when(jnp.logical_and(i == 1, j + 1 < pl.num_programs(0)))
    def _():
        w_copy(j + 1).start()

    for r in range(0, tm, ROW_CHUNK):
        rows = slice(r, r + ROW_CHUNK)
        o_ref[rows, :] = jnp.dot(u_ref[rows, :], wb_ref[...],
                                 preferred_element_type=F32).astype(BF16)


def _inproj_wide(u, us, w_in, side_ws, *, col0, ncols, tm, tn, side_steps=16):
    m, d = u.shape
    ms = us.shape[0]
    assert m % tm == 0 and m // tm >= 2 and tm % ROW_CHUNK == 0 and ncols % tn == 0
    ni = m // tm
    body = functools.partial(_inproj_wide_body, tm=tm, tn=tn, col0=col0,
                             n_side=len(side_ws), side_steps=side_steps)
    side_idx = lambda j, i: (jnp.minimum(j * ni + i, side_steps - 1), 0)
    side_in = [pl.BlockSpec((w.shape[0] // side_steps, w.shape[1]), side_idx) for w in side_ws]
    side_out = [pl.BlockSpec((w.shape[0] // side_steps // 2, w.shape[1]), side_idx)
                for w in side_ws]
    side_shape = [jax.ShapeDtypeStruct((w.shape[0] // 2, w.shape[1]), U32) for w in side_ws]
    out = pl.pallas_call(
        body,
        out_shape=[jax.ShapeDtypeStruct((m, ncols), BF16),
                   jax.ShapeDtypeStruct((ms, ncols), F32)] + side_shape,
        grid=(ncols // tn, ni),
        in_specs=[pl.BlockSpec((tm, d), lambda j, i: (i, 0)),
                  _resident((ms, d)),
                  pl.BlockSpec(memory_space=pl.ANY)] + side_in,
        out_specs=[pl.BlockSpec((tm, tn), lambda j, i: (i, j)),
                   pl.BlockSpec((ms, tn), lambda j, i: (0, j))] + side_out,
        scratch_shapes=[pltpu.VMEM((d, tn), F32),
                        pltpu.VMEM((d, tn), BF16),
                        pltpu.SemaphoreType.DMA((1,))],
        compiler_params=_params(("arbitrary", "arbitrary"), MLP_VMEM_LIMIT_BYTES_V7X),
        name="inproj_rest",
    )(u, us, w_in, *side_ws)
    return out[:2], out[2:]


def _sigmoid(x):
    return 0.5 * jnp.tanh(0.5 * x) + 0.5


def _mix_tail(pooled_bf, conv_bf, gp, gc, x, g1, sc2, sh2, wpu_ref, wcu_ref, wo_ref,
              lng_ref, lnb_ref, alpha):
    y_p = jnp.dot(pooled_bf, _unpack_rows(wpu_ref[...]), preferred_element_type=F32)
    y_c = jnp.dot(conv_bf, _unpack_rows(wcu_ref[...]), preferred_element_type=F32)
    mixed = (_sigmoid(gp) * y_p + _sigmoid(gc) * y_c).astype(BF16)
    mo = jnp.dot(mixed, _unpack_rows(wo_ref[...]), preferred_element_type=F32)
    x1 = _layernorm(alpha * x + g1 * mo, lng_ref[...], lnb_ref[...])
    return x1, (x1 * (1.0 + sc2) + sh2).astype(BF16)


def _split_cols(first_ref, rest_ref):
    p = first_ref.shape[-1]

    def cols(lo, hi):
        return first_ref[:, lo:hi] if hi <= p else rest_ref[:, lo - p:hi - p]
    return cols


def _mixer_body(pa_ref, pb_ref, ha_ref, hb_ref, x_ref, ada_ref, grp_ref, pscale_ref, convw_ref,
                wpu_ref, wcu_ref, wo_ref, lng_ref, lnb_ref, w1_ref, w2_ref,
                x1_ref, u2_ref, pst_ref, cst_ref, w1b_ref, w2b_ref,
                zz_ref, vv_ref, pm_ref, bc_ref, mixed_ref,
                *, tm, tiles_per_seq, d, alpha, steps):
    i = pl.program_id(0)
    p = d // 2
    gw = p // len(POOL_WINDOWS)
    h = HIST_ROWS
    proj = _split_cols(pa_ref, pb_ref)
    hist = _split_cols(ha_ref, hb_ref)

    cb = MIXER_COL_BLOCK
    ncb = d // cb
    tis = i % tiles_per_seq
    first = tis == 0

    def side_cast():
        tf = w1b_ref.shape[2]
        for fi in range(w1b_ref.shape[0]):
            w1b_ref[fi] = _pack_rows(w1_ref[:, fi * tf:(fi + 1) * tf].astype(BF16))
        w2b_ref[...] = _pack_rows(w2_ref[...].astype(BF16))

    def setup():
        zz_ref[0:h, :] = jnp.where(first, 0.0, hist(0, p).astype(F32))
        zz_ref[h:h + tm, :] = proj(0, p).astype(F32)
        vh = hist(3 * p, 4 * p).astype(F32) * hist(p, 2 * p).astype(F32)
        vv_ref[0:h, :] = jnp.where(first, 0.0, vh)
        vv_ref[h:h + tm, :] = proj(3 * p, 4 * p).astype(F32) * proj(p, 2 * p).astype(F32)
        pst_ref[0] = zz_ref[tm:tm + h, :]
        cst_ref[0] = vv_ref[tm:tm + h, :]

    def pool_group(g):
        w = POOL_WINDOWS[g]
        cols = slice(g * gw, (g + 1) * gw)
        pos = tis * tm + lax.broadcasted_iota(jnp.int32, (tm, 1), 0)
        s = zz_ref[:, cols]
        zt = s[h:, :]
        k = 1
        while k < w:
            s = s + pltpu.roll(s, k, axis=0)
            k *= 2
        inv_cnt = 1.0 / jnp.minimum(w, pos + 1).astype(F32)
        pooled = (s[h:, :] * inv_cnt - zt).astype(BF16)
        og = jnp.dot(pooled, _unpack_rows(grp_ref[g * gw // 2:(g + 1) * gw // 2, :]),
                     preferred_element_type=F32)
        pm_ref[:, cols] = (og * pscale_ref[:, cols]).astype(BF16)

    def conv_branch():
        conv = convw_ref[0:1, :] * vv_ref[h - 2:h - 2 + tm, :]
        conv = conv + convw_ref[1:2, :] * vv_ref[h - 1:h - 1 + tm, :]
        conv = conv + convw_ref[2:3, :] * vv_ref[h:h + tm, :]
        bc_ref[...] = (proj(2 * p, 3 * p).astype(F32) * conv).astype(BF16)

    def up_and_gate(k):
        cols = slice(k * cb, (k + 1) * cb)
        y_p = jnp.dot(pm_ref[...], _unpack_rows(wpu_ref[:, cols]), preferred_element_type=F32)
        y_c = jnp.dot(bc_ref[...], _unpack_rows(wcu_ref[:, cols]), preferred_element_type=F32)
        gp = proj(4 * p + k * cb, 4 * p + (k + 1) * cb).astype(F32)
        gc = proj(4 * p + d + k * cb, 4 * p + d + (k + 1) * cb).astype(F32)
        mixed_ref[:, cols] = (_sigmoid(gp) * y_p + _sigmoid(gc) * y_c).astype(BF16)

    def out_block(k):
        return jnp.dot(mixed_ref[...], _unpack_rows(wo_ref[:, k * cb:(k + 1) * cb]),
                       preferred_element_type=F32)

    def finish_rows(mo, r):
        nr = tm // MIXER_FINISH_SLABS
        rows = slice(r * nr, (r + 1) * nr)
        b = (i - 1) // tiles_per_seq
        g1 = ada_ref[pl.ds(b, 1), 2 * d:3 * d]
        sh2 = ada_ref[pl.ds(b, 1), 3 * d:4 * d]
        sc2 = ada_ref[pl.ds(b, 1), 4 * d:5 * d]
        x1 = _layernorm(alpha * x_ref[rows, :] + g1 * mo[rows, :], lng_ref[...], lnb_ref[...])
        x1_ref[rows, :] = x1
        u2_ref[rows, :] = (x1 * (1.0 + sc2) + sh2).astype(BF16)

    def step(has_head, has_tail):
        front = [side_cast, setup] + [functools.partial(pool_group, g)
                                      for g in range(len(POOL_WINDOWS))] + [conv_branch]
        mo_blocks = []
        for k in range(ncb):
            if has_tail:
                mo_blocks.append(out_block(k))
            if has_head and k < len(front):
                front[k]()
        if has_head:
            for piece in front[ncb:]:
                piece()
        mo = jnp.concatenate(mo_blocks, axis=1) if has_tail else None
        fin_at = {(r + 1) * ncb // MIXER_FINISH_SLABS - 1: r for r in range(MIXER_FINISH_SLABS)}
        for k in range(ncb):
            if has_head:
                up_and_gate(k)
            if has_tail and k in fin_at:
                finish_rows(mo, fin_at[k])

    pl.when(i == 0)(functools.partial(step, True, False))
    pl.when(jnp.logical_and(i > 0, i < steps))(functools.partial(step, True, True))
    pl.when(i == steps)(functools.partial(step, False, True))


def _mixer(proj_a, proj_b, xp, ada, grp_bf, pscale, convw, wpu_bf, wcu_bf, wo_bf, lng, lnb,
           w1, w2, *, seq, prompt_row, alpha, tf, tm=256):
    m, d = xp.shape
    p = d // 2
    nseq = m // seq
    h = HIST_ROWS
    assert seq % tm == 0 and m % seq == 0 and tm % h == 0 and d % MIXER_COL_BLOCK == 0
    assert proj_a.shape[1] == p and tm % MIXER_FINISH_SLABS == 0
    steps = m // tm
    dff = w1.shape[1]
    assert d % steps == 0 and dff % steps == 0 and dff % tf == 0
    body = functools.partial(_mixer_body, tm=tm, tiles_per_seq=seq // tm, d=d, alpha=alpha,
                             steps=steps)
    cur = lambda i: jnp.minimum(i, steps - 1)
    prev = lambda i: jnp.maximum(i - 1, 0)
    hist_idx = lambda i: (jnp.maximum(cur(i) * (tm // h) - 1, 0), 0)
    return pl.pallas_call(
        body,
        out_shape=[jax.ShapeDtypeStruct((m, d), F32),
                   jax.ShapeDtypeStruct((m, d), BF16),
                   jax.ShapeDtypeStruct((nseq, h, p), F32),
                   jax.ShapeDtypeStruct((nseq, h, p), F32),
                   jax.ShapeDtypeStruct((dff // tf, d // 2, tf), U32),
                   jax.ShapeDtypeStruct((dff // 2, d), U32)],
        grid=(steps + 1,),
        in_specs=[pl.BlockSpec((tm, proj_a.shape[1]), lambda i: (cur(i), 0)),
                  pl.BlockSpec((tm, proj_b.shape[1]), lambda i: (cur(i), 0)),
                  pl.BlockSpec((h, proj_a.shape[1]), hist_idx),
                  pl.BlockSpec((h, proj_b.shape[1]), hist_idx),
                  pl.BlockSpec((tm, d), lambda i: (prev(i), 0)),
                  pl.BlockSpec((8, N_ADA * d), lambda i: (prompt_row // 8, 0)),
                  _resident(grp_bf.shape),
                  _resident(pscale.shape),
                  _resident(convw.shape),
                  _resident(wpu_bf.shape),
                  _resident(wcu_bf.shape),
                  _resident(wo_bf.shape),
                  _resident(lng.shape),
                  _resident(lnb.shape),
                  pl.BlockSpec((d // steps, dff), lambda i: (cur(i), 0)),
                  pl.BlockSpec((dff // steps, d), lambda i: (cur(i), 0))],
        out_specs=[pl.BlockSpec((tm, d), lambda i: (prev(i), 0)),
                   pl.BlockSpec((tm, d), lambda i: (prev(i), 0)),
                   pl.BlockSpec((1, h, p), lambda i: (cur(i) // (seq // tm), 0, 0)),
                   pl.BlockSpec((1, h, p), lambda i: (cur(i) // (seq // tm), 0, 0)),
                   pl.BlockSpec((dff // tf, d // steps // 2, tf), lambda i: (0, cur(i), 0)),
                   pl.BlockSpec((dff // steps // 2, d), lambda i: (cur(i), 0))],
        scratch_shapes=[pltpu.VMEM((tm + h, p), F32),
                        pltpu.VMEM((tm + h, p), F32),
                        pltpu.VMEM((tm, p), BF16),
                        pltpu.VMEM((tm, p), BF16),
                        pltpu.VMEM((tm, d), BF16)],
        compiler_params=_params(("arbitrary",)),
        name="mixer",
    )(proj_a, proj_b, proj_a, proj_b, xp, ada, grp_bf, pscale, convw, wpu_bf, wcu_bf, wo_bf,
      lng, lnb, w1, w2)


def _mixer_step_body(pa_ref, pb_ref, pool_ref, cstate_ref, x_ref, ada_ref, grp_ref, pscale_ref,
                     convw_ref, wpu_ref, wcu_ref, wo_ref, lng_ref, lnb_ref,
                     x1_ref, u2_ref, npool_ref, nconv_ref, pm_ref, *, d, alpha):
    p = d // 2
    gw = p // len(POOL_WINDOWS)
    proj = _split_cols(pa_ref, pb_ref)
    z = proj(0, p)
    for g, w in enumerate(POOL_WINDOWS):
        cols = slice(g * gw, (g + 1) * gw)
        s = z[:, cols]
        for k in range(1, w):
            s = s + pool_ref[POOL_HIST - k, :, cols]
        inv_cnt = 1.0 / min(w, PAST_LEN + 1)
        pooled = (s * inv_cnt - z[:, cols]).astype(BF16)
        og = jnp.dot(pooled, _unpack_rows(grp_ref[g * gw // 2:(g + 1) * gw // 2, :]),
                         preferred_element_type=F32)
        pm_ref[:, cols] = (og * pscale_ref[:, cols]).astype(BF16)
    for r in range(POOL_HIST - 1):
        npool_ref[r] = pool_ref[r + 1]
    npool_ref[POOL_HIST - 1] = z

    xc = proj(p, 2 * p)
    bc = proj(2 * p, 3 * p)
    cc = proj(3 * p, 4 * p)
    v = cc * xc
    conv = convw_ref[0:1, :] * cstate_ref[0]
    conv = conv + convw_ref[1:2, :] * cstate_ref[1]
    conv = conv + convw_ref[2:3, :] * v
    nconv_ref[0] = cstate_ref[1]
    nconv_ref[1] = v

    gp = proj(4 * p, 4 * p + d)
    gc = proj(4 * p + d, 4 * p + 2 * d)
    g1 = ada_ref[:, 2 * d:3 * d]
    sh2 = ada_ref[:, 3 * d:4 * d]
    sc2 = ada_ref[:, 4 * d:5 * d]
    x1, u2 = _mix_tail(pm_ref[...], (bc * conv).astype(BF16), gp, gc, x_ref[...], g1, sc2, sh2,
                       wpu_ref, wcu_ref, wo_ref, lng_ref, lnb_ref, alpha)
    x1_ref[...] = x1
    u2_ref[...] = u2


def _mixer_step(proj_a, proj_b, pool_state, conv_state, xs, ada, grp_bf, pscale, convw,
                wpu_bf, wcu_bf, wo_bf, lng, lnb, *, alpha):
    ms, d = xs.shape
    p = d // 2
    body = functools.partial(_mixer_step_body, d=d, alpha=alpha)
    return pl.pallas_call(
        body,
        out_shape=[jax.ShapeDtypeStruct((ms, d), F32),
                   jax.ShapeDtypeStruct((ms, d), BF16),
                   jax.ShapeDtypeStruct(pool_state.shape, F32),
                   jax.ShapeDtypeStruct(conv_state.shape, F32)],
        grid=(1,),
        in_specs=[_resident(proj_a.shape),
                  _resident(proj_b.shape),
                  _resident(pool_state.shape),
                  _resident(conv_state.shape),
                  _resident((ms, d)),
                  _resident((ms, N_ADA * d)),
                  _resident(grp_bf.shape),
                  _resident(pscale.shape),
                  _resident(convw.shape),
                  _resident(wpu_bf.shape),
                  _resident(wcu_bf.shape),
                  _resident(wo_bf.shape),
                  _resident(lng.shape),
                  _resident(lnb.shape)],
        out_specs=[_resident((ms, d)),
                   _resident((ms, d)),
                   _resident(pool_state.shape),
                   _resident(conv_state.shape)],
        scratch_shapes=[pltpu.VMEM((ms, p), BF16)],
        compiler_params=_params(("arbitrary",)),
        name="mixer_step",
    )(proj_a, proj_b, pool_state, conv_state, xs, ada, grp_bf, pscale, convw, wpu_bf, wcu_bf,
      wo_bf, lng, lnb)


def _mlp_body(u2_ref, w1_hbm, b1_ref, w2_hbm, x1_ref, g2_ref, b2_ref, lng_ref, lnb_ref,
              u2s_ref, x1s_ref, g2s_ref, o_ref, os_ref,
              acc_ref, accs_ref, w1_buf, w2_buf, h_buf, hs_buf, sem,
              *, nm, nf, te, tm, alpha, rows_per_cond):
    m = pl.program_id(0)
    f = pl.program_id(1)
    slot = m % 2
    t = m * nf + f
    w2_rows = w2_buf.shape[0]
    n_chunks = tm // MLP_ROW_CHUNK
    tp = te // n_chunks

    def w1_copy(fi):
        return pltpu.make_async_copy(w1_hbm.at[fi], w1_buf, sem.at[0])

    def w2_copy(fi):
        return pltpu.make_async_copy(w2_hbm.at[pl.ds(fi * w2_rows, w2_rows), :], w2_buf,
                                     sem.at[1])

    has_main = m < nm

    @pl.when(t == 0)
    def _():
        w1_copy(0).start()

    @pl.when(has_main)
    def _():
        w2_copy(f).start()
        w1_copy(f).wait()

    def hidden(u2):
        h = jnp.dot(u2, _unpack_rows(w1_buf[...]), preferred_element_type=F32) + b1_ref[f]
        return jnp.square(jnp.maximum(h, 0.0)).astype(BF16)

    def finish(x1, g2, acc):
        return _layernorm(alpha * x1 + g2 * (acc + b2_ref[...]), lng_ref[...], lnb_ref[...])

    def epilogue(c):
        part_rows = pl.ds(pl.multiple_of(c * tp, tp), tp)
        base = pl.multiple_of(f * te + c * tp, tp)
        g2 = g2_ref[pl.ds(((m - 1) * tm) // rows_per_cond, 1), :]
        o_ref[part_rows, :] = finish(x1_ref[part_rows, :], g2,
                                     acc_ref[1 - slot, pl.ds(base, tp), :])

    def main(first, singles, with_epilogue):
        def phase_a(c, carry):
            if with_epilogue:
                epilogue(c)
            rows = pl.ds(pl.multiple_of(c * MLP_ROW_CHUNK, MLP_ROW_CHUNK), MLP_ROW_CHUNK)
            h_buf[rows, :] = hidden(u2_ref[rows, :])
            return carry

        lax.fori_loop(0, n_chunks, phase_a, 0)
        if singles:
            hs_buf[...] = hidden(u2s_ref[...])

        @pl.when(t + 1 < nm * nf)
        def _():
            w1_copy(jnp.where(f + 1 == nf, 0, f + 1)).start()

        w2_copy(f).wait()

        def phase_b(c, carry):
            rows = pl.ds(pl.multiple_of(c * MLP_ROW_CHUNK, MLP_ROW_CHUNK), MLP_ROW_CHUNK)
            part = jnp.dot(h_buf[rows, :], _unpack_rows(w2_buf[...]),
                           preferred_element_type=F32)
            if first:
                acc_ref[slot, rows, :] = part
            else:
                acc_ref[slot, rows, :] += part
            return carry

        lax.fori_loop(0, n_chunks, phase_b, 0)
        if singles:
            part = jnp.dot(hs_buf[...], _unpack_rows(w2_buf[...]), preferred_element_type=F32)
            if first:
                accs_ref[...] = part
            else:
                accs_ref[...] += part

    tile0 = m == 0
    later = jnp.logical_and(has_main, m > 0)
    pl.when(jnp.logical_and(tile0, f == 0))(functools.partial(main, True, True, False))
    pl.when(jnp.logical_and(tile0, f > 0))(functools.partial(main, False, True, False))
    pl.when(jnp.logical_and(later, f == 0))(functools.partial(main, True, False, True))
    pl.when(jnp.logical_and(later, f > 0))(functools.partial(main, False, False, True))

    @pl.when(m == nm)
    def _():
        lax.fori_loop(0, n_chunks, lambda c, carry: (epilogue(c), carry)[1], 0)

    @pl.when(jnp.logical_and(m == 1, f == 0))
    def _():
        os_ref[...] = finish(x1s_ref[...], g2s_ref[...], accs_ref[...])


def _mlp(u2, x1, u2s, x1s, ada, w1b, b1, w2b, b2, lng, lnb, *, alpha, rows_per_cond,
         cond_row, tm):
    m, d = u2.shape
    ms = u2s.shape[0]
    nf, _, tf = w1b.shape
    nm = m // tm
    te = tm // nf
    assert m % tm == 0 and tm % nf == 0 and tm % MLP_ROW_CHUNK == 0
    assert rows_per_cond % tm == 0 and te % (tm // MLP_ROW_CHUNK) == 0 and cond_row % 8 == 0
    g2_col = N_ADA - 1
    body = functools.partial(_mlp_body, nm=nm, nf=nf, te=te, tm=tm, alpha=alpha,
                             rows_per_cond=rows_per_cond)
    ep = lambda mi, f: (jnp.where(mi == 0, 0, (mi - 1) * nf + f), 0)
    return pl.pallas_call(
        body,
        out_shape=[jax.ShapeDtypeStruct((m, d), F32),
                   jax.ShapeDtypeStruct((ms, d), F32)],
        grid=(nm + 1, nf),
        in_specs=[pl.BlockSpec((tm, d), lambda mi, f: (jnp.minimum(mi, nm - 1), 0)),
                  pl.BlockSpec(memory_space=pl.ANY),
                  _resident((nf, 1, tf)),
                  pl.BlockSpec(memory_space=pl.ANY),
                  pl.BlockSpec((te, d), ep),
                  pl.BlockSpec((8, d), lambda mi, f: (cond_row // 8, g2_col)),
                  _resident((1, d)),
                  _resident((1, d)),
                  _resident((1, d)),
                  _resident((ms, d)),
                  _resident((ms, d)),
                  pl.BlockSpec((ms, d), lambda mi, f: (0, g2_col),
                               pipeline_mode=pl.Buffered(1))],
        out_specs=[pl.BlockSpec((te, d), ep),
                   _resident((ms, d))],
        scratch_shapes=[pltpu.VMEM((2, tm, d), F32),
                        pltpu.VMEM((ms, d), F32),
                        pltpu.VMEM((d // 2, tf), U32),
                        pltpu.VMEM((tf // 2, d), U32),
                        pltpu.VMEM((tm, tf), BF16),
                        pltpu.VMEM((ms, tf), BF16),
                        pltpu.SemaphoreType.DMA((2,))],
        compiler_params=_params(("arbitrary", "arbitrary"), MLP_VMEM_LIMIT_BYTES_V7X),
        name="mlp",
    )(u2, w1b, b1.reshape(nf, 1, tf), w2b, x1, ada, b2.reshape(1, d), lng, lnb, u2s, x1s, ada)


def kernel(x_prompt, x_sample, state_pool, state_conv, c_prompt, c_sample, w_ada, b_ada, w_in,
           pool_grp_w, pool_scale, conv_w, w_pool_up, w_conv_up, w_o, ln1_g, ln1_b, w_ff1,
           b_ff1, w_ff2, b_ff2, ln2_g, ln2_b):
    depth = w_ada.shape[0]
    nb, seq, d = x_prompt.shape
    ms, dec_seq, _ = x_sample.shape
    assert dec_seq == 1
    p = d // 2
    alpha = (2 * depth) ** 0.25

    prompt_row = ms
    pad = (-(ms + nb)) % 8
    c_all = jnp.concatenate([c_sample, c_prompt, jnp.zeros((pad, d), F32)], axis=0)

    xp = x_prompt.reshape(nb * seq, d)
    xs = x_sample.reshape(ms, d)
    pool_p, conv_p, pool_s, conv_s = [], [], [], []
    for l in range(depth):
        ada = _ada(c_all, w_ada[l], b_ada[l])
        pscale = pool_scale[l].reshape(1, p)
        lng1, lnb1 = ln1_g[l].reshape(1, d), ln1_b[l].reshape(1, d)
        lng2, lnb2 = ln2_g[l].reshape(1, d), ln2_b[l].reshape(1, d)

        n_in = w_in.shape[2]
        (pa_p, pa_s, u_bf, us_bf), _ = _inproj(xp, xs, ada, w_in[l], [], col0=0, ncols=p,
                                               modulate=True, seq=seq, prompt_row=prompt_row)
        (pb_p, pb_s), (grp_bf, wpu_bf, wcu_bf, wo_bf) = _inproj_wide(
            u_bf, us_bf, w_in[l],
            [pool_grp_w[l].reshape(p, -1), w_pool_up[l], w_conv_up[l], w_o[l]],
            col0=p, ncols=n_in - p, tm=INPROJ_BF16_TOKEN_TILE, tn=INPROJ_WIDE_COLS)

        x1p, u2p, pst, cst, w1b, w2b = _mixer(
            pa_p, pb_p, xp, ada, grp_bf, pscale, conv_w[l], wpu_bf, wcu_bf, wo_bf, lng1, lnb1,
            w_ff1[l], w_ff2[l], seq=seq, prompt_row=prompt_row, alpha=alpha, tf=MLP_FF_CHUNK)
        x1s, u2s, npool, nconv = _mixer_step(
            pa_s, pb_s, jnp.transpose(state_pool[l], (1, 0, 2)),
            jnp.transpose(state_conv[l], (1, 0, 2)), xs, ada, grp_bf, pscale, conv_w[l],
            wpu_bf, wcu_bf, wo_bf, lng1, lnb1, alpha=alpha)

        xp, xs = _mlp(u2p, x1p, u2s, x1s, ada, w1b, b_ff1[l], w2b, b_ff2[l], lng2, lnb2,
                      alpha=alpha, rows_per_cond=seq, cond_row=prompt_row,
                      tm=MLP_TOKEN_TILE)

        pool_p.append(pst[:, HIST_ROWS - POOL_HIST:, :])
        conv_p.append(cst[:, HIST_ROWS - CONV_HIST:, :])
        pool_s.append(jnp.transpose(npool, (1, 0, 2)))
        conv_s.append(jnp.transpose(nconv, (1, 0, 2)))

    return (xp.reshape(nb, seq, d), xs.reshape(ms, dec_seq, d),
            jnp.stack(pool_p), jnp.stack(conv_p), jnp.stack(pool_s), jnp.stack(conv_s))
```

```python
import functools

import jax
import jax.numpy as jnp
from jax import lax
from jax.experimental import pallas as pl
from jax.experimental.pallas import tpu as pltpu

F32 = jnp.float32
BF16 = jnp.bfloat16
U32 = jnp.uint32

POOL_WINDOWS = (2, 4, 8, 16)
POOL_HIST = max(POOL_WINDOWS) - 1
CONV_K = 3
CONV_HIST = CONV_K - 1
N_ADA = 6
PAST_LEN = 16384
LN_EPS = 1e-5

VMEM_LIMIT_BYTES_V7X = 56 * 1024 * 1024
VMEM_LIMIT_BIG_TILE_BYTES_V7X = 60 * 1024 * 1024
MLP_VMEM_LIMIT_BYTES_V7X = 62 * 1024 * 1024
HIST_ROWS = 16
ROW_CHUNK = 256
MIXER_COL_BLOCK = 256
MIXER_FINISH_SLABS = 4
INPROJ_BF16_TOKEN_TILE = 2048
INPROJ_WIDE_COLS = 1792
MLP_TOKEN_TILE = 1024
MLP_FF_CHUNK = 2048
MLP_ROW_CHUNK = 512


def _resident(shape):
    zeros = (0,) * len(shape)
    return pl.BlockSpec(shape, lambda *_: zeros, pipeline_mode=pl.Buffered(1))


def _params(semantics, vmem_limit_bytes=VMEM_LIMIT_BYTES_V7X):
    return pltpu.CompilerParams(dimension_semantics=semantics,
                                vmem_limit_bytes=vmem_limit_bytes)


def _layernorm(xf, g, b):
    mu = jnp.mean(xf, axis=-1, keepdims=True)
    var = jnp.mean(jnp.square(xf - mu), axis=-1, keepdims=True)
    return (xf - mu) * lax.rsqrt(var + LN_EPS) * g + b


def _ada_body(c_ref, w_ref, b_ref, o_ref):
    o_ref[...] = jnp.dot(c_ref[...].astype(BF16), w_ref[...].astype(BF16),
                         preferred_element_type=F32) + b_ref[...]


def _ada(c_all, w_ada, b_ada, tn=1024):
    m, d = c_all.shape
    n = w_ada.shape[1]
    return pl.pallas_call(
        _ada_body,
        out_shape=jax.ShapeDtypeStruct((m, n), F32),
        grid=(n // tn,),
        in_specs=[pl.BlockSpec((m, d), lambda j: (0, 0)),
                  pl.BlockSpec((d, tn), lambda j: (0, j)),
                  pl.BlockSpec((1, tn), lambda j: (0, j))],
        out_specs=pl.BlockSpec((m, tn), lambda j: (0, j)),
        compiler_params=pltpu.CompilerParams(
            dimension_semantics=("arbitrary",), vmem_limit_bytes=VMEM_LIMIT_BYTES_V7X,
            allow_input_fusion=[True, False, False]),
        name="ada",
    )(c_all, w_ada, b_ada.reshape(1, n))


def _pack_rows(x_bf16):
    return pltpu.bitcast(x_bf16, U32)


def _unpack_rows(x_u32):
    return pltpu.bitcast(x_u32, BF16)


def _inproj_body(lhs_ref, adap_ref, w_ref, xs_ref, adas_ref, *rest, tiles_per_seq, d, tm,
                 n_side, side_steps, modulate):
    side_in = rest[:n_side]
    outs = rest[n_side:]
    o_ref, os_ref = outs[:2]
    n_main = 4 if modulate else 2
    u_ref, usb_ref = outs[2:4] if modulate else (None, None)
    side_out = outs[n_main:n_main + n_side]
    wb_ref = outs[n_main + n_side]
    i = pl.program_id(1)

    if n_side:
        @pl.when(pl.program_id(0) * pl.num_programs(1) + i < side_steps)
        def _():
            for src, dst in zip(side_in, side_out):
                dst[...] = _pack_rows(src[...].astype(BF16))

    @pl.when(i == 0)
    def _():
        wb_ref[...] = w_ref[...].astype(BF16)
        us = (xs_ref[...] * (1.0 + adas_ref[:, d:2 * d]) + adas_ref[:, 0:d]).astype(BF16)
        os_ref[...] = jnp.dot(us, wb_ref[...], preferred_element_type=F32)
        if modulate:
            usb_ref[...] = us

    b = i // tiles_per_seq
    sh1 = adap_ref[pl.ds(b, 1), 0:d]
    sc1 = adap_ref[pl.ds(b, 1), d:2 * d]
    for r in range(0, tm, ROW_CHUNK):
        rows = slice(r, r + ROW_CHUNK)
        if modulate:
            u = (lhs_ref[rows, :] * (1.0 + sc1) + sh1).astype(BF16)
            u_ref[rows, :] = u
        else:
            u = lhs_ref[rows, :]
        o_ref[rows, :] = jnp.dot(u, wb_ref[...], preferred_element_type=F32).astype(BF16)


def _inproj(lhs, xs, ada, w_in, side_ws, *, col0, ncols, modulate, seq, prompt_row,
            tm=1024, tn=1024, side_steps=16):
    m, d = lhs.shape
    ms = xs.shape[0]
    assert seq % tm == 0 and m % seq == 0 and tm % ROW_CHUNK == 0
    assert col0 % tn == 0 and ncols % tn == 0 and prompt_row % 8 == 0
    ni = m // tm
    jb = col0 // tn
    body = functools.partial(_inproj_body, tiles_per_seq=seq // tm, d=d, tm=tm,
                             n_side=len(side_ws), side_steps=side_steps, modulate=modulate)
    side_idx = lambda j, i: (jnp.minimum(j * ni + i, side_steps - 1), 0)
    side_in = [pl.BlockSpec((w.shape[0] // side_steps, w.shape[1]), side_idx) for w in side_ws]
    side_out = [pl.BlockSpec((w.shape[0] // side_steps // 2, w.shape[1]), side_idx)
                for w in side_ws]
    side_shape = [jax.ShapeDtypeStruct((w.shape[0] // 2, w.shape[1]), U32) for w in side_ws]
    u_shape = ([jax.ShapeDtypeStruct((m, d), BF16), jax.ShapeDtypeStruct((ms, d), BF16)]
               if modulate else [])
    u_spec = ([pl.BlockSpec((tm, d), lambda j, i: (i, 0)), _resident((ms, d))]
              if modulate else [])
    out = pl.pallas_call(
        body,
        out_shape=[jax.ShapeDtypeStruct((m, ncols), BF16),
                   jax.ShapeDtypeStruct((ms, ncols), F32)] + u_shape + side_shape,
        grid=(ncols // tn, ni),
        in_specs=[pl.BlockSpec((tm, d), lambda j, i: (i, 0)),
                  pl.BlockSpec((8, 2 * d), lambda j, i: (prompt_row // 8, 0)),
                  pl.BlockSpec((d, tn), lambda j, i: (0, j + jb)),
                  _resident((ms, d)),
                  _resident((ms, 2 * d))] + side_in,
        out_specs=[pl.BlockSpec((tm, tn), lambda j, i: (i, j)),
                   pl.BlockSpec((ms, tn), lambda j, i: (0, j))] + u_spec + side_out,
        scratch_shapes=[pltpu.VMEM((d, tn), BF16)],
        compiler_params=_params(("arbitrary", "arbitrary"), VMEM_LIMIT_BIG_TILE_BYTES_V7X),
        name="inproj_first" if modulate else "inproj_rest",
    )(lhs, ada, w_in, xs, ada, *side_ws)
    n_main = 4 if modulate else 2
    return out[:n_main], out[n_main:]


def _inproj_wide_body(u_ref, us_ref, w_hbm, *rest, tm, tn, col0, n_side, side_steps):
    side_in = rest[:n_side]
    o_ref, os_ref = rest[n_side:n_side + 2]
    side_out = rest[n_side + 2:2 * n_side + 2]
    wf_ref, wb_ref, sem = rest[2 * n_side + 2:]
    j = pl.program_id(0)
    i = pl.program_id(1)

    def w_copy(jj):
        return pltpu.make_async_copy(w_hbm.at[:, pl.ds(col0 + jj * tn, tn)], wf_ref, sem.at[0])

    @pl.when(jnp.logical_and(j == 0, i == 0))
    def _():
        w_copy(0).start()

    @pl.when(j * pl.num_programs(1) + i < side_steps)
    def _():
        for src, dst in zip(side_in, side_out):
            dst[...] = _pack_rows(src[...].astype(BF16))

    @pl.when(i == 0)
    def _():
        w_copy(j).wait()
        wb_ref[...] = wf_ref[...].astype(BF16)
        os_ref[...] = jnp.dot(us_ref[...], wb_ref[...], preferred_element_type=F32)

    @pl.when(jnp.logical_and(i == 1, j + 1 < pl.num_programs(0)))
    def _():
        w_copy(j + 1).start()

    def chunk(c, carry):
        rows = pl.ds(pl.multiple_of(c * MLP_ROW_CHUNK, MLP_ROW_CHUNK), MLP_ROW_CHUNK)
        o_ref[rows, :] = jnp.dot(u_ref[rows, :], wb_ref[...],
                                 preferred_element_type=F32).astype(BF16)
        return carry

    lax.fori_loop(0, tm // MLP_ROW_CHUNK, chunk, 0)


def _inproj_wide(u, us, w_in, side_ws, *, col0, ncols, tm, tn, side_steps=16):
    m, d = u.shape
    ms = us.shape[0]
    assert m % tm == 0 and m // tm >= 2 and tm % MLP_ROW_CHUNK == 0 and ncols % tn == 0
    ni = m // tm
    body = functools.partial(_inproj_wide_body, tm=tm, tn=tn, col0=col0,
                             n_side=len(side_ws), side_steps=side_steps)
    side_idx = lambda j, i: (jnp.minimum(j * ni + i, side_steps - 1), 0)
    side_in = [pl.BlockSpec((w.shape[0] // side_steps, w.shape[1]), side_idx) for w in side_ws]
    side_out = [pl.BlockSpec((w.shape[0] // side_steps // 2, w.shape[1]), side_idx)
                for w in side_ws]
    side_shape = [jax.ShapeDtypeStruct((w.shape[0] // 2, w.shape[1]), U32) for w in side_ws]
    out = pl.pallas_call(
        body,
        out_shape=[jax.ShapeDtypeStruct((m, ncols), BF16),
                   jax.ShapeDtypeStruct((ms, ncols), F32)] + side_shape,
        grid=(ncols // tn, ni),
        in_specs=[pl.BlockSpec((tm, d), lambda j, i: (i, 0)),
                  _resident((ms, d)),
                  pl.BlockSpec(memory_space=pl.ANY)] + side_in,
        out_specs=[pl.BlockSpec((tm, tn), lambda j, i: (i, j)),
                   pl.BlockSpec((ms, tn), lambda j, i: (0, j))] + side_out,
        scratch_shapes=[pltpu.VMEM((d, tn), F32),
                        pltpu.VMEM((d, tn), BF16),
                        pltpu.SemaphoreType.DMA((1,))],
        compiler_params=_params(("arbitrary", "arbitrary"), MLP_VMEM_LIMIT_BYTES_V7X),
        name="inproj_rest",
    )(u, us, w_in, *side_ws)
    return out[:2], out[2:]


def _sigmoid(x):
    return 0.5 * jnp.tanh(0.5 * x) + 0.5


def _mix_tail(pooled_bf, conv_bf, gp, gc, x, g1, sc2, sh2, wpu_ref, wcu_ref, wo_ref,
              lng_ref, lnb_ref, alpha):
    y_p = jnp.dot(pooled_bf, _unpack_rows(wpu_ref[...]), preferred_element_type=F32)
    y_c = jnp.dot(conv_bf, _unpack_rows(wcu_ref[...]), preferred_element_type=F32)
    mixed = (_sigmoid(gp) * y_p + _sigmoid(gc) * y_c).astype(BF16)
    mo = jnp.dot(mixed, _unpack_rows(wo_ref[...]), preferred_element_type=F32)
    x1 = _layernorm(alpha * x + g1 * mo, lng_ref[...], lnb_ref[...])
    return x1, (x1 * (1.0 + sc2) + sh2).astype(BF16)


def _split_cols(first_ref, rest_ref):
    p = first_ref.shape[-1]

    def cols(lo, hi):
        return first_ref[:, lo:hi] if hi <= p else rest_ref[:, lo - p:hi - p]
    return cols


def _mixer_body(pa_ref, pb_ref, ha_ref, hb_ref, x_ref, ada_ref, grp_ref, pscale_ref, convw_ref,
                wpu_ref, wcu_ref, wo_ref, lng_ref, lnb_ref, w1_ref, w2_ref,
                x1_ref, u2_ref, pst_ref, cst_ref, w1b_ref, w2b_ref,
                zz_ref, vv_ref, pm_ref, bc_ref, mixed_ref,
                *, tm, tiles_per_seq, d, alpha, steps):
    i = pl.program_id(0)
    p = d // 2
    gw = p // len(POOL_WINDOWS)
    h = HIST_ROWS
    proj = _split_cols(pa_ref, pb_ref)
    hist = _split_cols(ha_ref, hb_ref)

    cb = MIXER_COL_BLOCK
    ncb = d // cb
    tis = i % tiles_per_seq
    first = tis == 0

    def side_cast():
        tf = w1b_ref.shape[2]
        for fi in range(w1b_ref.shape[0]):
            w1b_ref[fi] = _pack_rows(w1_ref[:, fi * tf:(fi + 1) * tf].astype(BF16))
        w2b_ref[...] = _pack_rows(w2_ref[...].astype(BF16))

    def setup():
        zz_ref[0:h, :] = jnp.where(first, 0.0, hist(0, p).astype(F32))
        zz_ref[h:h + tm, :] = proj(0, p).astype(F32)
        vh = hist(3 * p, 4 * p).astype(F32) * hist(p, 2 * p).astype(F32)
        vv_ref[0:h, :] = jnp.where(first, 0.0, vh)
        vv_ref[h:h + tm, :] = proj(3 * p, 4 * p).astype(F32) * proj(p, 2 * p).astype(F32)
        pst_ref[0] = zz_ref[tm:tm + h, :]
        cst_ref[0] = vv_ref[tm:tm + h, :]

    def pool_group(g):
        w = POOL_WINDOWS[g]
        cols = slice(g * gw, (g + 1) * gw)
        pos = tis * tm + lax.broadcasted_iota(jnp.int32, (tm, 1), 0)
        s = zz_ref[:, cols]
        zt = s[h:, :]
        k = 1
        while k < w:
            s = s + pltpu.roll(s, k, axis=0)
            k *= 2
        inv_cnt = 1.0 / jnp.minimum(w, pos + 1).astype(F32)
        pooled = (s[h:, :] * inv_cnt - zt).astype(BF16)
        og = jnp.dot(pooled, _unpack_rows(grp_ref[g * gw // 2:(g + 1) * gw // 2, :]),
                     preferred_element_type=F32)
        pm_ref[:, cols] = (og * pscale_ref[:, cols]).astype(BF16)

    def conv_branch():
        conv = convw_ref[0:1, :] * vv_ref[h - 2:h - 2 + tm, :]
        conv = conv + convw_ref[1:2, :] * vv_ref[h - 1:h - 1 + tm, :]
        conv = conv + convw_ref[2:3, :] * vv_ref[h:h + tm, :]
        bc_ref[...] = (proj(2 * p, 3 * p).astype(F32) * conv).astype(BF16)

    def up_and_gate(k):
        cols = slice(k * cb, (k + 1) * cb)
        y_p = jnp.dot(pm_ref[...], _unpack_rows(wpu_ref[:, cols]), preferred_element_type=F32)
        y_c = jnp.dot(bc_ref[...], _unpack_rows(wcu_ref[:, cols]), preferred_element_type=F32)
        gp = proj(4 * p + k * cb, 4 * p + (k + 1) * cb).astype(F32)
        gc = proj(4 * p + d + k * cb, 4 * p + d + (k + 1) * cb).astype(F32)
        mixed_ref[:, cols] = (_sigmoid(gp) * y_p + _sigmoid(gc) * y_c).astype(BF16)

    def out_block(k):
        return jnp.dot(mixed_ref[...], _unpack_rows(wo_ref[:, k * cb:(k + 1) * cb]),
                       preferred_element_type=F32)

    def finish_rows(mo, r):
        nr = tm // MIXER_FINISH_SLABS
        rows = slice(r * nr, (r + 1) * nr)
        b = (i - 1) // tiles_per_seq
        g1 = ada_ref[pl.ds(b, 1), 2 * d:3 * d]
        sh2 = ada_ref[pl.ds(b, 1), 3 * d:4 * d]
        sc2 = ada_ref[pl.ds(b, 1), 4 * d:5 * d]
        x1 = _layernorm(alpha * x_ref[rows, :] + g1 * mo[rows, :], lng_ref[...], lnb_ref[...])
        x1_ref[rows, :] = x1
        u2_ref[rows, :] = (x1 * (1.0 + sc2) + sh2).astype(BF16)

    def step(has_head, has_tail):
        front = [side_cast, setup] + [functools.partial(pool_group, g)
                                      for g in range(len(POOL_WINDOWS))] + [conv_branch]
        mo_blocks = []
        for k in range(ncb):
            if has_tail:
                mo_blocks.append(out_block(k))
            if has_head and k < len(front):
                front[k]()
        if has_head:
            for piece in front[ncb:]:
                piece()
        mo = jnp.concatenate(mo_blocks, axis=1) if has_tail else None
        fin_at = {(r + 1) * ncb // MIXER_FINISH_SLABS - 1: r for r in range(MIXER_FINISH_SLABS)}
        for k in range(ncb):
            if has_head:
                up_and_gate(k)
            if has_tail and k in fin_at:
                finish_rows(mo, fin_at[k])

    pl.when(i == 0)(functools.partial(step, True, False))
    pl.when(jnp.logical_and(i > 0, i < steps))(functools.partial(step, True, True))
    pl.when(i == steps)(functools.partial(step, False, True))


def _mixer(proj_a, proj_b, xp, ada, grp_bf, pscale, convw, wpu_bf, wcu_bf, wo_bf, lng, lnb,
           w1, w2, *, seq, prompt_row, alpha, tf, tm=256):
    m, d = xp.shape
    p = d // 2
    nseq = m // seq
    h = HIST_ROWS
    assert seq % tm == 0 and m % seq == 0 and tm % h == 0 and d % MIXER_COL_BLOCK == 0
    assert proj_a.shape[1] == p and tm % MIXER_FINISH_SLABS == 0
    steps = m // tm
    dff = w1.shape[1]
    assert d % steps == 0 and dff % steps == 0 and dff % tf == 0
    body = functools.partial(_mixer_body, tm=tm, tiles_per_seq=seq // tm, d=d, alpha=alpha,
                             steps=steps)
    cur = lambda i: jnp.minimum(i, steps - 1)
    prev = lambda i: jnp.maximum(i - 1, 0)
    hist_idx = lambda i: (jnp.maximum(cur(i) * (tm // h) - 1, 0), 0)
    return pl.pallas_call(
        body,
        out_shape=[jax.ShapeDtypeStruct((m, d), F32),
                   jax.ShapeDtypeStruct((m, d), BF16),
                   jax.ShapeDtypeStruct((nseq, h, p), F32),
                   jax.ShapeDtypeStruct((nseq, h, p), F32),
                   jax.ShapeDtypeStruct((dff // tf, d // 2, tf), U32),
                   jax.ShapeDtypeStruct((dff // 2, d), U32)],
        grid=(steps + 1,),
        in_specs=[pl.BlockSpec((tm, proj_a.shape[1]), lambda i: (cur(i), 0)),
                  pl.BlockSpec((tm, proj_b.shape[1]), lambda i: (cur(i), 0)),
                  pl.BlockSpec((h, proj_a.shape[1]), hist_idx),
                  pl.BlockSpec((h, proj_b.shape[1]), hist_idx),
                  pl.BlockSpec((tm, d), lambda i: (prev(i), 0)),
                  pl.BlockSpec((8, N_ADA * d), lambda i: (prompt_row // 8, 0)),
                  _resident(grp_bf.shape),
                  _resident(pscale.shape),
                  _resident(convw.shape),
                  _resident(wpu_bf.shape),
                  _resident(wcu_bf.shape),
                  _resident(wo_bf.shape),
                  _resident(lng.shape),
                  _resident(lnb.shape),
                  pl.BlockSpec((d // steps, dff), lambda i: (cur(i), 0)),
                  pl.BlockSpec((dff // steps, d), lambda i: (cur(i), 0))],
        out_specs=[pl.BlockSpec((tm, d), lambda i: (prev(i), 0)),
                   pl.BlockSpec((tm, d), lambda i: (prev(i), 0)),
                   pl.BlockSpec((1, h, p), lambda i: (cur(i) // (seq // tm), 0, 0)),
                   pl.BlockSpec((1, h, p), lambda i: (cur(i) // (seq // tm), 0, 0)),
                   pl.BlockSpec((dff // tf, d // steps // 2, tf), lambda i: (0, cur(i), 0)),
                   pl.BlockSpec((dff // steps // 2, d), lambda i: (cur(i), 0))],
        scratch_shapes=[pltpu.VMEM((tm + h, p), F32),
                        pltpu.VMEM((tm + h, p), F32),
                        pltpu.VMEM((tm, p), BF16),
                        pltpu.VMEM((tm, p), BF16),
                        pltpu.VMEM((tm, d), BF16)],
        compiler_params=_params(("arbitrary",)),
        name="mixer",
    )(proj_a, proj_b, proj_a, proj_b, xp, ada, grp_bf, pscale, convw, wpu_bf, wcu_bf, wo_bf,
      lng, lnb, w1, w2)


def _mixer_step_body(pa_ref, pb_ref, pool_ref, cstate_ref, x_ref, ada_ref, grp_ref, pscale_ref,
                     convw_ref, wpu_ref, wcu_ref, wo_ref, lng_ref, lnb_ref,
                     x1_ref, u2_ref, npool_ref, nconv_ref, pm_ref, *, d, alpha):
    p = d // 2
    gw = p // len(POOL_WINDOWS)
    proj = _split_cols(pa_ref, pb_ref)
    z = proj(0, p)
    for g, w in enumerate(POOL_WINDOWS):
        cols = slice(g * gw, (g + 1) * gw)
        s = z[:, cols]
        for k in range(1, w):
            s = s + pool_ref[POOL_HIST - k, :, cols]
        inv_cnt = 1.0 / min(w, PAST_LEN + 1)
        pooled = (s * inv_cnt - z[:, cols]).astype(BF16)
        og = jnp.dot(pooled, _unpack_rows(grp_ref[g * gw // 2:(g + 1) * gw // 2, :]),
                         preferred_element_type=F32)
        pm_ref[:, cols] = (og * pscale_ref[:, cols]).astype(BF16)
    for r in range(POOL_HIST - 1):
        npool_ref[r] = pool_ref[r + 1]
    npool_ref[POOL_HIST - 1] = z

    xc = proj(p, 2 * p)
    bc = proj(2 * p, 3 * p)
    cc = proj(3 * p, 4 * p)
    v = cc * xc
    conv = convw_ref[0:1, :] * cstate_ref[0]
    conv = conv + convw_ref[1:2, :] * cstate_ref[1]
    conv = conv + convw_ref[2:3, :] * v
    nconv_ref[0] = cstate_ref[1]
    nconv_ref[1] = v

    gp = proj(4 * p, 4 * p + d)
    gc = proj(4 * p + d, 4 * p + 2 * d)
    g1 = ada_ref[:, 2 * d:3 * d]
    sh2 = ada_ref[:, 3 * d:4 * d]
    sc2 = ada_ref[:, 4 * d:5 * d]
    x1, u2 = _mix_tail(pm_ref[...], (bc * conv).astype(BF16), gp, gc, x_ref[...], g1, sc2, sh2,
                       wpu_ref, wcu_ref, wo_ref, lng_ref, lnb_ref, alpha)
    x1_ref[...] = x1
    u2_ref[...] = u2


def _mixer_step(proj_a, proj_b, pool_state, conv_state, xs, ada, grp_bf, pscale, convw,
                wpu_bf, wcu_bf, wo_bf, lng, lnb, *, alpha):
    ms, d = xs.shape
    p = d // 2
    body = functools.partial(_mixer_step_body, d=d, alpha=alpha)
    return pl.pallas_call(
        body,
        out_shape=[jax.ShapeDtypeStruct((ms, d), F32),
                   jax.ShapeDtypeStruct((ms, d), BF16),
                   jax.ShapeDtypeStruct(pool_state.shape, F32),
                   jax.ShapeDtypeStruct(conv_state.shape, F32)],
        grid=(1,),
        in_specs=[_resident(proj_a.shape),
                  _resident(proj_b.shape),
                  _resident(pool_state.shape),
                  _resident(conv_state.shape),
                  _resident((ms, d)),
                  _resident((ms, N_ADA * d)),
                  _resident(grp_bf.shape),
                  _resident(pscale.shape),
                  _resident(convw.shape),
                  _resident(wpu_bf.shape),
                  _resident(wcu_bf.shape),
                  _resident(wo_bf.shape),
                  _resident(lng.shape),
                  _resident(lnb.shape)],
        out_specs=[_resident((ms, d)),
                   _resident((ms, d)),
                   _resident(pool_state.shape),
                   _resident(conv_state.shape)],
        scratch_shapes=[pltpu.VMEM((ms, p), BF16)],
        compiler_params=_params(("arbitrary",)),
        name="mixer_step",
    )(proj_a, proj_b, pool_state, conv_state, xs, ada, grp_bf, pscale, convw, wpu_bf, wcu_bf,
      wo_bf, lng, lnb)


def _mlp_body(u2_ref, w1_hbm, b1_ref, w2_hbm, x1_ref, g2_ref, b2_ref, lng_ref, lnb_ref,
              u2s_ref, x1s_ref, g2s_ref, o_ref, os_ref,
              acc_ref, accs_ref, w1_buf, w2_buf, h_buf, hs_buf, sem,
              *, nm, nf, te, tm, alpha, rows_per_cond):
    m = pl.program_id(0)
    f = pl.program_id(1)
    slot = m % 2
    t = m * nf + f
    w2_rows = w2_buf.shape[0]
    n_chunks = tm // MLP_ROW_CHUNK
    tp = te // n_chunks

    def w1_copy(fi):
        return pltpu.make_async_copy(w1_hbm.at[fi], w1_buf, sem.at[0])

    def w2_copy(fi):
        return pltpu.make_async_copy(w2_hbm.at[pl.ds(fi * w2_rows, w2_rows), :], w2_buf,
                                     sem.at[1])

    has_main = m < nm

    @pl.when(t == 0)
    def _():
        w1_copy(0).start()

    @pl.when(has_main)
    def _():
        w2_copy(f).start()
        w1_copy(f).wait()

    def hidden(u2):
        h = jnp.dot(u2, _unpack_rows(w1_buf[...]), preferred_element_type=F32) + b1_ref[f]
        return jnp.square(jnp.maximum(h, 0.0)).astype(BF16)

    def finish(x1, g2, acc):
        return _layernorm(alpha * x1 + g2 * (acc + b2_ref[...]), lng_ref[...], lnb_ref[...])

    def epilogue(c):
        part_rows = pl.ds(pl.multiple_of(c * tp, tp), tp)
        base = pl.multiple_of(f * te + c * tp, tp)
        g2 = g2_ref[pl.ds(((m - 1) * tm) // rows_per_cond, 1), :]
        o_ref[part_rows, :] = finish(x1_ref[part_rows, :], g2,
                                     acc_ref[1 - slot, pl.ds(base, tp), :])

    def main(first, singles, with_epilogue):
        def phase_a(c, carry):
            if with_epilogue:
                epilogue(c)
            rows = pl.ds(pl.multiple_of(c * MLP_ROW_CHUNK, MLP_ROW_CHUNK), MLP_ROW_CHUNK)
            h_buf[rows, :] = hidden(u2_ref[rows, :])
            return carry

        lax.fori_loop(0, n_chunks, phase_a, 0)
        if singles:
            hs_buf[...] = hidden(u2s_ref[...])

        @pl.when(t + 1 < nm * nf)
        def _():
            w1_copy(jnp.where(f + 1 == nf, 0, f + 1)).start()

        w2_copy(f).wait()

        def phase_b(c, carry):
            rows = pl.ds(pl.multiple_of(c * MLP_ROW_CHUNK, MLP_ROW_CHUNK), MLP_ROW_CHUNK)
            part = jnp.dot(h_buf[rows, :], _unpack_rows(w2_buf[...]),
                           preferred_element_type=F32)
            if first:
                acc_ref[slot, rows, :] = part
            else:
                acc_ref[slot, rows, :] += part
            return carry

        lax.fori_loop(0, n_chunks, phase_b, 0)
        if singles:
            part = jnp.dot(hs_buf[...], _unpack_rows(w2_buf[...]), preferred_element_type=F32)
            if first:
                accs_ref[...] = part
            else:
                accs_ref[...] += part

    tile0 = m == 0
    later = jnp.logical_and(has_main, m > 0)
    pl.when(jnp.logical_and(tile0, f == 0))(functools.partial(main, True, True, False))
    pl.when(jnp.logical_and(tile0, f > 0))(functools.partial(main, False, True, False))
    pl.when(jnp.logical_and(later, f == 0))(functools.partial(main, True, False, True))
    pl.when(jnp.logical_and(later, f > 0))(functools.partial(main, False, False, True))

    @pl.when(m == nm)
    def _():
        lax.fori_loop(0, n_chunks, lambda c, carry: (epilogue(c), carry)[1], 0)

    @pl.when(jnp.logical_and(m == 1, f == 0))
    def _():
        os_ref[...] = finish(x1s_ref[...], g2s_ref[...], accs_ref[...])


def _mlp(u2, x1, u2s, x1s, ada, w1b, b1, w2b, b2, lng, lnb, *, alpha, rows_per_cond,
         cond_row, tm):
    m, d = u2.shape
    ms = u2s.shape[0]
    nf, _, tf = w1b.shape
    nm = m // tm
    te = tm // nf
    assert m % tm == 0 and tm % nf == 0 and tm % MLP_ROW_CHUNK == 0
    assert rows_per_cond % tm == 0 and te % (tm // MLP_ROW_CHUNK) == 0 and cond_row % 8 == 0
    g2_col = N_ADA - 1
    body = functools.partial(_mlp_body, nm=nm, nf=nf, te=te, tm=tm, alpha=alpha,
                             rows_per_cond=rows_per_cond)
    ep = lambda mi, f: (jnp.where(mi == 0, 0, (mi - 1) * nf + f), 0)
    return pl.pallas_call(
        body,
        out_shape=[jax.ShapeDtypeStruct((m, d), F32),
                   jax.ShapeDtypeStruct((ms, d), F32)],
        grid=(nm + 1, nf),
        in_specs=[pl.BlockSpec((tm, d), lambda mi, f: (jnp.minimum(mi, nm - 1), 0)),
                  pl.BlockSpec(memory_space=pl.ANY),
                  _resident((nf, 1, tf)),
                  pl.BlockSpec(memory_space=pl.ANY),
                  pl.BlockSpec((te, d), ep),
                  pl.BlockSpec((8, d), lambda mi, f: (cond_row // 8, g2_col)),
                  _resident((1, d)),
                  _resident((1, d)),
                  _resident((1, d)),
                  _resident((ms, d)),
                  _resident((ms, d)),
                  pl.BlockSpec((ms, d), lambda mi, f: (0, g2_col),
                               pipeline_mode=pl.Buffered(1))],
        out_specs=[pl.BlockSpec((te, d), ep),
                   _resident((ms, d))],
        scratch_shapes=[pltpu.VMEM((2, tm, d), F32),
                        pltpu.VMEM((ms, d), F32),
                        pltpu.VMEM((d // 2, tf), U32),
                        pltpu.VMEM((tf // 2, d), U32),
                        pltpu.VMEM((tm, tf), BF16),
                        pltpu.VMEM((ms, tf), BF16),
                        pltpu.SemaphoreType.DMA((2,))],
        compiler_params=_params(("arbitrary", "arbitrary"), MLP_VMEM_LIMIT_BYTES_V7X),
        name="mlp",
    )(u2, w1b, b1.reshape(nf, 1, tf), w2b, x1, ada, b2.reshape(1, d), lng, lnb, u2s, x1s, ada)


def kernel(x_prompt, x_sample, state_pool, state_conv, c_prompt, c_sample, w_ada, b_ada, w_in,
           pool_grp_w, pool_scale, conv_w, w_pool_up, w_conv_up, w_o, ln1_g, ln1_b, w_ff1,
           b_ff1, w_ff2, b_ff2, ln2_g, ln2_b):
    depth = w_ada.shape[0]
    nb, seq, d = x_prompt.shape
    ms, dec_seq, _ = x_sample.shape
    assert dec_seq == 1
    p = d // 2
    alpha = (2 * depth) ** 0.25

    prompt_row = ms
    pad = (-(ms + nb)) % 8
    c_all = jnp.concatenate([c_sample, c_prompt, jnp.zeros((pad, d), F32)], axis=0)

    xp = x_prompt.reshape(nb * seq, d)
    xs = x_sample.reshape(ms, d)
    pool_p, conv_p, pool_s, conv_s = [], [], [], []
    for l in range(depth):
        ada = _ada(c_all, w_ada[l], b_ada[l])
        pscale = pool_scale[l].reshape(1, p)
        lng1, lnb1 = ln1_g[l].reshape(1, d), ln1_b[l].reshape(1, d)
        lng2, lnb2 = ln2_g[l].reshape(1, d), ln2_b[l].reshape(1, d)

        n_in = w_in.shape[2]
        (pa_p, pa_s, u_bf, us_bf), _ = _inproj(xp, xs, ada, w_in[l], [], col0=0, ncols=p,
                                               modulate=True, seq=seq, prompt_row=prompt_row)
        (pb_p, pb_s), (grp_bf, wpu_bf, wcu_bf, wo_bf) = _inproj_wide(
            u_bf, us_bf, w_in[l],
            [pool_grp_w[l].reshape(p, -1), w_pool_up[l], w_conv_up[l], w_o[l]],
            col0=p, ncols=n_in - p, tm=INPROJ_BF16_TOKEN_TILE, tn=INPROJ_WIDE_COLS)

        x1p, u2p, pst, cst, w1b, w2b = _mixer(
            pa_p, pb_p, xp, ada, grp_bf, pscale, conv_w[l], wpu_bf, wcu_bf, wo_bf, lng1, lnb1,
            w_ff1[l], w_ff2[l], seq=seq, prompt_row=prompt_row, alpha=alpha, tf=MLP_FF_CHUNK)
        x1s, u2s, npool, nconv = _mixer_step(
            pa_s, pb_s, jnp.transpose(state_pool[l], (1, 0, 2)),
            jnp.transpose(state_conv[l], (1, 0, 2)), xs, ada, grp_bf, pscale, conv_w[l],
            wpu_bf, wcu_bf, wo_bf, lng1, lnb1, alpha=alpha)

        xp, xs = _mlp(u2p, x1p, u2s, x1s, ada, w1b, b_ff1[l], w2b, b_ff2[l], lng2, lnb2,
                      alpha=alpha, rows_per_cond=seq, cond_row=prompt_row,
                      tm=MLP_TOKEN_TILE)

        pool_p.append(pst[:, HIST_ROWS - POOL_HIST:, :])
        conv_p.append(cst[:, HIST_ROWS - CONV_HIST:, :])
        pool_s.append(jnp.transpose(npool, (1, 0, 2)))
        conv_s.append(jnp.transpose(nconv, (1, 0, 2)))

    return (xp.reshape(nb, seq, d), xs.reshape(ms, dec_seq, d),
            jnp.stack(pool_p), jnp.stack(conv_p), jnp.stack(pool_s), jnp.stack(conv_s))
```

```python
import functools

import jax
import jax.numpy as jnp
from jax import lax
from jax.experimental import pallas as pl
from jax.experimental.pallas import tpu as pltpu

F32 = jnp.float32
BF16 = jnp.bfloat16
U32 = jnp.uint32

POOL_WINDOWS = (2, 4, 8, 16)
POOL_HIST = max(POOL_WINDOWS) - 1
CONV_K = 3
CONV_HIST = CONV_K - 1
N_ADA = 6
PAST_LEN = 16384
LN_EPS = 1e-5

VMEM_LIMIT_BYTES_V7X = 56 * 1024 * 1024
VMEM_LIMIT_BIG_TILE_BYTES_V7X = 60 * 1024 * 1024
MLP_VMEM_LIMIT_BYTES_V7X = 62 * 1024 * 1024
HIST_ROWS = 16
ROW_CHUNK = 256
MIXER_COL_BLOCK = 256
MIXER_FINISH_SLABS = 4
INPROJ_BF16_TOKEN_TILE = 2048
INPROJ_WIDE_COLS = 1024
SIDE_PIECE_ROWS = 128
MLP_TOKEN_TILE = 1024
MLP_FF_CHUNK = 2048
MLP_ROW_CHUNK = 512


def _resident(shape):
    zeros = (0,) * len(shape)
    return pl.BlockSpec(shape, lambda *_: zeros, pipeline_mode=pl.Buffered(1))


def _params(semantics, vmem_limit_bytes=VMEM_LIMIT_BYTES_V7X):
    return pltpu.CompilerParams(dimension_semantics=semantics,
                                vmem_limit_bytes=vmem_limit_bytes)


def _layernorm(xf, g, b):
    mu = jnp.mean(xf, axis=-1, keepdims=True)
    var = jnp.mean(jnp.square(xf - mu), axis=-1, keepdims=True)
    return (xf - mu) * lax.rsqrt(var + LN_EPS) * g + b


def _ada_body(c_ref, w_ref, b_ref, o_ref):
    o_ref[...] = jnp.dot(c_ref[...].astype(BF16), w_ref[...].astype(BF16),
                         preferred_element_type=F32) + b_ref[...]


def _ada(c_all, w_ada, b_ada, tn=1024):
    m, d = c_all.shape
    n = w_ada.shape[1]
    return pl.pallas_call(
        _ada_body,
        out_shape=jax.ShapeDtypeStruct((m, n), F32),
        grid=(n // tn,),
        in_specs=[pl.BlockSpec((m, d), lambda j: (0, 0)),
                  pl.BlockSpec((d, tn), lambda j: (0, j)),
                  pl.BlockSpec((1, tn), lambda j: (0, j))],
        out_specs=pl.BlockSpec((m, tn), lambda j: (0, j)),
        compiler_params=pltpu.CompilerParams(
            dimension_semantics=("arbitrary",), vmem_limit_bytes=VMEM_LIMIT_BYTES_V7X,
            allow_input_fusion=[True, False, False]),
        name="ada",
    )(c_all, w_ada, b_ada.reshape(1, n))


def _pack_rows(x_bf16):
    return pltpu.bitcast(x_bf16, U32)


def _unpack_rows(x_u32):
    return pltpu.bitcast(x_u32, BF16)


def _inproj_body(lhs_ref, adap_ref, w_ref, xs_ref, adas_ref, *rest, tiles_per_seq, d, tm,
                 n_side, side_steps, modulate):
    side_in = rest[:n_side]
    outs = rest[n_side:]
    o_ref, os_ref = outs[:2]
    n_main = 4 if modulate else 2
    u_ref, usb_ref = outs[2:4] if modulate else (None, None)
    side_out = outs[n_main:n_main + n_side]
    wb_ref = outs[n_main + n_side]
    i = pl.program_id(1)

    if n_side:
        @pl.when(pl.program_id(0) * pl.num_programs(1) + i < side_steps)
        def _():
            for src, dst in zip(side_in, side_out):
                dst[...] = _pack_rows(src[...].astype(BF16))

    @pl.when(i == 0)
    def _():
        wb_ref[...] = w_ref[...].astype(BF16)
        us = (xs_ref[...] * (1.0 + adas_ref[:, d:2 * d]) + adas_ref[:, 0:d]).astype(BF16)
        os_ref[...] = jnp.dot(us, wb_ref[...], preferred_element_type=F32)
        if modulate:
            usb_ref[...] = us

    b = i // tiles_per_seq
    sh1 = adap_ref[pl.ds(b, 1), 0:d]
    sc1 = adap_ref[pl.ds(b, 1), d:2 * d]
    for r in range(0, tm, ROW_CHUNK):
        rows = slice(r, r + ROW_CHUNK)
        if modulate:
            u = (lhs_ref[rows, :] * (1.0 + sc1) + sh1).astype(BF16)
            u_ref[rows, :] = u
        else:
            u = lhs_ref[rows, :]
        o_ref[rows, :] = jnp.dot(u, wb_ref[...], preferred_element_type=F32).astype(BF16)


def _inproj(lhs, xs, ada, w_in, side_ws, *, col0, ncols, modulate, seq, prompt_row,
            tm=1024, tn=1024, side_steps=16):
    m, d = lhs.shape
    ms = xs.shape[0]
    assert seq % tm == 0 and m % seq == 0 and tm % ROW_CHUNK == 0
    assert col0 % tn == 0 and ncols % tn == 0 and prompt_row % 8 == 0
    ni = m // tm
    jb = col0 // tn
    body = functools.partial(_inproj_body, tiles_per_seq=seq // tm, d=d, tm=tm,
                             n_side=len(side_ws), side_steps=side_steps, modulate=modulate)
    side_idx = lambda j, i: (jnp.minimum(j * ni + i, side_steps - 1), 0)
    side_in = [pl.BlockSpec((w.shape[0] // side_steps, w.shape[1]), side_idx) for w in side_ws]
    side_out = [pl.BlockSpec((w.shape[0] // side_steps // 2, w.shape[1]), side_idx)
                for w in side_ws]
    side_shape = [jax.ShapeDtypeStruct((w.shape[0] // 2, w.shape[1]), U32) for w in side_ws]
    u_shape = ([jax.ShapeDtypeStruct((m, d), BF16), jax.ShapeDtypeStruct((ms, d), BF16)]
               if modulate else [])
    u_spec = ([pl.BlockSpec((tm, d), lambda j, i: (i, 0)), _resident((ms, d))]
              if modulate else [])
    out = pl.pallas_call(
        body,
        out_shape=[jax.ShapeDtypeStruct((m, ncols), BF16),
                   jax.ShapeDtypeStruct((ms, ncols), F32)] + u_shape + side_shape,
        grid=(ncols // tn, ni),
        in_specs=[pl.BlockSpec((tm, d), lambda j, i: (i, 0)),
                  pl.BlockSpec((8, 2 * d), lambda j, i: (prompt_row // 8, 0)),
                  pl.BlockSpec((d, tn), lambda j, i: (0, j + jb)),
                  _resident((ms, d)),
                  _resident((ms, 2 * d))] + side_in,
        out_specs=[pl.BlockSpec((tm, tn), lambda j, i: (i, j)),
                   pl.BlockSpec((ms, tn), lambda j, i: (0, j))] + u_spec + side_out,
        scratch_shapes=[pltpu.VMEM((d, tn), BF16)],
        compiler_params=_params(("arbitrary", "arbitrary"), VMEM_LIMIT_BIG_TILE_BYTES_V7X),
        name="inproj_first" if modulate else "inproj_rest",
    )(lhs, ada, w_in, xs, ada, *side_ws)
    n_main = 4 if modulate else 2
    return out[:n_main], out[n_main:]


def _inproj_wide_body(u_ref, us_ref, w_hbm, *rest, tm, tn, col0, n_side, side_steps):
    side_in = rest[:n_side]
    o_ref, os_ref = rest[n_side:n_side + 2]
    side_out = rest[n_side + 2:2 * n_side + 2]
    wf_ref, wb_ref, sem = rest[2 * n_side + 2:]
    j = pl.program_id(0)
    i = pl.program_id(1)

    def w_copy(jj):
        return pltpu.make_async_copy(w_hbm.at[:, pl.ds(col0 + jj * tn, tn)], wf_ref, sem.at[0])

    @pl.when(jnp.logical_and(j == 0, i == 0))
    def _():
        w_copy(0).start()

    @pl.when(i == 0)
    def _():
        w_copy(j).wait()
        wb_ref[...] = wf_ref[...].astype(BF16)
        os_ref[...] = jnp.dot(us_ref[...], wb_ref[...], preferred_element_type=F32)

    @pl.when(jnp.logical_and(i == 1, j + 1 < pl.num_programs(0)))
    def _():
        w_copy(j + 1).start()

    def cast_piece(src, dst, lo, hi):
        dst[lo // 2:hi // 2, :] = _pack_rows(src[lo:hi, :].astype(BF16))

    pieces = []
    for src, dst in zip(side_in, side_out):
        rows = src.shape[0]
        nr = SIDE_PIECE_ROWS if rows % SIDE_PIECE_ROWS == 0 else rows
        pieces += [functools.partial(cast_piece, src, dst, lo, lo + nr)
                   for lo in range(0, rows, nr)]
    n_chunks = tm // ROW_CHUNK

    def main(with_side):
        for c in range(n_chunks):
            rows = slice(c * ROW_CHUNK, (c + 1) * ROW_CHUNK)
            o_ref[rows, :] = jnp.dot(u_ref[rows, :], wb_ref[...],
                                     preferred_element_type=F32).astype(BF16)
            if with_side:
                for piece in pieces[c::n_chunks]:
                    piece()

    has_side = j * pl.num_programs(1) + i < side_steps
    pl.when(has_side)(functools.partial(main, True))
    pl.when(jnp.logical_not(has_side))(functools.partial(main, False))


def _inproj_wide(u, us, w_in, side_ws, *, col0, ncols, tm, tn, side_steps=16):
    m, d = u.shape
    ms = us.shape[0]
    assert m % tm == 0 and m // tm >= 2 and tm % ROW_CHUNK == 0 and ncols % tn == 0
    ni = m // tm
    body = functools.partial(_inproj_wide_body, tm=tm, tn=tn, col0=col0,
                             n_side=len(side_ws), side_steps=side_steps)
    side_idx = lambda j, i: (jnp.minimum(j * ni + i, side_steps - 1), 0)
    side_in = [pl.BlockSpec((w.shape[0] // side_steps, w.shape[1]), side_idx) for w in side_ws]
    side_out = [pl.BlockSpec((w.shape[0] // side_steps // 2, w.shape[1]), side_idx)
                for w in side_ws]
    side_shape = [jax.ShapeDtypeStruct((w.shape[0] // 2, w.shape[1]), U32) for w in side_ws]
    out = pl.pallas_call(
        body,
        out_shape=[jax.ShapeDtypeStruct((m, ncols), BF16),
                   jax.ShapeDtypeStruct((ms, ncols), F32)] + side_shape,
        grid=(ncols // tn, ni),
        in_specs=[pl.BlockSpec((tm, d), lambda j, i: (i, 0)),
                  _resident((ms, d)),
                  pl.BlockSpec(memory_space=pl.ANY)] + side_in,
        out_specs=[pl.BlockSpec((tm, tn), lambda j, i: (i, j)),
                   pl.BlockSpec((ms, tn), lambda j, i: (0, j))] + side_out,
        scratch_shapes=[pltpu.VMEM((d, tn), F32),
                        pltpu.VMEM((d, tn), BF16),
                        pltpu.SemaphoreType.DMA((1,))],
        compiler_params=_params(("arbitrary", "arbitrary"), MLP_VMEM_LIMIT_BYTES_V7X),
        name="inproj_rest",
    )(u, us, w_in, *side_ws)
    return out[:2], out[2:]


def _sigmoid(x):
    return 0.5 * jnp.tanh(0.5 * x) + 0.5


def _mix_tail(pooled_bf, conv_bf, gp, gc, x, g1, sc2, sh2, wpu_ref, wcu_ref, wo_ref,
              lng_ref, lnb_ref, alpha):
    y_p = jnp.dot(pooled_bf, _unpack_rows(wpu_ref[...]), preferred_element_type=F32)
    y_c = jnp.dot(conv_bf, _unpack_rows(wcu_ref[...]), preferred_element_type=F32)
    mixed = (_sigmoid(gp) * y_p + _sigmoid(gc) * y_c).astype(BF16)
    mo = jnp.dot(mixed, _unpack_rows(wo_ref[...]), preferred_element_type=F32)
    x1 = _layernorm(alpha * x + g1 * mo, lng_ref[...], lnb_ref[...])
    return x1, (x1 * (1.0 + sc2) + sh2).astype(BF16)


def _split_cols(first_ref, rest_ref):
    p = first_ref.shape[-1]

    def cols(lo, hi):
        return first_ref[:, lo:hi] if hi <= p else rest_ref[:, lo - p:hi - p]
    return cols


def _mixer_body(pa_ref, pb_ref, ha_ref, hb_ref, x_ref, ada_ref, grp_ref, pscale_ref, convw_ref,
                wpu_ref, wcu_ref, wo_ref, lng_ref, lnb_ref, w1_ref,
                x1_ref, u2_ref, pst_ref, cst_ref, w1b_ref,
                zz_ref, vv_ref, pm_ref, bc_ref, mixed_ref,
                *, tm, tiles_per_seq, d, alpha, steps):
    i = pl.program_id(0)
    p = d // 2
    gw = p // len(POOL_WINDOWS)
    h = HIST_ROWS
    proj = _split_cols(pa_ref, pb_ref)
    hist = _split_cols(ha_ref, hb_ref)

    cb = MIXER_COL_BLOCK
    ncb = d // cb
    tis = i % tiles_per_seq
    first = tis == 0

    def side_cast():
        tf = w1b_ref.shape[2]
        for fi in range(w1b_ref.shape[0]):
            w1b_ref[fi] = _pack_rows(w1_ref[:, fi * tf:(fi + 1) * tf].astype(BF16))

    def setup():
        zz_ref[0:h, :] = jnp.where(first, 0.0, hist(0, p).astype(F32))
        zz_ref[h:h + tm, :] = proj(0, p).astype(F32)
        vh = hist(3 * p, 4 * p).astype(F32) * hist(p, 2 * p).astype(F32)
        vv_ref[0:h, :] = jnp.where(first, 0.0, vh)
        vv_ref[h:h + tm, :] = proj(3 * p, 4 * p).astype(F32) * proj(p, 2 * p).astype(F32)
        pst_ref[0] = zz_ref[tm:tm + h, :]
        cst_ref[0] = vv_ref[tm:tm + h, :]

    def pool_group(g):
        w = POOL_WINDOWS[g]
        cols = slice(g * gw, (g + 1) * gw)
        pos = tis * tm + lax.broadcasted_iota(jnp.int32, (tm, 1), 0)
        s = zz_ref[:, cols]
        zt = s[h:, :]
        k = 1
        while k < w:
            s = s + pltpu.roll(s, k, axis=0)
            k *= 2
        inv_cnt = 1.0 / jnp.minimum(w, pos + 1).astype(F32)
        pooled = (s[h:, :] * inv_cnt - zt).astype(BF16)
        og = jnp.dot(pooled, _unpack_rows(grp_ref[g * gw // 2:(g + 1) * gw // 2, :]),
                     preferred_element_type=F32)
        pm_ref[:, cols] = (og * pscale_ref[:, cols]).astype(BF16)

    def conv_branch():
        conv = convw_ref[0:1, :] * vv_ref[h - 2:h - 2 + tm, :]
        conv = conv + convw_ref[1:2, :] * vv_ref[h - 1:h - 1 + tm, :]
        conv = conv + convw_ref[2:3, :] * vv_ref[h:h + tm, :]
        bc_ref[...] = (proj(2 * p, 3 * p).astype(F32) * conv).astype(BF16)

    def up_and_gate(k):
        cols = slice(k * cb, (k + 1) * cb)
        y_p = jnp.dot(pm_ref[...], _unpack_rows(wpu_ref[:, cols]), preferred_element_type=F32)
        y_c = jnp.dot(bc_ref[...], _unpack_rows(wcu_ref[:, cols]), preferred_element_type=F32)
        gp = proj(4 * p + k * cb, 4 * p + (k + 1) * cb).astype(F32)
        gc = proj(4 * p + d + k * cb, 4 * p + d + (k + 1) * cb).astype(F32)
        mixed_ref[:, cols] = (_sigmoid(gp) * y_p + _sigmoid(gc) * y_c).astype(BF16)

    def out_block(k):
        return jnp.dot(mixed_ref[...], _unpack_rows(wo_ref[:, k * cb:(k + 1) * cb]),
                       preferred_element_type=F32)

    def finish_rows(mo, r):
        nr = tm // MIXER_FINISH_SLABS
        rows = slice(r * nr, (r + 1) * nr)
        b = (i - 1) // tiles_per_seq
        g1 = ada_ref[pl.ds(b, 1), 2 * d:3 * d]
        sh2 = ada_ref[pl.ds(b, 1), 3 * d:4 * d]
        sc2 = ada_ref[pl.ds(b, 1), 4 * d:5 * d]
        x1 = _layernorm(alpha * x_ref[rows, :] + g1 * mo[rows, :], lng_ref[...], lnb_ref[...])
        x1_ref[rows, :] = x1
        u2_ref[rows, :] = (x1 * (1.0 + sc2) + sh2).astype(BF16)

    def step(has_head, has_tail):
        front = [side_cast, setup] + [functools.partial(pool_group, g)
                                      for g in range(len(POOL_WINDOWS))] + [conv_branch]
        mo_blocks = []
        for k in range(ncb):
            if has_tail:
                mo_blocks.append(out_block(k))
            if has_head and k < len(front):
                front[k]()
        if has_head:
            for piece in front[ncb:]:
                piece()
        mo = jnp.concatenate(mo_blocks, axis=1) if has_tail else None
        fin_at = {(r + 1) * ncb // MIXER_FINISH_SLABS - 1: r for r in range(MIXER_FINISH_SLABS)}
        for k in range(ncb):
            if has_head:
                up_and_gate(k)
            if has_tail and k in fin_at:
                finish_rows(mo, fin_at[k])

    pl.when(i == 0)(functools.partial(step, True, False))
    pl.when(jnp.logical_and(i > 0, i < steps))(functools.partial(step, True, True))
    pl.when(i == steps)(functools.partial(step, False, True))


def _mixer(proj_a, proj_b, xp, ada, grp_bf, pscale, convw, wpu_bf, wcu_bf, wo_bf, lng, lnb,
           w1, *, seq, prompt_row, alpha, tf, tm=256):
    m, d = xp.shape
    p = d // 2
    nseq = m // seq
    h = HIST_ROWS
    assert seq % tm == 0 and m % seq == 0 and tm % h == 0 and d % MIXER_COL_BLOCK == 0
    assert proj_a.shape[1] == p and tm % MIXER_FINISH_SLABS == 0
    steps = m // tm
    dff = w1.shape[1]
    assert d % steps == 0 and dff % tf == 0
    body = functools.partial(_mixer_body, tm=tm, tiles_per_seq=seq // tm, d=d, alpha=alpha,
                             steps=steps)
    cur = lambda i: jnp.minimum(i, steps - 1)
    prev = lambda i: jnp.maximum(i - 1, 0)
    hist_idx = lambda i: (jnp.maximum(cur(i) * (tm // h) - 1, 0), 0)
    return pl.pallas_call(
        body,
        out_shape=[jax.ShapeDtypeStruct((m, d), F32),
                   jax.ShapeDtypeStruct((m, d), BF16),
                   jax.ShapeDtypeStruct((nseq, h, p), F32),
                   jax.ShapeDtypeStruct((nseq, h, p), F32),
                   jax.ShapeDtypeStruct((dff // tf, d // 2, tf), U32)],
        grid=(steps + 1,),
        in_specs=[pl.BlockSpec((tm, proj_a.shape[1]), lambda i: (cur(i), 0)),
                  pl.BlockSpec((tm, proj_b.shape[1]), lambda i: (cur(i), 0)),
                  pl.BlockSpec((h, proj_a.shape[1]), hist_idx),
                  pl.BlockSpec((h, proj_b.shape[1]), hist_idx),
                  pl.BlockSpec((tm, d), lambda i: (prev(i), 0)),
                  pl.BlockSpec((8, N_ADA * d), lambda i: (prompt_row // 8, 0)),
                  _resident(grp_bf.shape),
                  _resident(pscale.shape),
                  _resident(convw.shape),
                  _resident(wpu_bf.shape),
                  _resident(wcu_bf.shape),
                  _resident(wo_bf.shape),
                  _resident(lng.shape),
                  _resident(lnb.shape),
                  pl.BlockSpec((d // steps, dff), lambda i: (cur(i), 0))],
        out_specs=[pl.BlockSpec((tm, d), lambda i: (prev(i), 0)),
                   pl.BlockSpec((tm, d), lambda i: (prev(i), 0)),
                   pl.BlockSpec((1, h, p), lambda i: (cur(i) // (seq // tm), 0, 0)),
                   pl.BlockSpec((1, h, p), lambda i: (cur(i) // (seq // tm), 0, 0)),
                   pl.BlockSpec((dff // tf, d // steps // 2, tf), lambda i: (0, cur(i), 0))],
        scratch_shapes=[pltpu.VMEM((tm + h, p), F32),
                        pltpu.VMEM((tm + h, p), F32),
                        pltpu.VMEM((tm, p), BF16),
                        pltpu.VMEM((tm, p), BF16),
                        pltpu.VMEM((tm, d), BF16)],
        compiler_params=_params(("arbitrary",)),
        name="mixer",
    )(proj_a, proj_b, proj_a, proj_b, xp, ada, grp_bf, pscale, convw, wpu_bf, wcu_bf, wo_bf,
      lng, lnb, w1)


def _mixer_step_body(pa_ref, pb_ref, pool_ref, cstate_ref, x_ref, ada_ref, grp_ref, pscale_ref,
                     convw_ref, wpu_ref, wcu_ref, wo_ref, lng_ref, lnb_ref,
                     x1_ref, u2_ref, npool_ref, nconv_ref, pm_ref, *, d, alpha):
    p = d // 2
    gw = p // len(POOL_WINDOWS)
    proj = _split_cols(pa_ref, pb_ref)
    z = proj(0, p)
    for g, w in enumerate(POOL_WINDOWS):
        cols = slice(g * gw, (g + 1) * gw)
        s = z[:, cols]
        for k in range(1, w):
            s = s + pool_ref[POOL_HIST - k, :, cols]
        inv_cnt = 1.0 / min(w, PAST_LEN + 1)
        pooled = (s * inv_cnt - z[:, cols]).astype(BF16)
        og = jnp.dot(pooled, _unpack_rows(grp_ref[g * gw // 2:(g + 1) * gw // 2, :]),
                         preferred_element_type=F32)
        pm_ref[:, cols] = (og * pscale_ref[:, cols]).astype(BF16)
    for r in range(POOL_HIST - 1):
        npool_ref[r] = pool_ref[r + 1]
    npool_ref[POOL_HIST - 1] = z

    xc = proj(p, 2 * p)
    bc = proj(2 * p, 3 * p)
    cc = proj(3 * p, 4 * p)
    v = cc * xc
    conv = convw_ref[0:1, :] * cstate_ref[0]
    conv = conv + convw_ref[1:2, :] * cstate_ref[1]
    conv = conv + convw_ref[2:3, :] * v
    nconv_ref[0] = cstate_ref[1]
    nconv_ref[1] = v

    gp = proj(4 * p, 4 * p + d)
    gc = proj(4 * p + d, 4 * p + 2 * d)
    g1 = ada_ref[:, 2 * d:3 * d]
    sh2 = ada_ref[:, 3 * d:4 * d]
    sc2 = ada_ref[:, 4 * d:5 * d]
    x1, u2 = _mix_tail(pm_ref[...], (bc * conv).astype(BF16), gp, gc, x_ref[...], g1, sc2, sh2,
                       wpu_ref, wcu_ref, wo_ref, lng_ref, lnb_ref, alpha)
    x1_ref[...] = x1
    u2_ref[...] = u2


def _mixer_step(proj_a, proj_b, pool_state, conv_state, xs, ada, grp_bf, pscale, convw,
                wpu_bf, wcu_bf, wo_bf, lng, lnb, *, alpha):
    ms, d = xs.shape
    p = d // 2
    body = functools.partial(_mixer_step_body, d=d, alpha=alpha)
    return pl.pallas_call(
        body,
        out_shape=[jax.ShapeDtypeStruct((ms, d), F32),
                   jax.ShapeDtypeStruct((ms, d), BF16),
                   jax.ShapeDtypeStruct(pool_state.shape, F32),
                   jax.ShapeDtypeStruct(conv_state.shape, F32)],
        grid=(1,),
        in_specs=[_resident(proj_a.shape),
                  _resident(proj_b.shape),
                  _resident(pool_state.shape),
                  _resident(conv_state.shape),
                  _resident((ms, d)),
                  _resident((ms, N_ADA * d)),
                  _resident(grp_bf.shape),
                  _resident(pscale.shape),
                  _resident(convw.shape),
                  _resident(wpu_bf.shape),
                  _resident(wcu_bf.shape),
                  _resident(wo_bf.shape),
                  _resident(lng.shape),
                  _resident(lnb.shape)],
        out_specs=[_resident((ms, d)),
                   _resident((ms, d)),
                   _resident(pool_state.shape),
                   _resident(conv_state.shape)],
        scratch_shapes=[pltpu.VMEM((ms, p), BF16)],
        compiler_params=_params(("arbitrary",)),
        name="mixer_step",
    )(proj_a, proj_b, pool_state, conv_state, xs, ada, grp_bf, pscale, convw, wpu_bf, wcu_bf,
      wo_bf, lng, lnb)


def _mlp_body(u2_ref, w1_hbm, b1_ref, w2_hbm, x1_ref, g2_ref, b2_ref, lng_ref, lnb_ref,
              u2s_ref, x1s_ref, g2s_ref, o_ref, os_ref,
              acc_ref, accs_ref, w1_buf, w2_buf, h_buf, hs_buf, sem,
              *, nm, nf, te, tm, alpha, rows_per_cond):
    m = pl.program_id(0)
    f = pl.program_id(1)
    slot = m % 2
    t = m * nf + f
    w2_rows = w2_buf.shape[0]
    n_chunks = tm // MLP_ROW_CHUNK
    tp = te // n_chunks

    def w1_copy(fi):
        return pltpu.make_async_copy(w1_hbm.at[fi], w1_buf, sem.at[0])

    def w2_copy(fi):
        return pltpu.make_async_copy(w2_hbm.at[pl.ds(fi * w2_rows, w2_rows), :], w2_buf,
                                     sem.at[1])

    has_main = m < nm

    @pl.when(t == 0)
    def _():
        w1_copy(0).start()

    @pl.when(has_main)
    def _():
        w2_copy(f).start()
        w1_copy(f).wait()

    def hidden(u2):
        h = jnp.dot(u2, _unpack_rows(w1_buf[...]), preferred_element_type=F32) + b1_ref[f]
        return jnp.square(jnp.maximum(h, 0.0)).astype(BF16)

    def finish(x1, g2, acc):
        return _layernorm(alpha * x1 + g2 * (acc + b2_ref[...]), lng_ref[...], lnb_ref[...])

    def epilogue(c):
        part_rows = pl.ds(pl.multiple_of(c * tp, tp), tp)
        base = pl.multiple_of(f * te + c * tp, tp)
        g2 = g2_ref[pl.ds(((m - 1) * tm) // rows_per_cond, 1), :]
        o_ref[part_rows, :] = finish(x1_ref[part_rows, :], g2,
                                     acc_ref[1 - slot, pl.ds(base, tp), :])

    def main(first, singles, with_epilogue):
        def phase_a(c, carry):
            if with_epilogue:
                epilogue(c)
            rows = pl.ds(pl.multiple_of(c * MLP_ROW_CHUNK, MLP_ROW_CHUNK), MLP_ROW_CHUNK)
            h_buf[rows, :] = hidden(u2_ref[rows, :])
            return carry

        lax.fori_loop(0, n_chunks, phase_a, 0)
        if singles:
            hs_buf[...] = hidden(u2s_ref[...])

        @pl.when(t + 1 < nm * nf)
        def _():
            w1_copy(jnp.where(f + 1 == nf, 0, f + 1)).start()

        w2_copy(f).wait()

        def phase_b(c, carry):
            rows = pl.ds(pl.multiple_of(c * MLP_ROW_CHUNK, MLP_ROW_CHUNK), MLP_ROW_CHUNK)
            part = jnp.dot(h_buf[rows, :], _unpack_rows(w2_buf[...]),
                           preferred_element_type=F32)
            if first:
                acc_ref[slot, rows, :] = part
            else:
                acc_ref[slot, rows, :] += part
            return carry

        lax.fori_loop(0, n_chunks, phase_b, 0)
        if singles:
            part = jnp.dot(hs_buf[...], _unpack_rows(w2_buf[...]), preferred_element_type=F32)
            if first:
                accs_ref[...] = part
            else:
                accs_ref[...] += part

    tile0 = m == 0
    later = jnp.logical_and(has_main, m > 0)
    pl.when(jnp.logical_and(tile0, f == 0))(functools.partial(main, True, True, False))
    pl.when(jnp.logical_and(tile0, f > 0))(functools.partial(main, False, True, False))
    pl.when(jnp.logical_and(later, f == 0))(functools.partial(main, True, False, True))
    pl.when(jnp.logical_and(later, f > 0))(functools.partial(main, False, False, True))

    @pl.when(m == nm)
    def _():
        lax.fori_loop(0, n_chunks, lambda c, carry: (epilogue(c), carry)[1], 0)

    @pl.when(jnp.logical_and(m == 1, f == 0))
    def _():
        os_ref[...] = finish(x1s_ref[...], g2s_ref[...], accs_ref[...])


def _mlp(u2, x1, u2s, x1s, ada, w1b, b1, w2b, b2, lng, lnb, *, alpha, rows_per_cond,
         cond_row, tm):
    m, d = u2.shape
    ms = u2s.shape[0]
    nf, _, tf = w1b.shape
    nm = m // tm
    te = tm // nf
    assert m % tm == 0 and tm % nf == 0 and tm % MLP_ROW_CHUNK == 0
    assert rows_per_cond % tm == 0 and te % (tm // MLP_ROW_CHUNK) == 0 and cond_row % 8 == 0
    g2_col = N_ADA - 1
    body = functools.partial(_mlp_body, nm=nm, nf=nf, te=te, tm=tm, alpha=alpha,
                             rows_per_cond=rows_per_cond)
    ep = lambda mi, f: (jnp.where(mi == 0, 0, (mi - 1) * nf + f), 0)
    return pl.pallas_call(
        body,
        out_shape=[jax.ShapeDtypeStruct((m, d), F32),
                   jax.ShapeDtypeStruct((ms, d), F32)],
        grid=(nm + 1, nf),
        in_specs=[pl.BlockSpec((tm, d), lambda mi, f: (jnp.minimum(mi, nm - 1), 0)),
                  pl.BlockSpec(memory_space=pl.ANY),
                  _resident((nf, 1, tf)),
                  pl.BlockSpec(memory_space=pl.ANY),
                  pl.BlockSpec((te, d), ep),
                  pl.BlockSpec((8, d), lambda mi, f: (cond_row // 8, g2_col)),
                  _resident((1, d)),
                  _resident((1, d)),
                  _resident((1, d)),
                  _resident((ms, d)),
                  _resident((ms, d)),
                  pl.BlockSpec((ms, d), lambda mi, f: (0, g2_col),
                               pipeline_mode=pl.Buffered(1))],
        out_specs=[pl.BlockSpec((te, d), ep),
                   _resident((ms, d))],
        scratch_shapes=[pltpu.VMEM((2, tm, d), F32),
                        pltpu.VMEM((ms, d), F32),
                        pltpu.VMEM((d // 2, tf), U32),
                        pltpu.VMEM((tf // 2, d), U32),
                        pltpu.VMEM((tm, tf), BF16),
                        pltpu.VMEM((ms, tf), BF16),
                        pltpu.SemaphoreType.DMA((2,))],
        compiler_params=_params(("arbitrary", "arbitrary"), MLP_VMEM_LIMIT_BYTES_V7X),
        name="mlp",
    )(u2, w1b, b1.reshape(nf, 1, tf), w2b, x1, ada, b2.reshape(1, d), lng, lnb, u2s, x1s, ada)


def kernel(x_prompt, x_sample, state_pool, state_conv, c_prompt, c_sample, w_ada, b_ada, w_in,
           pool_grp_w, pool_scale, conv_w, w_pool_up, w_conv_up, w_o, ln1_g, ln1_b, w_ff1,
           b_ff1, w_ff2, b_ff2, ln2_g, ln2_b):
    depth = w_ada.shape[0]
    nb, seq, d = x_prompt.shape
    ms, dec_seq, _ = x_sample.shape
    assert dec_seq == 1
    p = d // 2
    alpha = (2 * depth) ** 0.25

    prompt_row = ms
    pad = (-(ms + nb)) % 8
    c_all = jnp.concatenate([c_sample, c_prompt, jnp.zeros((pad, d), F32)], axis=0)

    xp = x_prompt.reshape(nb * seq, d)
    xs = x_sample.reshape(ms, d)
    pool_p, conv_p, pool_s, conv_s = [], [], [], []
    for l in range(depth):
        ada = _ada(c_all, w_ada[l], b_ada[l])
        pscale = pool_scale[l].reshape(1, p)
        lng1, lnb1 = ln1_g[l].reshape(1, d), ln1_b[l].reshape(1, d)
        lng2, lnb2 = ln2_g[l].reshape(1, d), ln2_b[l].reshape(1, d)

        n_in = w_in.shape[2]
        (pa_p, pa_s, u_bf, us_bf), _ = _inproj(xp, xs, ada, w_in[l], [], col0=0, ncols=p,
                                               modulate=True, seq=seq, prompt_row=prompt_row)
        (pb_p, pb_s), (grp_bf, wpu_bf, wcu_bf, wo_bf, w2b) = _inproj_wide(
            u_bf, us_bf, w_in[l],
            [pool_grp_w[l].reshape(p, -1), w_pool_up[l], w_conv_up[l], w_o[l], w_ff2[l]],
            col0=p, ncols=n_in - p, tm=INPROJ_BF16_TOKEN_TILE, tn=INPROJ_WIDE_COLS)

        x1p, u2p, pst, cst, w1b = _mixer(
            pa_p, pb_p, xp, ada, grp_bf, pscale, conv_w[l], wpu_bf, wcu_bf, wo_bf, lng1, lnb1,
            w_ff1[l], seq=seq, prompt_row=prompt_row, alpha=alpha, tf=MLP_FF_CHUNK)
        x1s, u2s, npool, nconv = _mixer_step(
            pa_s, pb_s, jnp.transpose(state_pool[l], (1, 0, 2)),
            jnp.transpose(state_conv[l], (1, 0, 2)), xs, ada, grp_bf, pscale, conv_w[l],
            wpu_bf, wcu_bf, wo_bf, lng1, lnb1, alpha=alpha)

        xp, xs = _mlp(u2p, x1p, u2s, x1s, ada, w1b, b_ff1[l], w2b, b_ff2[l], lng2, lnb2,
                      alpha=alpha, rows_per_cond=seq, cond_row=prompt_row,
                      tm=MLP_TOKEN_TILE)

        pool_p.append(pst[:, HIST_ROWS - POOL_HIST:, :])
        conv_p.append(cst[:, HIST_ROWS - CONV_HIST:, :])
        pool_s.append(jnp.transpose(npool, (1, 0, 2)))
        conv_s.append(jnp.transpose(nconv, (1, 0, 2)))

    return (xp.reshape(nb, seq, d), xs.reshape(ms, dec_seq, d),
            jnp.stack(pool_p), jnp.stack(conv_p), jnp.stack(pool_s), jnp.stack(conv_s))
```

```python
import functools

import jax
import jax.numpy as jnp
from jax import lax
from jax.experimental import pallas as pl
from jax.experimental.pallas import tpu as pltpu

F32 = jnp.float32
BF16 = jnp.bfloat16
U32 = jnp.uint32

POOL_WINDOWS = (2, 4, 8, 16)
POOL_HIST = max(POOL_WINDOWS) - 1
CONV_K = 3
CONV_HIST = CONV_K - 1
N_ADA = 6
ADA_EARLY = 2
PAST_LEN = 16384
LN_EPS = 1e-5

VMEM_LIMIT_BYTES_V7X = 56 * 1024 * 1024
VMEM_LIMIT_BIG_TILE_BYTES_V7X = 60 * 1024 * 1024
MLP_VMEM_LIMIT_BYTES_V7X = 62 * 1024 * 1024
HIST_ROWS = 16
ROW_CHUNK = 256
MIXER_COL_BLOCK = 256
MIXER_FINISH_SLABS = 4
INPROJ_BF16_TOKEN_TILE = 2048
INPROJ_WIDE_COLS = 1024
SIDE_PIECE_ROWS = 128
MLP_TOKEN_TILE = 1024
MLP_FF_CHUNK = 2048
MLP_ROW_CHUNK = 512


def _resident(shape):
    zeros = (0,) * len(shape)
    return pl.BlockSpec(shape, lambda *_: zeros, pipeline_mode=pl.Buffered(1))


def _params(semantics, vmem_limit_bytes=VMEM_LIMIT_BYTES_V7X):
    return pltpu.CompilerParams(dimension_semantics=semantics,
                                vmem_limit_bytes=vmem_limit_bytes)


def _layernorm(xf, g, b):
    mu = jnp.mean(xf, axis=-1, keepdims=True)
    var = jnp.mean(jnp.square(xf - mu), axis=-1, keepdims=True)
    return (xf - mu) * lax.rsqrt(var + LN_EPS) * g + b


def _ada_body(c_ref, w_ref, b_ref, o_ref):
    o_ref[...] = jnp.dot(c_ref[...].astype(BF16), w_ref[...].astype(BF16),
                         preferred_element_type=F32) + b_ref[...]


def _ada(c_all, w_ada, b_ada, *, ncols, tn=1024):
    m, d = c_all.shape
    n = ncols
    assert n % tn == 0
    return pl.pallas_call(
        _ada_body,
        out_shape=jax.ShapeDtypeStruct((m, n), F32),
        grid=(n // tn,),
        in_specs=[pl.BlockSpec((m, d), lambda j: (0, 0)),
                  pl.BlockSpec((d, tn), lambda j: (0, j)),
                  pl.BlockSpec((1, tn), lambda j: (0, j))],
        out_specs=pl.BlockSpec((m, tn), lambda j: (0, j)),
        compiler_params=pltpu.CompilerParams(
            dimension_semantics=("arbitrary",), vmem_limit_bytes=VMEM_LIMIT_BYTES_V7X,
            allow_input_fusion=[True, False, False]),
        name="ada",
    )(c_all, w_ada, b_ada.reshape(1, -1))


def _pack_rows(x_bf16):
    return pltpu.bitcast(x_bf16, U32)


def _unpack_rows(x_u32):
    return pltpu.bitcast(x_u32, BF16)


def _inproj_body(lhs_ref, adap_ref, w_ref, xs_ref, adas_ref, *rest, tiles_per_seq, d, tm,
                 n_side, side_steps, modulate):
    side_in = rest[:n_side]
    outs = rest[n_side:]
    o_ref, os_ref = outs[:2]
    n_main = 4 if modulate else 2
    u_ref, usb_ref = outs[2:4] if modulate else (None, None)
    side_out = outs[n_main:n_main + n_side]
    wb_ref = outs[n_main + n_side]
    i = pl.program_id(1)

    if n_side:
        @pl.when(pl.program_id(0) * pl.num_programs(1) + i < side_steps)
        def _():
            for src, dst in zip(side_in, side_out):
                dst[...] = _pack_rows(src[...].astype(BF16))

    @pl.when(i == 0)
    def _():
        wb_ref[...] = w_ref[...].astype(BF16)
        us = (xs_ref[...] * (1.0 + adas_ref[:, d:2 * d]) + adas_ref[:, 0:d]).astype(BF16)
        os_ref[...] = jnp.dot(us, wb_ref[...], preferred_element_type=F32)
        if modulate:
            usb_ref[...] = us

    b = i // tiles_per_seq
    sh1 = adap_ref[pl.ds(b, 1), 0:d]
    sc1 = adap_ref[pl.ds(b, 1), d:2 * d]
    for r in range(0, tm, ROW_CHUNK):
        rows = slice(r, r + ROW_CHUNK)
        if modulate:
            u = (lhs_ref[rows, :] * (1.0 + sc1) + sh1).astype(BF16)
            u_ref[rows, :] = u
        else:
            u = lhs_ref[rows, :]
        o_ref[rows, :] = jnp.dot(u, wb_ref[...], preferred_element_type=F32).astype(BF16)


def _inproj(lhs, xs, ada, w_in, side_ws, *, col0, ncols, modulate, seq, prompt_row,
            tm=1024, tn=1024, side_steps=16):
    m, d = lhs.shape
    ms = xs.shape[0]
    assert seq % tm == 0 and m % seq == 0 and tm % ROW_CHUNK == 0
    assert col0 % tn == 0 and ncols % tn == 0 and prompt_row % 8 == 0
    ni = m // tm
    jb = col0 // tn
    body = functools.partial(_inproj_body, tiles_per_seq=seq // tm, d=d, tm=tm,
                             n_side=len(side_ws), side_steps=side_steps, modulate=modulate)
    side_idx = lambda j, i: (jnp.minimum(j * ni + i, side_steps - 1), 0)
    side_in = [pl.BlockSpec((w.shape[0] // side_steps, w.shape[1]), side_idx) for w in side_ws]
    side_out = [pl.BlockSpec((w.shape[0] // side_steps // 2, w.shape[1]), side_idx)
                for w in side_ws]
    side_shape = [jax.ShapeDtypeStruct((w.shape[0] // 2, w.shape[1]), U32) for w in side_ws]
    u_shape = ([jax.ShapeDtypeStruct((m, d), BF16), jax.ShapeDtypeStruct((ms, d), BF16)]
               if modulate else [])
    u_spec = ([pl.BlockSpec((tm, d), lambda j, i: (i, 0)), _resident((ms, d))]
              if modulate else [])
    out = pl.pallas_call(
        body,
        out_shape=[jax.ShapeDtypeStruct((m, ncols), BF16),
                   jax.ShapeDtypeStruct((ms, ncols), F32)] + u_shape + side_shape,
        grid=(ncols // tn, ni),
        in_specs=[pl.BlockSpec((tm, d), lambda j, i: (i, 0)),
                  pl.BlockSpec((8, 2 * d), lambda j, i: (prompt_row // 8, 0)),
                  pl.BlockSpec((d, tn), lambda j, i: (0, j + jb)),
                  _resident((ms, d)),
                  _resident((ms, 2 * d))] + side_in,
        out_specs=[pl.BlockSpec((tm, tn), lambda j, i: (i, j)),
                   pl.BlockSpec((ms, tn), lambda j, i: (0, j))] + u_spec + side_out,
        scratch_shapes=[pltpu.VMEM((d, tn), BF16)],
        compiler_params=_params(("arbitrary", "arbitrary"), VMEM_LIMIT_BIG_TILE_BYTES_V7X),
        name="inproj_first" if modulate else "inproj_rest",
    )(lhs, ada, w_in, xs, ada, *side_ws)
    n_main = 4 if modulate else 2
    return out[:n_main], out[n_main:]


def _inproj_wide_body(u_ref, us_ref, w_hbm, c_ref, wada_ref, bada_ref, *rest, tm, tn, col0,
                      n_side, side_steps):
    side_in = rest[:n_side]
    o_ref, os_ref, adab_ref = rest[n_side:n_side + 3]
    side_out = rest[n_side + 3:2 * n_side + 3]
    wf_ref, wb_ref, sem = rest[2 * n_side + 3:]
    j = pl.program_id(0)
    i = pl.program_id(1)

    def w_copy(jj):
        return pltpu.make_async_copy(w_hbm.at[:, pl.ds(col0 + jj * tn, tn)], wf_ref, sem.at[0])

    @pl.when(jnp.logical_and(j == 0, i == 0))
    def _():
        w_copy(0).start()

    @pl.when(i == 0)
    def _():
        w_copy(j).wait()
        wb_ref[...] = wf_ref[...].astype(BF16)
        os_ref[...] = jnp.dot(us_ref[...], wb_ref[...], preferred_element_type=F32)

    @pl.when(jnp.logical_and(i == 1, j + 1 < pl.num_programs(0)))
    def _():
        w_copy(j + 1).start()

    def cast_piece(src, dst, lo, hi):
        dst[lo // 2:hi // 2, :] = _pack_rows(src[lo:hi, :].astype(BF16))

    def ada_piece(lo, hi):
        adab_ref[:, lo:hi] = jnp.dot(c_ref[...].astype(BF16), wada_ref[:, lo:hi].astype(BF16),
                                     preferred_element_type=F32) + bada_ref[:, lo:hi]

    pieces = []
    for src, dst in zip(side_in, side_out):
        rows = src.shape[0]
        nr = SIDE_PIECE_ROWS if rows % SIDE_PIECE_ROWS == 0 else rows
        pieces += [functools.partial(cast_piece, src, dst, lo, lo + nr)
                   for lo in range(0, rows, nr)]
    half = adab_ref.shape[1] // 2
    pieces += [functools.partial(ada_piece, 0, half),
               functools.partial(ada_piece, half, 2 * half)]
    n_chunks = tm // ROW_CHUNK

    def main(with_side):
        for c in range(n_chunks):
            rows = slice(c * ROW_CHUNK, (c + 1) * ROW_CHUNK)
            o_ref[rows, :] = jnp.dot(u_ref[rows, :], wb_ref[...],
                                     preferred_element_type=F32).astype(BF16)
            if with_side:
                for piece in pieces[c::n_chunks]:
                    piece()

    has_side = j * pl.num_programs(1) + i < side_steps
    pl.when(has_side)(functools.partial(main, True))
    pl.when(jnp.logical_not(has_side))(functools.partial(main, False))


def _inproj_wide(u, us, w_in, c_all, w_ada, b_ada, side_ws, *, col0, ncols, ada_col0, tm, tn,
                 side_steps=16):
    m, d = u.shape
    ms = us.shape[0]
    mc = c_all.shape[0]
    ni = m // tm
    n_ada = w_ada.shape[1] - ada_col0
    ada_tn = n_ada // side_steps
    assert m % tm == 0 and ni >= 2 and tm % ROW_CHUNK == 0 and ncols % tn == 0
    assert ncols // tn * ni >= side_steps and n_ada % side_steps == 0
    assert ada_tn % 256 == 0 and ada_col0 % ada_tn == 0
    body = functools.partial(_inproj_wide_body, tm=tm, tn=tn, col0=col0,
                             n_side=len(side_ws), side_steps=side_steps)
    side_step = lambda j, i: jnp.minimum(j * ni + i, side_steps - 1)
    side_idx = lambda j, i: (side_step(j, i), 0)
    ada_in_idx = lambda j, i: (0, ada_col0 // ada_tn + side_step(j, i))
    side_in = [pl.BlockSpec((w.shape[0] // side_steps, w.shape[1]), side_idx) for w in side_ws]
    side_out = [pl.BlockSpec((w.shape[0] // side_steps // 2, w.shape[1]), side_idx)
                for w in side_ws]
    side_shape = [jax.ShapeDtypeStruct((w.shape[0] // 2, w.shape[1]), U32) for w in side_ws]
    out = pl.pallas_call(
        body,
        out_shape=[jax.ShapeDtypeStruct((m, ncols), BF16),
                   jax.ShapeDtypeStruct((ms, ncols), F32),
                   jax.ShapeDtypeStruct((mc, n_ada), F32)] + side_shape,
        grid=(ncols // tn, ni),
        in_specs=[pl.BlockSpec((tm, d), lambda j, i: (i, 0)),
                  _resident((ms, d)),
                  pl.BlockSpec(memory_space=pl.ANY),
                  _resident((mc, d)),
                  pl.BlockSpec((d, ada_tn), ada_in_idx),
                  pl.BlockSpec((1, ada_tn), ada_in_idx)] + side_in,
        out_specs=[pl.BlockSpec((tm, tn), lambda j, i: (i, j)),
                   pl.BlockSpec((ms, tn), lambda j, i: (0, j)),
                   pl.BlockSpec((mc, ada_tn), lambda j, i: (0, side_step(j, i)))] + side_out,
        scratch_shapes=[pltpu.VMEM((d, tn), F32),
                        pltpu.VMEM((d, tn), BF16),
                        pltpu.SemaphoreType.DMA((1,))],
        compiler_params=_params(("arbitrary", "arbitrary"), MLP_VMEM_LIMIT_BYTES_V7X),
        name="inproj_rest",
    )(u, us, w_in, c_all, w_ada, b_ada.reshape(1, -1), *side_ws)
    return out[:3], out[3:]


def _sigmoid(x):
    return 0.5 * jnp.tanh(0.5 * x) + 0.5


def _mix_tail(pooled_bf, conv_bf, gp, gc, x, g1, sc2, sh2, wpu_ref, wcu_ref, wo_ref,
              lng_ref, lnb_ref, alpha):
    y_p = jnp.dot(pooled_bf, _unpack_rows(wpu_ref[...]), preferred_element_type=F32)
    y_c = jnp.dot(conv_bf, _unpack_rows(wcu_ref[...]), preferred_element_type=F32)
    mixed = (_sigmoid(gp) * y_p + _sigmoid(gc) * y_c).astype(BF16)
    mo = jnp.dot(mixed, _unpack_rows(wo_ref[...]), preferred_element_type=F32)
    x1 = _layernorm(alpha * x + g1 * mo, lng_ref[...], lnb_ref[...])
    return x1, (x1 * (1.0 + sc2) + sh2).astype(BF16)


def _split_cols(first_ref, rest_ref):
    p = first_ref.shape[-1]

    def cols(lo, hi):
        return first_ref[:, lo:hi] if hi <= p else rest_ref[:, lo - p:hi - p]
    return cols


def _mixer_body(pa_ref, pb_ref, ha_ref, hb_ref, x_ref, ada_ref, grp_ref, pscale_ref, convw_ref,
                wpu_ref, wcu_ref, wo_ref, lng_ref, lnb_ref, w1_ref, w2_ref,
                x1_ref, u2_ref, pst_ref, cst_ref, w1b_ref, w2b_ref,
                zz_ref, vv_ref, pm_ref, bc_ref, mixed_ref,
                *, tm, tiles_per_seq, d, alpha, steps):
    i = pl.program_id(0)
    p = d // 2
    gw = p // len(POOL_WINDOWS)
    h = HIST_ROWS
    proj = _split_cols(pa_ref, pb_ref)
    hist = _split_cols(ha_ref, hb_ref)

    cb = MIXER_COL_BLOCK
    ncb = d // cb
    tis = i % tiles_per_seq
    first = tis == 0

    def side_cast():
        tf = w1b_ref.shape[2]
        for fi in range(w1b_ref.shape[0]):
            w1b_ref[fi] = _pack_rows(w1_ref[:, fi * tf:(fi + 1) * tf].astype(BF16))
        w2b_ref[...] = _pack_rows(w2_ref[...].astype(BF16))

    def setup():
        zz_ref[0:h, :] = jnp.where(first, 0.0, hist(0, p).astype(F32))
        zz_ref[h:h + tm, :] = proj(0, p).astype(F32)
        vh = hist(3 * p, 4 * p).astype(F32) * hist(p, 2 * p).astype(F32)
        vv_ref[0:h, :] = jnp.where(first, 0.0, vh)
        vv_ref[h:h + tm, :] = proj(3 * p, 4 * p).astype(F32) * proj(p, 2 * p).astype(F32)
        pst_ref[0] = zz_ref[tm:tm + h, :]
        cst_ref[0] = vv_ref[tm:tm + h, :]

    def pool_group(g):
        w = POOL_WINDOWS[g]
        cols = slice(g * gw, (g + 1) * gw)
        pos = tis * tm + lax.broadcasted_iota(jnp.int32, (tm, 1), 0)
        s = zz_ref[:, cols]
        zt = s[h:, :]
        k = 1
        while k < w:
            s = s + pltpu.roll(s, k, axis=0)
            k *= 2
        inv_cnt = 1.0 / jnp.minimum(w, pos + 1).astype(F32)
        pooled = (s[h:, :] * inv_cnt - zt).astype(BF16)
        og = jnp.dot(pooled, _unpack_rows(grp_ref[g * gw // 2:(g + 1) * gw // 2, :]),
                     preferred_element_type=F32)
        pm_ref[:, cols] = (og * pscale_ref[:, cols]).astype(BF16)

    def conv_branch():
        conv = convw_ref[0:1, :] * vv_ref[h - 2:h - 2 + tm, :]
        conv = conv + convw_ref[1:2, :] * vv_ref[h - 1:h - 1 + tm, :]
        conv = conv + convw_ref[2:3, :] * vv_ref[h:h + tm, :]
        bc_ref[...] = (proj(2 * p, 3 * p).astype(F32) * conv).astype(BF16)

    def up_and_gate(k):
        cols = slice(k * cb, (k + 1) * cb)
        y_p = jnp.dot(pm_ref[...], _unpack_rows(wpu_ref[:, cols]), preferred_element_type=F32)
        y_c = jnp.dot(bc_ref[...], _unpack_rows(wcu_ref[:, cols]), preferred_element_type=F32)
        gp = proj(4 * p + k * cb, 4 * p + (k + 1) * cb).astype(F32)
        gc = proj(4 * p + d + k * cb, 4 * p + d + (k + 1) * cb).astype(F32)
        mixed_ref[:, cols] = (_sigmoid(gp) * y_p + _sigmoid(gc) * y_c).astype(BF16)

    def out_block(k):
        return jnp.dot(mixed_ref[...], _unpack_rows(wo_ref[:, k * cb:(k + 1) * cb]),
                       preferred_element_type=F32)

    def finish_rows(mo, r):
        nr = tm // MIXER_FINISH_SLABS
        rows = slice(r * nr, (r + 1) * nr)
        b = (i - 1) // tiles_per_seq
        g1 = ada_ref[pl.ds(b, 1), 0:d]
        sh2 = ada_ref[pl.ds(b, 1), d:2 * d]
        sc2 = ada_ref[pl.ds(b, 1), 2 * d:3 * d]
        x1 = _layernorm(alpha * x_ref[rows, :] + g1 * mo[rows, :], lng_ref[...], lnb_ref[...])
        x1_ref[rows, :] = x1
        u2_ref[rows, :] = (x1 * (1.0 + sc2) + sh2).astype(BF16)

    def step(has_head, has_tail):
        front = [side_cast, setup] + [functools.partial(pool_group, g)
                                      for g in range(len(POOL_WINDOWS))] + [conv_branch]
        mo_blocks = []
        for k in range(ncb):
            if has_tail:
                mo_blocks.append(out_block(k))
            if has_head and k < len(front):
                front[k]()
        if has_head:
            for piece in front[ncb:]:
                piece()
        mo = jnp.concatenate(mo_blocks, axis=1) if has_tail else None
        fin_at = {(r + 1) * ncb // MIXER_FINISH_SLABS - 1: r for r in range(MIXER_FINISH_SLABS)}
        for k in range(ncb):
            if has_head:
                up_and_gate(k)
            if has_tail and k in fin_at:
                finish_rows(mo, fin_at[k])

    pl.when(i == 0)(functools.partial(step, True, False))
    pl.when(jnp.logical_and(i > 0, i < steps))(functools.partial(step, True, True))
    pl.when(i == steps)(functools.partial(step, False, True))


def _mixer(proj_a, proj_b, xp, ada, grp_bf, pscale, convw, wpu_bf, wcu_bf, wo_bf, lng, lnb,
           w1, w2, *, seq, prompt_row, alpha, tf, tm=256):
    m, d = xp.shape
    p = d // 2
    nseq = m // seq
    h = HIST_ROWS
    assert seq % tm == 0 and m % seq == 0 and tm % h == 0 and d % MIXER_COL_BLOCK == 0
    assert proj_a.shape[1] == p and tm % MIXER_FINISH_SLABS == 0
    steps = m // tm
    dff = w1.shape[1]
    assert d % steps == 0 and dff % steps == 0 and dff % tf == 0
    body = functools.partial(_mixer_body, tm=tm, tiles_per_seq=seq // tm, d=d, alpha=alpha,
                             steps=steps)
    cur = lambda i: jnp.minimum(i, steps - 1)
    prev = lambda i: jnp.maximum(i - 1, 0)
    hist_idx = lambda i: (jnp.maximum(cur(i) * (tm // h) - 1, 0), 0)
    return pl.pallas_call(
        body,
        out_shape=[jax.ShapeDtypeStruct((m, d), F32),
                   jax.ShapeDtypeStruct((m, d), BF16),
                   jax.ShapeDtypeStruct((nseq, h, p), F32),
                   jax.ShapeDtypeStruct((nseq, h, p), F32),
                   jax.ShapeDtypeStruct((dff // tf, d // 2, tf), U32),
                   jax.ShapeDtypeStruct((dff // 2, d), U32)],
        grid=(steps + 1,),
        in_specs=[pl.BlockSpec((tm, proj_a.shape[1]), lambda i: (cur(i), 0)),
                  pl.BlockSpec((tm, proj_b.shape[1]), lambda i: (cur(i), 0)),
                  pl.BlockSpec((h, proj_a.shape[1]), hist_idx),
                  pl.BlockSpec((h, proj_b.shape[1]), hist_idx),
                  pl.BlockSpec((tm, d), lambda i: (prev(i), 0)),
                  pl.BlockSpec((8, ada.shape[1]), lambda i: (prompt_row // 8, 0)),
                  _resident(grp_bf.shape),
                  _resident(pscale.shape),
                  _resident(convw.shape),
                  _resident(wpu_bf.shape),
                  _resident(wcu_bf.shape),
                  _resident(wo_bf.shape),
                  _resident(lng.shape),
                  _resident(lnb.shape),
                  pl.BlockSpec((d // steps, dff), lambda i: (cur(i), 0)),
                  pl.BlockSpec((dff // steps, d), lambda i: (cur(i), 0))],
        out_specs=[pl.BlockSpec((tm, d), lambda i: (prev(i), 0)),
                   pl.BlockSpec((tm, d), lambda i: (prev(i), 0)),
                   pl.BlockSpec((1, h, p), lambda i: (cur(i) // (seq // tm), 0, 0)),
                   pl.BlockSpec((1, h, p), lambda i: (cur(i) // (seq // tm), 0, 0)),
                   pl.BlockSpec((dff // tf, d // steps // 2, tf), lambda i: (0, cur(i), 0)),
                   pl.BlockSpec((dff // steps // 2, d), lambda i: (cur(i), 0))],
        scratch_shapes=[pltpu.VMEM((tm + h, p), F32),
                        pltpu.VMEM((tm + h, p), F32),
                        pltpu.VMEM((tm, p), BF16),
                        pltpu.VMEM((tm, p), BF16),
                        pltpu.VMEM((tm, d), BF16)],
        compiler_params=_params(("arbitrary",)),
        name="mixer",
    )(proj_a, proj_b, proj_a, proj_b, xp, ada, grp_bf, pscale, convw, wpu_bf, wcu_bf, wo_bf,
      lng, lnb, w1, w2)


def _mixer_step_body(pa_ref, pb_ref, pool_ref, cstate_ref, x_ref, ada_ref, grp_ref, pscale_ref,
                     convw_ref, wpu_ref, wcu_ref, wo_ref, lng_ref, lnb_ref,
                     x1_ref, u2_ref, npool_ref, nconv_ref, pm_ref, *, d, alpha):
    p = d // 2
    gw = p // len(POOL_WINDOWS)
    proj = _split_cols(pa_ref, pb_ref)
    z = proj(0, p)
    for g, w in enumerate(POOL_WINDOWS):
        cols = slice(g * gw, (g + 1) * gw)
        s = z[:, cols]
        for k in range(1, w):
            s = s + pool_ref[POOL_HIST - k, :, cols]
        inv_cnt = 1.0 / min(w, PAST_LEN + 1)
        pooled = (s * inv_cnt - z[:, cols]).astype(BF16)
        og = jnp.dot(pooled, _unpack_rows(grp_ref[g * gw // 2:(g + 1) * gw // 2, :]),
                         preferred_element_type=F32)
        pm_ref[:, cols] = (og * pscale_ref[:, cols]).astype(BF16)
    for r in range(POOL_HIST - 1):
        npool_ref[r] = pool_ref[r + 1]
    npool_ref[POOL_HIST - 1] = z

    xc = proj(p, 2 * p)
    bc = proj(2 * p, 3 * p)
    cc = proj(3 * p, 4 * p)
    v = cc * xc
    conv = convw_ref[0:1, :] * cstate_ref[0]
    conv = conv + convw_ref[1:2, :] * cstate_ref[1]
    conv = conv + convw_ref[2:3, :] * v
    nconv_ref[0] = cstate_ref[1]
    nconv_ref[1] = v

    gp = proj(4 * p, 4 * p + d)
    gc = proj(4 * p + d, 4 * p + 2 * d)
    g1 = ada_ref[:, 0:d]
    sh2 = ada_ref[:, d:2 * d]
    sc2 = ada_ref[:, 2 * d:3 * d]
    x1, u2 = _mix_tail(pm_ref[...], (bc * conv).astype(BF16), gp, gc, x_ref[...], g1, sc2, sh2,
                       wpu_ref, wcu_ref, wo_ref, lng_ref, lnb_ref, alpha)
    x1_ref[...] = x1
    u2_ref[...] = u2


def _mixer_step(proj_a, proj_b, pool_state, conv_state, xs, ada, grp_bf, pscale, convw,
                wpu_bf, wcu_bf, wo_bf, lng, lnb, *, alpha):
    ms, d = xs.shape
    p = d // 2
    body = functools.partial(_mixer_step_body, d=d, alpha=alpha)
    return pl.pallas_call(
        body,
        out_shape=[jax.ShapeDtypeStruct((ms, d), F32),
                   jax.ShapeDtypeStruct((ms, d), BF16),
                   jax.ShapeDtypeStruct(pool_state.shape, F32),
                   jax.ShapeDtypeStruct(conv_state.shape, F32)],
        grid=(1,),
        in_specs=[_resident(proj_a.shape),
                  _resident(proj_b.shape),
                  _resident(pool_state.shape),
                  _resident(conv_state.shape),
                  _resident((ms, d)),
                  _resident((ms, ada.shape[1])),
                  _resident(grp_bf.shape),
                  _resident(pscale.shape),
                  _resident(convw.shape),
                  _resident(wpu_bf.shape),
                  _resident(wcu_bf.shape),
                  _resident(wo_bf.shape),
                  _resident(lng.shape),
                  _resident(lnb.shape)],
        out_specs=[_resident((ms, d)),
                   _resident((ms, d)),
                   _resident(pool_state.shape),
                   _resident(conv_state.shape)],
        scratch_shapes=[pltpu.VMEM((ms, p), BF16)],
        compiler_params=_params(("arbitrary",)),
        name="mixer_step",
    )(proj_a, proj_b, pool_state, conv_state, xs, ada, grp_bf, pscale, convw, wpu_bf, wcu_bf,
      wo_bf, lng, lnb)


def _mlp_body(u2_ref, w1_hbm, b1_ref, w2_hbm, x1_ref, g2_ref, b2_ref, lng_ref, lnb_ref,
              u2s_ref, x1s_ref, g2s_ref, o_ref, os_ref,
              acc_ref, accs_ref, w1_buf, w2_buf, h_buf, hs_buf, sem,
              *, nm, nf, te, tm, alpha, rows_per_cond):
    m = pl.program_id(0)
    f = pl.program_id(1)
    slot = m % 2
    t = m * nf + f
    w2_rows = w2_buf.shape[0]
    n_chunks = tm // MLP_ROW_CHUNK
    tp = te // n_chunks

    def w1_copy(fi):
        return pltpu.make_async_copy(w1_hbm.at[fi], w1_buf, sem.at[0])

    def w2_copy(fi):
        return pltpu.make_async_copy(w2_hbm.at[pl.ds(fi * w2_rows, w2_rows), :], w2_buf,
                                     sem.at[1])

    has_main = m < nm

    @pl.when(t == 0)
    def _():
        w1_copy(0).start()

    @pl.when(has_main)
    def _():
        w2_copy(f).start()
        w1_copy(f).wait()

    def hidden(u2):
        h = jnp.dot(u2, _unpack_rows(w1_buf[...]), preferred_element_type=F32) + b1_ref[f]
        return jnp.square(jnp.maximum(h, 0.0)).astype(BF16)

    def finish(x1, g2, acc):
        return _layernorm(alpha * x1 + g2 * (acc + b2_ref[...]), lng_ref[...], lnb_ref[...])

    def epilogue(c):
        part_rows = pl.ds(pl.multiple_of(c * tp, tp), tp)
        base = pl.multiple_of(f * te + c * tp, tp)
        g2 = g2_ref[pl.ds(((m - 1) * tm) // rows_per_cond, 1), :]
        o_ref[part_rows, :] = finish(x1_ref[part_rows, :], g2,
                                     acc_ref[1 - slot, pl.ds(base, tp), :])

    def main(first, singles, with_epilogue):
        def phase_a(c, carry):
            if with_epilogue:
                epilogue(c)
            rows = pl.ds(pl.multiple_of(c * MLP_ROW_CHUNK, MLP_ROW_CHUNK), MLP_ROW_CHUNK)
            h_buf[rows, :] = hidden(u2_ref[rows, :])
            return carry

        lax.fori_loop(0, n_chunks, phase_a, 0)
        if singles:
            hs_buf[...] = hidden(u2s_ref[...])

        @pl.when(t + 1 < nm * nf)
        def _():
            w1_copy(jnp.where(f + 1 == nf, 0, f + 1)).start()

        w2_copy(f).wait()

        def phase_b(c, carry):
            rows = pl.ds(pl.multiple_of(c * MLP_ROW_CHUNK, MLP_ROW_CHUNK), MLP_ROW_CHUNK)
            part = jnp.dot(h_buf[rows, :], _unpack_rows(w2_buf[...]),
                           preferred_element_type=F32)
            if first:
                acc_ref[slot, rows, :] = part
            else:
                acc_ref[slot, rows, :] += part
            return carry

        lax.fori_loop(0, n_chunks, phase_b, 0)
        if singles:
            part = jnp.dot(hs_buf[...], _unpack_rows(w2_buf[...]), preferred_element_type=F32)
            if first:
                accs_ref[...] = part
            else:
                accs_ref[...] += part

    tile0 = m == 0
    later = jnp.logical_and(has_main, m > 0)
    pl.when(jnp.logical_and(tile0, f == 0))(functools.partial(main, True, True, False))
    pl.when(jnp.logical_and(tile0, f > 0))(functools.partial(main, False, True, False))
    pl.when(jnp.logical_and(later, f == 0))(functools.partial(main, True, False, True))
    pl.when(jnp.logical_and(later, f > 0))(functools.partial(main, False, False, True))

    @pl.when(m == nm)
    def _():
        lax.fori_loop(0, n_chunks, lambda c, carry: (epilogue(c), carry)[1], 0)

    @pl.when(jnp.logical_and(m == 1, f == 0))
    def _():
        os_ref[...] = finish(x1s_ref[...], g2s_ref[...], accs_ref[...])


def _mlp(u2, x1, u2s, x1s, ada, w1b, b1, w2b, b2, lng, lnb, *, alpha, rows_per_cond,
         cond_row, tm):
    m, d = u2.shape
    ms = u2s.shape[0]
    nf, _, tf = w1b.shape
    nm = m // tm
    te = tm // nf
    assert m % tm == 0 and tm % nf == 0 and tm % MLP_ROW_CHUNK == 0
    assert rows_per_cond % tm == 0 and te % (tm // MLP_ROW_CHUNK) == 0 and cond_row % 8 == 0
    g2_col = ada.shape[1] // d - 1
    body = functools.partial(_mlp_body, nm=nm, nf=nf, te=te, tm=tm, alpha=alpha,
                             rows_per_cond=rows_per_cond)
    ep = lambda mi, f: (jnp.where(mi == 0, 0, (mi - 1) * nf + f), 0)
    return pl.pallas_call(
        body,
        out_shape=[jax.ShapeDtypeStruct((m, d), F32),
                   jax.ShapeDtypeStruct((ms, d), F32)],
        grid=(nm + 1, nf),
        in_specs=[pl.BlockSpec((tm, d), lambda mi, f: (jnp.minimum(mi, nm - 1), 0)),
                  pl.BlockSpec(memory_space=pl.ANY),
                  _resident((nf, 1, tf)),
                  pl.BlockSpec(memory_space=pl.ANY),
                  pl.BlockSpec((te, d), ep),
                  pl.BlockSpec((8, d), lambda mi, f: (cond_row // 8, g2_col)),
                  _resident((1, d)),
                  _resident((1, d)),
                  _resident((1, d)),
                  _resident((ms, d)),
                  _resident((ms, d)),
                  pl.BlockSpec((ms, d), lambda mi, f: (0, g2_col),
                               pipeline_mode=pl.Buffered(1))],
        out_specs=[pl.BlockSpec((te, d), ep),
                   _resident((ms, d))],
        scratch_shapes=[pltpu.VMEM((2, tm, d), F32),
                        pltpu.VMEM((ms, d), F32),
                        pltpu.VMEM((d // 2, tf), U32),
                        pltpu.VMEM((tf // 2, d), U32),
                        pltpu.VMEM((tm, tf), BF16),
                        pltpu.VMEM((ms, tf), BF16),
                        pltpu.SemaphoreType.DMA((2,))],
        compiler_params=_params(("arbitrary", "arbitrary"), MLP_VMEM_LIMIT_BYTES_V7X),
        name="mlp",
    )(u2, w1b, b1.reshape(nf, 1, tf), w2b, x1, ada, b2.reshape(1, d), lng, lnb, u2s, x1s, ada)


def kernel(x_prompt, x_sample, state_pool, state_conv, c_prompt, c_sample, w_ada, b_ada, w_in,
           pool_grp_w, pool_scale, conv_w, w_pool_up, w_conv_up, w_o, ln1_g, ln1_b, w_ff1,
           b_ff1, w_ff2, b_ff2, ln2_g, ln2_b):
    depth = w_ada.shape[0]
    nb, seq, d = x_prompt.shape
    ms, dec_seq, _ = x_sample.shape
    assert dec_seq == 1
    p = d // 2
    alpha = (2 * depth) ** 0.25

    prompt_row = ms
    pad = (-(ms + nb)) % 8
    c_all = jnp.concatenate([c_sample, c_prompt, jnp.zeros((pad, d), F32)], axis=0)

    xp = x_prompt.reshape(nb * seq, d)
    xs = x_sample.reshape(ms, d)
    pool_p, conv_p, pool_s, conv_s = [], [], [], []
    for l in range(depth):
        ada_in = _ada(c_all, w_ada[l], b_ada[l], ncols=ADA_EARLY * d)
        pscale = pool_scale[l].reshape(1, p)
        lng1, lnb1 = ln1_g[l].reshape(1, d), ln1_b[l].reshape(1, d)
        lng2, lnb2 = ln2_g[l].reshape(1, d), ln2_b[l].reshape(1, d)

        n_in = w_in.shape[2]
        (pa_p, pa_s, u_bf, us_bf), _ = _inproj(xp, xs, ada_in, w_in[l], [], col0=0, ncols=p,
                                               modulate=True, seq=seq, prompt_row=prompt_row)
        (pb_p, pb_s, ada), (grp_bf, wpu_bf, wcu_bf, wo_bf) = _inproj_wide(
            u_bf, us_bf, w_in[l], c_all, w_ada[l], b_ada[l],
            [pool_grp_w[l].reshape(p, -1), w_pool_up[l], w_conv_up[l], w_o[l]],
            col0=p, ncols=n_in - p, ada_col0=ADA_EARLY * d, tm=INPROJ_BF16_TOKEN_TILE,
            tn=INPROJ_WIDE_COLS)

        x1p, u2p, pst, cst, w1b, w2b = _mixer(
            pa_p, pb_p, xp, ada, grp_bf, pscale, conv_w[l], wpu_bf, wcu_bf, wo_bf, lng1, lnb1,
            w_ff1[l], w_ff2[l], seq=seq, prompt_row=prompt_row, alpha=alpha, tf=MLP_FF_CHUNK)
        x1s, u2s, npool, nconv = _mixer_step(
            pa_s, pb_s, jnp.transpose(state_pool[l], (1, 0, 2)),
            jnp.transpose(state_conv[l], (1, 0, 2)), xs, ada, grp_bf, pscale, conv_w[l],
            wpu_bf, wcu_bf, wo_bf, lng1, lnb1, alpha=alpha)

        xp, xs = _mlp(u2p, x1p, u2s, x1s, ada, w1b, b_ff1[l], w2b, b_ff2[l], lng2, lnb2,
                      alpha=alpha, rows_per_cond=seq, cond_row=prompt_row,
                      tm=MLP_TOKEN_TILE)

        pool_p.append(pst[:, HIST_ROWS - POOL_HIST:, :])
        conv_p.append(cst[:, HIST_ROWS - CONV_HIST:, :])
        pool_s.append(jnp.transpose(npool, (1, 0, 2)))
        conv_s.append(jnp.transpose(nconv, (1, 0, 2)))

    return (xp.reshape(nb, seq, d), xs.reshape(ms, dec_seq, d),
            jnp.stack(pool_p), jnp.stack(conv_p), jnp.stack(pool_s), jnp.stack(conv_s))
```

```python
import functools

import jax
import jax.numpy as jnp
from jax import lax
from jax.experimental import pallas as pl
from jax.experimental.pallas import tpu as pltpu

F32 = jnp.float32
BF16 = jnp.bfloat16
U32 = jnp.uint32

POOL_WINDOWS = (2, 4, 8, 16)
POOL_HIST = max(POOL_WINDOWS) - 1
CONV_K = 3
CONV_HIST = CONV_K - 1
ADA_EARLY = 2
PAST_LEN = 16384
LN_EPS = 1e-5

VMEM_LIMIT_BYTES_V7X = 56 * 1024 * 1024
VMEM_LIMIT_BIG_TILE_BYTES_V7X = 60 * 1024 * 1024
MLP_VMEM_LIMIT_BYTES_V7X = 62 * 1024 * 1024
HIST_ROWS = 16
ROW_CHUNK = 256
MIXER_COL_BLOCK = 256
MIXER_FINISH_SLABS = 4
INPROJ_BF16_TOKEN_TILE = 2048
INPROJ_WIDE_COLS = 1024
SIDE_PIECE_ROWS = 128
MLP_TOKEN_TILE = 1024
MLP_FF_CHUNK = 2048
MLP_ROW_CHUNK = 512


def _resident(shape):
    zeros = (0,) * len(shape)
    return pl.BlockSpec(shape, lambda *_: zeros, pipeline_mode=pl.Buffered(1))


def _params(semantics, vmem_limit_bytes=VMEM_LIMIT_BYTES_V7X):
    return pltpu.CompilerParams(dimension_semantics=semantics,
                                vmem_limit_bytes=vmem_limit_bytes)


def _layernorm(xf, g, b):
    mu = jnp.mean(xf, axis=-1, keepdims=True)
    var = jnp.mean(jnp.square(xf - mu), axis=-1, keepdims=True)
    return (xf - mu) * lax.rsqrt(var + LN_EPS) * g + b


def _ada_body(c_ref, w_ref, b_ref, o_ref):
    o_ref[...] = jnp.dot(c_ref[...].astype(BF16), w_ref[...].astype(BF16),
                         preferred_element_type=F32) + b_ref[...]


def _ada(c_all, w_ada, b_ada, *, ncols, tn=1024):
    m, d = c_all.shape
    n = ncols
    assert n % tn == 0
    return pl.pallas_call(
        _ada_body,
        out_shape=jax.ShapeDtypeStruct((m, n), F32),
        grid=(n // tn,),
        in_specs=[pl.BlockSpec((m, d), lambda j: (0, 0)),
                  pl.BlockSpec((d, tn), lambda j: (0, j)),
                  pl.BlockSpec((1, tn), lambda j: (0, j))],
        out_specs=pl.BlockSpec((m, tn), lambda j: (0, j)),
        compiler_params=pltpu.CompilerParams(
            dimension_semantics=("arbitrary",), vmem_limit_bytes=VMEM_LIMIT_BYTES_V7X,
            allow_input_fusion=[True, False, False]),
        name="ada",
    )(c_all, w_ada, b_ada.reshape(1, -1))


def _pack_rows(x_bf16):
    return pltpu.bitcast(x_bf16, U32)


def _unpack_rows(x_u32):
    return pltpu.bitcast(x_u32, BF16)


def _inproj_first_body(x_ref, adap_ref, w_ref, xs_ref, adas_ref, o_ref, os_ref, u_ref, usb_ref,
                       wb_ref, *, tiles_per_seq, d, tm):
    i = pl.program_id(1)

    @pl.when(i == 0)
    def _():
        wb_ref[...] = w_ref[...].astype(BF16)
        us = (xs_ref[...] * (1.0 + adas_ref[:, d:2 * d]) + adas_ref[:, 0:d]).astype(BF16)
        os_ref[...] = jnp.dot(us, wb_ref[...], preferred_element_type=F32)
        usb_ref[...] = us

    b = i // tiles_per_seq
    sh1 = adap_ref[pl.ds(b, 1), 0:d]
    sc1 = adap_ref[pl.ds(b, 1), d:2 * d]
    for r in range(0, tm, ROW_CHUNK):
        rows = slice(r, r + ROW_CHUNK)
        u = (x_ref[rows, :] * (1.0 + sc1) + sh1).astype(BF16)
        u_ref[rows, :] = u
        o_ref[rows, :] = jnp.dot(u, wb_ref[...], preferred_element_type=F32).astype(BF16)


def _inproj_first(x, xs, ada, w_in, *, ncols, seq, prompt_row, tm=1024, tn=1024):
    m, d = x.shape
    ms = xs.shape[0]
    assert seq % tm == 0 and m % seq == 0 and tm % ROW_CHUNK == 0
    assert ncols % tn == 0 and prompt_row % 8 == 0
    body = functools.partial(_inproj_first_body, tiles_per_seq=seq // tm, d=d, tm=tm)
    return pl.pallas_call(
        body,
        out_shape=[jax.ShapeDtypeStruct((m, ncols), BF16),
                   jax.ShapeDtypeStruct((ms, ncols), F32),
                   jax.ShapeDtypeStruct((m, d), BF16),
                   jax.ShapeDtypeStruct((ms, d), BF16)],
        grid=(ncols // tn, m // tm),
        in_specs=[pl.BlockSpec((tm, d), lambda j, i: (i, 0)),
                  pl.BlockSpec((8, 2 * d), lambda j, i: (prompt_row // 8, 0)),
                  pl.BlockSpec((d, tn), lambda j, i: (0, j)),
                  _resident((ms, d)),
                  _resident((ms, 2 * d))],
        out_specs=[pl.BlockSpec((tm, tn), lambda j, i: (i, j)),
                   pl.BlockSpec((ms, tn), lambda j, i: (0, j)),
                   pl.BlockSpec((tm, d), lambda j, i: (i, 0)),
                   _resident((ms, d))],
        scratch_shapes=[pltpu.VMEM((d, tn), BF16)],
        compiler_params=_params(("arbitrary", "arbitrary"), VMEM_LIMIT_BIG_TILE_BYTES_V7X),
        name="inproj_first",
    )(x, ada, w_in, xs, ada)


def _inproj_wide_body(u_ref, us_ref, w_hbm, c_ref, wada_ref, bada_ref, *rest, tm, tn, col0,
                      n_side, side_steps):
    side_in = rest[:n_side]
    o_ref, os_ref, adab_ref = rest[n_side:n_side + 3]
    side_out = rest[n_side + 3:2 * n_side + 3]
    wf_ref, wb_ref, sem = rest[2 * n_side + 3:]
    j = pl.program_id(0)
    i = pl.program_id(1)

    def w_copy(jj):
        return pltpu.make_async_copy(w_hbm.at[:, pl.ds(col0 + jj * tn, tn)], wf_ref, sem.at[0])

    @pl.when(jnp.logical_and(j == 0, i == 0))
    def _():
        w_copy(0).start()

    @pl.when(i == 0)
    def _():
        w_copy(j).wait()
        wb_ref[...] = wf_ref[...].astype(BF16)
        os_ref[...] = jnp.dot(us_ref[...], wb_ref[...], preferred_element_type=F32)

    @pl.when(jnp.logical_and(i == 1, j + 1 < pl.num_programs(0)))
    def _():
        w_copy(j + 1).start()

    def cast_piece(src, dst, lo, hi):
        dst[lo // 2:hi // 2, :] = _pack_rows(src[lo:hi, :].astype(BF16))

    def ada_piece(lo, hi):
        adab_ref[:, lo:hi] = jnp.dot(c_ref[...].astype(BF16), wada_ref[:, lo:hi].astype(BF16),
                                     preferred_element_type=F32) + bada_ref[:, lo:hi]

    pieces = []
    for src, dst in zip(side_in, side_out):
        rows = src.shape[0]
        nr = SIDE_PIECE_ROWS if rows % SIDE_PIECE_ROWS == 0 else rows
        pieces += [functools.partial(cast_piece, src, dst, lo, lo + nr)
                   for lo in range(0, rows, nr)]
    half = adab_ref.shape[1] // 2
    pieces += [functools.partial(ada_piece, 0, half),
               functools.partial(ada_piece, half, 2 * half)]
    n_chunks = tm // ROW_CHUNK

    def main(with_side):
        for c in range(n_chunks):
            rows = slice(c * ROW_CHUNK, (c + 1) * ROW_CHUNK)
            o_ref[rows, :] = jnp.dot(u_ref[rows, :], wb_ref[...],
                                     preferred_element_type=F32).astype(BF16)
            if with_side:
                for piece in pieces[c::n_chunks]:
                    piece()

    has_side = j * pl.num_programs(1) + i < side_steps
    pl.when(has_side)(functools.partial(main, True))
    pl.when(jnp.logical_not(has_side))(functools.partial(main, False))


def _inproj_wide(u, us, w_in, c_all, w_ada, b_ada, side_ws, *, col0, ncols, ada_col0, tm, tn,
                 side_steps=16):
    m, d = u.shape
    ms = us.shape[0]
    mc = c_all.shape[0]
    ni = m // tm
    n_ada = w_ada.shape[1] - ada_col0
    ada_tn = n_ada // side_steps
    assert m % tm == 0 and ni >= 2 and tm % ROW_CHUNK == 0 and ncols % tn == 0
    assert ncols // tn * ni >= side_steps and n_ada % side_steps == 0
    assert ada_tn % 256 == 0 and ada_col0 % ada_tn == 0
    body = functools.partial(_inproj_wide_body, tm=tm, tn=tn, col0=col0,
                             n_side=len(side_ws), side_steps=side_steps)
    side_step = lambda j, i: jnp.minimum(j * ni + i, side_steps - 1)
    side_idx = lambda j, i: (side_step(j, i), 0)
    ada_in_idx = lambda j, i: (0, ada_col0 // ada_tn + side_step(j, i))
    side_in = [pl.BlockSpec((w.shape[0] // side_steps, w.shape[1]), side_idx) for w in side_ws]
    side_out = [pl.BlockSpec((w.shape[0] // side_steps // 2, w.shape[1]), side_idx)
                for w in side_ws]
    side_shape = [jax.ShapeDtypeStruct((w.shape[0] // 2, w.shape[1]), U32) for w in side_ws]
    out = pl.pallas_call(
        body,
        out_shape=[jax.ShapeDtypeStruct((m, ncols), BF16),
                   jax.ShapeDtypeStruct((ms, ncols), F32),
                   jax.ShapeDtypeStruct((mc, n_ada), F32)] + side_shape,
        grid=(ncols // tn, ni),
        in_specs=[pl.BlockSpec((tm, d), lambda j, i: (i, 0)),
                  _resident((ms, d)),
                  pl.BlockSpec(memory_space=pl.ANY),
                  _resident((mc, d)),
                  pl.BlockSpec((d, ada_tn), ada_in_idx),
                  pl.BlockSpec((1, ada_tn), ada_in_idx)] + side_in,
        out_specs=[pl.BlockSpec((tm, tn), lambda j, i: (i, j)),
                   pl.BlockSpec((ms, tn), lambda j, i: (0, j)),
                   pl.BlockSpec((mc, ada_tn), lambda j, i: (0, side_step(j, i)))] + side_out,
        scratch_shapes=[pltpu.VMEM((d, tn), F32),
                        pltpu.VMEM((d, tn), BF16),
                        pltpu.SemaphoreType.DMA((1,))],
        compiler_params=_params(("arbitrary", "arbitrary"), MLP_VMEM_LIMIT_BYTES_V7X),
        name="inproj_rest",
    )(u, us, w_in, c_all, w_ada, b_ada.reshape(1, -1), *side_ws)
    return out[:3], out[3:]


def _sigmoid(x):
    return 0.5 * jnp.tanh(0.5 * x) + 0.5


def _mix_tail(pooled_bf, conv_bf, gp, gc, x, g1, sc2, sh2, wpu_ref, wcu_ref, wo_ref,
              lng_ref, lnb_ref, alpha):
    y_p = jnp.dot(pooled_bf, _unpack_rows(wpu_ref[...]), preferred_element_type=F32)
    y_c = jnp.dot(conv_bf, _unpack_rows(wcu_ref[...]), preferred_element_type=F32)
    mixed = (_sigmoid(gp) * y_p + _sigmoid(gc) * y_c).astype(BF16)
    mo = jnp.dot(mixed, _unpack_rows(wo_ref[...]), preferred_element_type=F32)
    x1 = _layernorm(alpha * x + g1 * mo, lng_ref[...], lnb_ref[...])
    return x1, (x1 * (1.0 + sc2) + sh2).astype(BF16)


def _split_cols(first_ref, rest_ref):
    p = first_ref.shape[-1]

    def cols(lo, hi):
        return first_ref[:, lo:hi] if hi <= p else rest_ref[:, lo - p:hi - p]
    return cols


def _mixer_body(pa_ref, pb_ref, ha_ref, hb_ref, x_ref, ada_ref, grp_ref, pscale_ref, convw_ref,
                wpu_ref, wcu_ref, wo_ref, lng_ref, lnb_ref, w1_ref, w2_ref,
                x1_ref, u2_ref, pst_ref, cst_ref, w1b_ref, w2b_ref,
                zz_ref, vv_ref, pm_ref, bc_ref, mixed_ref,
                *, tm, tiles_per_seq, d, alpha, steps):
    i = pl.program_id(0)
    p = d // 2
    gw = p // len(POOL_WINDOWS)
    h = HIST_ROWS
    proj = _split_cols(pa_ref, pb_ref)
    hist = _split_cols(ha_ref, hb_ref)

    cb = MIXER_COL_BLOCK
    ncb = d // cb
    tis = i % tiles_per_seq
    first = tis == 0

    def side_cast():
        tf = w1b_ref.shape[2]
        for fi in range(w1b_ref.shape[0]):
            w1b_ref[fi] = _pack_rows(w1_ref[:, fi * tf:(fi + 1) * tf].astype(BF16))
        w2b_ref[...] = _pack_rows(w2_ref[...].astype(BF16))

    def setup():
        zz_ref[0:h, :] = jnp.where(first, 0.0, hist(0, p).astype(F32))
        zz_ref[h:h + tm, :] = proj(0, p).astype(F32)
        vh = hist(3 * p, 4 * p).astype(F32) * hist(p, 2 * p).astype(F32)
        vv_ref[0:h, :] = jnp.where(first, 0.0, vh)
        vv_ref[h:h + tm, :] = proj(3 * p, 4 * p).astype(F32) * proj(p, 2 * p).astype(F32)
        pst_ref[0] = zz_ref[tm:tm + h, :]
        cst_ref[0] = vv_ref[tm:tm + h, :]

    def pool_group(g):
        w = POOL_WINDOWS[g]
        cols = slice(g * gw, (g + 1) * gw)
        pos = tis * tm + lax.broadcasted_iota(jnp.int32, (tm, 1), 0)
        s = zz_ref[:, cols]
        zt = s[h:, :]
        k = 1
        while k < w:
            s = s + pltpu.roll(s, k, axis=0)
            k *= 2
        inv_cnt = 1.0 / jnp.minimum(w, pos + 1).astype(F32)
        pooled = (s[h:, :] * inv_cnt - zt).astype(BF16)
        og = jnp.dot(pooled, _unpack_rows(grp_ref[g * gw // 2:(g + 1) * gw // 2, :]),
                     preferred_element_type=F32)
        pm_ref[:, cols] = (og * pscale_ref[:, cols]).astype(BF16)

    def conv_branch():
        conv = convw_ref[0:1, :] * vv_ref[h - 2:h - 2 + tm, :]
        conv = conv + convw_ref[1:2, :] * vv_ref[h - 1:h - 1 + tm, :]
        conv = conv + convw_ref[2:3, :] * vv_ref[h:h + tm, :]
        bc_ref[...] = (proj(2 * p, 3 * p).astype(F32) * conv).astype(BF16)

    def up_and_gate(k):
        cols = slice(k * cb, (k + 1) * cb)
        y_p = jnp.dot(pm_ref[...], _unpack_rows(wpu_ref[:, cols]), preferred_element_type=F32)
        y_c = jnp.dot(bc_ref[...], _unpack_rows(wcu_ref[:, cols]), preferred_element_type=F32)
        gp = proj(4 * p + k * cb, 4 * p + (k + 1) * cb).astype(F32)
        gc = proj(4 * p + d + k * cb, 4 * p + d + (k + 1) * cb).astype(F32)
        mixed_ref[:, cols] = (_sigmoid(gp) * y_p + _sigmoid(gc) * y_c).astype(BF16)

    def out_block(k):
        return jnp.dot(mixed_ref[...], _unpack_rows(wo_ref[:, k * cb:(k + 1) * cb]),
                       preferred_element_type=F32)

    def finish_rows(mo, r):
        nr = tm // MIXER_FINISH_SLABS
        rows = slice(r * nr, (r + 1) * nr)
        b = (i - 1) // tiles_per_seq
        g1 = ada_ref[pl.ds(b, 1), 0:d]
        sh2 = ada_ref[pl.ds(b, 1), d:2 * d]
        sc2 = ada_ref[pl.ds(b, 1), 2 * d:3 * d]
        x1 = _layernorm(alpha * x_ref[rows, :] + g1 * mo[rows, :], lng_ref[...], lnb_ref[...])
        x1_ref[rows, :] = x1
        u2_ref[rows, :] = (x1 * (1.0 + sc2) + sh2).astype(BF16)

    def step(has_head, has_tail):
        front = [side_cast, setup] + [functools.partial(pool_group, g)
                                      for g in range(len(POOL_WINDOWS))] + [conv_branch]
        mo_blocks = []
        for k in range(ncb):
            if has_tail:
                mo_blocks.append(out_block(k))
            if has_head and k < len(front):
                front[k]()
        if has_head:
            for piece in front[ncb:]:
                piece()
        mo = jnp.concatenate(mo_blocks, axis=1) if has_tail else None
        fin_at = {(r + 1) * ncb // MIXER_FINISH_SLABS - 1: r for r in range(MIXER_FINISH_SLABS)}
        for k in range(ncb):
            if has_head:
                up_and_gate(k)
            if has_tail and k in fin_at:
                finish_rows(mo, fin_at[k])

    pl.when(i == 0)(functools.partial(step, True, False))
    pl.when(jnp.logical_and(i > 0, i < steps))(functools.partial(step, True, True))
    pl.when(i == steps)(functools.partial(step, False, True))


def _mixer(proj_a, proj_b, xp, ada, grp_bf, pscale, convw, wpu_bf, wcu_bf, wo_bf, lng, lnb,
           w1, w2, *, seq, prompt_row, alpha, tf, tm=256):
    m, d = xp.shape
    p = d // 2
    nseq = m // seq
    h = HIST_ROWS
    assert seq % tm == 0 and m % seq == 0 and tm % h == 0 and d % MIXER_COL_BLOCK == 0
    assert proj_a.shape[1] == p and tm % MIXER_FINISH_SLABS == 0
    steps = m // tm
    dff = w1.shape[1]
    assert d % steps == 0 and dff % steps == 0 and dff % tf == 0
    body = functools.partial(_mixer_body, tm=tm, tiles_per_seq=seq // tm, d=d, alpha=alpha,
                             steps=steps)
    cur = lambda i: jnp.minimum(i, steps - 1)
    prev = lambda i: jnp.maximum(i - 1, 0)
    hist_idx = lambda i: (jnp.maximum(cur(i) * (tm // h) - 1, 0), 0)
    return pl.pallas_call(
        body,
        out_shape=[jax.ShapeDtypeStruct((m, d), F32),
                   jax.ShapeDtypeStruct((m, d), BF16),
                   jax.ShapeDtypeStruct((nseq, h, p), F32),
                   jax.ShapeDtypeStruct((nseq, h, p), F32),
                   jax.ShapeDtypeStruct((dff // tf, d // 2, tf), U32),
                   jax.ShapeDtypeStruct((dff // 2, d), U32)],
        grid=(steps + 1,),
        in_specs=[pl.BlockSpec((tm, proj_a.shape[1]), lambda i: (cur(i), 0)),
                  pl.BlockSpec((tm, proj_b.shape[1]), lambda i: (cur(i), 0)),
                  pl.BlockSpec((h, proj_a.shape[1]), hist_idx),
                  pl.BlockSpec((h, proj_b.shape[1]), hist_idx),
                  pl.BlockSpec((tm, d), lambda i: (prev(i), 0)),
                  pl.BlockSpec((8, ada.shape[1]), lambda i: (prompt_row // 8, 0)),
                  _resident(grp_bf.shape),
                  _resident(pscale.shape),
                  _resident(convw.shape),
                  _resident(wpu_bf.shape),
                  _resident(wcu_bf.shape),
                  _resident(wo_bf.shape),
                  _resident(lng.shape),
                  _resident(lnb.shape),
                  pl.BlockSpec((d // steps, dff), lambda i: (cur(i), 0)),
                  pl.BlockSpec((dff // steps, d), lambda i: (cur(i), 0))],
        out_specs=[pl.BlockSpec((tm, d), lambda i: (prev(i), 0)),
                   pl.BlockSpec((tm, d), lambda i: (prev(i), 0)),
                   pl.BlockSpec((1, h, p), lambda i: (cur(i) // (seq // tm), 0, 0)),
                   pl.BlockSpec((1, h, p), lambda i: (cur(i) // (seq // tm), 0, 0)),
                   pl.BlockSpec((dff // tf, d // steps // 2, tf), lambda i: (0, cur(i), 0)),
                   pl.BlockSpec((dff // steps // 2, d), lambda i: (cur(i), 0))],
        scratch_shapes=[pltpu.VMEM((tm + h, p), F32),
                        pltpu.VMEM((tm + h, p), F32),
                        pltpu.VMEM((tm, p), BF16),
                        pltpu.VMEM((tm, p), BF16),
                        pltpu.VMEM((tm, d), BF16)],
        compiler_params=_params(("arbitrary",)),
        name="mixer",
    )(proj_a, proj_b, proj_a, proj_b, xp, ada, grp_bf, pscale, convw, wpu_bf, wcu_bf, wo_bf,
      lng, lnb, w1, w2)


def _mixer_step_body(pa_ref, pb_ref, pool_ref, cstate_ref, x_ref, ada_ref, grp_ref, pscale_ref,
                     convw_ref, wpu_ref, wcu_ref, wo_ref, lng_ref, lnb_ref,
                     x1_ref, u2_ref, npool_ref, nconv_ref, pm_ref, *, d, alpha):
    p = d // 2
    gw = p // len(POOL_WINDOWS)
    proj = _split_cols(pa_ref, pb_ref)
    z = proj(0, p)
    for g, w in enumerate(POOL_WINDOWS):
        cols = slice(g * gw, (g + 1) * gw)
        s = z[:, cols]
        for k in range(1, w):
            s = s + pool_ref[POOL_HIST - k, :, cols]
        inv_cnt = 1.0 / min(w, PAST_LEN + 1)
        pooled = (s * inv_cnt - z[:, cols]).astype(BF16)
        og = jnp.dot(pooled, _unpack_rows(grp_ref[g * gw // 2:(g + 1) * gw // 2, :]),
                         preferred_element_type=F32)
        pm_ref[:, cols] = (og * pscale_ref[:, cols]).astype(BF16)
    for r in range(POOL_HIST - 1):
        npool_ref[r] = pool_ref[r + 1]
    npool_ref[POOL_HIST - 1] = z

    xc = proj(p, 2 * p)
    bc = proj(2 * p, 3 * p)
    cc = proj(3 * p, 4 * p)
    v = cc * xc
    conv = convw_ref[0:1, :] * cstate_ref[0]
    conv = conv + convw_ref[1:2, :] * cstate_ref[1]
    conv = conv + convw_ref[2:3, :] * v
    nconv_ref[0] = cstate_ref[1]
    nconv_ref[1] = v

    gp = proj(4 * p, 4 * p + d)
    gc = proj(4 * p + d, 4 * p + 2 * d)
    g1 = ada_ref[:, 0:d]
    sh2 = ada_ref[:, d:2 * d]
    sc2 = ada_ref[:, 2 * d:3 * d]
    x1, u2 = _mix_tail(pm_ref[...], (bc * conv).astype(BF16), gp, gc, x_ref[...], g1, sc2, sh2,
                       wpu_ref, wcu_ref, wo_ref, lng_ref, lnb_ref, alpha)
    x1_ref[...] = x1
    u2_ref[...] = u2


def _mixer_step(proj_a, proj_b, pool_state, conv_state, xs, ada, grp_bf, pscale, convw,
                wpu_bf, wcu_bf, wo_bf, lng, lnb, *, alpha):
    ms, d = xs.shape
    p = d // 2
    body = functools.partial(_mixer_step_body, d=d, alpha=alpha)
    return pl.pallas_call(
        body,
        out_shape=[jax.ShapeDtypeStruct((ms, d), F32),
                   jax.ShapeDtypeStruct((ms, d), BF16),
                   jax.ShapeDtypeStruct(pool_state.shape, F32),
                   jax.ShapeDtypeStruct(conv_state.shape, F32)],
        grid=(1,),
        in_specs=[_resident(proj_a.shape),
                  _resident(proj_b.shape),
                  _resident(pool_state.shape),
                  _resident(conv_state.shape),
                  _resident((ms, d)),
                  _resident((ms, ada.shape[1])),
                  _resident(grp_bf.shape),
                  _resident(pscale.shape),
                  _resident(convw.shape),
                  _resident(wpu_bf.shape),
                  _resident(wcu_bf.shape),
                  _resident(wo_bf.shape),
                  _resident(lng.shape),
                  _resident(lnb.shape)],
        out_specs=[_resident((ms, d)),
                   _resident((ms, d)),
                   _resident(pool_state.shape),
                   _resident(conv_state.shape)],
        scratch_shapes=[pltpu.VMEM((ms, p), BF16)],
        compiler_params=_params(("arbitrary",)),
        name="mixer_step",
    )(proj_a, proj_b, pool_state, conv_state, xs, ada, grp_bf, pscale, convw, wpu_bf, wcu_bf,
      wo_bf, lng, lnb)


def _mlp_body(u2_ref, w1_hbm, b1_ref, w2_hbm, x1_ref, g2_ref, b2_ref, lng_ref, lnb_ref,
              u2s_ref, x1s_ref, g2s_ref, o_ref, os_ref,
              acc_ref, accs_ref, w1_buf, w2_buf, h_buf, hs_buf, sem,
              *, nm, nf, te, tm, alpha, rows_per_cond):
    m = pl.program_id(0)
    f = pl.program_id(1)
    slot = m % 2
    t = m * nf + f
    w2_rows = w2_buf.shape[0]
    n_chunks = tm // MLP_ROW_CHUNK
    tp = te // n_chunks

    def w1_copy(fi):
        return pltpu.make_async_copy(w1_hbm.at[fi], w1_buf, sem.at[0])

    def w2_copy(fi):
        return pltpu.make_async_copy(w2_hbm.at[pl.ds(fi * w2_rows, w2_rows), :], w2_buf,
                                     sem.at[1])

    has_main = m < nm

    @pl.when(t == 0)
    def _():
        w1_copy(0).start()

    @pl.when(has_main)
    def _():
        w2_copy(f).start()
        w1_copy(f).wait()

    def hidden(u2):
        h = jnp.dot(u2, _unpack_rows(w1_buf[...]), preferred_element_type=F32) + b1_ref[f]
        return jnp.square(jnp.maximum(h, 0.0)).astype(BF16)

    def finish(x1, g2, acc):
        return _layernorm(alpha * x1 + g2 * (acc + b2_ref[...]), lng_ref[...], lnb_ref[...])

    def epilogue(c):
        part_rows = pl.ds(pl.multiple_of(c * tp, tp), tp)
        base = pl.multiple_of(f * te + c * tp, tp)
        g2 = g2_ref[pl.ds(((m - 1) * tm) // rows_per_cond, 1), :]
        o_ref[part_rows, :] = finish(x1_ref[part_rows, :], g2,
                                     acc_ref[1 - slot, pl.ds(base, tp), :])

    def main(first, singles, with_epilogue):
        def phase_a(c, carry):
            if with_epilogue:
                epilogue(c)
            rows = pl.ds(pl.multiple_of(c * MLP_ROW_CHUNK, MLP_ROW_CHUNK), MLP_ROW_CHUNK)
            h_buf[rows, :] = hidden(u2_ref[rows, :])
            return carry

        lax.fori_loop(0, n_chunks, phase_a, 0)
        if singles:
            hs_buf[...] = hidden(u2s_ref[...])

        @pl.when(t + 1 < nm * nf)
        def _():
            w1_copy(jnp.where(f + 1 == nf, 0, f + 1)).start()

        w2_copy(f).wait()

        def phase_b(c, carry):
            rows = pl.ds(pl.multiple_of(c * MLP_ROW_CHUNK, MLP_ROW_CHUNK), MLP_ROW_CHUNK)
            part = jnp.dot(h_buf[rows, :], _unpack_rows(w2_buf[...]),
                           preferred_element_type=F32)
            if first:
                acc_ref[slot, rows, :] = part
            else:
                acc_ref[slot, rows, :] += part
            return carry

        lax.fori_loop(0, n_chunks, phase_b, 0)
        if singles:
            part = jnp.dot(hs_buf[...], _unpack_rows(w2_buf[...]), preferred_element_type=F32)
            if first:
                accs_ref[...] = part
            else:
                accs_ref[...] += part

    tile0 = m == 0
    later = jnp.logical_and(has_main, m > 0)
    pl.when(jnp.logical_and(tile0, f == 0))(functools.partial(main, True, True, False))
    pl.when(jnp.logical_and(tile0, f > 0))(functools.partial(main, False, True, False))
    pl.when(jnp.logical_and(later, f == 0))(functools.partial(main, True, False, True))
    pl.when(jnp.logical_and(later, f > 0))(functools.partial(main, False, False, True))

    @pl.when(m == nm)
    def _():
        lax.fori_loop(0, n_chunks, lambda c, carry: (epilogue(c), carry)[1], 0)

    @pl.when(jnp.logical_and(m == 1, f == 0))
    def _():
        os_ref[...] = finish(x1s_ref[...], g2s_ref[...], accs_ref[...])


def _mlp(u2, x1, u2s, x1s, ada, w1b, b1, w2b, b2, lng, lnb, *, alpha, rows_per_cond,
         cond_row, tm):
    m, d = u2.shape
    ms = u2s.shape[0]
    nf, _, tf = w1b.shape
    nm = m // tm
    te = tm // nf
    assert m % tm == 0 and tm % nf == 0 and tm % MLP_ROW_CHUNK == 0
    assert rows_per_cond % tm == 0 and te % (tm // MLP_ROW_CHUNK) == 0 and cond_row % 8 == 0
    g2_col = ada.shape[1] // d - 1
    body = functools.partial(_mlp_body, nm=nm, nf=nf, te=te, tm=tm, alpha=alpha,
                             rows_per_cond=rows_per_cond)
    ep = lambda mi, f: (jnp.where(mi == 0, 0, (mi - 1) * nf + f), 0)
    return pl.pallas_call(
        body,
        out_shape=[jax.ShapeDtypeStruct((m, d), F32),
                   jax.ShapeDtypeStruct((ms, d), F32)],
        grid=(nm + 1, nf),
        in_specs=[pl.BlockSpec((tm, d), lambda mi, f: (jnp.minimum(mi, nm - 1), 0)),
                  pl.BlockSpec(memory_space=pl.ANY),
                  _resident((nf, 1, tf)),
                  pl.BlockSpec(memory_space=pl.ANY),
                  pl.BlockSpec((te, d), ep),
                  pl.BlockSpec((8, d), lambda mi, f: (cond_row // 8, g2_col)),
                  _resident((1, d)),
                  _resident((1, d)),
                  _resident((1, d)),
                  _resident((ms, d)),
                  _resident((ms, d)),
                  pl.BlockSpec((ms, d), lambda mi, f: (0, g2_col),
                               pipeline_mode=pl.Buffered(1))],
        out_specs=[pl.BlockSpec((te, d), ep),
                   _resident((ms, d))],
        scratch_shapes=[pltpu.VMEM((2, tm, d), F32),
                        pltpu.VMEM((ms, d), F32),
                        pltpu.VMEM((d // 2, tf), U32),
                        pltpu.VMEM((tf // 2, d), U32),
                        pltpu.VMEM((tm, tf), BF16),
                        pltpu.VMEM((ms, tf), BF16),
                        pltpu.SemaphoreType.DMA((2,))],
        compiler_params=_params(("arbitrary", "arbitrary"), MLP_VMEM_LIMIT_BYTES_V7X),
        name="mlp",
    )(u2, w1b, b1.reshape(nf, 1, tf), w2b, x1, ada, b2.reshape(1, d), lng, lnb, u2s, x1s, ada)


def kernel(x_prompt, x_sample, state_pool, state_conv, c_prompt, c_sample, w_ada, b_ada, w_in,
           pool_grp_w, pool_scale, conv_w, w_pool_up, w_conv_up, w_o, ln1_g, ln1_b, w_ff1,
           b_ff1, w_ff2, b_ff2, ln2_g, ln2_b):
    depth = w_ada.shape[0]
    nb, seq, d = x_prompt.shape
    ms, dec_seq, _ = x_sample.shape
    assert dec_seq == 1
    p = d // 2
    alpha = (2 * depth) ** 0.25

    prompt_row = ms
    pad = (-(ms + nb)) % 8
    c_all = jnp.concatenate([c_sample, c_prompt, jnp.zeros((pad, d), F32)], axis=0)

    xp = x_prompt.reshape(nb * seq, d)
    xs = x_sample.reshape(ms, d)
    pool_p, conv_p, pool_s, conv_s = [], [], [], []
    for l in range(depth):
        ada_in = _ada(c_all, w_ada[l], b_ada[l], ncols=ADA_EARLY * d)
        pscale = pool_scale[l].reshape(1, p)
        lng1, lnb1 = ln1_g[l].reshape(1, d), ln1_b[l].reshape(1, d)
        lng2, lnb2 = ln2_g[l].reshape(1, d), ln2_b[l].reshape(1, d)

        n_in = w_in.shape[2]
        pa_p, pa_s, u_bf, us_bf = _inproj_first(xp, xs, ada_in, w_in[l], ncols=p, seq=seq,
                                                prompt_row=prompt_row)
        (pb_p, pb_s, ada), (grp_bf, wpu_bf, wcu_bf, wo_bf) = _inproj_wide(
            u_bf, us_bf, w_in[l], c_all, w_ada[l], b_ada[l],
            [pool_grp_w[l].reshape(p, -1), w_pool_up[l], w_conv_up[l], w_o[l]],
            col0=p, ncols=n_in - p, ada_col0=ADA_EARLY * d, tm=INPROJ_BF16_TOKEN_TILE,
            tn=INPROJ_WIDE_COLS)

        x1p, u2p, pst, cst, w1b, w2b = _mixer(
            pa_p, pb_p, xp, ada, grp_bf, pscale, conv_w[l], wpu_bf, wcu_bf, wo_bf, lng1, lnb1,
            w_ff1[l], w_ff2[l], seq=seq, prompt_row=prompt_row, alpha=alpha, tf=MLP_FF_CHUNK)
        x1s, u2s, npool, nconv = _mixer_step(
            pa_s, pb_s, jnp.transpose(state_pool[l], (1, 0, 2)),
            jnp.transpose(state_conv[l], (1, 0, 2)), xs, ada, grp_bf, pscale, conv_w[l],
            wpu_bf, wcu_bf, wo_bf, lng1, lnb1, alpha=alpha)

        xp, xs = _mlp(u2p, x1p, u2s, x1s, ada, w1b, b_ff1[l], w2b, b_ff2[l], lng2, lnb2,
                      alpha=alpha, rows_per_cond=seq, cond_row=prompt_row,
                      tm=MLP_TOKEN_TILE)

        pool_p.append(pst[:, HIST_ROWS - POOL_HIST:, :])
        conv_p.append(cst[:, HIST_ROWS - CONV_HIST:, :])
        pool_s.append(jnp.transpose(npool, (1, 0, 2)))
        conv_s.append(jnp.transpose(nconv, (1, 0, 2)))

    return (xp.reshape(nb, seq, d), xs.reshape(ms, dec_seq, d),
            jnp.stack(pool_p), jnp.stack(conv_p), jnp.stack(pool_s), jnp.stack(conv_s))
```

```python
import functools

import jax
import jax.numpy as jnp
from jax import lax
from jax.experimental import pallas as pl
from jax.experimental.pallas import tpu as pltpu

F32 = jnp.float32
BF16 = jnp.bfloat16
U32 = jnp.uint32

POOL_WINDOWS = (2, 4, 8, 16)
POOL_HIST = max(POOL_WINDOWS) - 1
CONV_K = 3
CONV_HIST = CONV_K - 1
ADA_EARLY = 2
PAST_LEN = 16384
LN_EPS = 1e-5

VMEM_LIMIT_BYTES_V7X = 56 * 1024 * 1024
VMEM_LIMIT_BIG_TILE_BYTES_V7X = 60 * 1024 * 1024
MLP_VMEM_LIMIT_BYTES_V7X = 62 * 1024 * 1024
HIST_ROWS = 16
ROW_CHUNK = 256
MIXER_COL_BLOCK = 256
MIXER_FINISH_SLABS = 4
INPROJ_BF16_TOKEN_TILE = 2048
INPROJ_WIDE_COLS = 1024
SIDE_PIECE_ROWS = 128
WEIGHT_CAST_COLS = 256
MLP_TOKEN_TILE = 1024
MLP_FF_CHUNK = 2048
MLP_ROW_CHUNK = 512


def _resident(shape):
    zeros = (0,) * len(shape)
    return pl.BlockSpec(shape, lambda *_: zeros, pipeline_mode=pl.Buffered(1))


def _params(semantics, vmem_limit_bytes=VMEM_LIMIT_BYTES_V7X):
    return pltpu.CompilerParams(dimension_semantics=semantics,
                                vmem_limit_bytes=vmem_limit_bytes)


def _layernorm(xf, g, b):
    mu = jnp.mean(xf, axis=-1, keepdims=True)
    var = jnp.mean(jnp.square(xf - mu), axis=-1, keepdims=True)
    return (xf - mu) * lax.rsqrt(var + LN_EPS) * g + b


def _ada_body(c_ref, w_ref, b_ref, o_ref):
    o_ref[...] = jnp.dot(c_ref[...].astype(BF16), w_ref[...].astype(BF16),
                         preferred_element_type=F32) + b_ref[...]


def _ada(c_all, w_ada, b_ada, *, ncols, tn=1024):
    m, d = c_all.shape
    n = ncols
    assert n % tn == 0
    return pl.pallas_call(
        _ada_body,
        out_shape=jax.ShapeDtypeStruct((m, n), F32),
        grid=(n // tn,),
        in_specs=[pl.BlockSpec((m, d), lambda j: (0, 0)),
                  pl.BlockSpec((d, tn), lambda j: (0, j)),
                  pl.BlockSpec((1, tn), lambda j: (0, j))],
        out_specs=pl.BlockSpec((m, tn), lambda j: (0, j)),
        compiler_params=pltpu.CompilerParams(
            dimension_semantics=("arbitrary",), vmem_limit_bytes=VMEM_LIMIT_BYTES_V7X,
            allow_input_fusion=[True, False, False]),
        name="ada",
    )(c_all, w_ada, b_ada.reshape(1, -1))


def _pack_rows(x_bf16):
    return pltpu.bitcast(x_bf16, U32)


def _unpack_rows(x_u32):
    return pltpu.bitcast(x_u32, BF16)


def _inproj_first_body(x_ref, adap_ref, w_ref, xs_ref, adas_ref, o_ref, os_ref, u_ref, usb_ref,
                       wb_ref, *, tiles_per_seq, d, tm):
    i = pl.program_id(1)

    @pl.when(i == 0)
    def _():
        wb_ref[...] = w_ref[...].astype(BF16)
        us = (xs_ref[...] * (1.0 + adas_ref[:, d:2 * d]) + adas_ref[:, 0:d]).astype(BF16)
        os_ref[...] = jnp.dot(us, wb_ref[...], preferred_element_type=F32)
        usb_ref[...] = us

    b = i // tiles_per_seq
    sh1 = adap_ref[pl.ds(b, 1), 0:d]
    sc1 = adap_ref[pl.ds(b, 1), d:2 * d]
    for r in range(0, tm, ROW_CHUNK):
        rows = slice(r, r + ROW_CHUNK)
        u = (x_ref[rows, :] * (1.0 + sc1) + sh1).astype(BF16)
        u_ref[rows, :] = u
        o_ref[rows, :] = jnp.dot(u, wb_ref[...], preferred_element_type=F32).astype(BF16)


def _inproj_first(x, xs, ada, w_in, *, ncols, seq, prompt_row, tm=1024, tn=1024):
    m, d = x.shape
    ms = xs.shape[0]
    assert seq % tm == 0 and m % seq == 0 and tm % ROW_CHUNK == 0
    assert ncols % tn == 0 and prompt_row % 8 == 0
    body = functools.partial(_inproj_first_body, tiles_per_seq=seq // tm, d=d, tm=tm)
    return pl.pallas_call(
        body,
        out_shape=[jax.ShapeDtypeStruct((m, ncols), BF16),
                   jax.ShapeDtypeStruct((ms, ncols), F32),
                   jax.ShapeDtypeStruct((m, d), BF16),
                   jax.ShapeDtypeStruct((ms, d), BF16)],
        grid=(ncols // tn, m // tm),
        in_specs=[pl.BlockSpec((tm, d), lambda j, i: (i, 0)),
                  pl.BlockSpec((8, 2 * d), lambda j, i: (prompt_row // 8, 0)),
                  pl.BlockSpec((d, tn), lambda j, i: (0, j)),
                  _resident((ms, d)),
                  _resident((ms, 2 * d))],
        out_specs=[pl.BlockSpec((tm, tn), lambda j, i: (i, j)),
                   pl.BlockSpec((ms, tn), lambda j, i: (0, j)),
                   pl.BlockSpec((tm, d), lambda j, i: (i, 0)),
                   _resident((ms, d))],
        scratch_shapes=[pltpu.VMEM((d, tn), BF16)],
        compiler_params=_params(("arbitrary", "arbitrary"), VMEM_LIMIT_BIG_TILE_BYTES_V7X),
        name="inproj_first",
    )(x, ada, w_in, xs, ada)


def _inproj_wide_body(u_ref, us_ref, w_hbm, c_ref, wada_ref, bada_ref, *rest, tm, tn, col0,
                      n_side, side_steps):
    side_in = rest[:n_side]
    o_ref, os_ref, adab_ref = rest[n_side:n_side + 3]
    side_out = rest[n_side + 3:2 * n_side + 3]
    wf_ref, wb_ref, sem = rest[2 * n_side + 3:]
    j = pl.program_id(0)
    i = pl.program_id(1)

    def w_copy(jj):
        return pltpu.make_async_copy(w_hbm.at[:, pl.ds(col0 + jj * tn, tn)], wf_ref, sem.at[0])

    @pl.when(jnp.logical_and(j == 0, i == 0))
    def _():
        w_copy(0).start()

    @pl.when(jnp.logical_and(i == 1, j + 1 < pl.num_programs(0)))
    def _():
        w_copy(j + 1).start()

    def cast_piece(src, dst, lo, hi):
        dst[lo // 2:hi // 2, :] = _pack_rows(src[lo:hi, :].astype(BF16))

    def ada_piece(lo, hi):
        adab_ref[:, lo:hi] = jnp.dot(c_ref[...].astype(BF16), wada_ref[:, lo:hi].astype(BF16),
                                     preferred_element_type=F32) + bada_ref[:, lo:hi]

    pieces = []
    for src, dst in zip(side_in, side_out):
        rows = src.shape[0]
        nr = SIDE_PIECE_ROWS if rows % SIDE_PIECE_ROWS == 0 else rows
        pieces += [functools.partial(cast_piece, src, dst, lo, lo + nr)
                   for lo in range(0, rows, nr)]
    half = adab_ref.shape[1] // 2
    pieces += [functools.partial(ada_piece, 0, half),
               functools.partial(ada_piece, half, 2 * half)]
    n_chunks = tm // ROW_CHUNK

    def first_step(with_side):
        w_copy(j).wait()
        nb = tn // WEIGHT_CAST_COLS
        for n in range(nb):
            cols = slice(n * WEIGHT_CAST_COLS, (n + 1) * WEIGHT_CAST_COLS)
            wb_ref[:, cols] = wf_ref[:, cols].astype(BF16)
            os_ref[:, cols] = jnp.dot(us_ref[...], wb_ref[:, cols], preferred_element_type=F32)
            for c in range(n_chunks):
                rows = slice(c * ROW_CHUNK, (c + 1) * ROW_CHUNK)
                o_ref[rows, cols] = jnp.dot(u_ref[rows, :], wb_ref[:, cols],
                                            preferred_element_type=F32).astype(BF16)
            if with_side:
                for piece in pieces[n::nb]:
                    piece()

    def later_step(with_side):
        for c in range(n_chunks):
            rows = slice(c * ROW_CHUNK, (c + 1) * ROW_CHUNK)
            o_ref[rows, :] = jnp.dot(u_ref[rows, :], wb_ref[...],
                                     preferred_element_type=F32).astype(BF16)
            if with_side:
                for piece in pieces[c::n_chunks]:
                    piece()

    has_side = j * pl.num_programs(1) + i < side_steps
    no_side = jnp.logical_not(has_side)
    pl.when(jnp.logical_and(i == 0, has_side))(functools.partial(first_step, True))
    pl.when(jnp.logical_and(i == 0, no_side))(functools.partial(first_step, False))
    pl.when(jnp.logical_and(i > 0, has_side))(functools.partial(later_step, True))
    pl.when(jnp.logical_and(i > 0, no_side))(functools.partial(later_step, False))


def _inproj_wide(u, us, w_in, c_all, w_ada, b_ada, side_ws, *, col0, ncols, ada_col0, tm, tn,
                 side_steps=16):
    m, d = u.shape
    ms = us.shape[0]
    mc = c_all.shape[0]
    ni = m // tm
    n_ada = w_ada.shape[1] - ada_col0
    ada_tn = n_ada // side_steps
    assert m % tm == 0 and ni >= 2 and tm % ROW_CHUNK == 0 and ncols % tn == 0
    assert ncols // tn * ni >= side_steps and n_ada % side_steps == 0
    assert ada_tn % 256 == 0 and ada_col0 % ada_tn == 0
    body = functools.partial(_inproj_wide_body, tm=tm, tn=tn, col0=col0,
                             n_side=len(side_ws), side_steps=side_steps)
    side_step = lambda j, i: jnp.minimum(j * ni + i, side_steps - 1)
    side_idx = lambda j, i: (side_step(j, i), 0)
    ada_in_idx = lambda j, i: (0, ada_col0 // ada_tn + side_step(j, i))
    side_in = [pl.BlockSpec((w.shape[0] // side_steps, w.shape[1]), side_idx) for w in side_ws]
    side_out = [pl.BlockSpec((w.shape[0] // side_steps // 2, w.shape[1]), side_idx)
                for w in side_ws]
    side_shape = [jax.ShapeDtypeStruct((w.shape[0] // 2, w.shape[1]), U32) for w in side_ws]
    out = pl.pallas_call(
        body,
        out_shape=[jax.ShapeDtypeStruct((m, ncols), BF16),
                   jax.ShapeDtypeStruct((ms, ncols), F32),
                   jax.ShapeDtypeStruct((mc, n_ada), F32)] + side_shape,
        grid=(ncols // tn, ni),
        in_specs=[pl.BlockSpec((tm, d), lambda j, i: (i, 0)),
                  _resident((ms, d)),
                  pl.BlockSpec(memory_space=pl.ANY),
                  _resident((mc, d)),
                  pl.BlockSpec((d, ada_tn), ada_in_idx),
                  pl.BlockSpec((1, ada_tn), ada_in_idx)] + side_in,
        out_specs=[pl.BlockSpec((tm, tn), lambda j, i: (i, j)),
                   pl.BlockSpec((ms, tn), lambda j, i: (0, j)),
                   pl.BlockSpec((mc, ada_tn), lambda j, i: (0, side_step(j, i)))] + side_out,
        scratch_shapes=[pltpu.VMEM((d, tn), F32),
                        pltpu.VMEM((d, tn), BF16),
                        pltpu.SemaphoreType.DMA((1,))],
        compiler_params=_params(("arbitrary", "arbitrary"), MLP_VMEM_LIMIT_BYTES_V7X),
        name="inproj_rest",
    )(u, us, w_in, c_all, w_ada, b_ada.reshape(1, -1), *side_ws)
    return out[:3], out[3:]


def _sigmoid(x):
    return 0.5 * jnp.tanh(0.5 * x) + 0.5


def _mix_tail(pooled_bf, conv_bf, gp, gc, x, g1, sc2, sh2, wpu_ref, wcu_ref, wo_ref,
              lng_ref, lnb_ref, alpha):
    y_p = jnp.dot(pooled_bf, _unpack_rows(wpu_ref[...]), preferred_element_type=F32)
    y_c = jnp.dot(conv_bf, _unpack_rows(wcu_ref[...]), preferred_element_type=F32)
    mixed = (_sigmoid(gp) * y_p + _sigmoid(gc) * y_c).astype(BF16)
    mo = jnp.dot(mixed, _unpack_rows(wo_ref[...]), preferred_element_type=F32)
    x1 = _layernorm(alpha * x + g1 * mo, lng_ref[...], lnb_ref[...])
    return x1, (x1 * (1.0 + sc2) + sh2).astype(BF16)


def _split_cols(first_ref, rest_ref):
    p = first_ref.shape[-1]

    def cols(lo, hi):
        return first_ref[:, lo:hi] if hi <= p else rest_ref[:, lo - p:hi - p]
    return cols


def _mixer_body(pa_ref, pb_ref, ha_ref, hb_ref, x_ref, ada_ref, grp_ref, pscale_ref, convw_ref,
                wpu_ref, wcu_ref, wo_ref, lng_ref, lnb_ref, w1_ref, w2_ref,
                x1_ref, u2_ref, pst_ref, cst_ref, w1b_ref, w2b_ref,
                zz_ref, vv_ref, pm_ref, bc_ref, mixed_ref,
                *, tm, tiles_per_seq, d, alpha, steps):
    i = pl.program_id(0)
    p = d // 2
    gw = p // len(POOL_WINDOWS)
    h = HIST_ROWS
    proj = _split_cols(pa_ref, pb_ref)
    hist = _split_cols(ha_ref, hb_ref)

    cb = MIXER_COL_BLOCK
    ncb = d // cb
    tis = i % tiles_per_seq
    first = tis == 0

    def side_cast():
        tf = w1b_ref.shape[2]
        for fi in range(w1b_ref.shape[0]):
            w1b_ref[fi] = _pack_rows(w1_ref[:, fi * tf:(fi + 1) * tf].astype(BF16))
        w2b_ref[...] = _pack_rows(w2_ref[...].astype(BF16))

    def setup():
        zz_ref[0:h, :] = jnp.where(first, 0.0, hist(0, p).astype(F32))
        zz_ref[h:h + tm, :] = proj(0, p).astype(F32)
        vh = hist(3 * p, 4 * p).astype(F32) * hist(p, 2 * p).astype(F32)
        vv_ref[0:h, :] = jnp.where(first, 0.0, vh)
        vv_ref[h:h + tm, :] = proj(3 * p, 4 * p).astype(F32) * proj(p, 2 * p).astype(F32)
        pst_ref[0] = zz_ref[tm:tm + h, :]
        cst_ref[0] = vv_ref[tm:tm + h, :]

    def pool_group(g):
        w = POOL_WINDOWS[g]
        cols = slice(g * gw, (g + 1) * gw)
        pos = tis * tm + lax.broadcasted_iota(jnp.int32, (tm, 1), 0)
        s = zz_ref[:, cols]
        zt = s[h:, :]
        k = 1
        while k < w:
            s = s + pltpu.roll(s, k, axis=0)
            k *= 2
        inv_cnt = 1.0 / jnp.minimum(w, pos + 1).astype(F32)
        pooled = (s[h:, :] * inv_cnt - zt).astype(BF16)
        og = jnp.dot(pooled, _unpack_rows(grp_ref[g * gw // 2:(g + 1) * gw // 2, :]),
                     preferred_element_type=F32)
        pm_ref[:, cols] = (og * pscale_ref[:, cols]).astype(BF16)

    def conv_branch():
        conv = convw_ref[0:1, :] * vv_ref[h - 2:h - 2 + tm, :]
        conv = conv + convw_ref[1:2, :] * vv_ref[h - 1:h - 1 + tm, :]
        conv = conv + convw_ref[2:3, :] * vv_ref[h:h + tm, :]
        bc_ref[...] = (proj(2 * p, 3 * p).astype(F32) * conv).astype(BF16)

    def up_and_gate(k):
        cols = slice(k * cb, (k + 1) * cb)
        y_p = jnp.dot(pm_ref[...], _unpack_rows(wpu_ref[:, cols]), preferred_element_type=F32)
        y_c = jnp.dot(bc_ref[...], _unpack_rows(wcu_ref[:, cols]), preferred_element_type=F32)
        gp = proj(4 * p + k * cb, 4 * p + (k + 1) * cb).astype(F32)
        gc = proj(4 * p + d + k * cb, 4 * p + d + (k + 1) * cb).astype(F32)
        mixed_ref[:, cols] = (_sigmoid(gp) * y_p + _sigmoid(gc) * y_c).astype(BF16)

    def out_block(k):
        return jnp.dot(mixed_ref[...], _unpack_rows(wo_ref[:, k * cb:(k + 1) * cb]),
                       preferred_element_type=F32)

    def finish_rows(mo, r):
        nr = tm // MIXER_FINISH_SLABS
        rows = slice(r * nr, (r + 1) * nr)
        b = (i - 1) // tiles_per_seq
        g1 = ada_ref[pl.ds(b, 1), 0:d]
        sh2 = ada_ref[pl.ds(b, 1), d:2 * d]
        sc2 = ada_ref[pl.ds(b, 1), 2 * d:3 * d]
        x1 = _layernorm(alpha * x_ref[rows, :] + g1 * mo[rows, :], lng_ref[...], lnb_ref[...])
        x1_ref[rows, :] = x1
        u2_ref[rows, :] = (x1 * (1.0 + sc2) + sh2).astype(BF16)

    def step(has_head, has_tail):
        front = [side_cast, setup] + [functools.partial(pool_group, g)
                                      for g in range(len(POOL_WINDOWS))] + [conv_branch]
        mo_blocks = []
        for k in range(ncb):
            if has_tail:
                mo_blocks.append(out_block(k))
            if has_head and k < len(front):
                front[k]()
        if has_head:
            for piece in front[ncb:]:
                piece()
        mo = jnp.concatenate(mo_blocks, axis=1) if has_tail else None
        fin_at = {(r + 1) * ncb // MIXER_FINISH_SLABS - 1: r for r in range(MIXER_FINISH_SLABS)}
        for k in range(ncb):
            if has_head:
                up_and_gate(k)
            if has_tail and k in fin_at:
                finish_rows(mo, fin_at[k])

    pl.when(i == 0)(functools.partial(step, True, False))
    pl.when(jnp.logical_and(i > 0, i < steps))(functools.partial(step, True, True))
    pl.when(i == steps)(functools.partial(step, False, True))


def _mixer(proj_a, proj_b, xp, ada, grp_bf, pscale, convw, wpu_bf, wcu_bf, wo_bf, lng, lnb,
           w1, w2, *, seq, prompt_row, alpha, tf, tm=256):
    m, d = xp.shape
    p = d // 2
    nseq = m // seq
    h = HIST_ROWS
    assert seq % tm == 0 and m % seq == 0 and tm % h == 0 and d % MIXER_COL_BLOCK == 0
    assert proj_a.shape[1] == p and tm % MIXER_FINISH_SLABS == 0
    steps = m // tm
    dff = w1.shape[1]
    assert d % steps == 0 and dff % steps == 0 and dff % tf == 0
    body = functools.partial(_mixer_body, tm=tm, tiles_per_seq=seq // tm, d=d, alpha=alpha,
                             steps=steps)
    cur = lambda i: jnp.minimum(i, steps - 1)
    prev = lambda i: jnp.maximum(i - 1, 0)
    hist_idx = lambda i: (jnp.maximum(cur(i) * (tm // h) - 1, 0), 0)
    return pl.pallas_call(
        body,
        out_shape=[jax.ShapeDtypeStruct((m, d), F32),
                   jax.ShapeDtypeStruct((m, d), BF16),
                   jax.ShapeDtypeStruct((nseq, h, p), F32),
                   jax.ShapeDtypeStruct((nseq, h, p), F32),
                   jax.ShapeDtypeStruct((dff // tf, d // 2, tf), U32),
                   jax.ShapeDtypeStruct((dff // 2, d), U32)],
        grid=(steps + 1,),
        in_specs=[pl.BlockSpec((tm, proj_a.shape[1]), lambda i: (cur(i), 0)),
                  pl.BlockSpec((tm, proj_b.shape[1]), lambda i: (cur(i), 0)),
                  pl.BlockSpec((h, proj_a.shape[1]), hist_idx),
                  pl.BlockSpec((h, proj_b.shape[1]), hist_idx),
                  pl.BlockSpec((tm, d), lambda i: (prev(i), 0)),
                  pl.BlockSpec((8, ada.shape[1]), lambda i: (prompt_row // 8, 0)),
                  _resident(grp_bf.shape),
                  _resident(pscale.shape),
                  _resident(convw.shape),
                  _resident(wpu_bf.shape),
                  _resident(wcu_bf.shape),
                  _resident(wo_bf.shape),
                  _resident(lng.shape),
                  _resident(lnb.shape),
                  pl.BlockSpec((d // steps, dff), lambda i: (cur(i), 0)),
                  pl.BlockSpec((dff // steps, d), lambda i: (cur(i), 0))],
        out_specs=[pl.BlockSpec((tm, d), lambda i: (prev(i), 0)),
                   pl.BlockSpec((tm, d), lambda i: (prev(i), 0)),
                   pl.BlockSpec((1, h, p), lambda i: (cur(i) // (seq // tm), 0, 0)),
                   pl.BlockSpec((1, h, p), lambda i: (cur(i) // (seq // tm), 0, 0)),
                   pl.BlockSpec((dff // tf, d // steps // 2, tf), lambda i: (0, cur(i), 0)),
                   pl.BlockSpec((dff // steps // 2, d), lambda i: (cur(i), 0))],
        scratch_shapes=[pltpu.VMEM((tm + h, p), F32),
                        pltpu.VMEM((tm + h, p), F32),
                        pltpu.VMEM((tm, p), BF16),
                        pltpu.VMEM((tm, p), BF16),
                        pltpu.VMEM((tm, d), BF16)],
        compiler_params=_params(("arbitrary",)),
        name="mixer",
    )(proj_a, proj_b, proj_a, proj_b, xp, ada, grp_bf, pscale, convw, wpu_bf, wcu_bf, wo_bf,
      lng, lnb, w1, w2)


def _mixer_step_body(pa_ref, pb_ref, pool_ref, cstate_ref, x_ref, ada_ref, grp_ref, pscale_ref,
                     convw_ref, wpu_ref, wcu_ref, wo_ref, lng_ref, lnb_ref,
                     x1_ref, u2_ref, npool_ref, nconv_ref, pm_ref, *, d, alpha):
    p = d // 2
    gw = p // len(POOL_WINDOWS)
    proj = _split_cols(pa_ref, pb_ref)
    z = proj(0, p)
    for g, w in enumerate(POOL_WINDOWS):
        cols = slice(g * gw, (g + 1) * gw)
        s = z[:, cols]
        for k in range(1, w):
            s = s + pool_ref[POOL_HIST - k, :, cols]
        inv_cnt = 1.0 / min(w, PAST_LEN + 1)
        pooled = (s * inv_cnt - z[:, cols]).astype(BF16)
        og = jnp.dot(pooled, _unpack_rows(grp_ref[g * gw // 2:(g + 1) * gw // 2, :]),
                         preferred_element_type=F32)
        pm_ref[:, cols] = (og * pscale_ref[:, cols]).astype(BF16)
    for r in range(POOL_HIST - 1):
        npool_ref[r] = pool_ref[r + 1]
    npool_ref[POOL_HIST - 1] = z

    xc = proj(p, 2 * p)
    bc = proj(2 * p, 3 * p)
    cc = proj(3 * p, 4 * p)
    v = cc * xc
    conv = convw_ref[0:1, :] * cstate_ref[0]
    conv = conv + convw_ref[1:2, :] * cstate_ref[1]
    conv = conv + convw_ref[2:3, :] * v
    nconv_ref[0] = cstate_ref[1]
    nconv_ref[1] = v

    gp = proj(4 * p, 4 * p + d)
    gc = proj(4 * p + d, 4 * p + 2 * d)
    g1 = ada_ref[:, 0:d]
    sh2 = ada_ref[:, d:2 * d]
    sc2 = ada_ref[:, 2 * d:3 * d]
    x1, u2 = _mix_tail(pm_ref[...], (bc * conv).astype(BF16), gp, gc, x_ref[...], g1, sc2, sh2,
                       wpu_ref, wcu_ref, wo_ref, lng_ref, lnb_ref, alpha)
    x1_ref[...] = x1
    u2_ref[...] = u2


def _mixer_step(proj_a, proj_b, pool_state, conv_state, xs, ada, grp_bf, pscale, convw,
                wpu_bf, wcu_bf, wo_bf, lng, lnb, *, alpha):
    ms, d = xs.shape
    p = d // 2
    body = functools.partial(_mixer_step_body, d=d, alpha=alpha)
    return pl.pallas_call(
        body,
        out_shape=[jax.ShapeDtypeStruct((ms, d), F32),
                   jax.ShapeDtypeStruct((ms, d), BF16),
                   jax.ShapeDtypeStruct(pool_state.shape, F32),
                   jax.ShapeDtypeStruct(conv_state.shape, F32)],
        grid=(1,),
        in_specs=[_resident(proj_a.shape),
                  _resident(proj_b.shape),
                  _resident(pool_state.shape),
                  _resident(conv_state.shape),
                  _resident((ms, d)),
                  _resident((ms, ada.shape[1])),
                  _resident(grp_bf.shape),
                  _resident(pscale.shape),
                  _resident(convw.shape),
                  _resident(wpu_bf.shape),
                  _resident(wcu_bf.shape),
                  _resident(wo_bf.shape),
                  _resident(lng.shape),
                  _resident(lnb.shape)],
        out_specs=[_resident((ms, d)),
                   _resident((ms, d)),
                   _resident(pool_state.shape),
                   _resident(conv_state.shape)],
        scratch_shapes=[pltpu.VMEM((ms, p), BF16)],
        compiler_params=_params(("arbitrary",)),
        name="mixer_step",
    )(proj_a, proj_b, pool_state, conv_state, xs, ada, grp_bf, pscale, convw, wpu_bf, wcu_bf,
      wo_bf, lng, lnb)


def _mlp_body(u2_ref, w1_hbm, b1_ref, w2_hbm, x1_ref, g2_ref, b2_ref, lng_ref, lnb_ref,
              u2s_ref, x1s_ref, g2s_ref, o_ref, os_ref,
              acc_ref, accs_ref, w1_buf, w2_buf, h_buf, hs_buf, sem,
              *, nm, nf, te, tm, alpha, rows_per_cond):
    m = pl.program_id(0)
    f = pl.program_id(1)
    slot = m % 2
    t = m * nf + f
    w2_rows = w2_buf.shape[0]
    n_chunks = tm // MLP_ROW_CHUNK
    tp = te // n_chunks

    def w1_copy(fi):
        return pltpu.make_async_copy(w1_hbm.at[fi], w1_buf, sem.at[0])

    def w2_copy(fi):
        return pltpu.make_async_copy(w2_hbm.at[pl.ds(fi * w2_rows, w2_rows), :], w2_buf,
                                     sem.at[1])

    has_main = m < nm

    @pl.when(t == 0)
    def _():
        w1_copy(0).start()

    @pl.when(has_main)
    def _():
        w2_copy(f).start()
        w1_copy(f).wait()

    def hidden(u2):
        h = jnp.dot(u2, _unpack_rows(w1_buf[...]), preferred_element_type=F32) + b1_ref[f]
        return jnp.square(jnp.maximum(h, 0.0)).astype(BF16)

    def finish(x1, g2, acc):
        return _layernorm(alpha * x1 + g2 * (acc + b2_ref[...]), lng_ref[...], lnb_ref[...])

    def epilogue(c):
        part_rows = pl.ds(pl.multiple_of(c * tp, tp), tp)
        base = pl.multiple_of(f * te + c * tp, tp)
        g2 = g2_ref[pl.ds(((m - 1) * tm) // rows_per_cond, 1), :]
        o_ref[part_rows, :] = finish(x1_ref[part_rows, :], g2,
                                     acc_ref[1 - slot, pl.ds(base, tp), :])

    def main(first, singles, with_epilogue):
        def phase_a(c, carry):
            if with_epilogue:
                epilogue(c)
            rows = pl.ds(pl.multiple_of(c * MLP_ROW_CHUNK, MLP_ROW_CHUNK), MLP_ROW_CHUNK)
            h_buf[rows, :] = hidden(u2_ref[rows, :])
            return carry

        lax.fori_loop(0, n_chunks, phase_a, 0)
        if singles:
            hs_buf[...] = hidden(u2s_ref[...])

        @pl.when(t + 1 < nm * nf)
        def _():
            w1_copy(jnp.where(f + 1 == nf, 0, f + 1)).start()

        w2_copy(f).wait()

        def phase_b(c, carry):
            rows = pl.ds(pl.multiple_of(c * MLP_ROW_CHUNK, MLP_ROW_CHUNK), MLP_ROW_CHUNK)
            part = jnp.dot(h_buf[rows, :], _unpack_rows(w2_buf[...]),
                           preferred_element_type=F32)
            if first:
                acc_ref[slot, rows, :] = part
            else:
                acc_ref[slot, rows, :] += part
            return carry

        lax.fori_loop(0, n_chunks, phase_b, 0)
        if singles:
            part = jnp.dot(hs_buf[...], _unpack_rows(w2_buf[...]), preferred_element_type=F32)
            if first:
                accs_ref[...] = part
            else:
                accs_ref[...] += part

    tile0 = m == 0
    later = jnp.logical_and(has_main, m > 0)
    pl.when(jnp.logical_and(tile0, f == 0))(functools.partial(main, True, True, False))
    pl.when(jnp.logical_and(tile0, f > 0))(functools.partial(main, False, True, False))
    pl.when(jnp.logical_and(later, f == 0))(functools.partial(main, True, False, True))
    pl.when(jnp.logical_and(later, f > 0))(functools.partial(main, False, False, True))

    @pl.when(m == nm)
    def _():
        lax.fori_loop(0, n_chunks, lambda c, carry: (epilogue(c), carry)[1], 0)

    @pl.when(jnp.logical_and(m == 1, f == 0))
    def _():
        os_ref[...] = finish(x1s_ref[...], g2s_ref[...], accs_ref[...])


def _mlp(u2, x1, u2s, x1s, ada, w1b, b1, w2b, b2, lng, lnb, *, alpha, rows_per_cond,
         cond_row, tm):
    m, d = u2.shape
    ms = u2s.shape[0]
    nf, _, tf = w1b.shape
    nm = m // tm
    te = tm // nf
    assert m % tm == 0 and tm % nf == 0 and tm % MLP_ROW_CHUNK == 0
    assert rows_per_cond % tm == 0 and te % (tm // MLP_ROW_CHUNK) == 0 and cond_row % 8 == 0
    g2_col = ada.shape[1] // d - 1
    body = functools.partial(_mlp_body, nm=nm, nf=nf, te=te, tm=tm, alpha=alpha,
                             rows_per_cond=rows_per_cond)
    ep = lambda mi, f: (jnp.where(mi == 0, 0, (mi - 1) * nf + f), 0)
    return pl.pallas_call(
        body,
        out_shape=[jax.ShapeDtypeStruct((m, d), F32),
                   jax.ShapeDtypeStruct((ms, d), F32)],
        grid=(nm + 1, nf),
        in_specs=[pl.BlockSpec((tm, d), lambda mi, f: (jnp.minimum(mi, nm - 1), 0)),
                  pl.BlockSpec(memory_space=pl.ANY),
                  _resident((nf, 1, tf)),
                  pl.BlockSpec(memory_space=pl.ANY),
                  pl.BlockSpec((te, d), ep),
                  pl.BlockSpec((8, d), lambda mi, f: (cond_row // 8, g2_col)),
                  _resident((1, d)),
                  _resident((1, d)),
                  _resident((1, d)),
                  _resident((ms, d)),
                  _resident((ms, d)),
                  pl.BlockSpec((ms, d), lambda mi, f: (0, g2_col),
                               pipeline_mode=pl.Buffered(1))],
        out_specs=[pl.BlockSpec((te, d), ep),
                   _resident((ms, d))],
        scratch_shapes=[pltpu.VMEM((2, tm, d), F32),
                        pltpu.VMEM((ms, d), F32),
                        pltpu.VMEM((d // 2, tf), U32),
                        pltpu.VMEM((tf // 2, d), U32),
                        pltpu.VMEM((tm, tf), BF16),
                        pltpu.VMEM((ms, tf), BF16),
                        pltpu.SemaphoreType.DMA((2,))],
        compiler_params=_params(("arbitrary", "arbitrary"), MLP_VMEM_LIMIT_BYTES_V7X),
        name="mlp",
    )(u2, w1b, b1.reshape(nf, 1, tf), w2b, x1, ada, b2.reshape(1, d), lng, lnb, u2s, x1s, ada)


def kernel(x_prompt, x_sample, state_pool, state_conv, c_prompt, c_sample, w_ada, b_ada, w_in,
           pool_grp_w, pool_scale, conv_w, w_pool_up, w_conv_up, w_o, ln1_g, ln1_b, w_ff1,
           b_ff1, w_ff2, b_ff2, ln2_g, ln2_b):
    depth = w_ada.shape[0]
    nb, seq, d = x_prompt.shape
    ms, dec_seq, _ = x_sample.shape
    assert dec_seq == 1
    p = d // 2
    alpha = (2 * depth) ** 0.25

    prompt_row = ms
    pad = (-(ms + nb)) % 8
    c_all = jnp.concatenate([c_sample, c_prompt, jnp.zeros((pad, d), F32)], axis=0)

    xp = x_prompt.reshape(nb * seq, d)
    xs = x_sample.reshape(ms, d)
    pool_p, conv_p, pool_s, conv_s = [], [], [], []
    for l in range(depth):
        ada_in = _ada(c_all, w_ada[l], b_ada[l], ncols=ADA_EARLY * d)
        pscale = pool_scale[l].reshape(1, p)
        lng1, lnb1 = ln1_g[l].reshape(1, d), ln1_b[l].reshape(1, d)
        lng2, lnb2 = ln2_g[l].reshape(1, d), ln2_b[l].reshape(1, d)

        n_in = w_in.shape[2]
        pa_p, pa_s, u_bf, us_bf = _inproj_first(xp, xs, ada_in, w_in[l], ncols=p, seq=seq,
                                                prompt_row=prompt_row)
        (pb_p, pb_s, ada), (grp_bf, wpu_bf, wcu_bf, wo_bf) = _inproj_wide(
            u_bf, us_bf, w_in[l], c_all, w_ada[l], b_ada[l],
            [pool_grp_w[l].reshape(p, -1), w_pool_up[l], w_conv_up[l], w_o[l]],
            col0=p, ncols=n_in - p, ada_col0=ADA_EARLY * d, tm=INPROJ_BF16_TOKEN_TILE,
            tn=INPROJ_WIDE_COLS)

        x1p, u2p, pst, cst, w1b, w2b = _mixer(
            pa_p, pb_p, xp, ada, grp_bf, pscale, conv_w[l], wpu_bf, wcu_bf, wo_bf, lng1, lnb1,
            w_ff1[l], w_ff2[l], seq=seq, prompt_row=prompt_row, alpha=alpha, tf=MLP_FF_CHUNK)
        x1s, u2s, npool, nconv = _mixer_step(
            pa_s, pb_s, jnp.transpose(state_pool[l], (1, 0, 2)),
            jnp.transpose(state_conv[l], (1, 0, 2)), xs, ada, grp_bf, pscale, conv_w[l],
            wpu_bf, wcu_bf, wo_bf, lng1, lnb1, alpha=alpha)

        xp, xs = _mlp(u2p, x1p, u2s, x1s, ada, w1b, b_ff1[l], w2b, b_ff2[l], lng2, lnb2,
                      alpha=alpha, rows_per_cond=seq, cond_row=prompt_row,
                      tm=MLP_TOKEN_TILE)

        pool_p.append(pst[:, HIST_ROWS - POOL_HIST:, :])
        conv_p.append(cst[:, HIST_ROWS - CONV_HIST:, :])
        pool_s.append(jnp.transpose(npool, (1, 0, 2)))
        conv_s.append(jnp.transpose(nconv, (1, 0, 2)))

    return (xp.reshape(nb, seq, d), xs.reshape(ms, dec_seq, d),
            jnp.stack(pool_p), jnp.stack(conv_p), jnp.stack(pool_s), jnp.stack(conv_s))
```

```python
import functools

import jax
import jax.numpy as jnp
from jax import lax
from jax.experimental import pallas as pl
from jax.experimental.pallas import tpu as pltpu

F32 = jnp.float32
BF16 = jnp.bfloat16
U32 = jnp.uint32

POOL_WINDOWS = (2, 4, 8, 16)
POOL_HIST = max(POOL_WINDOWS) - 1
CONV_K = 3
CONV_HIST = CONV_K - 1
ADA_EARLY = 2
PAST_LEN = 16384
LN_EPS = 1e-5

VMEM_LIMIT_BYTES_V7X = 56 * 1024 * 1024
VMEM_LIMIT_BIG_TILE_BYTES_V7X = 60 * 1024 * 1024
MLP_VMEM_LIMIT_BYTES_V7X = 62 * 1024 * 1024
HIST_ROWS = 16
ROW_CHUNK = 256
MIXER_COL_BLOCK = 256
MIXER_FINISH_SLABS = 4
INPROJ_BF16_TOKEN_TILE = 2048
INPROJ_WIDE_COLS = 1024
SIDE_PIECE_ROWS = 128
MLP_TOKEN_TILE = 1024
MLP_FF_CHUNK = 2048
MLP_ROW_CHUNK = 512


def _resident(shape):
    zeros = (0,) * len(shape)
    return pl.BlockSpec(shape, lambda *_: zeros, pipeline_mode=pl.Buffered(1))


def _params(semantics, vmem_limit_bytes=VMEM_LIMIT_BYTES_V7X):
    return pltpu.CompilerParams(dimension_semantics=semantics,
                                vmem_limit_bytes=vmem_limit_bytes)


def _layernorm(xf, g, b):
    mu = jnp.mean(xf, axis=-1, keepdims=True)
    var = jnp.mean(jnp.square(xf - mu), axis=-1, keepdims=True)
    return (xf - mu) * lax.rsqrt(var + LN_EPS) * g + b


def _ada_body(c_ref, w_ref, b_ref, o_ref):
    o_ref[...] = jnp.dot(c_ref[...].astype(BF16), w_ref[...].astype(BF16),
                         preferred_element_type=F32) + b_ref[...]


def _ada(c_all, w_ada, b_ada, *, ncols, tn=1024):
    m, d = c_all.shape
    n = ncols
    assert n % tn == 0
    return pl.pallas_call(
        _ada_body,
        out_shape=jax.ShapeDtypeStruct((m, n), F32),
        grid=(n // tn,),
        in_specs=[pl.BlockSpec((m, d), lambda j: (0, 0)),
                  pl.BlockSpec((d, tn), lambda j: (0, j)),
                  pl.BlockSpec((1, tn), lambda j: (0, j))],
        out_specs=pl.BlockSpec((m, tn), lambda j: (0, j)),
        compiler_params=pltpu.CompilerParams(
            dimension_semantics=("arbitrary",), vmem_limit_bytes=VMEM_LIMIT_BYTES_V7X,
            allow_input_fusion=[True, False, False]),
        name="ada",
    )(c_all, w_ada, b_ada.reshape(1, -1))


def _pack_rows(x_bf16):
    return pltpu.bitcast(x_bf16, U32)


def _unpack_rows(x_u32):
    return pltpu.bitcast(x_u32, BF16)


def _inproj_first_body(x_ref, adap_ref, w_ref, xs_ref, adas_ref, o_ref, os_ref, u_ref, usb_ref,
                       wb_ref, *, tiles_per_seq, d, tm):
    i = pl.program_id(1)

    @pl.when(i == 0)
    def _():
        wb_ref[...] = w_ref[...].astype(BF16)
        us = (xs_ref[...] * (1.0 + adas_ref[:, d:2 * d]) + adas_ref[:, 0:d]).astype(BF16)
        os_ref[...] = jnp.dot(us, wb_ref[...], preferred_element_type=F32)
        usb_ref[...] = us

    b = i // tiles_per_seq
    sh1 = adap_ref[pl.ds(b, 1), 0:d]
    sc1 = adap_ref[pl.ds(b, 1), d:2 * d]
    for r in range(0, tm, ROW_CHUNK):
        rows = slice(r, r + ROW_CHUNK)
        u = (x_ref[rows, :] * (1.0 + sc1) + sh1).astype(BF16)
        u_ref[rows, :] = u
        o_ref[rows, :] = jnp.dot(u, wb_ref[...], preferred_element_type=F32).astype(BF16)


def _inproj_first(x, xs, ada, w_in, *, ncols, seq, prompt_row, tm=1024, tn=1024):
    m, d = x.shape
    ms = xs.shape[0]
    assert seq % tm == 0 and m % seq == 0 and tm % ROW_CHUNK == 0
    assert ncols % tn == 0 and prompt_row % 8 == 0
    body = functools.partial(_inproj_first_body, tiles_per_seq=seq // tm, d=d, tm=tm)
    return pl.pallas_call(
        body,
        out_shape=[jax.ShapeDtypeStruct((m, ncols), BF16),
                   jax.ShapeDtypeStruct((ms, ncols), F32),
                   jax.ShapeDtypeStruct((m, d), BF16),
                   jax.ShapeDtypeStruct((ms, d), BF16)],
        grid=(ncols // tn, m // tm),
        in_specs=[pl.BlockSpec((tm, d), lambda j, i: (i, 0)),
                  pl.BlockSpec((8, 2 * d), lambda j, i: (prompt_row // 8, 0)),
                  pl.BlockSpec((d, tn), lambda j, i: (0, j)),
                  _resident((ms, d)),
                  _resident((ms, 2 * d))],
        out_specs=[pl.BlockSpec((tm, tn), lambda j, i: (i, j)),
                   pl.BlockSpec((ms, tn), lambda j, i: (0, j)),
                   pl.BlockSpec((tm, d), lambda j, i: (i, 0)),
                   _resident((ms, d))],
        scratch_shapes=[pltpu.VMEM((d, tn), BF16)],
        compiler_params=_params(("arbitrary", "arbitrary"), VMEM_LIMIT_BIG_TILE_BYTES_V7X),
        name="inproj_first",
    )(x, ada, w_in, xs, ada)


def _inproj_wide_body(u_ref, us_ref, w_hbm, c_ref, wada_ref, bada_ref, *rest, tm, tn, col0,
                      n_side, side_steps):
    side_in = rest[:n_side]
    o_ref, os_ref, adab_ref = rest[n_side:n_side + 3]
    side_out = rest[n_side + 3:2 * n_side + 3]
    wf_ref, wb_ref, sem = rest[2 * n_side + 3:]
    j = pl.program_id(0)
    i = pl.program_id(1)
    nj = pl.num_programs(0)
    ni = pl.num_programs(1)
    slot = j % 2
    has_next = j + 1 < nj

    def w_copy(jj):
        return pltpu.make_async_copy(w_hbm.at[:, pl.ds(col0 + jj * tn, tn)], wf_ref, sem.at[0])

    @pl.when(jnp.logical_and(j == 0, i == 0))
    def _():
        w_copy(0).start()
        w_copy(0).wait()
        wb_ref[0] = wf_ref[...].astype(BF16)

    @pl.when(i == 0)
    def _():
        os_ref[...] = jnp.dot(us_ref[...], wb_ref[slot], preferred_element_type=F32)

    @pl.when(jnp.logical_and(i == 0, has_next))
    def _():
        w_copy(j + 1).start()

    def cast_piece(src, dst, lo, hi):
        dst[lo // 2:hi // 2, :] = _pack_rows(src[lo:hi, :].astype(BF16))

    def ada_piece(lo, hi):
        adab_ref[:, lo:hi] = jnp.dot(c_ref[...].astype(BF16), wada_ref[:, lo:hi].astype(BF16),
                                     preferred_element_type=F32) + bada_ref[:, lo:hi]

    pieces = []
    for src, dst in zip(side_in, side_out):
        rows = src.shape[0]
        nr = SIDE_PIECE_ROWS if rows % SIDE_PIECE_ROWS == 0 else rows
        pieces += [functools.partial(cast_piece, src, dst, lo, lo + nr)
                   for lo in range(0, rows, nr)]
    half = adab_ref.shape[1] // 2
    pieces += [functools.partial(ada_piece, 0, half),
               functools.partial(ada_piece, half, 2 * half)]
    n_chunks = tm // ROW_CHUNK

    def next_cast_piece(lo, hi):
        wb_ref[1 - slot, :, lo:hi] = wf_ref[:, lo:hi].astype(BF16)

    nc = tn // n_chunks
    next_pieces = [functools.partial(next_cast_piece, c * nc, (c + 1) * nc)
                   for c in range(n_chunks)]

    def main(extra):
        for c in range(n_chunks):
            rows = slice(c * ROW_CHUNK, (c + 1) * ROW_CHUNK)
            o_ref[rows, :] = jnp.dot(u_ref[rows, :], wb_ref[slot],
                                     preferred_element_type=F32).astype(BF16)
            for piece in extra[c::n_chunks]:
                piece()

    def last_step_of_tile():
        w_copy(j + 1).wait()
        main(next_pieces)

    has_side = jnp.logical_and(i < ni - 1, j * (ni - 1) + i < side_steps)
    casts_next = jnp.logical_and(i == ni - 1, has_next)
    plain = jnp.logical_not(jnp.logical_or(has_side, casts_next))
    pl.when(has_side)(functools.partial(main, pieces))
    pl.when(casts_next)(last_step_of_tile)
    pl.when(plain)(functools.partial(main, []))


def _inproj_wide(u, us, w_in, c_all, w_ada, b_ada, side_ws, *, col0, ncols, ada_col0, tm, tn,
                 side_steps=16):
    m, d = u.shape
    ms = us.shape[0]
    mc = c_all.shape[0]
    ni = m // tm
    n_ada = w_ada.shape[1] - ada_col0
    ada_tn = n_ada // side_steps
    assert m % tm == 0 and ni >= 2 and tm % ROW_CHUNK == 0 and ncols % tn == 0
    assert ncols // tn * (ni - 1) >= side_steps and n_ada % side_steps == 0
    assert ada_tn % 256 == 0 and ada_col0 % ada_tn == 0
    body = functools.partial(_inproj_wide_body, tm=tm, tn=tn, col0=col0,
                             n_side=len(side_ws), side_steps=side_steps)
    side_step = lambda j, i: jnp.minimum(j * (ni - 1) + jnp.minimum(i, ni - 2), side_steps - 1)
    side_idx = lambda j, i: (side_step(j, i), 0)
    ada_in_idx = lambda j, i: (0, ada_col0 // ada_tn + side_step(j, i))
    side_in = [pl.BlockSpec((w.shape[0] // side_steps, w.shape[1]), side_idx) for w in side_ws]
    side_out = [pl.BlockSpec((w.shape[0] // side_steps // 2, w.shape[1]), side_idx)
                for w in side_ws]
    side_shape = [jax.ShapeDtypeStruct((w.shape[0] // 2, w.shape[1]), U32) for w in side_ws]
    out = pl.pallas_call(
        body,
        out_shape=[jax.ShapeDtypeStruct((m, ncols), BF16),
                   jax.ShapeDtypeStruct((ms, ncols), F32),
                   jax.ShapeDtypeStruct((mc, n_ada), F32)] + side_shape,
        grid=(ncols // tn, ni),
        in_specs=[pl.BlockSpec((tm, d), lambda j, i: (i, 0)),
                  _resident((ms, d)),
                  pl.BlockSpec(memory_space=pl.ANY),
                  _resident((mc, d)),
                  pl.BlockSpec((d, ada_tn), ada_in_idx),
                  pl.BlockSpec((1, ada_tn), ada_in_idx)] + side_in,
        out_specs=[pl.BlockSpec((tm, tn), lambda j, i: (i, j)),
                   pl.BlockSpec((ms, tn), lambda j, i: (0, j)),
                   pl.BlockSpec((mc, ada_tn), lambda j, i: (0, side_step(j, i)))] + side_out,
        scratch_shapes=[pltpu.VMEM((d, tn), F32),
                        pltpu.VMEM((2, d, tn), BF16),
                        pltpu.SemaphoreType.DMA((1,))],
        compiler_params=_params(("arbitrary", "arbitrary"), MLP_VMEM_LIMIT_BYTES_V7X),
        name="inproj_rest",
    )(u, us, w_in, c_all, w_ada, b_ada.reshape(1, -1), *side_ws)
    return out[:3], out[3:]


def _sigmoid(x):
    return 0.5 * jnp.tanh(0.5 * x) + 0.5


def _mix_tail(pooled_bf, conv_bf, gp, gc, x, g1, sc2, sh2, wpu_ref, wcu_ref, wo_ref,
              lng_ref, lnb_ref, alpha):
    y_p = jnp.dot(pooled_bf, _unpack_rows(wpu_ref[...]), preferred_element_type=F32)
    y_c = jnp.dot(conv_bf, _unpack_rows(wcu_ref[...]), preferred_element_type=F32)
    mixed = (_sigmoid(gp) * y_p + _sigmoid(gc) * y_c).astype(BF16)
    mo = jnp.dot(mixed, _unpack_rows(wo_ref[...]), preferred_element_type=F32)
    x1 = _layernorm(alpha * x + g1 * mo, lng_ref[...], lnb_ref[...])
    return x1, (x1 * (1.0 + sc2) + sh2).astype(BF16)


def _split_cols(first_ref, rest_ref):
    p = first_ref.shape[-1]

    def cols(lo, hi):
        return first_ref[:, lo:hi] if hi <= p else rest_ref[:, lo - p:hi - p]
    return cols


def _mixer_body(pa_ref, pb_ref, ha_ref, hb_ref, x_ref, ada_ref, grp_ref, pscale_ref, convw_ref,
                wpu_ref, wcu_ref, wo_ref, lng_ref, lnb_ref, w1_ref, w2_ref,
                x1_ref, u2_ref, pst_ref, cst_ref, w1b_ref, w2b_ref,
                zz_ref, vv_ref, pm_ref, bc_ref, mixed_ref,
                *, tm, tiles_per_seq, d, alpha, steps):
    i = pl.program_id(0)
    p = d // 2
    gw = p // len(POOL_WINDOWS)
    h = HIST_ROWS
    proj = _split_cols(pa_ref, pb_ref)
    hist = _split_cols(ha_ref, hb_ref)

    cb = MIXER_COL_BLOCK
    ncb = d // cb
    tis = i % tiles_per_seq
    first = tis == 0

    def side_cast():
        tf = w1b_ref.shape[2]
        for fi in range(w1b_ref.shape[0]):
            w1b_ref[fi] = _pack_rows(w1_ref[:, fi * tf:(fi + 1) * tf].astype(BF16))
        w2b_ref[...] = _pack_rows(w2_ref[...].astype(BF16))

    def setup():
        zz_ref[0:h, :] = jnp.where(first, 0.0, hist(0, p).astype(F32))
        zz_ref[h:h + tm, :] = proj(0, p).astype(F32)
        vh = hist(3 * p, 4 * p).astype(F32) * hist(p, 2 * p).astype(F32)
        vv_ref[0:h, :] = jnp.where(first, 0.0, vh)
        vv_ref[h:h + tm, :] = proj(3 * p, 4 * p).astype(F32) * proj(p, 2 * p).astype(F32)
        pst_ref[0] = zz_ref[tm:tm + h, :]
        cst_ref[0] = vv_ref[tm:tm + h, :]

    def pool_group(g):
        w = POOL_WINDOWS[g]
        cols = slice(g * gw, (g + 1) * gw)
        pos = tis * tm + lax.broadcasted_iota(jnp.int32, (tm, 1), 0)
        s = zz_ref[:, cols]
        zt = s[h:, :]
        k = 1
        while k < w:
            s = s + pltpu.roll(s, k, axis=0)
            k *= 2
        inv_cnt = 1.0 / jnp.minimum(w, pos + 1).astype(F32)
        pooled = (s[h:, :] * inv_cnt - zt).astype(BF16)
        og = jnp.dot(pooled, _unpack_rows(grp_ref[g * gw // 2:(g + 1) * gw // 2, :]),
                     preferred_element_type=F32)
        pm_ref[:, cols] = (og * pscale_ref[:, cols]).astype(BF16)

    def conv_branch():
        conv = convw_ref[0:1, :] * vv_ref[h - 2:h - 2 + tm, :]
        conv = conv + convw_ref[1:2, :] * vv_ref[h - 1:h - 1 + tm, :]
        conv = conv + convw_ref[2:3, :] * vv_ref[h:h + tm, :]
        bc_ref[...] = (proj(2 * p, 3 * p).astype(F32) * conv).astype(BF16)

    def up_and_gate(k):
        cols = slice(k * cb, (k + 1) * cb)
        y_p = jnp.dot(pm_ref[...], _unpack_rows(wpu_ref[:, cols]), preferred_element_type=F32)
        y_c = jnp.dot(bc_ref[...], _unpack_rows(wcu_ref[:, cols]), preferred_element_type=F32)
        gp = proj(4 * p + k * cb, 4 * p + (k + 1) * cb).astype(F32)
        gc = proj(4 * p + d + k * cb, 4 * p + d + (k + 1) * cb).astype(F32)
        mixed_ref[:, cols] = (_sigmoid(gp) * y_p + _sigmoid(gc) * y_c).astype(BF16)

    def out_block(k):
        return jnp.dot(mixed_ref[...], _unpack_rows(wo_ref[:, k * cb:(k + 1) * cb]),
                       preferred_element_type=F32)

    def finish_rows(mo, r):
        nr = tm // MIXER_FINISH_SLABS
        rows = slice(r * nr, (r + 1) * nr)
        b = (i - 1) // tiles_per_seq
        g1 = ada_ref[pl.ds(b, 1), 0:d]
        sh2 = ada_ref[pl.ds(b, 1), d:2 * d]
        sc2 = ada_ref[pl.ds(b, 1), 2 * d:3 * d]
        x1 = _layernorm(alpha * x_ref[rows, :] + g1 * mo[rows, :], lng_ref[...], lnb_ref[...])
        x1_ref[rows, :] = x1
        u2_ref[rows, :] = (x1 * (1.0 + sc2) + sh2).astype(BF16)

    def step(has_head, has_tail):
        front = [side_cast, setup] + [functools.partial(pool_group, g)
                                      for g in range(len(POOL_WINDOWS))] + [conv_branch]
        mo_blocks = []
        for k in range(ncb):
            if has_tail:
                mo_blocks.append(out_block(k))
            if has_head and k < len(front):
                front[k]()
        if has_head:
            for piece in front[ncb:]:
                piece()
        mo = jnp.concatenate(mo_blocks, axis=1) if has_tail else None
        fin_at = {(r + 1) * ncb // MIXER_FINISH_SLABS - 1: r for r in range(MIXER_FINISH_SLABS)}
        for k in range(ncb):
            if has_head:
                up_and_gate(k)
            if has_tail and k in fin_at:
                finish_rows(mo, fin_at[k])

    pl.when(i == 0)(functools.partial(step, True, False))
    pl.when(jnp.logical_and(i > 0, i < steps))(functools.partial(step, True, True))
    pl.when(i == steps)(functools.partial(step, False, True))


def _mixer(proj_a, proj_b, xp, ada, grp_bf, pscale, convw, wpu_bf, wcu_bf, wo_bf, lng, lnb,
           w1, w2, *, seq, prompt_row, alpha, tf, tm=256):
    m, d = xp.shape
    p = d // 2
    nseq = m // seq
    h = HIST_ROWS
    assert seq % tm == 0 and m % seq == 0 and tm % h == 0 and d % MIXER_COL_BLOCK == 0
    assert proj_a.shape[1] == p and tm % MIXER_FINISH_SLABS == 0
    steps = m // tm
    dff = w1.shape[1]
    assert d % steps == 0 and dff % steps == 0 and dff % tf == 0
    body = functools.partial(_mixer_body, tm=tm, tiles_per_seq=seq // tm, d=d, alpha=alpha,
                             steps=steps)
    cur = lambda i: jnp.minimum(i, steps - 1)
    prev = lambda i: jnp.maximum(i - 1, 0)
    hist_idx = lambda i: (jnp.maximum(cur(i) * (tm // h) - 1, 0), 0)
    return pl.pallas_call(
        body,
        out_shape=[jax.ShapeDtypeStruct((m, d), F32),
                   jax.ShapeDtypeStruct((m, d), BF16),
                   jax.ShapeDtypeStruct((nseq, h, p), F32),
                   jax.ShapeDtypeStruct((nseq, h, p), F32),
                   jax.ShapeDtypeStruct((dff // tf, d // 2, tf), U32),
                   jax.ShapeDtypeStruct((dff // 2, d), U32)],
        grid=(steps + 1,),
        in_specs=[pl.BlockSpec((tm, proj_a.shape[1]), lambda i: (cur(i), 0)),
                  pl.BlockSpec((tm, proj_b.shape[1]), lambda i: (cur(i), 0)),
                  pl.BlockSpec((h, proj_a.shape[1]), hist_idx),
                  pl.BlockSpec((h, proj_b.shape[1]), hist_idx),
                  pl.BlockSpec((tm, d), lambda i: (prev(i), 0)),
                  pl.BlockSpec((8, ada.shape[1]), lambda i: (prompt_row // 8, 0)),
                  _resident(grp_bf.shape),
                  _resident(pscale.shape),
                  _resident(convw.shape),
                  _resident(wpu_bf.shape),
                  _resident(wcu_bf.shape),
                  _resident(wo_bf.shape),
                  _resident(lng.shape),
                  _resident(lnb.shape),
                  pl.BlockSpec((d // steps, dff), lambda i: (cur(i), 0)),
                  pl.BlockSpec((dff // steps, d), lambda i: (cur(i), 0))],
        out_specs=[pl.BlockSpec((tm, d), lambda i: (prev(i), 0)),
                   pl.BlockSpec((tm, d), lambda i: (prev(i), 0)),
                   pl.BlockSpec((1, h, p), lambda i: (cur(i) // (seq // tm), 0, 0)),
                   pl.BlockSpec((1, h, p), lambda i: (cur(i) // (seq // tm), 0, 0)),
                   pl.BlockSpec((dff // tf, d // steps // 2, tf), lambda i: (0, cur(i), 0)),
                   pl.BlockSpec((dff // steps // 2, d), lambda i: (cur(i), 0))],
        scratch_shapes=[pltpu.VMEM((tm + h, p), F32),
                        pltpu.VMEM((tm + h, p), F32),
                        pltpu.VMEM((tm, p), BF16),
                        pltpu.VMEM((tm, p), BF16),
                        pltpu.VMEM((tm, d), BF16)],
        compiler_params=_params(("arbitrary",)),
        name="mixer",
    )(proj_a, proj_b, proj_a, proj_b, xp, ada, grp_bf, pscale, convw, wpu_bf, wcu_bf, wo_bf,
      lng, lnb, w1, w2)


def _mixer_step_body(pa_ref, pb_ref, pool_ref, cstate_ref, x_ref, ada_ref, grp_ref, pscale_ref,
                     convw_ref, wpu_ref, wcu_ref, wo_ref, lng_ref, lnb_ref,
                     x1_ref, u2_ref, npool_ref, nconv_ref, pm_ref, *, d, alpha):
    p = d // 2
    gw = p // len(POOL_WINDOWS)
    proj = _split_cols(pa_ref, pb_ref)
    z = proj(0, p)
    for g, w in enumerate(POOL_WINDOWS):
        cols = slice(g * gw, (g + 1) * gw)
        s = z[:, cols]
        for k in range(1, w):
            s = s + pool_ref[POOL_HIST - k, :, cols]
        inv_cnt = 1.0 / min(w, PAST_LEN + 1)
        pooled = (s * inv_cnt - z[:, cols]).astype(BF16)
        og = jnp.dot(pooled, _unpack_rows(grp_ref[g * gw // 2:(g + 1) * gw // 2, :]),
                         preferred_element_type=F32)
        pm_ref[:, cols] = (og * pscale_ref[:, cols]).astype(BF16)
    for r in range(POOL_HIST - 1):
        npool_ref[r] = pool_ref[r + 1]
    npool_ref[POOL_HIST - 1] = z

    xc = proj(p, 2 * p)
    bc = proj(2 * p, 3 * p)
    cc = proj(3 * p, 4 * p)
    v = cc * xc
    conv = convw_ref[0:1, :] * cstate_ref[0]
    conv = conv + convw_ref[1:2, :] * cstate_ref[1]
    conv = conv + convw_ref[2:3, :] * v
    nconv_ref[0] = cstate_ref[1]
    nconv_ref[1] = v

    gp = proj(4 * p, 4 * p + d)
    gc = proj(4 * p + d, 4 * p + 2 * d)
    g1 = ada_ref[:, 0:d]
    sh2 = ada_ref[:, d:2 * d]
    sc2 = ada_ref[:, 2 * d:3 * d]
    x1, u2 = _mix_tail(pm_ref[...], (bc * conv).astype(BF16), gp, gc, x_ref[...], g1, sc2, sh2,
                       wpu_ref, wcu_ref, wo_ref, lng_ref, lnb_ref, alpha)
    x1_ref[...] = x1
    u2_ref[...] = u2


def _mixer_step(proj_a, proj_b, pool_state, conv_state, xs, ada, grp_bf, pscale, convw,
                wpu_bf, wcu_bf, wo_bf, lng, lnb, *, alpha):
    ms, d = xs.shape
    p = d // 2
    body = functools.partial(_mixer_step_body, d=d, alpha=alpha)
    return pl.pallas_call(
        body,
        out_shape=[jax.ShapeDtypeStruct((ms, d), F32),
                   jax.ShapeDtypeStruct((ms, d), BF16),
                   jax.ShapeDtypeStruct(pool_state.shape, F32),
                   jax.ShapeDtypeStruct(conv_state.shape, F32)],
        grid=(1,),
        in_specs=[_resident(proj_a.shape),
                  _resident(proj_b.shape),
                  _resident(pool_state.shape),
                  _resident(conv_state.shape),
                  _resident((ms, d)),
                  _resident((ms, ada.shape[1])),
                  _resident(grp_bf.shape),
                  _resident(pscale.shape),
                  _resident(convw.shape),
                  _resident(wpu_bf.shape),
                  _resident(wcu_bf.shape),
                  _resident(wo_bf.shape),
                  _resident(lng.shape),
                  _resident(lnb.shape)],
        out_specs=[_resident((ms, d)),
                   _resident((ms, d)),
                   _resident(pool_state.shape),
                   _resident(conv_state.shape)],
        scratch_shapes=[pltpu.VMEM((ms, p), BF16)],
        compiler_params=_params(("arbitrary",)),
        name="mixer_step",
    )(proj_a, proj_b, pool_state, conv_state, xs, ada, grp_bf, pscale, convw, wpu_bf, wcu_bf,
      wo_bf, lng, lnb)


def _mlp_body(u2_ref, w1_hbm, b1_ref, w2_hbm, x1_ref, g2_ref, b2_ref, lng_ref, lnb_ref,
              u2s_ref, x1s_ref, g2s_ref, o_ref, os_ref,
              acc_ref, accs_ref, w1_buf, w2_buf, h_buf, hs_buf, sem,
              *, nm, nf, te, tm, alpha, rows_per_cond):
    m = pl.program_id(0)
    f = pl.program_id(1)
    slot = m % 2
    t = m * nf + f
    w2_rows = w2_buf.shape[0]
    n_chunks = tm // MLP_ROW_CHUNK
    tp = te // n_chunks

    def w1_copy(fi):
        return pltpu.make_async_copy(w1_hbm.at[fi], w1_buf, sem.at[0])

    def w2_copy(fi):
        return pltpu.make_async_copy(w2_hbm.at[pl.ds(fi * w2_rows, w2_rows), :], w2_buf,
                                     sem.at[1])

    has_main = m < nm

    @pl.when(t == 0)
    def _():
        w1_copy(0).start()

    @pl.when(has_main)
    def _():
        w2_copy(f).start()
        w1_copy(f).wait()

    def hidden(u2):
        h = jnp.dot(u2, _unpack_rows(w1_buf[...]), preferred_element_type=F32) + b1_ref[f]
        return jnp.square(jnp.maximum(h, 0.0)).astype(BF16)

    def finish(x1, g2, acc):
        return _layernorm(alpha * x1 + g2 * (acc + b2_ref[...]), lng_ref[...], lnb_ref[...])

    def epilogue(c):
        part_rows = pl.ds(pl.multiple_of(c * tp, tp), tp)
        base = pl.multiple_of(f * te + c * tp, tp)
        g2 = g2_ref[pl.ds(((m - 1) * tm) // rows_per_cond, 1), :]
        o_ref[part_rows, :] = finish(x1_ref[part_rows, :], g2,
                                     acc_ref[1 - slot, pl.ds(base, tp), :])

    def main(first, singles, with_epilogue):
        def phase_a(c, carry):
            if with_epilogue:
                epilogue(c)
            rows = pl.ds(pl.multiple_of(c * MLP_ROW_CHUNK, MLP_ROW_CHUNK), MLP_ROW_CHUNK)
            h_buf[rows, :] = hidden(u2_ref[rows, :])
            return carry

        lax.fori_loop(0, n_chunks, phase_a, 0)
        if singles:
            hs_buf[...] = hidden(u2s_ref[...])

        @pl.when(t + 1 < nm * nf)
        def _():
            w1_copy(jnp.where(f + 1 == nf, 0, f + 1)).start()

        w2_copy(f).wait()

        def phase_b(c, carry):
            rows = pl.ds(pl.multiple_of(c * MLP_ROW_CHUNK, MLP_ROW_CHUNK), MLP_ROW_CHUNK)
            part = jnp.dot(h_buf[rows, :], _unpack_rows(w2_buf[...]),
                           preferred_element_type=F32)
            if first:
                acc_ref[slot, rows, :] = part
            else:
                acc_ref[slot, rows, :] += part
            return carry

        lax.fori_loop(0, n_chunks, phase_b, 0)
        if singles:
            part = jnp.dot(hs_buf[...], _unpack_rows(w2_buf[...]), preferred_element_type=F32)
            if first:
                accs_ref[...] = part
            else:
                accs_ref[...] += part

    tile0 = m == 0
    later = jnp.logical_and(has_main, m > 0)
    pl.when(jnp.logical_and(tile0, f == 0))(functools.partial(main, True, True, False))
    pl.when(jnp.logical_and(tile0, f > 0))(functools.partial(main, False, True, False))
    pl.when(jnp.logical_and(later, f == 0))(functools.partial(main, True, False, True))
    pl.when(jnp.logical_and(later, f > 0))(functools.partial(main, False, False, True))

    @pl.when(m == nm)
    def _():
        lax.fori_loop(0, n_chunks, lambda c, carry: (epilogue(c), carry)[1], 0)

    @pl.when(jnp.logical_and(m == 1, f == 0))
    def _():
        os_ref[...] = finish(x1s_ref[...], g2s_ref[...], accs_ref[...])


def _mlp(u2, x1, u2s, x1s, ada, w1b, b1, w2b, b2, lng, lnb, *, alpha, rows_per_cond,
         cond_row, tm):
    m, d = u2.shape
    ms = u2s.shape[0]
    nf, _, tf = w1b.shape
    nm = m // tm
    te = tm // nf
    assert m % tm == 0 and tm % nf == 0 and tm % MLP_ROW_CHUNK == 0
    assert rows_per_cond % tm == 0 and te % (tm // MLP_ROW_CHUNK) == 0 and cond_row % 8 == 0
    g2_col = ada.shape[1] // d - 1
    body = functools.partial(_mlp_body, nm=nm, nf=nf, te=te, tm=tm, alpha=alpha,
                             rows_per_cond=rows_per_cond)
    ep = lambda mi, f: (jnp.where(mi == 0, 0, (mi - 1) * nf + f), 0)
    return pl.pallas_call(
        body,
        out_shape=[jax.ShapeDtypeStruct((m, d), F32),
                   jax.ShapeDtypeStruct((ms, d), F32)],
        grid=(nm + 1, nf),
        in_specs=[pl.BlockSpec((tm, d), lambda mi, f: (jnp.minimum(mi, nm - 1), 0)),
                  pl.BlockSpec(memory_space=pl.ANY),
                  _resident((nf, 1, tf)),
                  pl.BlockSpec(memory_space=pl.ANY),
                  pl.BlockSpec((te, d), ep),
                  pl.BlockSpec((8, d), lambda mi, f: (cond_row // 8, g2_col)),
                  _resident((1, d)),
                  _resident((1, d)),
                  _resident((1, d)),
                  _resident((ms, d)),
                  _resident((ms, d)),
                  pl.BlockSpec((ms, d), lambda mi, f: (0, g2_col),
                               pipeline_mode=pl.Buffered(1))],
        out_specs=[pl.BlockSpec((te, d), ep),
                   _resident((ms, d))],
        scratch_shapes=[pltpu.VMEM((2, tm, d), F32),
                        pltpu.VMEM((ms, d), F32),
                        pltpu.VMEM((d // 2, tf), U32),
                        pltpu.VMEM((tf // 2, d), U32),
                        pltpu.VMEM((tm, tf), BF16),
                        pltpu.VMEM((ms, tf), BF16),
                        pltpu.SemaphoreType.DMA((2,))],
        compiler_params=_params(("arbitrary", "arbitrary"), MLP_VMEM_LIMIT_BYTES_V7X),
        name="mlp",
    )(u2, w1b, b1.reshape(nf, 1, tf), w2b, x1, ada, b2.reshape(1, d), lng, lnb, u2s, x1s, ada)


def kernel(x_prompt, x_sample, state_pool, state_conv, c_prompt, c_sample, w_ada, b_ada, w_in,
           pool_grp_w, pool_scale, conv_w, w_pool_up, w_conv_up, w_o, ln1_g, ln1_b, w_ff1,
           b_ff1, w_ff2, b_ff2, ln2_g, ln2_b):
    depth = w_ada.shape[0]
    nb, seq, d = x_prompt.shape
    ms, dec_seq, _ = x_sample.shape
    assert dec_seq == 1
    p = d // 2
    alpha = (2 * depth) ** 0.25

    prompt_row = ms
    pad = (-(ms + nb)) % 8
    c_all = jnp.concatenate([c_sample, c_prompt, jnp.zeros((pad, d), F32)], axis=0)

    xp = x_prompt.reshape(nb * seq, d)
    xs = x_sample.reshape(ms, d)
    pool_p, conv_p, pool_s, conv_s = [], [], [], []
    for l in range(depth):
        ada_in = _ada(c_all, w_ada[l], b_ada[l], ncols=ADA_EARLY * d)
        pscale = pool_scale[l].reshape(1, p)
        lng1, lnb1 = ln1_g[l].reshape(1, d), ln1_b[l].reshape(1, d)
        lng2, lnb2 = ln2_g[l].reshape(1, d), ln2_b[l].reshape(1, d)

        n_in = w_in.shape[2]
        pa_p, pa_s, u_bf, us_bf = _inproj_first(xp, xs, ada_in, w_in[l], ncols=p, seq=seq,
                                                prompt_row=prompt_row)
        (pb_p, pb_s, ada), (grp_bf, wpu_bf, wcu_bf, wo_bf) = _inproj_wide(
            u_bf, us_bf, w_in[l], c_all, w_ada[l], b_ada[l],
            [pool_grp_w[l].reshape(p, -1), w_pool_up[l], w_conv_up[l], w_o[l]],
            col0=p, ncols=n_in - p, ada_col0=ADA_EARLY * d, tm=INPROJ_BF16_TOKEN_TILE,
            tn=INPROJ_WIDE_COLS)

        x1p, u2p, pst, cst, w1b, w2b = _mixer(
            pa_p, pb_p, xp, ada, grp_bf, pscale, conv_w[l], wpu_bf, wcu_bf, wo_bf, lng1, lnb1,
            w_ff1[l], w_ff2[l], seq=seq, prompt_row=prompt_row, alpha=alpha, tf=MLP_FF_CHUNK)
        x1s, u2s, npool, nconv = _mixer_step(
            pa_s, pb_s, jnp.transpose(state_pool[l], (1, 0, 2)),
            jnp.transpose(state_conv[l], (1, 0, 2)), xs, ada, grp_bf, pscale, conv_w[l],
            wpu_bf, wcu_bf, wo_bf, lng1, lnb1, alpha=alpha)

        xp, xs = _mlp(u2p, x1p, u2s, x1s, ada, w1b, b_ff1[l], w2b, b_ff2[l], lng2, lnb2,
                      alpha=alpha, rows_per_cond=seq, cond_row=prompt_row,
                      tm=MLP_TOKEN_TILE)

        pool_p.append(pst[:, HIST_ROWS - POOL_HIST:, :])
        conv_p.append(cst[:, HIST_ROWS - CONV_HIST:, :])
        pool_s.append(jnp.transpose(npool, (1, 0, 2)))
        conv_s.append(jnp.transpose(nconv, (1, 0, 2)))

    return (xp.reshape(nb, seq, d), xs.reshape(ms, dec_seq, d),
            jnp.stack(pool_p), jnp.stack(conv_p), jnp.stack(pool_s), jnp.stack(conv_s))
```

```python
import functools

import jax
import jax.numpy as jnp
from jax import lax
from jax.experimental import pallas as pl
from jax.experimental.pallas import tpu as pltpu

F32 = jnp.float32
BF16 = jnp.bfloat16
U32 = jnp.uint32

POOL_WINDOWS = (2, 4, 8, 16)
POOL_HIST = max(POOL_WINDOWS) - 1
CONV_K = 3
CONV_HIST = CONV_K - 1
ADA_EARLY = 2
PAST_LEN = 16384
LN_EPS = 1e-5

VMEM_LIMIT_BYTES_V7X = 56 * 1024 * 1024
VMEM_LIMIT_BIG_TILE_BYTES_V7X = 60 * 1024 * 1024
MLP_VMEM_LIMIT_BYTES_V7X = 62 * 1024 * 1024
HIST_ROWS = 16
ROW_CHUNK = 256
MIXER_COL_BLOCK = 256
MIXER_FINISH_SLABS = 4
INPROJ_BF16_TOKEN_TILE = 2048
INPROJ_WIDE_COLS = 1024
SIDE_PIECE_ROWS = 128
MLP_TOKEN_TILE = 1024
MLP_FF_CHUNK = 2048
MLP_ROW_CHUNK = 512


def _resident(shape):
    zeros = (0,) * len(shape)
    return pl.BlockSpec(shape, lambda *_: zeros, pipeline_mode=pl.Buffered(1))


def _params(semantics, vmem_limit_bytes=VMEM_LIMIT_BYTES_V7X):
    return pltpu.CompilerParams(dimension_semantics=semantics,
                                vmem_limit_bytes=vmem_limit_bytes)


def _layernorm(xf, g, b):
    mu = jnp.mean(xf, axis=-1, keepdims=True)
    var = jnp.mean(jnp.square(xf - mu), axis=-1, keepdims=True)
    return (xf - mu) * lax.rsqrt(var + LN_EPS) * g + b


def _ada_body(c_ref, w_ref, b_ref, o_ref):
    o_ref[...] = jnp.dot(c_ref[...].astype(BF16), w_ref[...].astype(BF16),
                         preferred_element_type=F32) + b_ref[...]


def _ada(c_all, w_ada, b_ada, *, ncols, tn=1024):
    m, d = c_all.shape
    n = ncols
    assert n % tn == 0
    return pl.pallas_call(
        _ada_body,
        out_shape=jax.ShapeDtypeStruct((m, n), F32),
        grid=(n // tn,),
        in_specs=[pl.BlockSpec((m, d), lambda j: (0, 0)),
                  pl.BlockSpec((d, tn), lambda j: (0, j)),
                  pl.BlockSpec((1, tn), lambda j: (0, j))],
        out_specs=pl.BlockSpec((m, tn), lambda j: (0, j)),
        compiler_params=pltpu.CompilerParams(
            dimension_semantics=("arbitrary",), vmem_limit_bytes=VMEM_LIMIT_BYTES_V7X,
            allow_input_fusion=[True, False, False]),
        name="ada",
    )(c_all, w_ada, b_ada.reshape(1, -1))


def _pack_rows(x_bf16):
    return pltpu.bitcast(x_bf16, U32)


def _unpack_rows(x_u32):
    return pltpu.bitcast(x_u32, BF16)


def _inproj_first_body(x_ref, adap_ref, w_ref, xs_ref, adas_ref, o_ref, os_ref, u_ref, usb_ref,
                       wb_ref, *, tiles_per_seq, d, tm):
    i = pl.program_id(1)

    @pl.when(i == 0)
    def _():
        wb_ref[...] = w_ref[...].astype(BF16)
        us = (xs_ref[...] * (1.0 + adas_ref[:, d:2 * d]) + adas_ref[:, 0:d]).astype(BF16)
        os_ref[...] = jnp.dot(us, wb_ref[...], preferred_element_type=F32)
        usb_ref[...] = us

    b = i // tiles_per_seq
    sh1 = adap_ref[pl.ds(b, 1), 0:d]
    sc1 = adap_ref[pl.ds(b, 1), d:2 * d]
    for r in range(0, tm, ROW_CHUNK):
        rows = slice(r, r + ROW_CHUNK)
        u = (x_ref[rows, :] * (1.0 + sc1) + sh1).astype(BF16)
        u_ref[rows, :] = u
        o_ref[rows, :] = jnp.dot(u, wb_ref[...], preferred_element_type=F32).astype(BF16)


def _inproj_first(x, xs, ada, w_in, *, ncols, seq, prompt_row, tm=1024, tn=1024):
    m, d = x.shape
    ms = xs.shape[0]
    assert seq % tm == 0 and m % seq == 0 and tm % ROW_CHUNK == 0
    assert ncols % tn == 0 and prompt_row % 8 == 0
    body = functools.partial(_inproj_first_body, tiles_per_seq=seq // tm, d=d, tm=tm)
    return pl.pallas_call(
        body,
        out_shape=[jax.ShapeDtypeStruct((m, ncols), BF16),
                   jax.ShapeDtypeStruct((ms, ncols), F32),
                   jax.ShapeDtypeStruct((m, d), BF16),
                   jax.ShapeDtypeStruct((ms, d), BF16)],
        grid=(ncols // tn, m // tm),
        in_specs=[pl.BlockSpec((tm, d), lambda j, i: (i, 0)),
                  pl.BlockSpec((8, 2 * d), lambda j, i: (prompt_row // 8, 0)),
                  pl.BlockSpec((d, tn), lambda j, i: (0, j)),
                  _resident((ms, d)),
                  _resident((ms, 2 * d))],
        out_specs=[pl.BlockSpec((tm, tn), lambda j, i: (i, j)),
                   pl.BlockSpec((ms, tn), lambda j, i: (0, j)),
                   pl.BlockSpec((tm, d), lambda j, i: (i, 0)),
                   _resident((ms, d))],
        scratch_shapes=[pltpu.VMEM((d, tn), BF16)],
        compiler_params=_params(("arbitrary", "arbitrary"), VMEM_LIMIT_BIG_TILE_BYTES_V7X),
        name="inproj_first",
    )(x, ada, w_in, xs, ada)


def _inproj_wide_body(u_ref, us_ref, w_hbm, c_ref, wada_ref, bada_ref, *rest, tm, tn, col0,
                      n_side, side_steps):
    side_in = rest[:n_side]
    o_ref, os_ref, adab_ref = rest[n_side:n_side + 3]
    side_out = rest[n_side + 3:2 * n_side + 3]
    wf_ref, wb_ref, sem = rest[2 * n_side + 3:]
    j = pl.program_id(0)
    i = pl.program_id(1)

    def w_copy(jj):
        return pltpu.make_async_copy(w_hbm.at[:, pl.ds(col0 + jj * tn, tn)], wf_ref, sem.at[0])

    @pl.when(jnp.logical_and(j == 0, i == 0))
    def _():
        w_copy(0).start()

    @pl.when(i == 0)
    def _():
        w_copy(j).wait()
        wb_ref[...] = wf_ref[...].astype(BF16)
        os_ref[...] = jnp.dot(us_ref[...], wb_ref[...], preferred_element_type=F32)

    @pl.when(jnp.logical_and(i == 1, j + 1 < pl.num_programs(0)))
    def _():
        w_copy(j + 1).start()

    def cast_piece(src, dst, lo, hi):
        dst[lo // 2:hi // 2, :] = _pack_rows(src[lo:hi, :].astype(BF16))

    def ada_piece(lo, hi):
        adab_ref[:, lo:hi] = jnp.dot(c_ref[...].astype(BF16), wada_ref[:, lo:hi].astype(BF16),
                                     preferred_element_type=F32) + bada_ref[:, lo:hi]

    pieces = []
    for src, dst in zip(side_in, side_out):
        rows = src.shape[0]
        nr = SIDE_PIECE_ROWS if rows % SIDE_PIECE_ROWS == 0 else rows
        pieces += [functools.partial(cast_piece, src, dst, lo, lo + nr)
                   for lo in range(0, rows, nr)]
    half = adab_ref.shape[1] // 2
    pieces += [functools.partial(ada_piece, 0, half),
               functools.partial(ada_piece, half, 2 * half)]
    n_chunks = tm // ROW_CHUNK

    def main(with_side):
        for c in range(n_chunks):
            rows = slice(c * ROW_CHUNK, (c + 1) * ROW_CHUNK)
            o_ref[rows, :] = jnp.dot(u_ref[rows, :], wb_ref[...],
                                     preferred_element_type=F32).astype(BF16)
            if with_side:
                for piece in pieces[c::n_chunks]:
                    piece()

    has_side = j * pl.num_programs(1) + i < side_steps

    @pl.when(has_side)
    def _():
        for piece in pieces:
            piece()

    main(False)


def _inproj_wide(u, us, w_in, c_all, w_ada, b_ada, side_ws, *, col0, ncols, ada_col0, tm, tn,
                 side_steps=16):
    m, d = u.shape
    ms = us.shape[0]
    mc = c_all.shape[0]
    ni = m // tm
    n_ada = w_ada.shape[1] - ada_col0
    ada_tn = n_ada // side_steps
    assert m % tm == 0 and ni >= 2 and tm % ROW_CHUNK == 0 and ncols % tn == 0
    assert ncols // tn * ni >= side_steps and n_ada % side_steps == 0
    assert ada_tn % 256 == 0 and ada_col0 % ada_tn == 0
    body = functools.partial(_inproj_wide_body, tm=tm, tn=tn, col0=col0,
                             n_side=len(side_ws), side_steps=side_steps)
    side_step = lambda j, i: jnp.minimum(j * ni + i, side_steps - 1)
    side_idx = lambda j, i: (side_step(j, i), 0)
    ada_in_idx = lambda j, i: (0, ada_col0 // ada_tn + side_step(j, i))
    side_in = [pl.BlockSpec((w.shape[0] // side_steps, w.shape[1]), side_idx) for w in side_ws]
    side_out = [pl.BlockSpec((w.shape[0] // side_steps // 2, w.shape[1]), side_idx)
                for w in side_ws]
    side_shape = [jax.ShapeDtypeStruct((w.shape[0] // 2, w.shape[1]), U32) for w in side_ws]
    out = pl.pallas_call(
        body,
        out_shape=[jax.ShapeDtypeStruct((m, ncols), BF16),
                   jax.ShapeDtypeStruct((ms, ncols), F32),
                   jax.ShapeDtypeStruct((mc, n_ada), F32)] + side_shape,
        grid=(ncols // tn, ni),
        in_specs=[pl.BlockSpec((tm, d), lambda j, i: (i, 0)),
                  _resident((ms, d)),
                  pl.BlockSpec(memory_space=pl.ANY),
                  _resident((mc, d)),
                  pl.BlockSpec((d, ada_tn), ada_in_idx),
                  pl.BlockSpec((1, ada_tn), ada_in_idx)] + side_in,
        out_specs=[pl.BlockSpec((tm, tn), lambda j, i: (i, j)),
                   pl.BlockSpec((ms, tn), lambda j, i: (0, j)),
                   pl.BlockSpec((mc, ada_tn), lambda j, i: (0, side_step(j, i)))] + side_out,
        scratch_shapes=[pltpu.VMEM((d, tn), F32),
                        pltpu.VMEM((d, tn), BF16),
                        pltpu.SemaphoreType.DMA((1,))],
        compiler_params=_params(("arbitrary", "arbitrary"), MLP_VMEM_LIMIT_BYTES_V7X),
        name="inproj_rest",
    )(u, us, w_in, c_all, w_ada, b_ada.reshape(1, -1), *side_ws)
    return out[:3], out[3:]


def _sigmoid(x):
    return 0.5 * jnp.tanh(0.5 * x) + 0.5


def _mix_tail(pooled_bf, conv_bf, gp, gc, x, g1, sc2, sh2, wpu_ref, wcu_ref, wo_ref,
              lng_ref, lnb_ref, alpha):
    y_p = jnp.dot(pooled_bf, _unpack_rows(wpu_ref[...]), preferred_element_type=F32)
    y_c = jnp.dot(conv_bf, _unpack_rows(wcu_ref[...]), preferred_element_type=F32)
    mixed = (_sigmoid(gp) * y_p + _sigmoid(gc) * y_c).astype(BF16)
    mo = jnp.dot(mixed, _unpack_rows(wo_ref[...]), preferred_element_type=F32)
    x1 = _layernorm(alpha * x + g1 * mo, lng_ref[...], lnb_ref[...])
    return x1, (x1 * (1.0 + sc2) + sh2).astype(BF16)


def _split_cols(first_ref, rest_ref):
    p = first_ref.shape[-1]

    def cols(lo, hi):
        return first_ref[:, lo:hi] if hi <= p else rest_ref[:, lo - p:hi - p]
    return cols


def _mixer_body(pa_ref, pb_ref, ha_ref, hb_ref, x_ref, ada_ref, grp_ref, pscale_ref, convw_ref,
                wpu_ref, wcu_ref, wo_ref, lng_ref, lnb_ref, w1_ref, w2_ref,
                x1_ref, u2_ref, pst_ref, cst_ref, w1b_ref, w2b_ref,
                zz_ref, vv_ref, pm_ref, bc_ref, mixed_ref,
                *, tm, tiles_per_seq, d, alpha, steps):
    i = pl.program_id(0)
    p = d // 2
    gw = p // len(POOL_WINDOWS)
    h = HIST_ROWS
    proj = _split_cols(pa_ref, pb_ref)
    hist = _split_cols(ha_ref, hb_ref)

    cb = MIXER_COL_BLOCK
    ncb = d // cb
    tis = i % tiles_per_seq
    first = tis == 0

    def side_cast():
        tf = w1b_ref.shape[2]
        for fi in range(w1b_ref.shape[0]):
            w1b_ref[fi] = _pack_rows(w1_ref[:, fi * tf:(fi + 1) * tf].astype(BF16))
        w2b_ref[...] = _pack_rows(w2_ref[...].astype(BF16))

    def setup():
        zz_ref[0:h, :] = jnp.where(first, 0.0, hist(0, p).astype(F32))
        zz_ref[h:h + tm, :] = proj(0, p).astype(F32)
        vh = hist(3 * p, 4 * p).astype(F32) * hist(p, 2 * p).astype(F32)
        vv_ref[0:h, :] = jnp.where(first, 0.0, vh)
        vv_ref[h:h + tm, :] = proj(3 * p, 4 * p).astype(F32) * proj(p, 2 * p).astype(F32)
        pst_ref[0] = zz_ref[tm:tm + h, :]
        cst_ref[0] = vv_ref[tm:tm + h, :]

    def pool_group(g):
        w = POOL_WINDOWS[g]
        cols = slice(g * gw, (g + 1) * gw)
        pos = tis * tm + lax.broadcasted_iota(jnp.int32, (tm, 1), 0)
        s = zz_ref[:, cols]
        zt = s[h:, :]
        k = 1
        while k < w:
            s = s + pltpu.roll(s, k, axis=0)
            k *= 2
        inv_cnt = 1.0 / jnp.minimum(w, pos + 1).astype(F32)
        pooled = (s[h:, :] * inv_cnt - zt).astype(BF16)
        og = jnp.dot(pooled, _unpack_rows(grp_ref[g * gw // 2:(g + 1) * gw // 2, :]),
                     preferred_element_type=F32)
        pm_ref[:, cols] = (og * pscale_ref[:, cols]).astype(BF16)

    def conv_branch():
        conv = convw_ref[0:1, :] * vv_ref[h - 2:h - 2 + tm, :]
        conv = conv + convw_ref[1:2, :] * vv_ref[h - 1:h - 1 + tm, :]
        conv = conv + convw_ref[2:3, :] * vv_ref[h:h + tm, :]
        bc_ref[...] = (proj(2 * p, 3 * p).astype(F32) * conv).astype(BF16)

    def up_and_gate(k):
        cols = slice(k * cb, (k + 1) * cb)
        y_p = jnp.dot(pm_ref[...], _unpack_rows(wpu_ref[:, cols]), preferred_element_type=F32)
        y_c = jnp.dot(bc_ref[...], _unpack_rows(wcu_ref[:, cols]), preferred_element_type=F32)
        gp = proj(4 * p + k * cb, 4 * p + (k + 1) * cb).astype(F32)
        gc = proj(4 * p + d + k * cb, 4 * p + d + (k + 1) * cb).astype(F32)
        mixed_ref[:, cols] = (_sigmoid(gp) * y_p + _sigmoid(gc) * y_c).astype(BF16)

    def out_block(k):
        return jnp.dot(mixed_ref[...], _unpack_rows(wo_ref[:, k * cb:(k + 1) * cb]),
                       preferred_element_type=F32)

    def finish_rows(mo, r):
        nr = tm // MIXER_FINISH_SLABS
        rows = slice(r * nr, (r + 1) * nr)
        b = (i - 1) // tiles_per_seq
        g1 = ada_ref[pl.ds(b, 1), 0:d]
        sh2 = ada_ref[pl.ds(b, 1), d:2 * d]
        sc2 = ada_ref[pl.ds(b, 1), 2 * d:3 * d]
        x1 = _layernorm(alpha * x_ref[rows, :] + g1 * mo[rows, :], lng_ref[...], lnb_ref[...])
        x1_ref[rows, :] = x1
        u2_ref[rows, :] = (x1 * (1.0 + sc2) + sh2).astype(BF16)

    def step(has_head, has_tail):
        front = [side_cast, setup] + [functools.partial(pool_group, g)
                                      for g in range(len(POOL_WINDOWS))] + [conv_branch]
        mo_blocks = []
        for k in range(ncb):
            if has_tail:
                mo_blocks.append(out_block(k))
            if has_head and k < len(front):
                front[k]()
        if has_head:
            for piece in front[ncb:]:
                piece()
        mo = jnp.concatenate(mo_blocks, axis=1) if has_tail else None
        fin_at = {(r + 1) * ncb // MIXER_FINISH_SLABS - 1: r for r in range(MIXER_FINISH_SLABS)}
        for k in range(ncb):
            if has_head:
                up_and_gate(k)
            if has_tail and k in fin_at:
                finish_rows(mo, fin_at[k])

    pl.when(i == 0)(functools.partial(step, True, False))
    pl.when(jnp.logical_and(i > 0, i < steps))(functools.partial(step, True, True))
    pl.when(i == steps)(functools.partial(step, False, True))


def _mixer(proj_a, proj_b, xp, ada, grp_bf, pscale, convw, wpu_bf, wcu_bf, wo_bf, lng, lnb,
           w1, w2, *, seq, prompt_row, alpha, tf, tm=256):
    m, d = xp.shape
    p = d // 2
    nseq = m // seq
    h = HIST_ROWS
    assert seq % tm == 0 and m % seq == 0 and tm % h == 0 and d % MIXER_COL_BLOCK == 0
    assert proj_a.shape[1] == p and tm % MIXER_FINISH_SLABS == 0
    steps = m // tm
    dff = w1.shape[1]
    assert d % steps == 0 and dff % steps == 0 and dff % tf == 0
    body = functools.partial(_mixer_body, tm=tm, tiles_per_seq=seq // tm, d=d, alpha=alpha,
                             steps=steps)
    cur = lambda i: jnp.minimum(i, steps - 1)
    prev = lambda i: jnp.maximum(i - 1, 0)
    hist_idx = lambda i: (jnp.maximum(cur(i) * (tm // h) - 1, 0), 0)
    return pl.pallas_call(
        body,
        out_shape=[jax.ShapeDtypeStruct((m, d), F32),
                   jax.ShapeDtypeStruct((m, d), BF16),
                   jax.ShapeDtypeStruct((nseq, h, p), F32),
                   jax.ShapeDtypeStruct((nseq, h, p), F32),
                   jax.ShapeDtypeStruct((dff // tf, d // 2, tf), U32),
                   jax.ShapeDtypeStruct((dff // 2, d), U32)],
        grid=(steps + 1,),
        in_specs=[pl.BlockSpec((tm, proj_a.shape[1]), lambda i: (cur(i), 0)),
                  pl.BlockSpec((tm, proj_b.shape[1]), lambda i: (cur(i), 0)),
                  pl.BlockSpec((h, proj_a.shape[1]), hist_idx),
                  pl.BlockSpec((h, proj_b.shape[1]), hist_idx),
                  pl.BlockSpec((tm, d), lambda i: (prev(i), 0)),
                  pl.BlockSpec((8, ada.shape[1]), lambda i: (prompt_row // 8, 0)),
                  _resident(grp_bf.shape),
                  _resident(pscale.shape),
                  _resident(convw.shape),
                  _resident(wpu_bf.shape),
                  _resident(wcu_bf.shape),
                  _resident(wo_bf.shape),
                  _resident(lng.shape),
                  _resident(lnb.shape),
                  pl.BlockSpec((d // steps, dff), lambda i: (cur(i), 0)),
                  pl.BlockSpec((dff // steps, d), lambda i: (cur(i), 0))],
        out_specs=[pl.BlockSpec((tm, d), lambda i: (prev(i), 0)),
                   pl.BlockSpec((tm, d), lambda i: (prev(i), 0)),
                   pl.BlockSpec((1, h, p), lambda i: (cur(i) // (seq // tm), 0, 0)),
                   pl.BlockSpec((1, h, p), lambda i: (cur(i) // (seq // tm), 0, 0)),
                   pl.BlockSpec((dff // tf, d // steps // 2, tf), lambda i: (0, cur(i), 0)),
                   pl.BlockSpec((dff // steps // 2, d), lambda i: (cur(i), 0))],
        scratch_shapes=[pltpu.VMEM((tm + h, p), F32),
                        pltpu.VMEM((tm + h, p), F32),
                        pltpu.VMEM((tm, p), BF16),
                        pltpu.VMEM((tm, p), BF16),
                        pltpu.VMEM((tm, d), BF16)],
        compiler_params=_params(("arbitrary",)),
        name="mixer",
    )(proj_a, proj_b, proj_a, proj_b, xp, ada, grp_bf, pscale, convw, wpu_bf, wcu_bf, wo_bf,
      lng, lnb, w1, w2)


def _mixer_step_body(pa_ref, pb_ref, pool_ref, cstate_ref, x_ref, ada_ref, grp_ref, pscale_ref,
                     convw_ref, wpu_ref, wcu_ref, wo_ref, lng_ref, lnb_ref,
                     x1_ref, u2_ref, npool_ref, nconv_ref, pm_ref, *, d, alpha):
    p = d // 2
    gw = p // len(POOL_WINDOWS)
    proj = _split_cols(pa_ref, pb_ref)
    z = proj(0, p)
    for g, w in enumerate(POOL_WINDOWS):
        cols = slice(g * gw, (g + 1) * gw)
        s = z[:, cols]
        for k in range(1, w):
            s = s + pool_ref[POOL_HIST - k, :, cols]
        inv_cnt = 1.0 / min(w, PAST_LEN + 1)
        pooled = (s * inv_cnt - z[:, cols]).astype(BF16)
        og = jnp.dot(pooled, _unpack_rows(grp_ref[g * gw // 2:(g + 1) * gw // 2, :]),
                         preferred_element_type=F32)
        pm_ref[:, cols] = (og * pscale_ref[:, cols]).astype(BF16)
    for r in range(POOL_HIST - 1):
        npool_ref[r] = pool_ref[r + 1]
    npool_ref[POOL_HIST - 1] = z

    xc = proj(p, 2 * p)
    bc = proj(2 * p, 3 * p)
    cc = proj(3 * p, 4 * p)
    v = cc * xc
    conv = convw_ref[0:1, :] * cstate_ref[0]
    conv = conv + convw_ref[1:2, :] * cstate_ref[1]
    conv = conv + convw_ref[2:3, :] * v
    nconv_ref[0] = cstate_ref[1]
    nconv_ref[1] = v

    gp = proj(4 * p, 4 * p + d)
    gc = proj(4 * p + d, 4 * p + 2 * d)
    g1 = ada_ref[:, 0:d]
    sh2 = ada_ref[:, d:2 * d]
    sc2 = ada_ref[:, 2 * d:3 * d]
    x1, u2 = _mix_tail(pm_ref[...], (bc * conv).astype(BF16), gp, gc, x_ref[...], g1, sc2, sh2,
                       wpu_ref, wcu_ref, wo_ref, lng_ref, lnb_ref, alpha)
    x1_ref[...] = x1
    u2_ref[...] = u2


def _mixer_step(proj_a, proj_b, pool_state, conv_state, xs, ada, grp_bf, pscale, convw,
                wpu_bf, wcu_bf, wo_bf, lng, lnb, *, alpha):
    ms, d = xs.shape
    p = d // 2
    body = functools.partial(_mixer_step_body, d=d, alpha=alpha)
    return pl.pallas_call(
        body,
        out_shape=[jax.ShapeDtypeStruct((ms, d), F32),
                   jax.ShapeDtypeStruct((ms, d), BF16),
                   jax.ShapeDtypeStruct(pool_state.shape, F32),
                   jax.ShapeDtypeStruct(conv_state.shape, F32)],
        grid=(1,),
        in_specs=[_resident(proj_a.shape),
                  _resident(proj_b.shape),
                  _resident(pool_state.shape),
                  _resident(conv_state.shape),
                  _resident((ms, d)),
                  _resident((ms, ada.shape[1])),
                  _resident(grp_bf.shape),
                  _resident(pscale.shape),
                  _resident(convw.shape),
                  _resident(wpu_bf.shape),
                  _resident(wcu_bf.shape),
                  _resident(wo_bf.shape),
                  _resident(lng.shape),
                  _resident(lnb.shape)],
        out_specs=[_resident((ms, d)),
                   _resident((ms, d)),
                   _resident(pool_state.shape),
                   _resident(conv_state.shape)],
        scratch_shapes=[pltpu.VMEM((ms, p), BF16)],
        compiler_params=_params(("arbitrary",)),
        name="mixer_step",
    )(proj_a, proj_b, pool_state, conv_state, xs, ada, grp_bf, pscale, convw, wpu_bf, wcu_bf,
      wo_bf, lng, lnb)


def _mlp_body(u2_ref, w1_hbm, b1_ref, w2_hbm, x1_ref, g2_ref, b2_ref, lng_ref, lnb_ref,
              u2s_ref, x1s_ref, g2s_ref, o_ref, os_ref,
              acc_ref, accs_ref, w1_buf, w2_buf, h_buf, hs_buf, sem,
              *, nm, nf, te, tm, alpha, rows_per_cond):
    m = pl.program_id(0)
    f = pl.program_id(1)
    slot = m % 2
    t = m * nf + f
    w2_rows = w2_buf.shape[0]
    n_chunks = tm // MLP_ROW_CHUNK
    tp = te // n_chunks

    def w1_copy(fi):
        return pltpu.make_async_copy(w1_hbm.at[fi], w1_buf, sem.at[0])

    def w2_copy(fi):
        return pltpu.make_async_copy(w2_hbm.at[pl.ds(fi * w2_rows, w2_rows), :], w2_buf,
                                     sem.at[1])

    has_main = m < nm

    @pl.when(t == 0)
    def _():
        w1_copy(0).start()

    @pl.when(has_main)
    def _():
        w2_copy(f).start()
        w1_copy(f).wait()

    def hidden(u2):
        h = jnp.dot(u2, _unpack_rows(w1_buf[...]), preferred_element_type=F32) + b1_ref[f]
        return jnp.square(jnp.maximum(h, 0.0)).astype(BF16)

    def finish(x1, g2, acc):
        return _layernorm(alpha * x1 + g2 * (acc + b2_ref[...]), lng_ref[...], lnb_ref[...])

    def epilogue(c):
        part_rows = pl.ds(pl.multiple_of(c * tp, tp), tp)
        base = pl.multiple_of(f * te + c * tp, tp)
        g2 = g2_ref[pl.ds(((m - 1) * tm) // rows_per_cond, 1), :]
        o_ref[part_rows, :] = finish(x1_ref[part_rows, :], g2,
                                     acc_ref[1 - slot, pl.ds(base, tp), :])

    def main(first, singles, with_epilogue):
        def phase_a(c, carry):
            if with_epilogue:
                epilogue(c)
            rows = pl.ds(pl.multiple_of(c * MLP_ROW_CHUNK, MLP_ROW_CHUNK), MLP_ROW_CHUNK)
            h_buf[rows, :] = hidden(u2_ref[rows, :])
            return carry

        lax.fori_loop(0, n_chunks, phase_a, 0)
        if singles:
            hs_buf[...] = hidden(u2s_ref[...])

        @pl.when(t + 1 < nm * nf)
        def _():
            w1_copy(jnp.where(f + 1 == nf, 0, f + 1)).start()

        w2_copy(f).wait()

        def phase_b(c, carry):
            rows = pl.ds(pl.multiple_of(c * MLP_ROW_CHUNK, MLP_ROW_CHUNK), MLP_ROW_CHUNK)
            part = jnp.dot(h_buf[rows, :], _unpack_rows(w2_buf[...]),
                           preferred_element_type=F32)
            if first:
                acc_ref[slot, rows, :] = part
            else:
                acc_ref[slot, rows, :] += part
            return carry

        lax.fori_loop(0, n_chunks, phase_b, 0)
        if singles:
            part = jnp.dot(hs_buf[...], _unpack_rows(w2_buf[...]), preferred_element_type=F32)
            if first:
                accs_ref[...] = part
            else:
                accs_ref[...] += part

    tile0 = m == 0
    later = jnp.logical_and(has_main, m > 0)
    pl.when(jnp.logical_and(tile0, f == 0))(functools.partial(main, True, True, False))
    pl.when(jnp.logical_and(tile0, f > 0))(functools.partial(main, False, True, False))
    pl.when(jnp.logical_and(later, f == 0))(functools.partial(main, True, False, True))
    pl.when(jnp.logical_and(later, f > 0))(functools.partial(main, False, False, True))

    @pl.when(m == nm)
    def _():
        lax.fori_loop(0, n_chunks, lambda c, carry: (epilogue(c), carry)[1], 0)

    @pl.when(jnp.logical_and(m == 1, f == 0))
    def _():
        os_ref[...] = finish(x1s_ref[...], g2s_ref[...], accs_ref[...])


def _mlp(u2, x1, u2s, x1s, ada, w1b, b1, w2b, b2, lng, lnb, *, alpha, rows_per_cond,
         cond_row, tm):
    m, d = u2.shape
    ms = u2s.shape[0]
    nf, _, tf = w1b.shape
    nm = m // tm
    te = tm // nf
    assert m % tm == 0 and tm % nf == 0 and tm % MLP_ROW_CHUNK == 0
    assert rows_per_cond % tm == 0 and te % (tm // MLP_ROW_CHUNK) == 0 and cond_row % 8 == 0
    g2_col = ada.shape[1] // d - 1
    body = functools.partial(_mlp_body, nm=nm, nf=nf, te=te, tm=tm, alpha=alpha,
                             rows_per_cond=rows_per_cond)
    ep = lambda mi, f: (jnp.where(mi == 0, 0, (mi - 1) * nf + f), 0)
    return pl.pallas_call(
        body,
        out_shape=[jax.ShapeDtypeStruct((m, d), F32),
                   jax.ShapeDtypeStruct((ms, d), F32)],
        grid=(nm + 1, nf),
        in_specs=[pl.BlockSpec((tm, d), lambda mi, f: (jnp.minimum(mi, nm - 1), 0)),
                  pl.BlockSpec(memory_space=pl.ANY),
                  _resident((nf, 1, tf)),
                  pl.BlockSpec(memory_space=pl.ANY),
                  pl.BlockSpec((te, d), ep),
                  pl.BlockSpec((8, d), lambda mi, f: (cond_row // 8, g2_col)),
                  _resident((1, d)),
                  _resident((1, d)),
                  _resident((1, d)),
                  _resident((ms, d)),
                  _resident((ms, d)),
                  pl.BlockSpec((ms, d), lambda mi, f: (0, g2_col),
                               pipeline_mode=pl.Buffered(1))],
        out_specs=[pl.BlockSpec((te, d), ep),
                   _resident((ms, d))],
        scratch_shapes=[pltpu.VMEM((2, tm, d), F32),
                        pltpu.VMEM((ms, d), F32),
                        pltpu.VMEM((d // 2, tf), U32),
                        pltpu.VMEM((tf // 2, d), U32),
                        pltpu.VMEM((tm, tf), BF16),
                        pltpu.VMEM((ms, tf), BF16),
                        pltpu.SemaphoreType.DMA((2,))],
        compiler_params=_params(("arbitrary", "arbitrary"), MLP_VMEM_LIMIT_BYTES_V7X),
        name="mlp",
    )(u2, w1b, b1.reshape(nf, 1, tf), w2b, x1, ada, b2.reshape(1, d), lng, lnb, u2s, x1s, ada)


def kernel(x_prompt, x_sample, state_pool, state_conv, c_prompt, c_sample, w_ada, b_ada, w_in,
           pool_grp_w, pool_scale, conv_w, w_pool_up, w_conv_up, w_o, ln1_g, ln1_b, w_ff1,
           b_ff1, w_ff2, b_ff2, ln2_g, ln2_b):
    depth = w_ada.shape[0]
    nb, seq, d = x_prompt.shape
    ms, dec_seq, _ = x_sample.shape
    assert dec_seq == 1
    p = d // 2
    alpha = (2 * depth) ** 0.25

    prompt_row = ms
    pad = (-(ms + nb)) % 8
    c_all = jnp.concatenate([c_sample, c_prompt, jnp.zeros((pad, d), F32)], axis=0)

    xp = x_prompt.reshape(nb * seq, d)
    xs = x_sample.reshape(ms, d)
    pool_p, conv_p, pool_s, conv_s = [], [], [], []
    for l in range(depth):
        ada_in = _ada(c_all, w_ada[l], b_ada[l], ncols=ADA_EARLY * d)
        pscale = pool_scale[l].reshape(1, p)
        lng1, lnb1 = ln1_g[l].reshape(1, d), ln1_b[l].reshape(1, d)
        lng2, lnb2 = ln2_g[l].reshape(1, d), ln2_b[l].reshape(1, d)

        n_in = w_in.shape[2]
        pa_p, pa_s, u_bf, us_bf = _inproj_first(xp, xs, ada_in, w_in[l], ncols=p, seq=seq,
                                                prompt_row=prompt_row)
        (pb_p, pb_s, ada), (grp_bf, wpu_bf, wcu_bf, wo_bf) = _inproj_wide(
            u_bf, us_bf, w_in[l], c_all, w_ada[l], b_ada[l],
            [pool_grp_w[l].reshape(p, -1), w_pool_up[l], w_conv_up[l], w_o[l]],
            col0=p, ncols=n_in - p, ada_col0=ADA_EARLY * d, tm=INPROJ_BF16_TOKEN_TILE,
            tn=INPROJ_WIDE_COLS)

        x1p, u2p, pst, cst, w1b, w2b = _mixer(
            pa_p, pb_p, xp, ada, grp_bf, pscale, conv_w[l], wpu_bf, wcu_bf, wo_bf, lng1, lnb1,
            w_ff1[l], w_ff2[l], seq=seq, prompt_row=prompt_row, alpha=alpha, tf=MLP_FF_CHUNK)
        x1s, u2s, npool, nconv = _mixer_step(
            pa_s, pb_s, jnp.transpose(state_pool[l], (1, 0, 2)),
            jnp.transpose(state_conv[l], (1, 0, 2)), xs, ada, grp_bf, pscale, conv_w[l],
            wpu_bf, wcu_bf, wo_bf, lng1, lnb1, alpha=alpha)

        xp, xs = _mlp(u2p, x1p, u2s, x1s, ada, w1b, b_ff1[l], w2b, b_ff2[l], lng2, lnb2,
                      alpha=alpha, rows_per_cond=seq, cond_row=prompt_row,
                      tm=MLP_TOKEN_TILE)

        pool_p.append(pst[:, HIST_ROWS - POOL_HIST:, :])
        conv_p.append(cst[:, HIST_ROWS - CONV_HIST:, :])
        pool_s.append(jnp.transpose(npool, (1, 0, 2)))
        conv_s.append(jnp.transpose(nconv, (1, 0, 2)))

    return (xp.reshape(nb, seq, d), xs.reshape(ms, dec_seq, d),
            jnp.stack(pool_p), jnp.stack(conv_p), jnp.stack(pool_s), jnp.stack(conv_s))
```
